```python
import jax, jax.numpy as jnp
from jax import lax
import numpy as np

D_MODEL = 1024
BATCH = 8
SEQ = 2048
DEPTH = 1
DEC_BATCH = 8
DEC_SEQ = 64
PAST_LEN = 2048

CHUNK = 64
EPS = 1e-6
MLSTM_HEADS = 4
MLSTM_HEAD_DIM = D_MODEL // MLSTM_HEADS
MLSTM_WIDTH = MLSTM_HEADS * MLSTM_HEAD_DIM
CONV_W = 4
GMLP_GROUPS = 4
GMLP_WIDTH = D_MODEL
GMLP_GROUP_DIM = GMLP_WIDTH // GMLP_GROUPS
GMLP_CHUNK = 128
N_MEM = 256
MEM_HEADS = 4
MEM_HEAD_DIM = D_MODEL // MEM_HEADS
MEM_WIDTH = MEM_HEADS * MEM_HEAD_DIM
N_BRANCH = 3
N_GROUPS = 4
EXPERTS_PER_GROUP = 8
N_EXPERTS = N_GROUPS * EXPERTS_PER_GROUP
TOP_K = 2
D_EXPERT = 256
IN_COLS = 4 * MLSTM_WIDTH + 2 * MLSTM_HEADS + 2 * GMLP_WIDTH + MEM_WIDTH + N_BRANCH * D_MODEL

kernel_name = "hybrid_mlstm_gmlp_memattn_hmoe_step"


def rmsnorm(x, g):
    xf = x.astype(jnp.float32)
    r = lax.rsqrt(jnp.mean(xf * xf, axis=-1, keepdims=True) + EPS)
    return (xf * r).astype(x.dtype) * g


def layernorm(x, g, b):
    xf = x.astype(jnp.float32)
    mu = jnp.mean(xf, axis=-1, keepdims=True)
    xc = xf - mu
    r = lax.rsqrt(jnp.mean(xc * xc, axis=-1, keepdims=True) + EPS)
    return (xc * r).astype(x.dtype) * g + b


def split_in_proj(z):
    sizes = (2 * MLSTM_WIDTH, MLSTM_WIDTH, MLSTM_WIDTH, MLSTM_HEADS, MLSTM_HEADS,
             GMLP_WIDTH, GMLP_WIDTH, MEM_WIDTH, N_BRANCH * D_MODEL)
    idx = np.cumsum(sizes)[:-1].tolist()
    return jnp.split(z, idx, axis=-1)


def causal_conv(x, buf, w, b):
    L = x.shape[1]
    xp = jnp.concatenate([buf, x], axis=1)
    y = b + sum(xp[:, j:j + L] * w[j] for j in range(CONV_W))
    return y, xp[:, -(CONV_W - 1):]


def mlstm_block(carry, inp):
    C0, n0, m0 = carry
    q, k, v, ig, lf = inp
    L = q.shape[2]
    bcum = jnp.cumsum(lf, axis=-1)
    inter = bcum + m0[..., None]
    tri = jnp.tril(jnp.ones((L, L), dtype=bool))
    dmat = jnp.where(tri, bcum[..., :, None] - bcum[..., None, :] + ig[..., None, :], -jnp.inf)
    m = jnp.maximum(inter, jnp.max(dmat, axis=-1))
    w_intra = jnp.exp(dmat - m[..., None])
    w_inter = jnp.exp(inter - m)
    s = jnp.einsum('bhtd,bhsd->bhts', q, k) * w_intra
    num = w_inter[..., None] * jnp.einsum('bhtd,bhde->bhte', q, C0) + jnp.einsum('bhts,bhse->bhte', s, v)
    den = w_inter * jnp.einsum('bhtd,bhd->bht', q, n0) + jnp.sum(s, axis=-1)
    h = num / jnp.maximum(jnp.abs(den), jnp.exp(-m))[..., None]
    m_last = m[..., -1]
    w_last = jnp.exp(bcum[..., -1:] - bcum + ig - m_last[..., None])
    decay = jnp.exp(bcum[..., -1] + m0 - m_last)
    C1 = decay[..., None, None] * C0 + jnp.einsum('bhs,bhsd,bhse->bhde', w_last, k, v)
    n1 = decay[..., None] * n0 + jnp.einsum('bhs,bhsd->bhd', w_last, k)
    return (C1, n1, m_last), h


def mlstm_sequence(q, k, v, ig, lf, carry):
    B, H, L, d = q.shape
    blk = CHUNK if L % CHUNK == 0 else L
    nb = L // blk

    def to_blocks(t):
        return jnp.moveaxis(t.reshape(t.shape[:2] + (nb, blk) + t.shape[3:]), 2, 0)

    carry, h = lax.scan(mlstm_block, carry, (to_blocks(q), to_blocks(k), to_blocks(v), to_blocks(ig), to_blocks(lf)))
    h = jnp.moveaxis(h, 0, 2).reshape(B, H, L, d)
    return h, carry


def memory_kv(mem, g, w_k, w_v):
    B = mem.shape[0]
    mn = rmsnorm(mem, g)
    k = (mn @ w_k).reshape(B, N_MEM, MEM_HEADS, MEM_HEAD_DIM)
    v = (mn @ w_v).reshape(B, N_MEM, MEM_HEADS, MEM_HEAD_DIM)
    return k, v


def hier_moe(x, w_rg, b_rg, w_re, b_re, w_g, w_u, w_d):
    B, L, D = x.shape
    xf = x.reshape(-1, D)
    p_group = jax.nn.softmax((xf @ w_rg + b_rg).astype(jnp.float32), axis=-1)
    p_top, grp = lax.top_k(p_group, 1)
    e_logits = (xf @ w_re + b_re).astype(jnp.float32).reshape(-1, N_GROUPS, EXPERTS_PER_GROUP)
    e_in = jnp.take_along_axis(e_logits, grp[:, :, None], axis=1)[:, 0]
    top_vals, top_idx = lax.top_k(e_in, TOP_K)
    p_exp = jax.nn.softmax(top_vals, axis=-1) * p_top
    expert_id = grp * EXPERTS_PER_GROUP + top_idx
    combine = jnp.sum(jax.nn.one_hot(expert_id, N_EXPERTS, dtype=jnp.float32) * p_exp[..., None], axis=1)
    combine = combine.astype(x.dtype)
    y = jnp.zeros_like(xf)
    for e in range(N_EXPERTS):
        hidden = jax.nn.silu(xf @ w_g[e]) * (xf @ w_u[e])
        y = y + combine[:, e:e + 1] * (hidden @ w_d[e])
    return y.reshape(B, L, D)


def trunk_layer(x, mem_k, mem_v, C0, n0, m0, conv_buf, p):
    B, L, _ = x.shape
    f32 = jnp.float32
    xn = rmsnorm(x, p["norm_mix_g"])
    z = xn @ p["w_in"]
    qk_raw, v_raw, o_raw, i_raw, f_raw, u_raw, gv_raw, mq_raw, gate_raw = split_in_proj(z)

    qk, conv_new = causal_conv(qk_raw, conv_buf, p["mlstm_conv_w"], p["mlstm_conv_b"])
    qk = jax.nn.silu(qk)

    def heads(t):
        return t.reshape(B, L, MLSTM_HEADS, MLSTM_HEAD_DIM).transpose(0, 2, 1, 3).astype(f32)

    q = heads(qk[..., :MLSTM_WIDTH]) * (MLSTM_HEAD_DIM ** -0.5)
    k = heads(qk[..., MLSTM_WIDTH:])
    v = heads(v_raw)
    ig = (i_raw + p["mlstm_i_b"]).astype(f32).transpose(0, 2, 1)
    lf = jax.nn.log_sigmoid((f_raw + p["mlstm_f_b"]).astype(f32)).transpose(0, 2, 1)
    h, (C1, n1, m1) = mlstm_sequence(q, k, v, ig, lf, (C0, n0, m0))
    h = h.transpose(0, 2, 1, 3)
    h = h * lax.rsqrt(jnp.mean(h * h, axis=-1, keepdims=True) + EPS)
    h = h.reshape(B, L, MLSTM_WIDTH).astype(x.dtype) * p["mlstm_norm_g"]
    br_mlstm = (h * jax.nn.sigmoid(o_raw)) @ p["w_br_mlstm"]

    u = jax.nn.gelu(u_raw)
    gv = layernorm(jax.nn.gelu(gv_raw), p["gmlp_norm_g"], p["gmlp_norm_b"])
    cl = min(L, GMLP_CHUNK)
    nc = L // cl
    tri = jnp.tril(jnp.ones((cl, cl), dtype=bool))
    w_s = jnp.where(tri, p["gmlp_w_s"][:, :cl, :cl], 0.0).astype(x.dtype)
    gvb = gv.reshape(B, nc, cl, GMLP_GROUPS, GMLP_GROUP_DIM)
    sp = jnp.einsum('gts,bcsgd->bctgd', w_s, gvb) + p["gmlp_b_s"][:, :cl].T[None, None, :, :, None]
    br_gmlp = (u * sp.reshape(B, L, GMLP_WIDTH)) @ p["w_br_gmlp"]

    qm = mq_raw.reshape(B, L, MEM_HEADS, MEM_HEAD_DIM)
    sc = jnp.einsum('blhd,bmhd->bhlm', qm, mem_k).astype(f32) * (MEM_HEAD_DIM ** -0.5)
    a = jax.nn.softmax(sc, axis=-1).astype(x.dtype)
    att = jnp.einsum('bhlm,bmhd->blhd', a, mem_v).reshape(B, L, MEM_WIDTH)
    br_mem = att @ p["w_br_mem"]

    gates = jax.nn.sigmoid(gate_raw).reshape(B, L, N_BRANCH, D_MODEL)
    mixed = gates[:, :, 0] * br_mlstm + gates[:, :, 1] * br_gmlp + gates[:, :, 2] * br_mem
    x = x + mixed @ p["w_out"]

    x = x + hier_moe(rmsnorm(x, p["norm_moe_g"]), p["w_router_group"], p["b_router_group"],
                     p["w_router_expert"], p["b_router_expert"], p["w_exp_gate"], p["w_exp_up"], p["w_exp_down"])
    return x, C1, n1, m1, conv_new, gv


def setup_inputs(seed: int = 0) -> dict:
    key = jax.random.key(seed)
    ks = iter(jax.random.split(key, 40))

    def nrm(shape, s):
        return jax.random.normal(next(ks), shape, jnp.float32) * s

    Lr = DEPTH
    return {
        "x_prompt": nrm((BATCH, SEQ, D_MODEL), 1.0),
        "x_sample": nrm((DEC_BATCH, DEC_SEQ, D_MODEL), 1.0),
        "mem_prompt": nrm((BATCH, N_MEM, D_MODEL), 1.0),
        "cache_mem_k": nrm((Lr, DEC_BATCH, N_MEM, MEM_HEADS, MEM_HEAD_DIM), 1.0),
        "cache_mem_v": nrm((Lr, DEC_BATCH, N_MEM, MEM_HEADS, MEM_HEAD_DIM), 1.0),
        "state_mlstm_C": nrm((Lr, DEC_BATCH, MLSTM_HEADS, MLSTM_HEAD_DIM, MLSTM_HEAD_DIM), 0.3),
        "state_mlstm_n": nrm((Lr, DEC_BATCH, MLSTM_HEADS, MLSTM_HEAD_DIM), 0.3),
        "state_mlstm_m": 1.0 + nrm((Lr, DEC_BATCH, MLSTM_HEADS), 0.5),
        "state_mlstm_conv": nrm((Lr, DEC_BATCH, CONV_W - 1, 2 * MLSTM_WIDTH), 1.0),
        "norm_mix_g": 1.0 + nrm((Lr, D_MODEL), 0.02),
        "w_in": nrm((Lr, D_MODEL, IN_COLS), D_MODEL ** -0.5),
        "mlstm_i_b": nrm((Lr, MLSTM_HEADS), 0.1),
        "mlstm_f_b": 3.0 + nrm((Lr, MLSTM_HEADS), 0.5),
        "mlstm_conv_w": nrm((Lr, CONV_W, 2 * MLSTM_WIDTH), CONV_W ** -0.5),
        "mlstm_conv_b": nrm((Lr, 2 * MLSTM_WIDTH), 0.02),
        "mlstm_norm_g": 1.0 + nrm((Lr, MLSTM_WIDTH), 0.02),
        "gmlp_norm_g": 1.0 + nrm((Lr, GMLP_WIDTH), 0.02),
        "gmlp_norm_b": nrm((Lr, GMLP_WIDTH), 0.02),
        "gmlp_w_s": nrm((Lr, GMLP_GROUPS, GMLP_CHUNK, GMLP_CHUNK), GMLP_CHUNK ** -0.5),
        "gmlp_b_s": 1.0 + nrm((Lr, GMLP_GROUPS, GMLP_CHUNK), 0.1),
        "mem_norm_g": 1.0 + nrm((Lr, D_MODEL), 0.02),
        "w_mem_k": nrm((Lr, D_MODEL, MEM_WIDTH), D_MODEL ** -0.5),
        "w_mem_v": nrm((Lr, D_MODEL, MEM_WIDTH), D_MODEL ** -0.5),
        "w_br_mlstm": nrm((Lr, MLSTM_WIDTH, D_MODEL), MLSTM_WIDTH ** -0.5),
        "w_br_gmlp": nrm((Lr, GMLP_WIDTH, D_MODEL), GMLP_WIDTH ** -0.5),
        "w_br_mem": nrm((Lr, MEM_WIDTH, D_MODEL), MEM_WIDTH ** -0.5),
        "w_out": nrm((Lr, D_MODEL, D_MODEL), D_MODEL ** -0.5),
        "norm_moe_g": 1.0 + nrm((Lr, D_MODEL), 0.02),
        "w_router_group": nrm((Lr, D_MODEL, N_GROUPS), D_MODEL ** -0.5),
        "b_router_group": nrm((Lr, N_GROUPS), 0.01),
        "w_router_expert": nrm((Lr, D_MODEL, N_EXPERTS), D_MODEL ** -0.5),
        "b_router_expert": nrm((Lr, N_EXPERTS), 0.01),
        "w_exp_gate": nrm((Lr, N_EXPERTS, D_MODEL, D_EXPERT), D_MODEL ** -0.5),
        "w_exp_up": nrm((Lr, N_EXPERTS, D_MODEL, D_EXPERT), D_MODEL ** -0.5),
        "w_exp_down": nrm((Lr, N_EXPERTS, D_EXPERT, D_MODEL), D_EXPERT ** -0.5),
        "norm_final_g": 1.0 + nrm((D_MODEL,), 0.02),
    }


def reference(x_prompt, x_sample, mem_prompt, cache_mem_k, cache_mem_v, state_mlstm_C, state_mlstm_n,
              state_mlstm_m, state_mlstm_conv, norm_mix_g, w_in, mlstm_i_b, mlstm_f_b, mlstm_conv_w,
              mlstm_conv_b, mlstm_norm_g, gmlp_norm_g, gmlp_norm_b, gmlp_w_s, gmlp_b_s, mem_norm_g,
              w_mem_k, w_mem_v, w_br_mlstm, w_br_gmlp, w_br_mem, w_out, norm_moe_g, w_router_group,
              b_router_group, w_router_expert, b_router_expert, w_exp_gate, w_exp_up, w_exp_down,
              norm_final_g):
    f32 = jnp.float32
    bp = x_prompt.shape[0]
    hp, hs = x_prompt, x_sample
    mk_p_l, mv_p_l, Cp_l, np_l, mp_l, cvp_l = [], [], [], [], [], []
    Cs_l, ns_l, ms_l, cvs_l, gvs_l = [], [], [], [], []
    for l in range(DEPTH):
        p = {
            "norm_mix_g": norm_mix_g[l], "w_in": w_in[l], "mlstm_i_b": mlstm_i_b[l], "mlstm_f_b": mlstm_f_b[l],
            "mlstm_conv_w": mlstm_conv_w[l], "mlstm_conv_b": mlstm_conv_b[l], "mlstm_norm_g": mlstm_norm_g[l],
            "gmlp_norm_g": gmlp_norm_g[l], "gmlp_norm_b": gmlp_norm_b[l], "gmlp_w_s": gmlp_w_s[l],
            "gmlp_b_s": gmlp_b_s[l], "w_br_mlstm": w_br_mlstm[l], "w_br_gmlp": w_br_gmlp[l],
            "w_br_mem": w_br_mem[l], "w_out": w_out[l], "norm_moe_g": norm_moe_g[l],
            "w_router_group": w_router_group[l], "b_router_group": b_router_group[l],
            "w_router_expert": w_router_expert[l], "b_router_expert": b_router_expert[l],
            "w_exp_gate": w_exp_gate[l], "w_exp_up": w_exp_up[l], "w_exp_down": w_exp_down[l],
        }
        mk_p, mv_p = memory_kv(mem_prompt, mem_norm_g[l], w_mem_k[l], w_mem_v[l])
        C0 = jnp.zeros((bp, MLSTM_HEADS, MLSTM_HEAD_DIM, MLSTM_HEAD_DIM), f32)
        n0 = jnp.zeros((bp, MLSTM_HEADS, MLSTM_HEAD_DIM), f32)
        m0 = jnp.zeros((bp, MLSTM_HEADS), f32)
        cv0 = jnp.zeros((bp, CONV_W - 1, 2 * MLSTM_WIDTH), x_prompt.dtype)
        hp, Cp, np_, mp, cvp, _ = trunk_layer(hp, mk_p, mv_p, C0, n0, m0, cv0, p)
        hs, Cs, ns, ms, cvs, gvs = trunk_layer(
            hs, cache_mem_k[l], cache_mem_v[l], state_mlstm_C[l].astype(f32), state_mlstm_n[l].astype(f32),
            state_mlstm_m[l].astype(f32), state_mlstm_conv[l], p)
        mk_p_l.append(mk_p); mv_p_l.append(mv_p)
        Cp_l.append(Cp); np_l.append(np_); mp_l.append(mp); cvp_l.append(cvp)
        Cs_l.append(Cs); ns_l.append(ns); ms_l.append(ms); cvs_l.append(cvs); gvs_l.append(gvs)
    y_prompt = rmsnorm(hp, norm_final_g)
    y_sample = rmsnorm(hs, norm_final_g)
    return (y_prompt, y_sample,
            jnp.stack(mk_p_l), jnp.stack(mv_p_l), jnp.stack(Cp_l), jnp.stack(np_l), jnp.stack(mp_l), jnp.stack(cvp_l),
            jnp.stack(Cs_l), jnp.stack(ns_l), jnp.stack(ms_l), jnp.stack(cvs_l), jnp.stack(gvs_l))
```

```python
import functools

import jax
import jax.numpy as jnp
from jax import lax
from jax.experimental import pallas as pl
from jax.experimental.pallas import tpu as pltpu

D_MODEL = 1024
CHUNK = 64
EPS = 1e-6
HEADS = 4
HEAD_DIM = 256
CONV_W = 4
GMLP_GROUPS = 4
GMLP_GROUP_DIM = 256
GMLP_CHUNK = 128
N_MEM = 256
N_GROUPS = 4
EXPERTS_PER_GROUP = 8
N_EXPERTS = 32
D_EXPERT = 256
LANES = 128
CONV_PAD = 8

F32 = jnp.float32
BF16 = jnp.bfloat16
NEG_INF = float("-inf")

VMEM_LIMIT = 56 * 1024 * 1024


def _cparams(sem):
    return pltpu.CompilerParams(dimension_semantics=sem, vmem_limit_bytes=VMEM_LIMIT)


def _const_spec(shape):
    nd = len(shape)
    return pl.BlockSpec(shape, lambda *_: (0,) * nd, pipeline_mode=pl.Buffered(1))


def _sigmoid(x):
    return 0.5 * (jnp.tanh(0.5 * x) + 1.0)


def _log_sigmoid(x):
    return jnp.minimum(x, 0.0) - jnp.log(1.0 + jnp.exp(-jnp.abs(x)))


def _rmsnorm(x, g):
    r = lax.rsqrt(jnp.mean(x * x, axis=-1, keepdims=True) + EPS)
    return (x * r) * g


def _dot(a, b):
    return jnp.dot(a, b, preferred_element_type=F32)


def _dot_nt(a, b):
    return lax.dot_general(a, b, (((1,), (1,)), ((), ())), preferred_element_type=F32)


def _dot_tn(a, b):
    return lax.dot_general(a, b, (((0,), (0,)), ((), ())), preferred_element_type=F32)


def _memkv_kernel(mem_ref, g_ref, wk_ref, wv_ref, k_ref, v_ref):
    mn = _rmsnorm(mem_ref[...], g_ref[...]).astype(BF16)
    k_ref[...] = _dot(mn, wk_ref[...])
    v_ref[...] = _dot(mn, wv_ref[...])


def _memory_kv(mem2d, g, wk, wv):
    n = mem2d.shape[0]
    tm = 512
    row = pl.BlockSpec((tm, D_MODEL), lambda i: (i, 0))
    return pl.pallas_call(
        _memkv_kernel,
        grid=(n // tm,),
        in_specs=[row, _const_spec((1, D_MODEL)), _const_spec((D_MODEL, D_MODEL)),
                  _const_spec((D_MODEL, D_MODEL))],
        out_specs=[row, row],
        out_shape=[jax.ShapeDtypeStruct((n, D_MODEL), F32)] * 2,
        compiler_params=_cparams(("parallel",)),
        name="memory_kv",
    )(mem2d, g, wk, wv)


def _inproj_a_kernel(x_ref, g_ref, wqk_ref, wv_ref, wo_ref, wift_ref, bif_ref, cw_ref, cb_ref, cs_ref,
                     q_ref, k_ref, v_ref, o_ref, gt_ref, cn_ref, ext_ref, *, tm):
    i = pl.program_id(1)

    @pl.when(i == 0)
    def _():
        ext_ref[0:CONV_PAD, :] = jnp.zeros((CONV_PAD, 2 * D_MODEL), F32)
        ext_ref[CONV_PAD - (CONV_W - 1):CONV_PAD, :] = cs_ref[...]

    xn = _rmsnorm(x_ref[...], g_ref[...]).astype(BF16)

    ext_ref[CONV_PAD:CONV_PAD + tm, :] = _dot(xn, wqk_ref[...])
    acc = cb_ref[...] + ext_ref[CONV_PAD:CONV_PAD + tm, :] * cw_ref[CONV_W - 1:CONV_W, :]
    for j in range(CONV_W - 1):
        s = CONV_PAD - (CONV_W - 1) + j
        acc = acc + ext_ref[s:s + tm, :] * cw_ref[j:j + 1, :]
    qk = acc * _sigmoid(acc)
    q_ref[...] = (qk[:, :D_MODEL] * (HEAD_DIM ** -0.5)).astype(BF16)
    k_ref[...] = qk[:, D_MODEL:].astype(BF16)
    cn_ref[...] = ext_ref[tm + CONV_PAD - (CONV_W - 1):tm + CONV_PAD, :]
    ext_ref[0:CONV_PAD, :] = ext_ref[tm:tm + CONV_PAD, :]

    v_ref[...] = _dot(xn, wv_ref[...]).astype(BF16)
    o_ref[...] = _sigmoid(_dot(xn, wo_ref[...])).astype(BF16)

    row = lax.broadcasted_iota(jnp.int32, (2 * HEADS, CHUNK), 0)
    for c in range(tm // CHUNK):
        z = _dot_nt(wift_ref[...], xn[c * CHUNK:(c + 1) * CHUNK, :]) + bif_ref[...]
        gt_ref[c] = jnp.where(row < HEADS, z, _log_sigmoid(z))


def _inproj_a(x, g, wqk, wv, wo, wift, bif, cw, cb, cs, tm):
    b, l, _ = x.shape
    nt = l // tm
    tok = lambda w: pl.BlockSpec((None, tm, w), lambda bi, i: (bi, i, 0))
    state = pl.BlockSpec((None, CONV_W - 1, 2 * D_MODEL), lambda bi, i: (bi, 0, 0))
    return pl.pallas_call(
        functools.partial(_inproj_a_kernel, tm=tm),
        grid=(b, nt),
        in_specs=[tok(D_MODEL), _const_spec((1, D_MODEL)), _const_spec((D_MODEL, 2 * D_MODEL)),
                  _const_spec((D_MODEL, D_MODEL)), _const_spec((D_MODEL, D_MODEL)),
                  _const_spec((2 * HEADS, D_MODEL)), _const_spec((2 * HEADS, 1)),
                  _const_spec((CONV_W, 2 * D_MODEL)), _const_spec((1, 2 * D_MODEL)), state],
        out_specs=[tok(D_MODEL), tok(D_MODEL), tok(D_MODEL), tok(D_MODEL),
                   pl.BlockSpec((None, tm // CHUNK, 2 * HEADS, CHUNK), lambda bi, i: (bi, i, 0, 0)),
                   state],
        out_shape=[jax.ShapeDtypeStruct((b, l, D_MODEL), BF16)] * 4
        + [jax.ShapeDtypeStruct((b, l // CHUNK, 2 * HEADS, CHUNK), F32),
           jax.ShapeDtypeStruct((b, CONV_W - 1, 2 * D_MODEL), F32)],
        scratch_shapes=[pltpu.VMEM((tm + CONV_PAD, 2 * D_MODEL), F32)],
        compiler_params=_cparams(("parallel", "arbitrary")),
        name="inproj_a",
    )(x, g, wqk, wv, wo, wift, bif, cw, cb, cs)


def _inproj_b_kernel(x_ref, g_ref, wu_ref, wgv_ref, wmq_ref, wgate_ref, lng_ref, lnb_ref, ws_ref, bst_ref,
                     mk_ref, mv_ref, ug_ref, att_ref, gates_ref, *rest, tm, cl, emit_gv):
    xn = _rmsnorm(x_ref[...], g_ref[...]).astype(BF16)

    gates_ref[...] = _sigmoid(_dot(xn, wgate_ref[...])).astype(BF16)

    gvr = jax.nn.gelu(_dot(xn, wgv_ref[...]))
    mu = jnp.mean(gvr, axis=-1, keepdims=True)
    xc = gvr - mu
    r = lax.rsqrt(jnp.mean(xc * xc, axis=-1, keepdims=True) + EPS)
    gv = (xc * r) * lng_ref[...] + lnb_ref[...]
    if emit_gv:
        rest[0][...] = gv
    gvb = gv.astype(BF16)
    u = jax.nn.gelu(_dot(xn, wu_ref[...]))
    tri = (lax.broadcasted_iota(jnp.int32, (cl, cl), 0) >= lax.broadcasted_iota(jnp.int32, (cl, cl), 1))
    for gi in range(GMLP_GROUPS):
        wsg = jnp.where(tri, ws_ref[gi], 0.0).astype(BF16)
        lo, hi = gi * GMLP_GROUP_DIM, (gi + 1) * GMLP_GROUP_DIM
        for c in range(tm // cl):
            sp = _dot(wsg, gvb[c * cl:(c + 1) * cl, lo:hi]) + bst_ref[:, gi:gi + 1]
            ug_ref[c * cl:(c + 1) * cl, lo:hi] = (u[c * cl:(c + 1) * cl, lo:hi] * sp).astype(BF16)

    mq = _dot(xn, wmq_ref[...]).astype(BF16)
    for h in range(HEADS):
        lo, hi = h * HEAD_DIM, (h + 1) * HEAD_DIM
        sc = _dot_nt(mq[:, lo:hi], mk_ref[:, lo:hi]) * (HEAD_DIM ** -0.5)
        e = jnp.exp(sc - jnp.max(sc, axis=-1, keepdims=True))
        a = (e / jnp.sum(e, axis=-1, keepdims=True)).astype(BF16)
        att_ref[:, lo:hi] = _dot(a, mv_ref[:, lo:hi]).astype(BF16)


def _inproj_b(x, g, wu, wgv, wmq, wgate, lng, lnb, ws, bst, mk, mv, tm, cl, emit_gv):
    b, l, _ = x.shape
    nt = l // tm
    tok = lambda w: pl.BlockSpec((None, tm, w), lambda bi, i: (bi, i, 0))
    mem = pl.BlockSpec((None, N_MEM, D_MODEL), lambda bi, i: (bi, 0, 0))
    out_specs = [tok(D_MODEL), tok(D_MODEL), tok(3 * D_MODEL)]
    out_shape = [jax.ShapeDtypeStruct((b, l, D_MODEL), BF16), jax.ShapeDtypeStruct((b, l, D_MODEL), BF16),
                 jax.ShapeDtypeStruct((b, l, 3 * D_MODEL), BF16)]
    if emit_gv:
        out_specs.append(tok(D_MODEL))
        out_shape.append(jax.ShapeDtypeStruct((b, l, D_MODEL), F32))
    return pl.pallas_call(
        functools.partial(_inproj_b_kernel, tm=tm, cl=cl, emit_gv=emit_gv),
        grid=(b, nt),
        in_specs=[tok(D_MODEL), _const_spec((1, D_MODEL)), _const_spec((D_MODEL, D_MODEL)),
                  _const_spec((D_MODEL, D_MODEL)), _const_spec((D_MODEL, D_MODEL)),
                  _const_spec((D_MODEL, 3 * D_MODEL)), _const_spec((1, D_MODEL)), _const_spec((1, D_MODEL)),
                  _const_spec((GMLP_GROUPS, cl, cl)), _const_spec((cl, GMLP_GROUPS)), mem, mem],
        out_specs=out_specs,
        out_shape=out_shape,
        compiler_params=_cparams(("parallel", "parallel")),
        name="inproj_b",
    )(x, g, wu, wgv, wmq, wgate, lng, lnb, ws, bst, mk, mv)


def _mlstm_kernel(q_ref, k_ref, v_ref, o_ref, gt_ref, c0_ref, n0_ref, m0_ref, ng_ref,
                  hm_ref, c_ref, n_ref, m_ref, *, cb):
    i = pl.program_id(1)

    @pl.when(i == 0)
    def _():
        c_ref[...] = c0_ref[...]
        n_ref[...] = n0_ref[...]
        m_ref[...] = m0_ref[...]

    L = CHUNK
    ti = lax.broadcasted_iota(jnp.int32, (L, L), 0)
    si = lax.broadcasted_iota(jnp.int32, (L, L), 1)
    tri = ti >= si
    eye = ti == si
    lane = lax.broadcasted_iota(jnp.int32, (2 * HEADS, L), 1)

    for c in range(cb // L):
        g = gt_ref[c]
        bc = g
        for sh in (1, 2, 4, 8, 16, 32):
            bc = bc + jnp.where(lane >= sh, pltpu.roll(bc, sh, axis=1), 0.0)
        r0, r1 = c * L, (c + 1) * L
        for h in range(HEADS):
            lo, hi = h * HEAD_DIM, (h + 1) * HEAD_DIM
            ig_r = g[h:h + 1, :]
            bc_r = bc[HEADS + h:HEADS + h + 1, :]
            a_r = ig_r - bc_r
            bc_c = jnp.sum(jnp.where(eye, bc_r, 0.0), axis=-1, keepdims=True)
            a_c = jnp.sum(jnp.where(eye, a_r, 0.0), axis=-1, keepdims=True)
            m0 = m_ref[h:h + 1, 0:1]
            dmat = jnp.where(tri, bc_c + a_r, NEG_INF)
            inter = bc_c + m0
            m = jnp.maximum(inter, jnp.max(dmat, axis=-1, keepdims=True))
            w_intra = jnp.exp(dmat - m)
            w_inter = jnp.exp(inter - m)
            q = q_ref[r0:r1, lo:hi]
            k = k_ref[r0:r1, lo:hi]
            v = v_ref[r0:r1, lo:hi]
            c0 = c_ref[h]
            n0 = n_ref[h:h + 1, :]
            s = _dot_nt(q, k) * w_intra
            num = w_inter * _dot(q, c0.astype(BF16)) + _dot(s.astype(BF16), v)
            qn = jnp.sum(q.astype(F32) * n0, axis=-1, keepdims=True)
            den = w_inter * qn + jnp.sum(s, axis=-1, keepdims=True)
            hh = num / jnp.maximum(jnp.abs(den), jnp.exp(-m))
            m_last = m[L - 1:L, :]
            bc_last = bc_r[:, L - 1:L]
            w_last = jnp.exp(bc_last + a_c - m_last)
            decay = jnp.exp(bc_last + m0 - m_last)
            kw = k.astype(F32) * w_last
            c_ref[h] = decay * c0 + _dot_tn(kw.astype(BF16), v)
            n_ref[h:h + 1, :] = decay * n0 + jnp.sum(kw, axis=0, keepdims=True)
            m_ref[h:h + 1, :] = jnp.broadcast_to(m_last, (1, LANES))
            hn = hh * lax.rsqrt(jnp.mean(hh * hh, axis=-1, keepdims=True) + EPS)
            hm_ref[r0:r1, lo:hi] = ((hn * ng_ref[:, lo:hi]) * o_ref[r0:r1, lo:hi].astype(F32)).astype(BF16)


def _mlstm(q, k, v, o, gt, c0, n0, m0, ng, cb):
    b, l, _ = q.shape
    nt = l // cb
    tok = pl.BlockSpec((None, cb, D_MODEL), lambda bi, i: (bi, i, 0))
    cs = pl.BlockSpec((None, HEADS, HEAD_DIM, HEAD_DIM), lambda bi, i: (bi, 0, 0, 0))
    ns = pl.BlockSpec((None, HEADS, HEAD_DIM), lambda bi, i: (bi, 0, 0))
    ms = pl.BlockSpec((None, HEADS, LANES), lambda bi, i: (bi, 0, 0))
    return pl.pallas_call(
        functools.partial(_mlstm_kernel, cb=cb),
        grid=(b, nt),
        in_specs=[tok, tok, tok, tok,
                  pl.BlockSpec((None, cb // CHUNK, 2 * HEADS, CHUNK), lambda bi, i: (bi, i, 0, 0)),
                  cs, ns, ms, _const_spec((1, D_MODEL))],
        out_specs=[tok, cs, ns, ms],
        out_shape=[jax.ShapeDtypeStruct((b, l, D_MODEL), BF16),
                   jax.ShapeDtypeStruct((b, HEADS, HEAD_DIM, HEAD_DIM), F32),
                   jax.ShapeDtypeStruct((b, HEADS, HEAD_DIM), F32),
                   jax.ShapeDtypeStruct((b, HEADS, LANES), F32)],
        compiler_params=_cparams(("parallel", "arbitrary")),
        name="mlstm",
    )(q, k, v, o, gt, c0, n0, m0, ng)


def _merge_kernel(x_ref, hm_ref, ug_ref, att_ref, gates_ref, wa_ref, wb_ref, wc_ref, wo_ref, h_ref):
    g = gates_ref[...].astype(F32)
    mixed = g[:, :D_MODEL] * _dot(hm_ref[...], wa_ref[...])
    mixed = mixed + g[:, D_MODEL:2 * D_MODEL] * _dot(ug_ref[...], wb_ref[...])
    mixed = mixed + g[:, 2 * D_MODEL:] * _dot(att_ref[...], wc_ref[...])
    h_ref[...] = x_ref[...] + _dot(mixed.astype(BF16), wo_ref[...])


def _merge(x2d, hm, ug, att, gates, wa, wb, wc, wo, tm):
    n = x2d.shape[0]
    row = lambda w: pl.BlockSpec((tm, w), lambda i: (i, 0))
    wspec = _const_spec((D_MODEL, D_MODEL))
    return pl.pallas_call(
        _merge_kernel,
        grid=(n // tm,),
        in_specs=[row(D_MODEL), row(D_MODEL), row(D_MODEL), row(D_MODEL), row(3 * D_MODEL),
                  wspec, wspec, wspec, wspec],
        out_specs=row(D_MODEL),
        out_shape=jax.ShapeDtypeStruct((n, D_MODEL), F32),
        compiler_params=_cparams(("parallel",)),
        name="merge",
    )(x2d, hm, ug, att, gates, wa, wb, wc, wo)


def _route(xm, wrg_ref, brg_ref, wre_ref, bre_ref):
    tm = xm.shape[0]
    lane = lax.broadcasted_iota(jnp.int32, (tm, LANES), 1)
    lg = jnp.where(lane < N_GROUPS, _dot(xm, wrg_ref[...]) + brg_ref[...], NEG_INF)
    gmax = jnp.max(lg, axis=-1, keepdims=True)
    p_top = 1.0 / jnp.sum(jnp.exp(lg - gmax), axis=-1, keepdims=True)
    grp = jnp.min(jnp.where(lg == gmax, lane, LANES), axis=-1, keepdims=True)
    el = _dot(xm, wre_ref[...]) + bre_ref[...]
    in_grp = (lane >= grp * EXPERTS_PER_GROUP) & (lane < (grp + 1) * EXPERTS_PER_GROUP)
    vals = jnp.where(in_grp, el, NEG_INF)
    v1 = jnp.max(vals, axis=-1, keepdims=True)
    i1 = jnp.min(jnp.where(vals == v1, lane, LANES), axis=-1, keepdims=True)
    vals2 = jnp.where(lane == i1, NEG_INF, vals)
    v2 = jnp.max(vals2, axis=-1, keepdims=True)
    i2 = jnp.min(jnp.where(vals2 == v2, lane, LANES), axis=-1, keepdims=True)
    t = jnp.exp(v2 - v1)
    p1 = p_top / (1.0 + t)
    p2 = p_top * t / (1.0 + t)
    return jnp.where(lane == i1, p1, 0.0) + jnp.where(lane == i2, p2, 0.0)


def _moe_dense_kernel(h_ref, g_ref, wrg_ref, brg_ref, wre_ref, bre_ref, wgu_ref, wd_ref, gf_ref,
                      y_ref, xm_ref, comb_ref, acc_ref):
    e = pl.program_id(1)

    @pl.when(e == 0)
    def _():
        xm = _rmsnorm(h_ref[...], g_ref[...]).astype(BF16)
        xm_ref[...] = xm
        comb_ref[...] = _route(xm, wrg_ref, brg_ref, wre_ref, bre_ref)
        acc_ref[...] = jnp.zeros_like(acc_ref)

    xm = xm_ref[...]
    gu = _dot(xm, wgu_ref[...])
    gate = gu[:, :D_EXPERT]
    hid = (gate * _sigmoid(gate)) * gu[:, D_EXPERT:]
    lane = lax.broadcasted_iota(jnp.int32, comb_ref.shape, 1)
    ce = jnp.sum(jnp.where(lane == e, comb_ref[...], 0.0), axis=-1, keepdims=True)
    acc_ref[...] += ce * _dot(hid.astype(BF16), wd_ref[...])

    @pl.when(e == N_EXPERTS - 1)
    def _():
        y_ref[...] = _rmsnorm(h_ref[...] + acc_ref[...], gf_ref[...])


def _moe_dense(h, g, wrg, brg, wre, bre, wgu, wd, gf, tm):
    n = h.shape[0]
    row = pl.BlockSpec((tm, D_MODEL), lambda i, e: (i, 0))
    return pl.pallas_call(
        _moe_dense_kernel,
        grid=(n // tm, N_EXPERTS),
        in_specs=[row, _const_spec((1, D_MODEL)), _const_spec((D_MODEL, LANES)), _const_spec((1, LANES)),
                  _const_spec((D_MODEL, LANES)), _const_spec((1, LANES)),
                  pl.BlockSpec((None, D_MODEL, 2 * D_EXPERT), lambda i, e: (e, 0, 0)),
                  pl.BlockSpec((None, D_EXPERT, D_MODEL), lambda i, e: (e, 0, 0)),
                  _const_spec((1, D_MODEL))],
        out_specs=row,
        out_shape=jax.ShapeDtypeStruct((n, D_MODEL), F32),
        scratch_shapes=[pltpu.VMEM((tm, D_MODEL), BF16), pltpu.VMEM((tm, LANES), F32),
                        pltpu.VMEM((tm, D_MODEL), F32)],
        compiler_params=_cparams(("parallel", "arbitrary")),
        name="moe_dense",
    )(h, g, wrg, brg, wre, bre, wgu, wd, gf)


def _trunk(x, mem_k, mem_v, c0, n0, m0, conv0, p, *, tm, cb, cl, emit_gv):
    b, l, _ = x.shape
    q, k, v, osig, gt, conv_new = _inproj_a(
        x, p["norm_mix_g"], p["w_qk"], p["w_v"], p["w_o"], p["w_ift"], p["b_if"],
        p["conv_w"], p["conv_b"], conv0, tm)
    outs = _inproj_b(x, p["norm_mix_g"], p["w_u"], p["w_gv"], p["w_mq"], p["w_gate"],
                     p["gmlp_norm_g"], p["gmlp_norm_b"], p["w_s"][:, :cl, :cl], p["b_st"][:cl],
                     mem_k, mem_v, tm, cl, emit_gv)
    ug, att, gates = outs[:3]
    m0b = jnp.broadcast_to(m0[..., None], m0.shape + (LANES,))
    hm, c1, n1, m1 = _mlstm(q, k, v, osig, gt, c0, n0, m0b, p["mlstm_norm_g"], cb)
    n = b * l
    h = _merge(x.reshape(n, D_MODEL), hm.reshape(n, D_MODEL), ug.reshape(n, D_MODEL),
               att.reshape(n, D_MODEL), gates.reshape(n, 3 * D_MODEL),
               p["w_br_mlstm"], p["w_br_gmlp"], p["w_br_mem"], p["w_out"], 512)
    y = _moe_dense(h, p["norm_moe_g"], p["w_rg"], p["b_rg"], p["w_re"], p["b_re"], p["w_gu"], p["w_d"],
                   p["norm_final_g"], 512)
    return y.reshape(b, l, D_MODEL), c1, n1, m1[..., 0], conv_new, (outs[3] if emit_gv else None)


def kernel(x_prompt, x_sample, mem_prompt, cache_mem_k, cache_mem_v, state_mlstm_C, state_mlstm_n, state_mlstm_m, state_mlstm_conv, norm_mix_g, w_in, mlstm_i_b, mlstm_f_b, mlstm_conv_w, mlstm_conv_b, mlstm_norm_g, gmlp_norm_g, gmlp_norm_b, gmlp_w_s, gmlp_b_s, mem_norm_g, w_mem_k, w_mem_v, w_br_mlstm, w_br_gmlp, w_br_mem, w_out, norm_moe_g, w_router_group, b_router_group, w_router_expert, b_router_expert, w_exp_gate, w_exp_up, w_exp_down, norm_final_g):
    bp = x_prompt.shape[0]
    bs = x_sample.shape[0]
    W = D_MODEL
    wi = w_in[0]
    o_qk, o_v, o_o, o_i = 0, 2 * W, 3 * W, 4 * W
    o_f = o_i + HEADS
    o_u = o_f + HEADS
    o_gv, o_mq, o_gate = o_u + W, o_u + 2 * W, o_u + 3 * W
    row = lambda a: a.reshape(1, -1)
    pad_l = lambda a: jnp.pad(a, ((0, 0), (0, LANES - a.shape[1])))
    p = {
        "norm_mix_g": row(norm_mix_g[0]),
        "w_qk": wi[:, o_qk:o_v].astype(BF16), "w_v": wi[:, o_v:o_o].astype(BF16),
        "w_o": wi[:, o_o:o_i].astype(BF16),
        "w_ift": wi[:, o_i:o_u].T.astype(BF16),
        "b_if": jnp.concatenate([mlstm_i_b[0], mlstm_f_b[0]]).reshape(2 * HEADS, 1),
        "conv_w": mlstm_conv_w[0], "conv_b": row(mlstm_conv_b[0]),
        "w_u": wi[:, o_u:o_gv].astype(BF16), "w_gv": wi[:, o_gv:o_mq].astype(BF16),
        "w_mq": wi[:, o_mq:o_gate].astype(BF16), "w_gate": wi[:, o_gate:].astype(BF16),
        "gmlp_norm_g": row(gmlp_norm_g[0]), "gmlp_norm_b": row(gmlp_norm_b[0]),
        "w_s": gmlp_w_s[0], "b_st": gmlp_b_s[0].T,
        "mlstm_norm_g": row(mlstm_norm_g[0]),
        "w_br_mlstm": w_br_mlstm[0].astype(BF16), "w_br_gmlp": w_br_gmlp[0].astype(BF16),
        "w_br_mem": w_br_mem[0].astype(BF16), "w_out": w_out[0].astype(BF16),
        "norm_moe_g": row(norm_moe_g[0]),
        "w_rg": pad_l(w_router_group[0]).astype(BF16), "b_rg": pad_l(row(b_router_group[0])),
        "w_re": pad_l(w_router_expert[0]).astype(BF16), "b_re": pad_l(row(b_router_expert[0])),
        "w_gu": jnp.concatenate([w_exp_gate[0], w_exp_up[0]], axis=-1).astype(BF16),
        "w_d": w_exp_down[0].astype(BF16),
        "norm_final_g": row(norm_final_g),
    }

    mk_p, mv_p = _memory_kv(mem_prompt.reshape(bp * N_MEM, W), row(mem_norm_g[0]),
                            w_mem_k[0].astype(BF16), w_mem_v[0].astype(BF16))
    mk_p3, mv_p3 = mk_p.reshape(bp, N_MEM, W), mv_p.reshape(bp, N_MEM, W)

    zeros = lambda *s: jnp.zeros(s, F32)
    yp, cp, np_, mp, cvp, _ = _trunk(
        x_prompt, mk_p3.astype(BF16), mv_p3.astype(BF16),
        zeros(bp, HEADS, HEAD_DIM, HEAD_DIM), zeros(bp, HEADS, HEAD_DIM), zeros(bp, HEADS),
        zeros(bp, CONV_W - 1, 2 * W), p, tm=256, cb=256, cl=GMLP_CHUNK, emit_gv=False)
    ls = x_sample.shape[1]
    ys, cs, ns, ms, cvs, gvs = _trunk(
        x_sample, cache_mem_k[0].reshape(bs, N_MEM, W).astype(BF16),
        cache_mem_v[0].reshape(bs, N_MEM, W).astype(BF16),
        state_mlstm_C[0], state_mlstm_n[0], state_mlstm_m[0], state_mlstm_conv[0], p,
        tm=ls, cb=ls, cl=min(ls, GMLP_CHUNK), emit_gv=True)

    kv_shape = (1, bp, N_MEM, HEADS, HEAD_DIM)
    return (yp, ys, mk_p.reshape(kv_shape), mv_p.reshape(kv_shape),
            cp[None], np_[None], mp[None], cvp[None],
            cs[None], ns[None], ms[None], cvs[None], gvs[None])
```

```python
import functools

import jax
import jax.numpy as jnp
from jax import lax
from jax.experimental import pallas as pl
from jax.experimental.pallas import tpu as pltpu

D_MODEL = 1024
CHUNK = 64
EPS = 1e-6
HEADS = 4
HEAD_DIM = 256
CONV_W = 4
GMLP_GROUPS = 4
GMLP_GROUP_DIM = 256
GMLP_CHUNK = 128
N_MEM = 256
N_GROUPS = 4
EXPERTS_PER_GROUP = 8
N_EXPERTS = 32
D_EXPERT = 256
LANES = 128
CONV_PAD = 8

F32 = jnp.float32
BF16 = jnp.bfloat16
NEG_INF = float("-inf")

VMEM_LIMIT = 56 * 1024 * 1024


def _cparams(sem):
    return pltpu.CompilerParams(dimension_semantics=sem, vmem_limit_bytes=VMEM_LIMIT)


def _const_spec(shape):
    nd = len(shape)
    return pl.BlockSpec(shape, lambda *_: (0,) * nd, pipeline_mode=pl.Buffered(1))


def _sigmoid(x):
    return 0.5 * (jnp.tanh(0.5 * x) + 1.0)


def _log_sigmoid(x):
    return jnp.minimum(x, 0.0) - jnp.log(1.0 + jnp.exp(-jnp.abs(x)))


def _rmsnorm(x, g):
    r = lax.rsqrt(jnp.mean(x * x, axis=-1, keepdims=True) + EPS)
    return (x * r) * g


def _dot(a, b):
    return jnp.dot(a, b, preferred_element_type=F32)


def _dot_nt(a, b):
    return lax.dot_general(a, b, (((1,), (1,)), ((), ())), preferred_element_type=F32)


def _dot_tn(a, b):
    return lax.dot_general(a, b, (((0,), (0,)), ((), ())), preferred_element_type=F32)


def _memkv_kernel(mem_ref, g_ref, wk_ref, wv_ref, k_ref, v_ref):
    mn = _rmsnorm(mem_ref[...], g_ref[...]).astype(BF16)
    k_ref[...] = _dot(mn, wk_ref[...])
    v_ref[...] = _dot(mn, wv_ref[...])


def _memory_kv(mem2d, g, wk, wv):
    n = mem2d.shape[0]
    tm = 512
    row = pl.BlockSpec((tm, D_MODEL), lambda i: (i, 0))
    return pl.pallas_call(
        _memkv_kernel,
        grid=(n // tm,),
        in_specs=[row, _const_spec((1, D_MODEL)), _const_spec((D_MODEL, D_MODEL)),
                  _const_spec((D_MODEL, D_MODEL))],
        out_specs=[row, row],
        out_shape=[jax.ShapeDtypeStruct((n, D_MODEL), F32)] * 2,
        compiler_params=_cparams(("parallel",)),
        name="memory_kv",
    )(mem2d, g, wk, wv)


def _inproj_a_kernel(x_ref, g_ref, wqk_ref, wv_ref, wo_ref, wift_ref, bif_ref, cw_ref, cb_ref, cs_ref,
                     q_ref, k_ref, v_ref, o_ref, gt_ref, cn_ref, ext_ref, *, tm):
    i = pl.program_id(1)

    @pl.when(i == 0)
    def _():
        ext_ref[0:CONV_PAD, :] = jnp.zeros((CONV_PAD, 2 * D_MODEL), F32)
        ext_ref[CONV_PAD - (CONV_W - 1):CONV_PAD, :] = cs_ref[...]

    xn = _rmsnorm(x_ref[...], g_ref[...]).astype(BF16)

    ext_ref[CONV_PAD:CONV_PAD + tm, :] = _dot(xn, wqk_ref[...])
    acc = cb_ref[...] + ext_ref[CONV_PAD:CONV_PAD + tm, :] * cw_ref[CONV_W - 1:CONV_W, :]
    for j in range(CONV_W - 1):
        s = CONV_PAD - (CONV_W - 1) + j
        acc = acc + ext_ref[s:s + tm, :] * cw_ref[j:j + 1, :]
    qk = acc * _sigmoid(acc)
    q_ref[...] = (qk[:, :D_MODEL] * (HEAD_DIM ** -0.5)).astype(BF16)
    k_ref[...] = qk[:, D_MODEL:].astype(BF16)
    cn_ref[...] = ext_ref[tm + CONV_PAD - (CONV_W - 1):tm + CONV_PAD, :]
    ext_ref[0:CONV_PAD, :] = ext_ref[tm:tm + CONV_PAD, :]

    v_ref[...] = _dot(xn, wv_ref[...]).astype(BF16)
    o_ref[...] = _sigmoid(_dot(xn, wo_ref[...])).astype(BF16)

    row = lax.broadcasted_iota(jnp.int32, (2 * HEADS, CHUNK), 0)
    for c in range(tm // CHUNK):
        z = _dot_nt(wift_ref[...], xn[c * CHUNK:(c + 1) * CHUNK, :]) + bif_ref[...]
        gt_ref[c] = jnp.where(row < HEADS, z, _log_sigmoid(z))


def _inproj_a(x, g, wqk, wv, wo, wift, bif, cw, cb, cs, tm):
    b, l, _ = x.shape
    nt = l // tm
    tok = lambda w: pl.BlockSpec((None, tm, w), lambda bi, i: (bi, i, 0))
    state = pl.BlockSpec((None, CONV_W - 1, 2 * D_MODEL), lambda bi, i: (bi, 0, 0))
    return pl.pallas_call(
        functools.partial(_inproj_a_kernel, tm=tm),
        grid=(b, nt),
        in_specs=[tok(D_MODEL), _const_spec((1, D_MODEL)), _const_spec((D_MODEL, 2 * D_MODEL)),
                  _const_spec((D_MODEL, D_MODEL)), _const_spec((D_MODEL, D_MODEL)),
                  _const_spec((2 * HEADS, D_MODEL)), _const_spec((2 * HEADS, 1)),
                  _const_spec((CONV_W, 2 * D_MODEL)), _const_spec((1, 2 * D_MODEL)), state],
        out_specs=[tok(D_MODEL), tok(D_MODEL), tok(D_MODEL), tok(D_MODEL),
                   pl.BlockSpec((None, tm // CHUNK, 2 * HEADS, CHUNK), lambda bi, i: (bi, i, 0, 0)),
                   state],
        out_shape=[jax.ShapeDtypeStruct((b, l, D_MODEL), BF16)] * 4
        + [jax.ShapeDtypeStruct((b, l // CHUNK, 2 * HEADS, CHUNK), F32),
           jax.ShapeDtypeStruct((b, CONV_W - 1, 2 * D_MODEL), F32)],
        scratch_shapes=[pltpu.VMEM((tm + CONV_PAD, 2 * D_MODEL), F32)],
        compiler_params=_cparams(("parallel", "arbitrary")),
        name="inproj_a",
    )(x, g, wqk, wv, wo, wift, bif, cw, cb, cs)


def _inproj_b_kernel(x_ref, g_ref, wu_ref, wgv_ref, wmq_ref, wgate_ref, lng_ref, lnb_ref, ws_ref, bst_ref,
                     mk_ref, mv_ref, ug_ref, att_ref, gates_ref, *rest, tm, cl, emit_gv):
    xn = _rmsnorm(x_ref[...], g_ref[...]).astype(BF16)

    gates_ref[...] = _sigmoid(_dot(xn, wgate_ref[...])).astype(BF16)

    gvr = jax.nn.gelu(_dot(xn, wgv_ref[...]))
    mu = jnp.mean(gvr, axis=-1, keepdims=True)
    xc = gvr - mu
    r = lax.rsqrt(jnp.mean(xc * xc, axis=-1, keepdims=True) + EPS)
    gv = (xc * r) * lng_ref[...] + lnb_ref[...]
    if emit_gv:
        rest[0][...] = gv
    gvb = gv.astype(BF16)
    u = jax.nn.gelu(_dot(xn, wu_ref[...]))
    tri = (lax.broadcasted_iota(jnp.int32, (cl, cl), 0) >= lax.broadcasted_iota(jnp.int32, (cl, cl), 1))
    for gi in range(GMLP_GROUPS):
        wsg = jnp.where(tri, ws_ref[gi], 0.0).astype(BF16)
        lo, hi = gi * GMLP_GROUP_DIM, (gi + 1) * GMLP_GROUP_DIM
        for c in range(tm // cl):
            sp = _dot(wsg, gvb[c * cl:(c + 1) * cl, lo:hi]) + bst_ref[:, gi:gi + 1]
            ug_ref[c * cl:(c + 1) * cl, lo:hi] = (u[c * cl:(c + 1) * cl, lo:hi] * sp).astype(BF16)

    mq = _dot(xn, wmq_ref[...]).astype(BF16)
    for h in range(HEADS):
        lo, hi = h * HEAD_DIM, (h + 1) * HEAD_DIM
        sc = _dot_nt(mq[:, lo:hi], mk_ref[:, lo:hi]) * (HEAD_DIM ** -0.5)
        e = jnp.exp(sc - jnp.max(sc, axis=-1, keepdims=True))
        a = (e / jnp.sum(e, axis=-1, keepdims=True)).astype(BF16)
        att_ref[:, lo:hi] = _dot(a, mv_ref[:, lo:hi]).astype(BF16)


def _inproj_b(x, g, wu, wgv, wmq, wgate, lng, lnb, ws, bst, mk, mv, tm, cl, emit_gv):
    b, l, _ = x.shape
    nt = l // tm
    tok = lambda w: pl.BlockSpec((None, tm, w), lambda bi, i: (bi, i, 0))
    mem = pl.BlockSpec((None, N_MEM, D_MODEL), lambda bi, i: (bi, 0, 0))
    out_specs = [tok(D_MODEL), tok(D_MODEL), tok(3 * D_MODEL)]
    out_shape = [jax.ShapeDtypeStruct((b, l, D_MODEL), BF16), jax.ShapeDtypeStruct((b, l, D_MODEL), BF16),
                 jax.ShapeDtypeStruct((b, l, 3 * D_MODEL), BF16)]
    if emit_gv:
        out_specs.append(tok(D_MODEL))
        out_shape.append(jax.ShapeDtypeStruct((b, l, D_MODEL), F32))
    return pl.pallas_call(
        functools.partial(_inproj_b_kernel, tm=tm, cl=cl, emit_gv=emit_gv),
        grid=(b, nt),
        in_specs=[tok(D_MODEL), _const_spec((1, D_MODEL)), _const_spec((D_MODEL, D_MODEL)),
                  _const_spec((D_MODEL, D_MODEL)), _const_spec((D_MODEL, D_MODEL)),
                  _const_spec((D_MODEL, 3 * D_MODEL)), _const_spec((1, D_MODEL)), _const_spec((1, D_MODEL)),
                  _const_spec((GMLP_GROUPS, cl, cl)), _const_spec((cl, GMLP_GROUPS)), mem, mem],
        out_specs=out_specs,
        out_shape=out_shape,
        compiler_params=_cparams(("parallel", "parallel")),
        name="inproj_b",
    )(x, g, wu, wgv, wmq, wgate, lng, lnb, ws, bst, mk, mv)


def _mlstm_kernel(q_ref, k_ref, v_ref, o_ref, gt_ref, c0_ref, n0_ref, m0_ref, ng_ref,
                  hm_ref, c_ref, n_ref, m_ref, *, cb):
    i = pl.program_id(1)

    @pl.when(i == 0)
    def _():
        c_ref[...] = c0_ref[...]
        n_ref[...] = n0_ref[...]
        m_ref[...] = m0_ref[...]

    L = CHUNK
    ti = lax.broadcasted_iota(jnp.int32, (L, L), 0)
    si = lax.broadcasted_iota(jnp.int32, (L, L), 1)
    tri = ti >= si
    eye = ti == si
    lane = lax.broadcasted_iota(jnp.int32, (2 * HEADS, L), 1)

    for c in range(cb // L):
        g = gt_ref[c]
        bc = g
        for sh in (1, 2, 4, 8, 16, 32):
            bc = bc + jnp.where(lane >= sh, pltpu.roll(bc, sh, axis=1), 0.0)
        r0, r1 = c * L, (c + 1) * L
        for h in range(HEADS):
            lo, hi = h * HEAD_DIM, (h + 1) * HEAD_DIM
            ig_r = g[h:h + 1, :]
            bc_r = bc[HEADS + h:HEADS + h + 1, :]
            a_r = ig_r - bc_r
            bc_c = jnp.sum(jnp.where(eye, bc_r, 0.0), axis=-1, keepdims=True)
            a_c = jnp.sum(jnp.where(eye, a_r, 0.0), axis=-1, keepdims=True)
            m0 = m_ref[h:h + 1, 0:1]
            dmat = jnp.where(tri, bc_c + a_r, NEG_INF)
            inter = bc_c + m0
            m = jnp.maximum(inter, jnp.max(dmat, axis=-1, keepdims=True))
            w_intra = jnp.exp(dmat - m)
            w_inter = jnp.exp(inter - m)
            q = q_ref[r0:r1, lo:hi]
            k = k_ref[r0:r1, lo:hi]
            v = v_ref[r0:r1, lo:hi]
            c0 = c_ref[h]
            n0 = n_ref[h:h + 1, :]
            s = _dot_nt(q, k) * w_intra
            num = w_inter * _dot(q, c0.astype(BF16)) + _dot(s.astype(BF16), v)
            qn = jnp.sum(q.astype(F32) * n0, axis=-1, keepdims=True)
            den = w_inter * qn + jnp.sum(s, axis=-1, keepdims=True)
            hh = num / jnp.maximum(jnp.abs(den), jnp.exp(-m))
            m_last = m[L - 1:L, :]
            bc_last = bc_r[:, L - 1:L]
            w_last = jnp.exp(bc_last + a_c - m_last)
            decay = jnp.exp(bc_last + m0 - m_last)
            kw = k.astype(F32) * w_last
            c_ref[h] = decay * c0 + _dot_tn(kw.astype(BF16), v)
            n_ref[h:h + 1, :] = decay * n0 + jnp.sum(kw, axis=0, keepdims=True)
            m_ref[h:h + 1, :] = jnp.broadcast_to(m_last, (1, LANES))
            hn = hh * lax.rsqrt(jnp.mean(hh * hh, axis=-1, keepdims=True) + EPS)
            hm_ref[r0:r1, lo:hi] = ((hn * ng_ref[:, lo:hi]) * o_ref[r0:r1, lo:hi].astype(F32)).astype(BF16)


def _mlstm(q, k, v, o, gt, c0, n0, m0, ng, cb):
    b, l, _ = q.shape
    nt = l // cb
    tok = pl.BlockSpec((None, cb, D_MODEL), lambda bi, i: (bi, i, 0))
    cs = pl.BlockSpec((None, HEADS, HEAD_DIM, HEAD_DIM), lambda bi, i: (bi, 0, 0, 0))
    ns = pl.BlockSpec((None, HEADS, HEAD_DIM), lambda bi, i: (bi, 0, 0))
    ms = pl.BlockSpec((None, HEADS, LANES), lambda bi, i: (bi, 0, 0))
    return pl.pallas_call(
        functools.partial(_mlstm_kernel, cb=cb),
        grid=(b, nt),
        in_specs=[tok, tok, tok, tok,
                  pl.BlockSpec((None, cb // CHUNK, 2 * HEADS, CHUNK), lambda bi, i: (bi, i, 0, 0)),
                  cs, ns, ms, _const_spec((1, D_MODEL))],
        out_specs=[tok, cs, ns, ms],
        out_shape=[jax.ShapeDtypeStruct((b, l, D_MODEL), BF16),
                   jax.ShapeDtypeStruct((b, HEADS, HEAD_DIM, HEAD_DIM), F32),
                   jax.ShapeDtypeStruct((b, HEADS, HEAD_DIM), F32),
                   jax.ShapeDtypeStruct((b, HEADS, LANES), F32)],
        compiler_params=_cparams(("parallel", "arbitrary")),
        name="mlstm",
    )(q, k, v, o, gt, c0, n0, m0, ng)


def _merge_kernel(x_ref, hm_ref, ug_ref, att_ref, gates_ref, wa_ref, wb_ref, wc_ref, wo_ref, h_ref):
    g = gates_ref[...].astype(F32)
    mixed = g[:, :D_MODEL] * _dot(hm_ref[...], wa_ref[...])
    mixed = mixed + g[:, D_MODEL:2 * D_MODEL] * _dot(ug_ref[...], wb_ref[...])
    mixed = mixed + g[:, 2 * D_MODEL:] * _dot(att_ref[...], wc_ref[...])
    h_ref[...] = x_ref[...] + _dot(mixed.astype(BF16), wo_ref[...])


def _merge(x2d, hm, ug, att, gates, wa, wb, wc, wo, tm):
    n = x2d.shape[0]
    row = lambda w: pl.BlockSpec((tm, w), lambda i: (i, 0))
    wspec = _const_spec((D_MODEL, D_MODEL))
    return pl.pallas_call(
        _merge_kernel,
        grid=(n // tm,),
        in_specs=[row(D_MODEL), row(D_MODEL), row(D_MODEL), row(D_MODEL), row(3 * D_MODEL),
                  wspec, wspec, wspec, wspec],
        out_specs=row(D_MODEL),
        out_shape=jax.ShapeDtypeStruct((n, D_MODEL), F32),
        compiler_params=_cparams(("parallel",)),
        name="merge",
    )(x2d, hm, ug, att, gates, wa, wb, wc, wo)


MOE_T = 256
SEG_ALIGN = 16
MOE_S = 1024
EXPERT_BLOCK = 128


def _moe_route_kernel(h_ref, g_ref, wrg_ref, brg_ref, wre_ref, bre_ref, slots_ref, info_ref, meta_ref):
    t = MOE_T
    xm = _rmsnorm(h_ref[...], g_ref[...]).astype(BF16)
    lane = lax.broadcasted_iota(jnp.int32, (t, LANES), 1)
    lg = jnp.where(lane < N_GROUPS, _dot(xm, wrg_ref[...]) + brg_ref[...], NEG_INF)
    gmax = jnp.max(lg, axis=-1, keepdims=True)
    p_top = 1.0 / jnp.sum(jnp.exp(lg - gmax), axis=-1, keepdims=True)
    grp = jnp.min(jnp.where(lg == gmax, lane, LANES), axis=-1, keepdims=True)
    el = _dot(xm, wre_ref[...]) + bre_ref[...]
    in_grp = (lane >= grp * EXPERTS_PER_GROUP) & (lane < (grp + 1) * EXPERTS_PER_GROUP)
    vals = jnp.where(in_grp, el, NEG_INF)
    v1 = jnp.max(vals, axis=-1, keepdims=True)
    i1 = jnp.min(jnp.where(vals == v1, lane, LANES), axis=-1, keepdims=True)
    vals2 = jnp.where(lane == i1, NEG_INF, vals)
    v2 = jnp.max(vals2, axis=-1, keepdims=True)
    i2 = jnp.min(jnp.where(vals2 == v2, lane, LANES), axis=-1, keepdims=True)
    r = jnp.exp(v2 - v1)
    p1 = p_top / (1.0 + r)
    p2 = p_top * r / (1.0 + r)

    sel1 = lane == i1
    sel2 = lane == i2
    onehot = jnp.where(sel1 | sel2, 1.0, 0.0)
    cnt = jnp.sum(onehot, axis=0, keepdims=True).astype(jnp.int32)
    pn = jnp.bitwise_and(cnt + (SEG_ALIGN - 1), -SEG_ALIGN)
    pn8 = jnp.broadcast_to(pn, (8, LANES))
    lane8 = lax.broadcasted_iota(jnp.int32, (8, LANES), 1)
    inc = pn8
    for sh in (1, 2, 4, 8, 16):
        inc = inc + jnp.where(lane8 >= sh, pltpu.roll(inc, sh, axis=1), 0)
    off8 = inc - pn8
    row8 = lax.broadcasted_iota(jnp.int32, (8, LANES), 0)
    meta_ref[...] = jnp.where(row8 == 0, off8, jnp.where(row8 == 1, pn8, 0))

    ti = lax.broadcasted_iota(jnp.int32, (t, t), 0)
    si = lax.broadcasted_iota(jnp.int32, (t, t), 1)
    before = jnp.where(ti > si, 1.0, 0.0).astype(BF16)
    rank = _dot(before, onehot.astype(BF16))
    posmat = off8[0:1, :].astype(F32) + rank
    pos1 = jnp.sum(jnp.where(sel1, posmat, 0.0), axis=-1, keepdims=True)
    pos2 = jnp.sum(jnp.where(sel2, posmat, 0.0), axis=-1, keepdims=True)
    info_ref[...] = (jnp.where(lane == 0, pos1, 0.0) + jnp.where(lane == 1, pos2, 0.0)
                     + jnp.where(lane == 2, p1, 0.0) + jnp.where(lane == 3, p2, 0.0))

    eye = ti == si
    pos1_r = jnp.sum(jnp.where(eye, pos1, 0.0), axis=0, keepdims=True)
    pos2_r = jnp.sum(jnp.where(eye, pos2, 0.0), axis=0, keepdims=True)
    srow = lax.broadcasted_iota(jnp.int32, (MOE_S, t), 0).astype(F32)
    pick = jnp.where((srow == pos1_r) | (srow == pos2_r), 1.0, 0.0).astype(BF16)
    slots_ref[...] = _dot(pick, xm).astype(BF16)


def _moe_route(h, g, wrg, brg, wre, bre):
    n = h.shape[0]
    ns = n // MOE_T
    return pl.pallas_call(
        _moe_route_kernel,
        grid=(ns,),
        in_specs=[pl.BlockSpec((MOE_T, D_MODEL), lambda j: (j, 0)), _const_spec((1, D_MODEL)),
                  _const_spec((D_MODEL, LANES)), _const_spec((1, LANES)),
                  _const_spec((D_MODEL, LANES)), _const_spec((1, LANES))],
        out_specs=[pl.BlockSpec((None, MOE_S, D_MODEL), lambda j: (j, 0, 0)),
                   pl.BlockSpec((MOE_T, LANES), lambda j: (j, 0)),
                   pl.BlockSpec((None, 8, LANES), lambda j: (j, 0, 0))],
        out_shape=[jax.ShapeDtypeStruct((ns, MOE_S, D_MODEL), BF16),
                   jax.ShapeDtypeStruct((n, LANES), F32),
                   jax.ShapeDtypeStruct((ns, 8, LANES), jnp.int32)],
        compiler_params=_cparams(("parallel",)),
        name="moe_route",
    )(h, g, wrg, brg, wre, bre)


def _moe_expert_kernel(offs_ref, pns_ref, slots_hbm, wg_ref, wu_ref, wd_ref, out_hbm,
                       xbuf, obuf, wgu, wdb, sem_in, sem_out, *, n_sub, jb):
    e = pl.program_id(0)

    @pl.when(e == 0)
    def _():
        xbuf[...] = jnp.zeros_like(xbuf)

    wgu[:, :D_EXPERT] = wg_ref[...].astype(BF16)
    wgu[:, D_EXPERT:] = wu_ref[...].astype(BF16)
    wdb[...] = wd_ref[...].astype(BF16)

    def copy_in(j, src, dst):
        return pltpu.make_async_copy(slots_hbm.at[j, pl.ds(src, SEG_ALIGN), :],
                                     xbuf.at[pl.ds(dst, SEG_ALIGN), :], sem_in)

    def copy_out(j, src, dst):
        return pltpu.make_async_copy(obuf.at[pl.ds(src, SEG_ALIGN), :],
                                     out_hbm.at[j, pl.ds(dst, SEG_ALIGN), :], sem_out)

    def for_each_chunk(g, fn):
        def seg(jj, cur):
            j = g * jb + jj
            n = pns_ref[j * N_EXPERTS + e]
            off = offs_ref[j * N_EXPERTS + e]

            def chunk(k, c):
                fn(j, pl.multiple_of(off + k * SEG_ALIGN, SEG_ALIGN), pl.multiple_of(cur + k * SEG_ALIGN, SEG_ALIGN))
                return c
            lax.fori_loop(0, n // SEG_ALIGN, chunk, 0)
            return cur + n
        return lax.fori_loop(0, jb, seg, 0)

    def group(g, carry):
        total = for_each_chunk(g, lambda j, r, b: copy_in(j, r, b).start())
        nchunk = total // SEG_ALIGN

        def wait_in(k, c):
            copy_in(0, 0, 0).wait()
            return c
        lax.fori_loop(0, nchunk, wait_in, 0)

        def block(bi, c):
            r0 = pl.multiple_of(bi * EXPERT_BLOCK, EXPERT_BLOCK)
            gu = _dot(xbuf[pl.ds(r0, EXPERT_BLOCK), :], wgu[...])
            gate = gu[:, :D_EXPERT]
            hid = (gate * _sigmoid(gate)) * gu[:, D_EXPERT:]
            obuf[pl.ds(r0, EXPERT_BLOCK), :] = _dot(hid.astype(BF16), wdb[...]).astype(BF16)
            return c
        lax.fori_loop(0, (total + EXPERT_BLOCK - 1) // EXPERT_BLOCK, block, 0)

        for_each_chunk(g, lambda j, r, b: copy_out(j, b, r).start())

        def wait_out(k, c):
            copy_out(0, 0, 0).wait()
            return c
        lax.fori_loop(0, nchunk, wait_out, 0)
        return carry

    lax.fori_loop(0, n_sub // jb, group, 0)


def _moe_expert(slots, offs, pns, wg, wu, wd, jb):
    ns = slots.shape[0]
    rows = jb * MOE_T + EXPERT_BLOCK
    grid_spec = pltpu.PrefetchScalarGridSpec(
        num_scalar_prefetch=2,
        grid=(N_EXPERTS,),
        in_specs=[pl.BlockSpec(memory_space=pl.ANY),
                  pl.BlockSpec((None, D_MODEL, D_EXPERT), lambda e, *_: (e, 0, 0)),
                  pl.BlockSpec((None, D_MODEL, D_EXPERT), lambda e, *_: (e, 0, 0)),
                  pl.BlockSpec((None, D_EXPERT, D_MODEL), lambda e, *_: (e, 0, 0))],
        out_specs=pl.BlockSpec(memory_space=pl.ANY),
        scratch_shapes=[pltpu.VMEM((rows, D_MODEL), BF16), pltpu.VMEM((rows, D_MODEL), BF16),
                        pltpu.VMEM((D_MODEL, 2 * D_EXPERT), BF16), pltpu.VMEM((D_EXPERT, D_MODEL), BF16),
                        pltpu.SemaphoreType.DMA(()), pltpu.SemaphoreType.DMA(())],
    )
    return pl.pallas_call(
        functools.partial(_moe_expert_kernel, n_sub=ns, jb=jb),
        grid_spec=grid_spec,
        out_shape=jax.ShapeDtypeStruct(slots.shape, slots.dtype),
        input_output_aliases={2: 0},
        compiler_params=_cparams(("arbitrary",)),
        name="moe_expert",
    )(offs, pns, slots, wg, wu, wd)


def _moe_combine_kernel(h_ref, eo_ref, info_ref, gf_ref, y_ref):
    info = info_ref[...]
    scol = lax.broadcasted_iota(jnp.int32, (MOE_T, MOE_S), 1).astype(F32)
    w = (jnp.where(scol == info[:, 0:1], info[:, 2:3], 0.0)
         + jnp.where(scol == info[:, 1:2], info[:, 3:4], 0.0)).astype(BF16)
    y_ref[...] = _rmsnorm(h_ref[...] + _dot(w, eo_ref[...]), gf_ref[...])


def _moe_combine(h, eo, info, gf):
    n = h.shape[0]
    row = pl.BlockSpec((MOE_T, D_MODEL), lambda j: (j, 0))
    return pl.pallas_call(
        _moe_combine_kernel,
        grid=(n // MOE_T,),
        in_specs=[row, pl.BlockSpec((None, MOE_S, D_MODEL), lambda j: (j, 0, 0)),
                  pl.BlockSpec((MOE_T, LANES), lambda j: (j, 0)), _const_spec((1, D_MODEL))],
        out_specs=row,
        out_shape=jax.ShapeDtypeStruct((n, D_MODEL), F32),
        compiler_params=_cparams(("parallel",)),
        name="moe_combine",
    )(h, eo, info, gf)


def _moe(h, p, jb):
    slots, info, meta = _moe_route(h, p["norm_moe_g"], p["w_rg"], p["b_rg"], p["w_re"], p["b_re"])
    offs = meta[:, 0, :N_EXPERTS].reshape(-1)
    pns = meta[:, 1, :N_EXPERTS].reshape(-1)
    eo = _moe_expert(slots, offs, pns, p["w_eg"], p["w_eu"], p["w_ed"], jb)
    return _moe_combine(h, eo, info, p["norm_final_g"])


def _trunk(x, mem_k, mem_v, c0, n0, m0, conv0, p, *, tm, cb, cl, emit_gv):
    b, l, _ = x.shape
    q, k, v, osig, gt, conv_new = _inproj_a(
        x, p["norm_mix_g"], p["w_qk"], p["w_v"], p["w_o"], p["w_ift"], p["b_if"],
        p["conv_w"], p["conv_b"], conv0, tm)
    outs = _inproj_b(x, p["norm_mix_g"], p["w_u"], p["w_gv"], p["w_mq"], p["w_gate"],
                     p["gmlp_norm_g"], p["gmlp_norm_b"], p["w_s"][:, :cl, :cl], p["b_st"][:cl],
                     mem_k, mem_v, tm, cl, emit_gv)
    ug, att, gates = outs[:3]
    m0b = jnp.broadcast_to(m0[..., None], m0.shape + (LANES,))
    hm, c1, n1, m1 = _mlstm(q, k, v, osig, gt, c0, n0, m0b, p["mlstm_norm_g"], cb)
    n = b * l
    h = _merge(x.reshape(n, D_MODEL), hm.reshape(n, D_MODEL), ug.reshape(n, D_MODEL),
               att.reshape(n, D_MODEL), gates.reshape(n, 3 * D_MODEL),
               p["w_br_mlstm"], p["w_br_gmlp"], p["w_br_mem"], p["w_out"], 512)
    y = _moe(h, p, min(8, n // MOE_T))
    return y.reshape(b, l, D_MODEL), c1, n1, m1[..., 0], conv_new, (outs[3] if emit_gv else None)


def kernel(x_prompt, x_sample, mem_prompt, cache_mem_k, cache_mem_v, state_mlstm_C, state_mlstm_n, state_mlstm_m, state_mlstm_conv, norm_mix_g, w_in, mlstm_i_b, mlstm_f_b, mlstm_conv_w, mlstm_conv_b, mlstm_norm_g, gmlp_norm_g, gmlp_norm_b, gmlp_w_s, gmlp_b_s, mem_norm_g, w_mem_k, w_mem_v, w_br_mlstm, w_br_gmlp, w_br_mem, w_out, norm_moe_g, w_router_group, b_router_group, w_router_expert, b_router_expert, w_exp_gate, w_exp_up, w_exp_down, norm_final_g):
    bp = x_prompt.shape[0]
    bs = x_sample.shape[0]
    W = D_MODEL
    wi = w_in[0]
    o_qk, o_v, o_o, o_i = 0, 2 * W, 3 * W, 4 * W
    o_f = o_i + HEADS
    o_u = o_f + HEADS
    o_gv, o_mq, o_gate = o_u + W, o_u + 2 * W, o_u + 3 * W
    row = lambda a: a.reshape(1, -1)
    pad_l = lambda a: jnp.pad(a, ((0, 0), (0, LANES - a.shape[1])))
    p = {
        "norm_mix_g": row(norm_mix_g[0]),
        "w_qk": wi[:, o_qk:o_v].astype(BF16), "w_v": wi[:, o_v:o_o].astype(BF16),
        "w_o": wi[:, o_o:o_i].astype(BF16),
        "w_ift": wi[:, o_i:o_u].T.astype(BF16),
        "b_if": jnp.concatenate([mlstm_i_b[0], mlstm_f_b[0]]).reshape(2 * HEADS, 1),
        "conv_w": mlstm_conv_w[0], "conv_b": row(mlstm_conv_b[0]),
        "w_u": wi[:, o_u:o_gv].astype(BF16), "w_gv": wi[:, o_gv:o_mq].astype(BF16),
        "w_mq": wi[:, o_mq:o_gate].astype(BF16), "w_gate": wi[:, o_gate:].astype(BF16),
        "gmlp_norm_g": row(gmlp_norm_g[0]), "gmlp_norm_b": row(gmlp_norm_b[0]),
        "w_s": gmlp_w_s[0], "b_st": gmlp_b_s[0].T,
        "mlstm_norm_g": row(mlstm_norm_g[0]),
        "w_br_mlstm": w_br_mlstm[0].astype(BF16), "w_br_gmlp": w_br_gmlp[0].astype(BF16),
        "w_br_mem": w_br_mem[0].astype(BF16), "w_out": w_out[0].astype(BF16),
        "norm_moe_g": row(norm_moe_g[0]),
        "w_rg": pad_l(w_router_group[0]).astype(BF16), "b_rg": pad_l(row(b_router_group[0])),
        "w_re": pad_l(w_router_expert[0]).astype(BF16), "b_re": pad_l(row(b_router_expert[0])),
        "w_eg": w_exp_gate[0], "w_eu": w_exp_up[0], "w_ed": w_exp_down[0],
        "norm_final_g": row(norm_final_g),
    }

    mk_p, mv_p = _memory_kv(mem_prompt.reshape(bp * N_MEM, W), row(mem_norm_g[0]),
                            w_mem_k[0].astype(BF16), w_mem_v[0].astype(BF16))
    mk_p3, mv_p3 = mk_p.reshape(bp, N_MEM, W), mv_p.reshape(bp, N_MEM, W)

    zeros = lambda *s: jnp.zeros(s, F32)
    yp, cp, np_, mp, cvp, _ = _trunk(
        x_prompt, mk_p3.astype(BF16), mv_p3.astype(BF16),
        zeros(bp, HEADS, HEAD_DIM, HEAD_DIM), zeros(bp, HEADS, HEAD_DIM), zeros(bp, HEADS),
        zeros(bp, CONV_W - 1, 2 * W), p, tm=256, cb=256, cl=GMLP_CHUNK, emit_gv=False)
    ls = x_sample.shape[1]
    ys, cs, ns, ms, cvs, gvs = _trunk(
        x_sample, cache_mem_k[0].reshape(bs, N_MEM, W).astype(BF16),
        cache_mem_v[0].reshape(bs, N_MEM, W).astype(BF16),
        state_mlstm_C[0], state_mlstm_n[0], state_mlstm_m[0], state_mlstm_conv[0], p,
        tm=ls, cb=ls, cl=min(ls, GMLP_CHUNK), emit_gv=True)

    kv_shape = (1, bp, N_MEM, HEADS, HEAD_DIM)
    return (yp, ys, mk_p.reshape(kv_shape), mv_p.reshape(kv_shape),
            cp[None], np_[None], mp[None], cvp[None],
            cs[None], ns[None], ms[None], cvs[None], gvs[None])
```

```python
import functools

import jax
import jax.numpy as jnp
from jax import lax
from jax.experimental import pallas as pl
from jax.experimental.pallas import tpu as pltpu

D_MODEL = 1024
CHUNK = 64
EPS = 1e-6
HEADS = 4
HEAD_DIM = 256
CONV_W = 4
GMLP_GROUPS = 4
GMLP_GROUP_DIM = 256
GMLP_CHUNK = 128
N_MEM = 256
N_GROUPS = 4
EXPERTS_PER_GROUP = 8
N_EXPERTS = 32
D_EXPERT = 256
LANES = 128
CONV_PAD = 8

F32 = jnp.float32
BF16 = jnp.bfloat16
NEG_INF = float("-inf")

VMEM_LIMIT = 56 * 1024 * 1024


def _cparams(sem):
    return pltpu.CompilerParams(dimension_semantics=sem, vmem_limit_bytes=VMEM_LIMIT)


def _const_spec(shape):
    nd = len(shape)
    return pl.BlockSpec(shape, lambda *_: (0,) * nd, pipeline_mode=pl.Buffered(1))


def _sigmoid(x):
    return 0.5 * (jnp.tanh(0.5 * x) + 1.0)


def _log_sigmoid(x):
    return jnp.minimum(x, 0.0) - jnp.log(1.0 + jnp.exp(-jnp.abs(x)))


def _rmsnorm(x, g):
    r = lax.rsqrt(jnp.mean(x * x, axis=-1, keepdims=True) + EPS)
    return (x * r) * g


def _dot(a, b):
    return jnp.dot(a, b, preferred_element_type=F32)


def _dot_nt(a, b):
    return lax.dot_general(a, b, (((1,), (1,)), ((), ())), preferred_element_type=F32)


def _dot_tn(a, b):
    return lax.dot_general(a, b, (((0,), (0,)), ((), ())), preferred_element_type=F32)


def _memkv_kernel(mem_ref, g_ref, wk_ref, wv_ref, k_ref, v_ref):
    mn = _rmsnorm(mem_ref[...], g_ref[...]).astype(BF16)
    k_ref[...] = _dot(mn, wk_ref[...])
    v_ref[...] = _dot(mn, wv_ref[...])


def _memory_kv(mem2d, g, wk, wv):
    n = mem2d.shape[0]
    tm = 512
    row = pl.BlockSpec((tm, D_MODEL), lambda i: (i, 0))
    return pl.pallas_call(
        _memkv_kernel,
        grid=(n // tm,),
        in_specs=[row, _const_spec((1, D_MODEL)), _const_spec((D_MODEL, D_MODEL)),
                  _const_spec((D_MODEL, D_MODEL))],
        out_specs=[row, row],
        out_shape=[jax.ShapeDtypeStruct((n, D_MODEL), F32)] * 2,
        compiler_params=_cparams(("parallel",)),
        name="memory_kv",
    )(mem2d, g, wk, wv)


def _inproj_a_kernel(x_ref, g_ref, wqk_ref, wv_ref, wo_ref, wift_ref, bif_ref, cw_ref, cb_ref, cs_ref,
                     q_ref, k_ref, v_ref, o_ref, gt_ref, cn_ref, ext_ref, *, tm):
    i = pl.program_id(1)

    @pl.when(i == 0)
    def _():
        ext_ref[0:CONV_PAD, :] = jnp.zeros((CONV_PAD, 2 * D_MODEL), F32)
        ext_ref[CONV_PAD - (CONV_W - 1):CONV_PAD, :] = cs_ref[...]

    xn = _rmsnorm(x_ref[...], g_ref[...]).astype(BF16)

    ext_ref[CONV_PAD:CONV_PAD + tm, :] = _dot(xn, wqk_ref[...])
    acc = cb_ref[...] + ext_ref[CONV_PAD:CONV_PAD + tm, :] * cw_ref[CONV_W - 1:CONV_W, :]
    for j in range(CONV_W - 1):
        s = CONV_PAD - (CONV_W - 1) + j
        acc = acc + ext_ref[s:s + tm, :] * cw_ref[j:j + 1, :]
    qk = acc * _sigmoid(acc)
    q_ref[...] = (qk[:, :D_MODEL] * (HEAD_DIM ** -0.5)).astype(BF16)
    k_ref[...] = qk[:, D_MODEL:].astype(BF16)
    cn_ref[...] = ext_ref[tm + CONV_PAD - (CONV_W - 1):tm + CONV_PAD, :]
    ext_ref[0:CONV_PAD, :] = ext_ref[tm:tm + CONV_PAD, :]

    v_ref[...] = _dot(xn, wv_ref[...]).astype(BF16)
    o_ref[...] = _sigmoid(_dot(xn, wo_ref[...])).astype(BF16)

    row = lax.broadcasted_iota(jnp.int32, (2 * HEADS, CHUNK), 0)
    for c in range(tm // CHUNK):
        z = _dot_nt(wift_ref[...], xn[c * CHUNK:(c + 1) * CHUNK, :]) + bif_ref[...]
        gt_ref[c] = jnp.where(row < HEADS, z, _log_sigmoid(z))


def _inproj_a(x, g, wqk, wv, wo, wift, bif, cw, cb, cs, tm):
    b, l, _ = x.shape
    nt = l // tm
    tok = lambda w: pl.BlockSpec((None, tm, w), lambda bi, i: (bi, i, 0))
    state = pl.BlockSpec((None, CONV_W - 1, 2 * D_MODEL), lambda bi, i: (bi, 0, 0))
    return pl.pallas_call(
        functools.partial(_inproj_a_kernel, tm=tm),
        grid=(b, nt),
        in_specs=[tok(D_MODEL), _const_spec((1, D_MODEL)), _const_spec((D_MODEL, 2 * D_MODEL)),
                  _const_spec((D_MODEL, D_MODEL)), _const_spec((D_MODEL, D_MODEL)),
                  _const_spec((2 * HEADS, D_MODEL)), _const_spec((2 * HEADS, 1)),
                  _const_spec((CONV_W, 2 * D_MODEL)), _const_spec((1, 2 * D_MODEL)), state],
        out_specs=[tok(D_MODEL), tok(D_MODEL), tok(D_MODEL), tok(D_MODEL),
                   pl.BlockSpec((None, tm // CHUNK, 2 * HEADS, CHUNK), lambda bi, i: (bi, i, 0, 0)),
                   state],
        out_shape=[jax.ShapeDtypeStruct((b, l, D_MODEL), BF16)] * 4
        + [jax.ShapeDtypeStruct((b, l // CHUNK, 2 * HEADS, CHUNK), F32),
           jax.ShapeDtypeStruct((b, CONV_W - 1, 2 * D_MODEL), F32)],
        scratch_shapes=[pltpu.VMEM((tm + CONV_PAD, 2 * D_MODEL), F32)],
        compiler_params=_cparams(("parallel", "arbitrary")),
        name="inproj_a",
    )(x, g, wqk, wv, wo, wift, bif, cw, cb, cs)


def _inproj_b_kernel(x_ref, g_ref, wu_ref, wgv_ref, wmq_ref, wgate_ref, lng_ref, lnb_ref, ws_ref, bst_ref,
                     mk_ref, mv_ref, ug_ref, att_ref, gates_ref, *rest, tm, cl, emit_gv):
    xn = _rmsnorm(x_ref[...], g_ref[...]).astype(BF16)

    gates_ref[...] = _sigmoid(_dot(xn, wgate_ref[...])).astype(BF16)

    gvr = jax.nn.gelu(_dot(xn, wgv_ref[...]))
    mu = jnp.mean(gvr, axis=-1, keepdims=True)
    xc = gvr - mu
    r = lax.rsqrt(jnp.mean(xc * xc, axis=-1, keepdims=True) + EPS)
    gv = (xc * r) * lng_ref[...] + lnb_ref[...]
    if emit_gv:
        rest[0][...] = gv
    gvb = gv.astype(BF16)
    u = jax.nn.gelu(_dot(xn, wu_ref[...]))
    tri = (lax.broadcasted_iota(jnp.int32, (cl, cl), 0) >= lax.broadcasted_iota(jnp.int32, (cl, cl), 1))
    for gi in range(GMLP_GROUPS):
        wsg = jnp.where(tri, ws_ref[gi], 0.0).astype(BF16)
        lo, hi = gi * GMLP_GROUP_DIM, (gi + 1) * GMLP_GROUP_DIM
        for c in range(tm // cl):
            sp = _dot(wsg, gvb[c * cl:(c + 1) * cl, lo:hi]) + bst_ref[:, gi:gi + 1]
            ug_ref[c * cl:(c + 1) * cl, lo:hi] = (u[c * cl:(c + 1) * cl, lo:hi] * sp).astype(BF16)

    mq = _dot(xn, wmq_ref[...]).astype(BF16)
    for h in range(HEADS):
        lo, hi = h * HEAD_DIM, (h + 1) * HEAD_DIM
        sc = _dot_nt(mq[:, lo:hi], mk_ref[:, lo:hi]) * (HEAD_DIM ** -0.5)
        e = jnp.exp(sc - jnp.max(sc, axis=-1, keepdims=True))
        a = (e / jnp.sum(e, axis=-1, keepdims=True)).astype(BF16)
        att_ref[:, lo:hi] = _dot(a, mv_ref[:, lo:hi]).astype(BF16)


def _inproj_b(x, g, wu, wgv, wmq, wgate, lng, lnb, ws, bst, mk, mv, tm, cl, emit_gv):
    b, l, _ = x.shape
    nt = l // tm
    tok = lambda w: pl.BlockSpec((None, tm, w), lambda bi, i: (bi, i, 0))
    mem = pl.BlockSpec((None, N_MEM, D_MODEL), lambda bi, i: (bi, 0, 0))
    out_specs = [tok(D_MODEL), tok(D_MODEL), tok(3 * D_MODEL)]
    out_shape = [jax.ShapeDtypeStruct((b, l, D_MODEL), BF16), jax.ShapeDtypeStruct((b, l, D_MODEL), BF16),
                 jax.ShapeDtypeStruct((b, l, 3 * D_MODEL), BF16)]
    if emit_gv:
        out_specs.append(tok(D_MODEL))
        out_shape.append(jax.ShapeDtypeStruct((b, l, D_MODEL), F32))
    return pl.pallas_call(
        functools.partial(_inproj_b_kernel, tm=tm, cl=cl, emit_gv=emit_gv),
        grid=(b, nt),
        in_specs=[tok(D_MODEL), _const_spec((1, D_MODEL)), _const_spec((D_MODEL, D_MODEL)),
                  _const_spec((D_MODEL, D_MODEL)), _const_spec((D_MODEL, D_MODEL)),
                  _const_spec((D_MODEL, 3 * D_MODEL)), _const_spec((1, D_MODEL)), _const_spec((1, D_MODEL)),
                  _const_spec((GMLP_GROUPS, cl, cl)), _const_spec((cl, GMLP_GROUPS)), mem, mem],
        out_specs=out_specs,
        out_shape=out_shape,
        compiler_params=_cparams(("parallel", "parallel")),
        name="inproj_b",
    )(x, g, wu, wgv, wmq, wgate, lng, lnb, ws, bst, mk, mv)


def _mlstm_kernel(q_ref, k_ref, v_ref, o_ref, gt_ref, c0_ref, n0_ref, m0_ref, ng_ref,
                  hm_ref, c_ref, n_ref, m_ref, *, cb):
    i = pl.program_id(1)

    @pl.when(i == 0)
    def _():
        c_ref[...] = c0_ref[...]
        n_ref[...] = n0_ref[...]
        m_ref[...] = m0_ref[...]

    L = CHUNK
    ti = lax.broadcasted_iota(jnp.int32, (L, L), 0)
    si = lax.broadcasted_iota(jnp.int32, (L, L), 1)
    tri = ti >= si
    eye = ti == si
    lane = lax.broadcasted_iota(jnp.int32, (2 * HEADS, L), 1)

    for c in range(cb // L):
        g = gt_ref[c]
        bc = g
        for sh in (1, 2, 4, 8, 16, 32):
            bc = bc + jnp.where(lane >= sh, pltpu.roll(bc, sh, axis=1), 0.0)
        r0, r1 = c * L, (c + 1) * L
        for h in range(HEADS):
            lo, hi = h * HEAD_DIM, (h + 1) * HEAD_DIM
            ig_r = g[h:h + 1, :]
            bc_r = bc[HEADS + h:HEADS + h + 1, :]
            a_r = ig_r - bc_r
            bc_c = jnp.sum(jnp.where(eye, bc_r, 0.0), axis=-1, keepdims=True)
            a_c = jnp.sum(jnp.where(eye, a_r, 0.0), axis=-1, keepdims=True)
            m0 = m_ref[h:h + 1, 0:1]
            dmat = jnp.where(tri, bc_c + a_r, NEG_INF)
            inter = bc_c + m0
            m = jnp.maximum(inter, jnp.max(dmat, axis=-1, keepdims=True))
            w_intra = jnp.exp(dmat - m)
            w_inter = jnp.exp(inter - m)
            q = q_ref[r0:r1, lo:hi]
            k = k_ref[r0:r1, lo:hi]
            v = v_ref[r0:r1, lo:hi]
            c0 = c_ref[h]
            n0 = n_ref[h:h + 1, :]
            s = _dot_nt(q, k) * w_intra
            num = w_inter * _dot(q, c0.astype(BF16)) + _dot(s.astype(BF16), v)
            qn = jnp.sum(q.astype(F32) * n0, axis=-1, keepdims=True)
            den = w_inter * qn + jnp.sum(s, axis=-1, keepdims=True)
            hh = num / jnp.maximum(jnp.abs(den), jnp.exp(-m))
            m_last = m[L - 1:L, :]
            bc_last = bc_r[:, L - 1:L]
            w_last = jnp.exp(bc_last + a_c - m_last)
            decay = jnp.exp(bc_last + m0 - m_last)
            kw = k.astype(F32) * w_last
            c_ref[h] = decay * c0 + _dot_tn(kw.astype(BF16), v)
            n_ref[h:h + 1, :] = decay * n0 + jnp.sum(kw, axis=0, keepdims=True)
            m_ref[h:h + 1, :] = jnp.broadcast_to(m_last, (1, LANES))
            hn = hh * lax.rsqrt(jnp.mean(hh * hh, axis=-1, keepdims=True) + EPS)
            hm_ref[r0:r1, lo:hi] = ((hn * ng_ref[:, lo:hi]) * o_ref[r0:r1, lo:hi].astype(F32)).astype(BF16)


def _mlstm(q, k, v, o, gt, c0, n0, m0, ng, cb):
    b, l, _ = q.shape
    nt = l // cb
    tok = pl.BlockSpec((None, cb, D_MODEL), lambda bi, i: (bi, i, 0))
    cs = pl.BlockSpec((None, HEADS, HEAD_DIM, HEAD_DIM), lambda bi, i: (bi, 0, 0, 0))
    ns = pl.BlockSpec((None, HEADS, HEAD_DIM), lambda bi, i: (bi, 0, 0))
    ms = pl.BlockSpec((None, HEADS, LANES), lambda bi, i: (bi, 0, 0))
    return pl.pallas_call(
        functools.partial(_mlstm_kernel, cb=cb),
        grid=(b, nt),
        in_specs=[tok, tok, tok, tok,
                  pl.BlockSpec((None, cb // CHUNK, 2 * HEADS, CHUNK), lambda bi, i: (bi, i, 0, 0)),
                  cs, ns, ms, _const_spec((1, D_MODEL))],
        out_specs=[tok, cs, ns, ms],
        out_shape=[jax.ShapeDtypeStruct((b, l, D_MODEL), BF16),
                   jax.ShapeDtypeStruct((b, HEADS, HEAD_DIM, HEAD_DIM), F32),
                   jax.ShapeDtypeStruct((b, HEADS, HEAD_DIM), F32),
                   jax.ShapeDtypeStruct((b, HEADS, LANES), F32)],
        compiler_params=_cparams(("parallel", "arbitrary")),
        name="mlstm",
    )(q, k, v, o, gt, c0, n0, m0, ng)


def _merge_kernel(x_ref, hm_ref, ug_ref, att_ref, gates_ref, wa_ref, wb_ref, wc_ref, wo_ref, h_ref):
    g = gates_ref[...].astype(F32)
    mixed = g[:, :D_MODEL] * _dot(hm_ref[...], wa_ref[...])
    mixed = mixed + g[:, D_MODEL:2 * D_MODEL] * _dot(ug_ref[...], wb_ref[...])
    mixed = mixed + g[:, 2 * D_MODEL:] * _dot(att_ref[...], wc_ref[...])
    h_ref[...] = x_ref[...] + _dot(mixed.astype(BF16), wo_ref[...])


def _merge(x2d, hm, ug, att, gates, wa, wb, wc, wo, tm):
    n = x2d.shape[0]
    row = lambda w: pl.BlockSpec((tm, w), lambda i: (i, 0))
    wspec = _const_spec((D_MODEL, D_MODEL))
    return pl.pallas_call(
        _merge_kernel,
        grid=(n // tm,),
        in_specs=[row(D_MODEL), row(D_MODEL), row(D_MODEL), row(D_MODEL), row(3 * D_MODEL),
                  wspec, wspec, wspec, wspec],
        out_specs=row(D_MODEL),
        out_shape=jax.ShapeDtypeStruct((n, D_MODEL), F32),
        compiler_params=_cparams(("parallel",)),
        name="merge",
    )(x2d, hm, ug, att, gates, wa, wb, wc, wo)


MOE_T = 256
MOE_CAP = 32
MOE_S = N_EXPERTS * MOE_CAP
MOE_OVF = 512
SEG_ALIGN = 16
EXPERT_BLOCK = 128


def _moe_route_kernel(hp_ref, hs_ref, g_ref, wrg_ref, brg_ref, wre_ref, bre_ref,
                      main_ref, ovf_ref, info_ref, meta_ref, *, nsp):
    t = MOE_T
    h = jnp.where(pl.program_id(0) < nsp, hp_ref[...], hs_ref[...])
    xm = _rmsnorm(h, g_ref[...]).astype(BF16)
    lane = lax.broadcasted_iota(jnp.int32, (t, LANES), 1)
    lg = jnp.where(lane < N_GROUPS, _dot(xm, wrg_ref[...]) + brg_ref[...], NEG_INF)
    gmax = jnp.max(lg, axis=-1, keepdims=True)
    p_top = 1.0 / jnp.sum(jnp.exp(lg - gmax), axis=-1, keepdims=True)
    grp = jnp.min(jnp.where(lg == gmax, lane, LANES), axis=-1, keepdims=True)
    el = _dot(xm, wre_ref[...]) + bre_ref[...]
    in_grp = (lane >= grp * EXPERTS_PER_GROUP) & (lane < (grp + 1) * EXPERTS_PER_GROUP)
    vals = jnp.where(in_grp, el, NEG_INF)
    v1 = jnp.max(vals, axis=-1, keepdims=True)
    i1 = jnp.min(jnp.where(vals == v1, lane, LANES), axis=-1, keepdims=True)
    vals2 = jnp.where(lane == i1, NEG_INF, vals)
    v2 = jnp.max(vals2, axis=-1, keepdims=True)
    i2 = jnp.min(jnp.where(vals2 == v2, lane, LANES), axis=-1, keepdims=True)
    r = jnp.exp(v2 - v1)
    p1 = p_top / (1.0 + r)
    p2 = p_top * r / (1.0 + r)

    sel1 = lane == i1
    sel2 = lane == i2
    onehot = jnp.where(sel1 | sel2, 1.0, 0.0)
    cnt = jnp.sum(onehot, axis=0, keepdims=True).astype(jnp.int32)
    pn = jnp.bitwise_and(jnp.maximum(cnt - MOE_CAP, 0) + (SEG_ALIGN - 1), -SEG_ALIGN)
    pn8 = jnp.broadcast_to(pn, (8, LANES))
    lane8 = lax.broadcasted_iota(jnp.int32, (8, LANES), 1)
    inc = pn8
    for sh in (1, 2, 4, 8, 16):
        inc = inc + jnp.where(lane8 >= sh, pltpu.roll(inc, sh, axis=1), 0)
    off8 = inc - pn8
    row8 = lax.broadcasted_iota(jnp.int32, (8, LANES), 0)
    meta_ref[...] = jnp.where(row8 == 0, off8, jnp.where(row8 == 1, pn8, 0))

    ti = lax.broadcasted_iota(jnp.int32, (t, t), 0)
    si = lax.broadcasted_iota(jnp.int32, (t, t), 1)
    before = jnp.where(ti > si, 1.0, 0.0).astype(BF16)
    rank = _dot(before, onehot.astype(BF16))
    off_f = off8[0:1, :].astype(F32)

    def slot_row(sel, idx):
        rk = jnp.sum(jnp.where(sel, rank, 0.0), axis=-1, keepdims=True)
        of = jnp.sum(jnp.where(sel, off_f, 0.0), axis=-1, keepdims=True)
        return jnp.where(rk < MOE_CAP, idx.astype(F32) * MOE_CAP + rk, MOE_S - MOE_CAP + of + rk)

    pos1 = slot_row(sel1, i1)
    pos2 = slot_row(sel2, i2)
    info_ref[...] = (jnp.where(lane == 0, pos1, 0.0) + jnp.where(lane == 1, pos2, 0.0)
                     + jnp.where(lane == 2, p1, 0.0) + jnp.where(lane == 3, p2, 0.0))

    eye = ti == si
    pos1_r = jnp.sum(jnp.where(eye, pos1, 0.0), axis=0, keepdims=True)
    pos2_r = jnp.sum(jnp.where(eye, pos2, 0.0), axis=0, keepdims=True)

    def gather(rows, base):
        srow = (lax.broadcasted_iota(jnp.int32, (rows, t), 0) + base).astype(F32)
        pick = jnp.where((srow == pos1_r) | (srow == pos2_r), 1.0, 0.0).astype(BF16)
        return _dot(pick, xm).astype(BF16)

    main_ref[...] = gather(MOE_S, 0)
    has_ovf = jnp.sum(pn) > 0

    @pl.when(has_ovf)
    def _():
        ovf_ref[...] = gather(MOE_OVF, MOE_S)

    @pl.when(jnp.logical_not(has_ovf))
    def _():
        ovf_ref[...] = jnp.zeros_like(ovf_ref)


def _two_group_specs(nsp, cols):
    return (pl.BlockSpec((MOE_T, cols), lambda j, *_: (jnp.minimum(j, nsp - 1), 0)),
            pl.BlockSpec((MOE_T, cols), lambda j, *_: (jnp.maximum(j - nsp, 0), 0)))


def _moe_route(hp, hs, g, wrg, brg, wre, bre):
    nsp = hp.shape[0] // MOE_T
    ns = nsp + hs.shape[0] // MOE_T
    return pl.pallas_call(
        functools.partial(_moe_route_kernel, nsp=nsp),
        grid=(ns,),
        in_specs=[*_two_group_specs(nsp, D_MODEL), _const_spec((1, D_MODEL)),
                  _const_spec((D_MODEL, LANES)), _const_spec((1, LANES)),
                  _const_spec((D_MODEL, LANES)), _const_spec((1, LANES))],
        out_specs=[pl.BlockSpec((None, MOE_S, D_MODEL), lambda j: (j, 0, 0)),
                   pl.BlockSpec((None, MOE_OVF, D_MODEL), lambda j: (j, 0, 0)),
                   pl.BlockSpec((MOE_T, LANES), lambda j: (j, 0)),
                   pl.BlockSpec((None, 8, LANES), lambda j: (j, 0, 0))],
        out_shape=[jax.ShapeDtypeStruct((ns, MOE_S, D_MODEL), BF16),
                   jax.ShapeDtypeStruct((ns, MOE_OVF, D_MODEL), BF16),
                   jax.ShapeDtypeStruct((ns * MOE_T, LANES), F32),
                   jax.ShapeDtypeStruct((ns, 8, LANES), jnp.int32)],
        compiler_params=_cparams(("arbitrary",)),
        name="moe_route",
    )(hp, hs, g, wrg, brg, wre, bre)


def _swiglu(x, wgu, wdb):
    gu = _dot(x, wgu[...])
    gate = gu[:, :D_EXPERT]
    hid = (gate * _sigmoid(gate)) * gu[:, D_EXPERT:]
    return _dot(hid.astype(BF16), wdb[...]).astype(BF16)


def _moe_expert_kernel(offs_ref, pns_ref, eflag_ref, main_ref, slots_hbm, wg_ref, wu_ref, wd_ref,
                       mout_ref, out_hbm, xbuf, obuf, wgu, wdb, sem_in, sem_out, *, n_sub, jb, n_rb):
    e = pl.program_id(0)
    rb = pl.program_id(1)

    @pl.when((e == 0) & (rb == 0))
    def _():
        xbuf[...] = jnp.zeros_like(xbuf)

    @pl.when(rb == 0)
    def _():
        wgu[:, :D_EXPERT] = wg_ref[...].astype(BF16)
        wgu[:, D_EXPERT:] = wu_ref[...].astype(BF16)
        wdb[...] = wd_ref[...].astype(BF16)

    sb = main_ref.shape[0]
    y = _swiglu(main_ref[...].reshape(sb * MOE_CAP, D_MODEL), wgu, wdb)
    mout_ref[...] = y.reshape(sb, MOE_CAP, D_MODEL)

    def copy_in(j, src, dst):
        return pltpu.make_async_copy(slots_hbm.at[j, pl.ds(src, SEG_ALIGN), :],
                                     xbuf.at[pl.ds(dst, SEG_ALIGN), :], sem_in)

    def copy_out(j, src, dst):
        return pltpu.make_async_copy(obuf.at[pl.ds(src, SEG_ALIGN), :],
                                     out_hbm.at[j, pl.ds(dst, SEG_ALIGN), :], sem_out)

    def for_each_chunk(g, fn):
        def seg(jj, cur):
            j = g * jb + jj
            n = pns_ref[j * N_EXPERTS + e]
            off = offs_ref[j * N_EXPERTS + e]

            def chunk(k, c):
                fn(j, pl.multiple_of(off + k * SEG_ALIGN, SEG_ALIGN), pl.multiple_of(cur + k * SEG_ALIGN, SEG_ALIGN))
                return c
            lax.fori_loop(0, n // SEG_ALIGN, chunk, 0)
            return cur + n
        return lax.fori_loop(0, jb, seg, 0)

    def group(g, carry):
        total = for_each_chunk(g, lambda j, r, b: copy_in(j, r, b).start())
        nchunk = total // SEG_ALIGN

        def wait_in(k, c):
            copy_in(0, 0, 0).wait()
            return c
        lax.fori_loop(0, nchunk, wait_in, 0)

        def block(bi, c):
            r0 = pl.multiple_of(bi * EXPERT_BLOCK, EXPERT_BLOCK)
            obuf[pl.ds(r0, EXPERT_BLOCK), :] = _swiglu(xbuf[pl.ds(r0, EXPERT_BLOCK), :], wgu, wdb)
            return c
        lax.fori_loop(0, (total + EXPERT_BLOCK - 1) // EXPERT_BLOCK, block, 0)

        for_each_chunk(g, lambda j, r, b: copy_out(j, b, r).start())

        def wait_out(k, c):
            copy_out(0, 0, 0).wait()
            return c
        lax.fori_loop(0, nchunk, wait_out, 0)
        return carry

    @pl.when((rb == n_rb - 1) & (eflag_ref[e] > 0))
    def _():
        lax.fori_loop(0, n_sub // jb, group, 0)


def _moe_expert(main, ovf, offs, pns, eflag, wg, wu, wd):
    ns = main.shape[0]
    n_rb = 2 if ns % 2 == 0 else 1
    sb = ns // n_rb
    jb = max(d for d in range(1, 12) if ns % d == 0)
    rows = jb * MOE_T + EXPERT_BLOCK
    main4 = main.reshape(ns, N_EXPERTS, MOE_CAP, D_MODEL)
    mspec = pl.BlockSpec((sb, None, MOE_CAP, D_MODEL), lambda e, rb, *_: (rb, e, 0, 0))
    wspec = lambda a, b: pl.BlockSpec((None, a, b), lambda e, rb, *_: (e, 0, 0))
    grid_spec = pltpu.PrefetchScalarGridSpec(
        num_scalar_prefetch=3,
        grid=(N_EXPERTS, n_rb),
        in_specs=[mspec, pl.BlockSpec(memory_space=pl.ANY),
                  wspec(D_MODEL, D_EXPERT), wspec(D_MODEL, D_EXPERT), wspec(D_EXPERT, D_MODEL)],
        out_specs=[mspec, pl.BlockSpec(memory_space=pl.ANY)],
        scratch_shapes=[pltpu.VMEM((rows, D_MODEL), BF16), pltpu.VMEM((rows, D_MODEL), BF16),
                        pltpu.VMEM((D_MODEL, 2 * D_EXPERT), BF16), pltpu.VMEM((D_EXPERT, D_MODEL), BF16),
                        pltpu.SemaphoreType.DMA(()), pltpu.SemaphoreType.DMA(())],
    )
    mout, oout = pl.pallas_call(
        functools.partial(_moe_expert_kernel, n_sub=ns, jb=jb, n_rb=n_rb),
        grid_spec=grid_spec,
        out_shape=[jax.ShapeDtypeStruct(main4.shape, main4.dtype), jax.ShapeDtypeStruct(ovf.shape, ovf.dtype)],
        input_output_aliases={3: 0, 4: 1},
        compiler_params=_cparams(("arbitrary", "arbitrary")),
        name="moe_expert",
    )(offs, pns, eflag, main4, ovf, wg, wu, wd)
    return mout.reshape(main.shape), oout


def _moe_combine_kernel(jflag_ref, hp_ref, hs_ref, main_ref, ovf_ref, info_ref, gf_ref, yp_ref, ys_ref, acc_ref,
                        *, nsp):
    j = pl.program_id(0)
    info = info_ref[...]

    def weights(cols, base):
        scol = (lax.broadcasted_iota(jnp.int32, (MOE_T, cols), 1) + base).astype(F32)
        return (jnp.where(scol == info[:, 0:1], info[:, 2:3], 0.0)
                + jnp.where(scol == info[:, 1:2], info[:, 3:4], 0.0)).astype(BF16)

    h = jnp.where(j < nsp, hp_ref[...], hs_ref[...])
    acc_ref[...] = h + _dot(weights(MOE_S, 0), main_ref[...])

    @pl.when(jflag_ref[j] > 0)
    def _():
        acc_ref[...] += _dot(weights(MOE_OVF, MOE_S), ovf_ref[...])

    y = _rmsnorm(acc_ref[...], gf_ref[...])

    @pl.when(j < nsp)
    def _():
        yp_ref[...] = y

    @pl.when(j >= nsp)
    def _():
        ys_ref[...] = y


def _moe_combine(hp, hs, main, ovf, info, jflag, gf):
    nsp = hp.shape[0] // MOE_T
    ns = main.shape[0]
    grid_spec = pltpu.PrefetchScalarGridSpec(
        num_scalar_prefetch=1,
        grid=(ns,),
        in_specs=[*_two_group_specs(nsp, D_MODEL),
                  pl.BlockSpec((None, MOE_S, D_MODEL), lambda j, jf: (j, 0, 0)),
                  pl.BlockSpec((None, MOE_OVF, D_MODEL), lambda j, jf: (jnp.where(jf[j] > 0, j, 0), 0, 0)),
                  pl.BlockSpec((MOE_T, LANES), lambda j, jf: (j, 0)),
                  pl.BlockSpec((1, D_MODEL), lambda j, jf: (0, 0))],
        out_specs=list(_two_group_specs(nsp, D_MODEL)),
        scratch_shapes=[pltpu.VMEM((MOE_T, D_MODEL), F32)],
    )
    return pl.pallas_call(
        functools.partial(_moe_combine_kernel, nsp=nsp),
        grid_spec=grid_spec,
        out_shape=[jax.ShapeDtypeStruct(hp.shape, F32), jax.ShapeDtypeStruct(hs.shape, F32)],
        compiler_params=_cparams(("arbitrary",)),
        name="moe_combine",
    )(jflag, hp, hs, main, ovf, info, gf)


def _moe(hp, hs, p):
    main, ovf, info, meta = _moe_route(hp, hs, p["norm_moe_g"], p["w_rg"], p["b_rg"], p["w_re"], p["b_re"])
    pn = meta[:, 1, :N_EXPERTS]
    offs = meta[:, 0, :N_EXPERTS].reshape(-1)
    eflag = (jnp.sum(pn, axis=0) > 0).astype(jnp.int32)
    jflag = (jnp.sum(pn, axis=1) > 0).astype(jnp.int32)
    main, ovf = _moe_expert(main, ovf, offs, pn.reshape(-1), eflag, p["w_eg"], p["w_eu"], p["w_ed"])
    return _moe_combine(hp, hs, main, ovf, info, jflag, p["norm_final_g"])


def _trunk(x, mem_k, mem_v, c0, n0, m0, conv0, p, *, tm, cb, cl, emit_gv):
    b, l, _ = x.shape
    q, k, v, osig, gt, conv_new = _inproj_a(
        x, p["norm_mix_g"], p["w_qk"], p["w_v"], p["w_o"], p["w_ift"], p["b_if"],
        p["conv_w"], p["conv_b"], conv0, tm)
    outs = _inproj_b(x, p["norm_mix_g"], p["w_u"], p["w_gv"], p["w_mq"], p["w_gate"],
                     p["gmlp_norm_g"], p["gmlp_norm_b"], p["w_s"][:, :cl, :cl], p["b_st"][:cl],
                     mem_k, mem_v, tm, cl, emit_gv)
    ug, att, gates = outs[:3]
    m0b = jnp.broadcast_to(m0[..., None], m0.shape + (LANES,))
    hm, c1, n1, m1 = _mlstm(q, k, v, osig, gt, c0, n0, m0b, p["mlstm_norm_g"], cb)
    n = b * l
    h = _merge(x.reshape(n, D_MODEL), hm.reshape(n, D_MODEL), ug.reshape(n, D_MODEL),
               att.reshape(n, D_MODEL), gates.reshape(n, 3 * D_MODEL),
               p["w_br_mlstm"], p["w_br_gmlp"], p["w_br_mem"], p["w_out"], 512)
    return h, c1, n1, m1[..., 0], conv_new, (outs[3] if emit_gv else None)


def kernel(x_prompt, x_sample, mem_prompt, cache_mem_k, cache_mem_v, state_mlstm_C, state_mlstm_n, state_mlstm_m, state_mlstm_conv, norm_mix_g, w_in, mlstm_i_b, mlstm_f_b, mlstm_conv_w, mlstm_conv_b, mlstm_norm_g, gmlp_norm_g, gmlp_norm_b, gmlp_w_s, gmlp_b_s, mem_norm_g, w_mem_k, w_mem_v, w_br_mlstm, w_br_gmlp, w_br_mem, w_out, norm_moe_g, w_router_group, b_router_group, w_router_expert, b_router_expert, w_exp_gate, w_exp_up, w_exp_down, norm_final_g):
    bp = x_prompt.shape[0]
    bs = x_sample.shape[0]
    W = D_MODEL
    wi = w_in[0]
    o_qk, o_v, o_o, o_i = 0, 2 * W, 3 * W, 4 * W
    o_f = o_i + HEADS
    o_u = o_f + HEADS
    o_gv, o_mq, o_gate = o_u + W, o_u + 2 * W, o_u + 3 * W
    row = lambda a: a.reshape(1, -1)
    pad_l = lambda a: jnp.pad(a, ((0, 0), (0, LANES - a.shape[1])))
    p = {
        "norm_mix_g": row(norm_mix_g[0]),
        "w_qk": wi[:, o_qk:o_v].astype(BF16), "w_v": wi[:, o_v:o_o].astype(BF16),
        "w_o": wi[:, o_o:o_i].astype(BF16),
        "w_ift": wi[:, o_i:o_u].T.astype(BF16),
        "b_if": jnp.concatenate([mlstm_i_b[0], mlstm_f_b[0]]).reshape(2 * HEADS, 1),
        "conv_w": mlstm_conv_w[0], "conv_b": row(mlstm_conv_b[0]),
        "w_u": wi[:, o_u:o_gv].astype(BF16), "w_gv": wi[:, o_gv:o_mq].astype(BF16),
        "w_mq": wi[:, o_mq:o_gate].astype(BF16), "w_gate": wi[:, o_gate:].astype(BF16),
        "gmlp_norm_g": row(gmlp_norm_g[0]), "gmlp_norm_b": row(gmlp_norm_b[0]),
        "w_s": gmlp_w_s[0], "b_st": gmlp_b_s[0].T,
        "mlstm_norm_g": row(mlstm_norm_g[0]),
        "w_br_mlstm": w_br_mlstm[0].astype(BF16), "w_br_gmlp": w_br_gmlp[0].astype(BF16),
        "w_br_mem": w_br_mem[0].astype(BF16), "w_out": w_out[0].astype(BF16),
        "norm_moe_g": row(norm_moe_g[0]),
        "w_rg": pad_l(w_router_group[0]).astype(BF16), "b_rg": pad_l(row(b_router_group[0])),
        "w_re": pad_l(w_router_expert[0]).astype(BF16), "b_re": pad_l(row(b_router_expert[0])),
        "w_eg": w_exp_gate[0], "w_eu": w_exp_up[0], "w_ed": w_exp_down[0],
        "norm_final_g": row(norm_final_g),
    }

    mk_p, mv_p = _memory_kv(mem_prompt.reshape(bp * N_MEM, W), row(mem_norm_g[0]),
                            w_mem_k[0].astype(BF16), w_mem_v[0].astype(BF16))
    mk_p3, mv_p3 = mk_p.reshape(bp, N_MEM, W), mv_p.reshape(bp, N_MEM, W)

    zeros = lambda *s: jnp.zeros(s, F32)
    hp, cp, np_, mp, cvp, _ = _trunk(
        x_prompt, mk_p3.astype(BF16), mv_p3.astype(BF16),
        zeros(bp, HEADS, HEAD_DIM, HEAD_DIM), zeros(bp, HEADS, HEAD_DIM), zeros(bp, HEADS),
        zeros(bp, CONV_W - 1, 2 * W), p, tm=256, cb=256, cl=GMLP_CHUNK, emit_gv=False)
    ls = x_sample.shape[1]
    hs, cs, ns, ms, cvs, gvs = _trunk(
        x_sample, cache_mem_k[0].reshape(bs, N_MEM, W).astype(BF16),
        cache_mem_v[0].reshape(bs, N_MEM, W).astype(BF16),
        state_mlstm_C[0], state_mlstm_n[0], state_mlstm_m[0], state_mlstm_conv[0], p,
        tm=ls, cb=ls, cl=min(ls, GMLP_CHUNK), emit_gv=True)

    yp, ys = _moe(hp, hs, p)
    kv_shape = (1, bp, N_MEM, HEADS, HEAD_DIM)
    return (yp.reshape(x_prompt.shape), ys.reshape(x_sample.shape), mk_p.reshape(kv_shape), mv_p.reshape(kv_shape),
            cp[None], np_[None], mp[None], cvp[None],
            cs[None], ns[None], ms[None], cvs[None], gvs[None])
```

```python
import functools

import jax
import jax.numpy as jnp
from jax import lax
from jax.experimental import pallas as pl
from jax.experimental.pallas import tpu as pltpu

D_MODEL = 1024
MLSTM_BLOCK = 128
EPS = 1e-6
HEADS = 4
HEAD_DIM = 256
CONV_W = 4
GMLP_GROUPS = 4
GMLP_GROUP_DIM = 256
GMLP_CHUNK = 128
N_MEM = 256
N_GROUPS = 4
EXPERTS_PER_GROUP = 8
N_EXPERTS = 32
D_EXPERT = 256
LANES = 128
CONV_PAD = 8

F32 = jnp.float32
BF16 = jnp.bfloat16
NEG_INF = float("-inf")

VMEM_LIMIT = 56 * 1024 * 1024


def _cparams(sem):
    return pltpu.CompilerParams(dimension_semantics=sem, vmem_limit_bytes=VMEM_LIMIT)


def _const_spec(shape):
    nd = len(shape)
    return pl.BlockSpec(shape, lambda *_: (0,) * nd, pipeline_mode=pl.Buffered(1))


def _sigmoid(x):
    return 0.5 * (jnp.tanh(0.5 * x) + 1.0)


def _log_sigmoid(x):
    return jnp.minimum(x, 0.0) - jnp.log(1.0 + jnp.exp(-jnp.abs(x)))


def _rmsnorm(x, g):
    r = lax.rsqrt(jnp.mean(x * x, axis=-1, keepdims=True) + EPS)
    return (x * r) * g


def _dot(a, b):
    return jnp.dot(a, b, preferred_element_type=F32)


def _dot_nt(a, b):
    return lax.dot_general(a, b, (((1,), (1,)), ((), ())), preferred_element_type=F32)


def _dot_tn(a, b):
    return lax.dot_general(a, b, (((0,), (0,)), ((), ())), preferred_element_type=F32)


def _memkv_kernel(mem_ref, g_ref, wk_ref, wv_ref, k_ref, v_ref):
    mn = _rmsnorm(mem_ref[...], g_ref[...]).astype(BF16)
    k_ref[...] = _dot(mn, wk_ref[...])
    v_ref[...] = _dot(mn, wv_ref[...])


def _memory_kv(mem2d, g, wk, wv):
    n = mem2d.shape[0]
    tm = 512
    row = pl.BlockSpec((tm, D_MODEL), lambda i: (i, 0))
    return pl.pallas_call(
        _memkv_kernel,
        grid=(n // tm,),
        in_specs=[row, _const_spec((1, D_MODEL)), _const_spec((D_MODEL, D_MODEL)),
                  _const_spec((D_MODEL, D_MODEL))],
        out_specs=[row, row],
        out_shape=[jax.ShapeDtypeStruct((n, D_MODEL), F32)] * 2,
        compiler_params=_cparams(("parallel",)),
        name="memory_kv",
    )(mem2d, g, wk, wv)


def _inproj_a_kernel(x_ref, g_ref, wqk_ref, wv_ref, wo_ref, wift_ref, bif_ref, cw_ref, cb_ref, cs_ref,
                     q_ref, kt_ref, v_ref, o_ref, gt_ref, cn_ref, ext_ref, *, tm, lc):
    i = pl.program_id(1)

    @pl.when(i == 0)
    def _():
        ext_ref[0:CONV_PAD, :] = jnp.zeros((CONV_PAD, 2 * D_MODEL), F32)
        ext_ref[CONV_PAD - (CONV_W - 1):CONV_PAD, :] = cs_ref[...]

    xn = _rmsnorm(x_ref[...], g_ref[...]).astype(BF16)

    ext_ref[CONV_PAD:CONV_PAD + tm, :] = _dot(xn, wqk_ref[...])
    acc = cb_ref[...] + ext_ref[CONV_PAD:CONV_PAD + tm, :] * cw_ref[CONV_W - 1:CONV_W, :]
    for j in range(CONV_W - 1):
        s = CONV_PAD - (CONV_W - 1) + j
        acc = acc + ext_ref[s:s + tm, :] * cw_ref[j:j + 1, :]
    qk = acc * _sigmoid(acc)
    q_ref[...] = (qk[:, :D_MODEL] * (HEAD_DIM ** -0.5)).astype(BF16)
    k = qk[:, D_MODEL:]
    if tm % LANES:
        k = jnp.concatenate([k, jnp.zeros((LANES - tm % LANES, D_MODEL), F32)], axis=0)
    kt = k.T.astype(BF16)
    for c in range(tm // lc):
        kt_ref[c] = kt[:, c * lc:(c + 1) * lc]
    cn_ref[...] = ext_ref[tm + CONV_PAD - (CONV_W - 1):tm + CONV_PAD, :]
    ext_ref[0:CONV_PAD, :] = ext_ref[tm:tm + CONV_PAD, :]

    v_ref[...] = _dot(xn, wv_ref[...]).astype(BF16)
    o_ref[...] = _sigmoid(_dot(xn, wo_ref[...])).astype(BF16)

    nch = tm // lc
    z = jnp.concatenate([_dot_nt(wift_ref[...], xn[c * lc:(c + 1) * lc, :]) + bif_ref[...]
                         for c in range(nch)], axis=0)
    is_ig = (lax.broadcasted_iota(jnp.int32, z.shape, 0) % (2 * HEADS)) < HEADS
    g = jnp.where(is_ig, z, _log_sigmoid(z))
    upper = jnp.where(lax.broadcasted_iota(jnp.int32, (lc, lc), 0)
                      <= lax.broadcasted_iota(jnp.int32, (lc, lc), 1), 1.0, 0.0)
    bc = jnp.dot(g, upper, preferred_element_type=F32, precision=lax.Precision.HIGHEST)
    a = g - pltpu.roll(bc, nch * 2 * HEADS - HEADS, axis=0)
    amax = jnp.broadcast_to(jnp.max(a, axis=-1, keepdims=True), z.shape)
    gb = jnp.where(is_ig, g, bc)
    for c in range(nch):
        gt_ref[c, 0:2 * HEADS, :] = gb[c * 8:(c + 1) * 8, :]
        gt_ref[c, 2 * HEADS:4 * HEADS, :] = amax[c * 8:(c + 1) * 8, :]


def _inproj_a(x, g, wqk, wv, wo, wift, bif, cw, cb, cs, tm):
    b, l, _ = x.shape
    nt = l // tm
    CHUNK = min(MLSTM_BLOCK, l)
    tok = lambda w: pl.BlockSpec((None, tm, w), lambda bi, i: (bi, i, 0))
    state = pl.BlockSpec((None, CONV_W - 1, 2 * D_MODEL), lambda bi, i: (bi, 0, 0))
    return pl.pallas_call(
        functools.partial(_inproj_a_kernel, tm=tm, lc=CHUNK),
        grid=(b, nt),
        in_specs=[tok(D_MODEL), _const_spec((1, D_MODEL)), _const_spec((D_MODEL, 2 * D_MODEL)),
                  _const_spec((D_MODEL, D_MODEL)), _const_spec((D_MODEL, D_MODEL)),
                  _const_spec((2 * HEADS, D_MODEL)), _const_spec((2 * HEADS, 1)),
                  _const_spec((CONV_W, 2 * D_MODEL)), _const_spec((1, 2 * D_MODEL)), state],
        out_specs=[tok(D_MODEL),
                   pl.BlockSpec((None, tm // CHUNK, D_MODEL, CHUNK), lambda bi, i: (bi, i, 0, 0)),
                   tok(D_MODEL), tok(D_MODEL),
                   pl.BlockSpec((None, tm // CHUNK, 4 * HEADS, CHUNK), lambda bi, i: (bi, i, 0, 0)),
                   state],
        out_shape=[jax.ShapeDtypeStruct((b, l, D_MODEL), BF16),
                   jax.ShapeDtypeStruct((b, l // CHUNK, D_MODEL, CHUNK), BF16),
                   jax.ShapeDtypeStruct((b, l, D_MODEL), BF16), jax.ShapeDtypeStruct((b, l, D_MODEL), BF16),
                   jax.ShapeDtypeStruct((b, l // CHUNK, 4 * HEADS, CHUNK), F32),
                   jax.ShapeDtypeStruct((b, CONV_W - 1, 2 * D_MODEL), F32)],
        scratch_shapes=[pltpu.VMEM((tm + CONV_PAD, 2 * D_MODEL), F32)],
        compiler_params=_cparams(("parallel", "arbitrary")),
        name="inproj_a",
    )(x, g, wqk, wv, wo, wift, bif, cw, cb, cs)


def _inproj_b_kernel(x_ref, g_ref, wu_ref, wgv_ref, wmq_ref, wgate_ref, lng_ref, lnb_ref, ws_ref, bst_ref,
                     mk_ref, mv_ref, ug_ref, att_ref, gates_ref, *rest, tm, cl, emit_gv):
    xn = _rmsnorm(x_ref[...], g_ref[...]).astype(BF16)

    gates_ref[...] = _sigmoid(_dot(xn, wgate_ref[...])).astype(BF16)

    gvr = jax.nn.gelu(_dot(xn, wgv_ref[...]))
    mu = jnp.mean(gvr, axis=-1, keepdims=True)
    xc = gvr - mu
    r = lax.rsqrt(jnp.mean(xc * xc, axis=-1, keepdims=True) + EPS)
    gv = (xc * r) * lng_ref[...] + lnb_ref[...]
    if emit_gv:
        rest[0][...] = gv
    gvb = gv.astype(BF16)
    u = jax.nn.gelu(_dot(xn, wu_ref[...]))
    tri = (lax.broadcasted_iota(jnp.int32, (cl, cl), 0) >= lax.broadcasted_iota(jnp.int32, (cl, cl), 1))
    for gi in range(GMLP_GROUPS):
        wsg = jnp.where(tri, ws_ref[gi], 0.0).astype(BF16)
        lo, hi = gi * GMLP_GROUP_DIM, (gi + 1) * GMLP_GROUP_DIM
        for c in range(tm // cl):
            sp = _dot(wsg, gvb[c * cl:(c + 1) * cl, lo:hi]) + bst_ref[:, gi:gi + 1]
            ug_ref[c * cl:(c + 1) * cl, lo:hi] = (u[c * cl:(c + 1) * cl, lo:hi] * sp).astype(BF16)

    mq = _dot(xn, wmq_ref[...]).astype(BF16)
    for h in range(HEADS):
        lo, hi = h * HEAD_DIM, (h + 1) * HEAD_DIM
        sc = _dot_nt(mq[:, lo:hi], mk_ref[:, lo:hi]) * (HEAD_DIM ** -0.5)
        e = jnp.exp(sc - jnp.max(sc, axis=-1, keepdims=True))
        a = (e / jnp.sum(e, axis=-1, keepdims=True)).astype(BF16)
        att_ref[:, lo:hi] = _dot(a, mv_ref[:, lo:hi]).astype(BF16)


def _inproj_b(x, g, wu, wgv, wmq, wgate, lng, lnb, ws, bst, mk, mv, tm, cl, emit_gv):
    b, l, _ = x.shape
    nt = l // tm
    tok = lambda w: pl.BlockSpec((None, tm, w), lambda bi, i: (bi, i, 0))
    mem = pl.BlockSpec((None, N_MEM, D_MODEL), lambda bi, i: (bi, 0, 0))
    out_specs = [tok(D_MODEL), tok(D_MODEL), tok(3 * D_MODEL)]
    out_shape = [jax.ShapeDtypeStruct((b, l, D_MODEL), BF16), jax.ShapeDtypeStruct((b, l, D_MODEL), BF16),
                 jax.ShapeDtypeStruct((b, l, 3 * D_MODEL), BF16)]
    if emit_gv:
        out_specs.append(tok(D_MODEL))
        out_shape.append(jax.ShapeDtypeStruct((b, l, D_MODEL), F32))
    return pl.pallas_call(
        functools.partial(_inproj_b_kernel, tm=tm, cl=cl, emit_gv=emit_gv),
        grid=(b, nt),
        in_specs=[tok(D_MODEL), _const_spec((1, D_MODEL)), _const_spec((D_MODEL, D_MODEL)),
                  _const_spec((D_MODEL, D_MODEL)), _const_spec((D_MODEL, D_MODEL)),
                  _const_spec((D_MODEL, 3 * D_MODEL)), _const_spec((1, D_MODEL)), _const_spec((1, D_MODEL)),
                  _const_spec((GMLP_GROUPS, cl, cl)), _const_spec((cl, GMLP_GROUPS)), mem, mem],
        out_specs=out_specs,
        out_shape=out_shape,
        compiler_params=_cparams(("parallel", "parallel")),
        name="inproj_b",
    )(x, g, wu, wgv, wmq, wgate, lng, lnb, ws, bst, mk, mv)


def _mlstm_kernel(q_ref, kt_ref, v_ref, gt_ref, c0_ref, m0_ref, hm_ref, c_ref, m_ref, *, cb):
    i = pl.program_id(1)

    @pl.when(i == 0)
    def _():
        c_ref[...] = c0_ref[...]
        m_ref[...] = m0_ref[...]

    L = kt_ref.shape[-1]
    nch = cb // L
    ti = lax.broadcasted_iota(jnp.int32, (L, L), 0)
    si = lax.broadcasted_iota(jnp.int32, (L, L), 1)
    tri = ti >= si
    eye = ti == si

    rows = 4 * HEADS
    g_all = gt_ref[...].reshape(nch * rows, L)

    m_in = [m_ref[:, 0:1]]
    for c in range(nch):
        b_last4 = g_all[c * rows + HEADS:c * rows + 2 * HEADS, L - 1:L]
        amax4 = g_all[c * rows + 2 * HEADS:c * rows + 3 * HEADS, 0:1]
        m_in.append(jnp.maximum(b_last4 + m_in[-1], b_last4 + amax4))

    ones = jnp.ones((L, LANES), BF16)
    st = [c_ref[h] for h in range(HEADS)]
    for c in range(nch):
        r0, r1 = c * L, (c + 1) * L
        for h in range(HEADS):
            lo, hi = h * HEAD_DIM, (h + 1) * HEAD_DIM
            ig_r = g_all[c * rows + h:c * rows + h + 1, :]
            bc_r = g_all[c * rows + HEADS + h:c * rows + HEADS + h + 1, :]
            a_r = ig_r - bc_r
            bc_c = jnp.sum(jnp.where(eye, bc_r, 0.0), axis=-1, keepdims=True)
            m0 = m_in[c][h:h + 1, :]
            m_last = m_in[c + 1][h:h + 1, :]
            dmat = jnp.where(tri, bc_c + a_r, NEG_INF)
            inter = bc_c + m0
            m = jnp.maximum(inter, jnp.max(dmat, axis=-1, keepdims=True))
            w_intra = jnp.exp(dmat - m)
            w_inter = jnp.exp(inter - m)
            q = q_ref[r0:r1, lo:hi]
            kt = kt_ref[c, lo:hi, :]
            v = v_ref[r0:r1, lo:hi]
            s = _dot(q, kt) * w_intra
            qs = _dot(q, st[h].astype(BF16))
            num = w_inter * qs[:, :HEAD_DIM] + _dot(s.astype(BF16), v)
            den = w_inter * qs[:, HEAD_DIM:HEAD_DIM + 1] + jnp.sum(s, axis=-1, keepdims=True)
            hh = num / jnp.maximum(jnp.abs(den), jnp.exp(-m))
            bc_last = bc_r[:, L - 1:L]
            w_last = jnp.exp(bc_last + a_r - m_last)
            decay = jnp.exp(bc_last + m0 - m_last)
            ktw = (kt.astype(F32) * w_last).astype(BF16)
            st[h] = decay * st[h] + _dot(ktw, jnp.concatenate([v, ones], axis=1))
            hm_ref[r0:r1, lo:hi] = hh.astype(BF16)

    for h in range(HEADS):
        c_ref[h] = st[h]
    m_ref[...] = jnp.broadcast_to(m_in[nch], (HEADS, LANES))


def _mlstm(q, kt, v, gt, c0, m0, cb):
    b, l, _ = q.shape
    nt = l // cb
    CHUNK = kt.shape[-1]
    tok = pl.BlockSpec((None, cb, D_MODEL), lambda bi, i: (bi, i, 0))
    cs = pl.BlockSpec((None, HEADS, HEAD_DIM, HEAD_DIM + LANES), lambda bi, i: (bi, 0, 0, 0))
    ms = pl.BlockSpec((None, HEADS, LANES), lambda bi, i: (bi, 0, 0))
    return pl.pallas_call(
        functools.partial(_mlstm_kernel, cb=cb),
        grid=(b, nt),
        in_specs=[tok, pl.BlockSpec((None, cb // CHUNK, D_MODEL, CHUNK), lambda bi, i: (bi, i, 0, 0)), tok,
                  pl.BlockSpec((None, cb // CHUNK, 4 * HEADS, CHUNK), lambda bi, i: (bi, i, 0, 0)),
                  cs, ms],
        out_specs=[tok, cs, ms],
        out_shape=[jax.ShapeDtypeStruct((b, l, D_MODEL), BF16),
                   jax.ShapeDtypeStruct((b, HEADS, HEAD_DIM, HEAD_DIM + LANES), F32),
                   jax.ShapeDtypeStruct((b, HEADS, LANES), F32)],
        compiler_params=_cparams(("parallel", "arbitrary")),
        name="mlstm",
    )(q, kt, v, gt, c0, m0)


def _merge_kernel(x_ref, hh_ref, o_ref, ng_ref, ug_ref, att_ref, gates_ref, wa_ref, wb_ref, wc_ref, wo_ref, h_ref):
    parts = []
    for h in range(HEADS):
        lo, hi = h * HEAD_DIM, (h + 1) * HEAD_DIM
        hh = hh_ref[:, lo:hi].astype(F32)
        hn = hh * lax.rsqrt(jnp.mean(hh * hh, axis=-1, keepdims=True) + EPS)
        parts.append(((hn * ng_ref[:, lo:hi]) * o_ref[:, lo:hi].astype(F32)).astype(BF16))
    hm = jnp.concatenate(parts, axis=1)
    g = gates_ref[...].astype(F32)
    mixed = g[:, :D_MODEL] * _dot(hm, wa_ref[...])
    mixed = mixed + g[:, D_MODEL:2 * D_MODEL] * _dot(ug_ref[...], wb_ref[...])
    mixed = mixed + g[:, 2 * D_MODEL:] * _dot(att_ref[...], wc_ref[...])
    h_ref[...] = x_ref[...] + _dot(mixed.astype(BF16), wo_ref[...])


def _merge(x2d, hh, osig, ng, ug, att, gates, wa, wb, wc, wo, tm):
    n = x2d.shape[0]
    row = lambda w: pl.BlockSpec((tm, w), lambda i: (i, 0))
    wspec = _const_spec((D_MODEL, D_MODEL))
    return pl.pallas_call(
        _merge_kernel,
        grid=(n // tm,),
        in_specs=[row(D_MODEL), row(D_MODEL), row(D_MODEL), _const_spec((1, D_MODEL)), row(D_MODEL), row(D_MODEL),
                  row(3 * D_MODEL), wspec, wspec, wspec, wspec],
        out_specs=row(D_MODEL),
        out_shape=jax.ShapeDtypeStruct((n, D_MODEL), F32),
        compiler_params=_cparams(("parallel",)),
        name="merge",
    )(x2d, hh, osig, ng, ug, att, gates, wa, wb, wc, wo)


MOE_T = 256
MOE_CAP = 32
MOE_S = N_EXPERTS * MOE_CAP
MOE_OVF = 512
SEG_ALIGN = 16
EXPERT_BLOCK = 128


def _moe_route_kernel(hp_ref, hs_ref, g_ref, wrg_ref, brg_ref, wre_ref, bre_ref,
                      main_ref, ovf_ref, info_ref, meta_ref, *, nsp):
    t = MOE_T
    h = jnp.where(pl.program_id(0) < nsp, hp_ref[...], hs_ref[...])
    xm = _rmsnorm(h, g_ref[...]).astype(BF16)
    lane = lax.broadcasted_iota(jnp.int32, (t, LANES), 1)
    lg = jnp.where(lane < N_GROUPS, _dot(xm, wrg_ref[...]) + brg_ref[...], NEG_INF)
    gmax = jnp.max(lg, axis=-1, keepdims=True)
    p_top = 1.0 / jnp.sum(jnp.exp(lg - gmax), axis=-1, keepdims=True)
    grp = jnp.min(jnp.where(lg == gmax, lane, LANES), axis=-1, keepdims=True)
    el = _dot(xm, wre_ref[...]) + bre_ref[...]
    in_grp = (lane >= grp * EXPERTS_PER_GROUP) & (lane < (grp + 1) * EXPERTS_PER_GROUP)
    vals = jnp.where(in_grp, el, NEG_INF)
    v1 = jnp.max(vals, axis=-1, keepdims=True)
    i1 = jnp.min(jnp.where(vals == v1, lane, LANES), axis=-1, keepdims=True)
    vals2 = jnp.where(lane == i1, NEG_INF, vals)
    v2 = jnp.max(vals2, axis=-1, keepdims=True)
    i2 = jnp.min(jnp.where(vals2 == v2, lane, LANES), axis=-1, keepdims=True)
    r = jnp.exp(v2 - v1)
    p1 = p_top / (1.0 + r)
    p2 = p_top * r / (1.0 + r)

    sel1 = lane == i1
    sel2 = lane == i2
    onehot = jnp.where(sel1 | sel2, 1.0, 0.0)
    cnt = jnp.sum(onehot, axis=0, keepdims=True).astype(jnp.int32)
    pn = jnp.bitwise_and(jnp.maximum(cnt - MOE_CAP, 0) + (SEG_ALIGN - 1), -SEG_ALIGN)
    pn8 = jnp.broadcast_to(pn, (8, LANES))
    lane8 = lax.broadcasted_iota(jnp.int32, (8, LANES), 1)
    inc = pn8
    for sh in (1, 2, 4, 8, 16):
        inc = inc + jnp.where(lane8 >= sh, pltpu.roll(inc, sh, axis=1), 0)
    off8 = inc - pn8
    row8 = lax.broadcasted_iota(jnp.int32, (8, LANES), 0)
    meta_ref[...] = jnp.where(row8 == 0, off8, jnp.where(row8 == 1, pn8, 0))

    ti = lax.broadcasted_iota(jnp.int32, (t, t), 0)
    si = lax.broadcasted_iota(jnp.int32, (t, t), 1)
    before = jnp.where(ti > si, 1.0, 0.0).astype(BF16)
    rank = _dot(before, onehot.astype(BF16))
    off_f = off8[0:1, :].astype(F32)

    def slot_row(sel, idx):
        rk = jnp.sum(jnp.where(sel, rank, 0.0), axis=-1, keepdims=True)
        of = jnp.sum(jnp.where(sel, off_f, 0.0), axis=-1, keepdims=True)
        return jnp.where(rk < MOE_CAP, idx.astype(F32) * MOE_CAP + rk, MOE_S - MOE_CAP + of + rk)

    pos1 = slot_row(sel1, i1)
    pos2 = slot_row(sel2, i2)
    info_ref[...] = (jnp.where(lane == 0, pos1, 0.0) + jnp.where(lane == 1, pos2, 0.0)
                     + jnp.where(lane == 2, p1, 0.0) + jnp.where(lane == 3, p2, 0.0))

    eye = ti == si
    pos1_r = jnp.sum(jnp.where(eye, pos1, 0.0), axis=0, keepdims=True)
    pos2_r = jnp.sum(jnp.where(eye, pos2, 0.0), axis=0, keepdims=True)

    def gather(rows, base):
        srow = (lax.broadcasted_iota(jnp.int32, (rows, t), 0) + base).astype(F32)
        pick = jnp.where((srow == pos1_r) | (srow == pos2_r), 1.0, 0.0).astype(BF16)
        return _dot(pick, xm).astype(BF16)

    main_ref[...] = gather(MOE_S, 0)
    has_ovf = jnp.sum(pn) > 0

    @pl.when(has_ovf)
    def _():
        ovf_ref[...] = gather(MOE_OVF, MOE_S)

    @pl.when(jnp.logical_not(has_ovf))
    def _():
        ovf_ref[...] = jnp.zeros_like(ovf_ref)


def _two_group_specs(nsp, cols):
    return (pl.BlockSpec((MOE_T, cols), lambda j, *_: (jnp.minimum(j, nsp - 1), 0)),
            pl.BlockSpec((MOE_T, cols), lambda j, *_: (jnp.maximum(j - nsp, 0), 0)))


def _moe_route(hp, hs, g, wrg, brg, wre, bre):
    nsp = hp.shape[0] // MOE_T
    ns = nsp + hs.shape[0] // MOE_T
    return pl.pallas_call(
        functools.partial(_moe_route_kernel, nsp=nsp),
        grid=(ns,),
        in_specs=[*_two_group_specs(nsp, D_MODEL), _const_spec((1, D_MODEL)),
                  _const_spec((D_MODEL, LANES)), _const_spec((1, LANES)),
                  _const_spec((D_MODEL, LANES)), _const_spec((1, LANES))],
        out_specs=[pl.BlockSpec((None, MOE_S, D_MODEL), lambda j: (j, 0, 0)),
                   pl.BlockSpec((None, MOE_OVF, D_MODEL), lambda j: (j, 0, 0)),
                   pl.BlockSpec((MOE_T, LANES), lambda j: (j, 0)),
                   pl.BlockSpec((None, 8, LANES), lambda j: (j, 0, 0))],
        out_shape=[jax.ShapeDtypeStruct((ns, MOE_S, D_MODEL), BF16),
                   jax.ShapeDtypeStruct((ns, MOE_OVF, D_MODEL), BF16),
                   jax.ShapeDtypeStruct((ns * MOE_T, LANES), F32),
                   jax.ShapeDtypeStruct((ns, 8, LANES), jnp.int32)],
        compiler_params=_cparams(("arbitrary",)),
        name="moe_route",
    )(hp, hs, g, wrg, brg, wre, bre)


def _swiglu(x, wgu, wdb):
    gu = _dot(x, wgu[...])
    gate = gu[:, :D_EXPERT]
    hid = (gate * _sigmoid(gate)) * gu[:, D_EXPERT:]
    return _dot(hid.astype(BF16), wdb[...]).astype(BF16)


def _moe_expert_kernel(offs_ref, pns_ref, eflag_ref, main_ref, slots_hbm, wg_ref, wu_ref, wd_ref,
                       mout_ref, out_hbm, xbuf, obuf, wgu, wdb, sem_in, sem_out, *, n_sub, jb, n_rb):
    e = pl.program_id(0)
    rb = pl.program_id(1)

    @pl.when((e == 0) & (rb == 0))
    def _():
        xbuf[...] = jnp.zeros_like(xbuf)

    @pl.when(rb == 0)
    def _():
        wgu[:, :D_EXPERT] = wg_ref[...].astype(BF16)
        wgu[:, D_EXPERT:] = wu_ref[...].astype(BF16)
        wdb[...] = wd_ref[...].astype(BF16)

    sb = main_ref.shape[0]
    y = _swiglu(main_ref[...].reshape(sb * MOE_CAP, D_MODEL), wgu, wdb)
    mout_ref[...] = y.reshape(sb, MOE_CAP, D_MODEL)

    def copy_in(j, src, dst):
        return pltpu.make_async_copy(slots_hbm.at[j, pl.ds(src, SEG_ALIGN), :],
                                     xbuf.at[pl.ds(dst, SEG_ALIGN), :], sem_in)

    def copy_out(j, src, dst):
        return pltpu.make_async_copy(obuf.at[pl.ds(src, SEG_ALIGN), :],
                                     out_hbm.at[j, pl.ds(dst, SEG_ALIGN), :], sem_out)

    def for_each_chunk(g, fn):
        def seg(jj, cur):
            j = g * jb + jj
            n = pns_ref[j * N_EXPERTS + e]
            off = offs_ref[j * N_EXPERTS + e]

            def chunk(k, c):
                fn(j, pl.multiple_of(off + k * SEG_ALIGN, SEG_ALIGN), pl.multiple_of(cur + k * SEG_ALIGN, SEG_ALIGN))
                return c
            lax.fori_loop(0, n // SEG_ALIGN, chunk, 0)
            return cur + n
        return lax.fori_loop(0, jb, seg, 0)

    def group(g, carry):
        total = for_each_chunk(g, lambda j, r, b: copy_in(j, r, b).start())
        nchunk = total // SEG_ALIGN

        def wait_in(k, c):
            copy_in(0, 0, 0).wait()
            return c
        lax.fori_loop(0, nchunk, wait_in, 0)

        def block(bi, c):
            r0 = pl.multiple_of(bi * EXPERT_BLOCK, EXPERT_BLOCK)
            obuf[pl.ds(r0, EXPERT_BLOCK), :] = _swiglu(xbuf[pl.ds(r0, EXPERT_BLOCK), :], wgu, wdb)
            return c
        lax.fori_loop(0, (total + EXPERT_BLOCK - 1) // EXPERT_BLOCK, block, 0)

        for_each_chunk(g, lambda j, r, b: copy_out(j, b, r).start())

        def wait_out(k, c):
            copy_out(0, 0, 0).wait()
            return c
        lax.fori_loop(0, nchunk, wait_out, 0)
        return carry

    @pl.when((rb == n_rb - 1) & (eflag_ref[e] > 0))
    def _():
        lax.fori_loop(0, n_sub // jb, group, 0)


def _moe_expert(main, ovf, offs, pns, eflag, wg, wu, wd):
    ns = main.shape[0]
    n_rb = 2 if ns % 2 == 0 else 1
    sb = ns // n_rb
    jb = max(d for d in range(1, 12) if ns % d == 0)
    rows = jb * MOE_T + EXPERT_BLOCK
    main4 = main.reshape(ns, N_EXPERTS, MOE_CAP, D_MODEL)
    mspec = pl.BlockSpec((sb, None, MOE_CAP, D_MODEL), lambda e, rb, *_: (rb, e, 0, 0))
    wspec = lambda a, b: pl.BlockSpec((None, a, b), lambda e, rb, *_: (e, 0, 0))
    grid_spec = pltpu.PrefetchScalarGridSpec(
        num_scalar_prefetch=3,
        grid=(N_EXPERTS, n_rb),
        in_specs=[mspec, pl.BlockSpec(memory_space=pl.ANY),
                  wspec(D_MODEL, D_EXPERT), wspec(D_MODEL, D_EXPERT), wspec(D_EXPERT, D_MODEL)],
        out_specs=[mspec, pl.BlockSpec(memory_space=pl.ANY)],
        scratch_shapes=[pltpu.VMEM((rows, D_MODEL), BF16), pltpu.VMEM((rows, D_MODEL), BF16),
                        pltpu.VMEM((D_MODEL, 2 * D_EXPERT), BF16), pltpu.VMEM((D_EXPERT, D_MODEL), BF16),
                        pltpu.SemaphoreType.DMA(()), pltpu.SemaphoreType.DMA(())],
    )
    mout, oout = pl.pallas_call(
        functools.partial(_moe_expert_kernel, n_sub=ns, jb=jb, n_rb=n_rb),
        grid_spec=grid_spec,
        out_shape=[jax.ShapeDtypeStruct(main4.shape, main4.dtype), jax.ShapeDtypeStruct(ovf.shape, ovf.dtype)],
        input_output_aliases={3: 0, 4: 1},
        compiler_params=_cparams(("arbitrary", "arbitrary")),
        name="moe_expert",
    )(offs, pns, eflag, main4, ovf, wg, wu, wd)
    return mout.reshape(main.shape), oout


def _moe_combine_kernel(jflag_ref, hp_ref, hs_ref, main_ref, ovf_ref, info_ref, gf_ref, yp_ref, ys_ref, acc_ref,
                        *, nsp):
    j = pl.program_id(0)
    info = info_ref[...]

    def weights(cols, base):
        scol = (lax.broadcasted_iota(jnp.int32, (MOE_T, cols), 1) + base).astype(F32)
        return (jnp.where(scol == info[:, 0:1], info[:, 2:3], 0.0)
                + jnp.where(scol == info[:, 1:2], info[:, 3:4], 0.0)).astype(BF16)

    h = jnp.where(j < nsp, hp_ref[...], hs_ref[...])
    acc_ref[...] = h + _dot(weights(MOE_S, 0), main_ref[...])

    @pl.when(jflag_ref[j] > 0)
    def _():
        acc_ref[...] += _dot(weights(MOE_OVF, MOE_S), ovf_ref[...])

    y = _rmsnorm(acc_ref[...], gf_ref[...])

    @pl.when(j < nsp)
    def _():
        yp_ref[...] = y

    @pl.when(j >= nsp)
    def _():
        ys_ref[...] = y


def _moe_combine(hp, hs, main, ovf, info, jflag, gf):
    nsp = hp.shape[0] // MOE_T
    ns = main.shape[0]
    grid_spec = pltpu.PrefetchScalarGridSpec(
        num_scalar_prefetch=1,
        grid=(ns,),
        in_specs=[*_two_group_specs(nsp, D_MODEL),
                  pl.BlockSpec((None, MOE_S, D_MODEL), lambda j, jf: (j, 0, 0)),
                  pl.BlockSpec((None, MOE_OVF, D_MODEL), lambda j, jf: (jnp.where(jf[j] > 0, j, 0), 0, 0)),
                  pl.BlockSpec((MOE_T, LANES), lambda j, jf: (j, 0)),
                  pl.BlockSpec((1, D_MODEL), lambda j, jf: (0, 0))],
        out_specs=list(_two_group_specs(nsp, D_MODEL)),
        scratch_shapes=[pltpu.VMEM((MOE_T, D_MODEL), F32)],
    )
    return pl.pallas_call(
        functools.partial(_moe_combine_kernel, nsp=nsp),
        grid_spec=grid_spec,
        out_shape=[jax.ShapeDtypeStruct(hp.shape, F32), jax.ShapeDtypeStruct(hs.shape, F32)],
        compiler_params=_cparams(("arbitrary",)),
        name="moe_combine",
    )(jflag, hp, hs, main, ovf, info, gf)


def _moe(hp, hs, p):
    main, ovf, info, meta = _moe_route(hp, hs, p["norm_moe_g"], p["w_rg"], p["b_rg"], p["w_re"], p["b_re"])
    pn = meta[:, 1, :N_EXPERTS]
    offs = meta[:, 0, :N_EXPERTS].reshape(-1)
    eflag = (jnp.sum(pn, axis=0) > 0).astype(jnp.int32)
    jflag = (jnp.sum(pn, axis=1) > 0).astype(jnp.int32)
    main, ovf = _moe_expert(main, ovf, offs, pn.reshape(-1), eflag, p["w_eg"], p["w_eu"], p["w_ed"])
    return _moe_combine(hp, hs, main, ovf, info, jflag, p["norm_final_g"])


def _trunk(x, mem_k, mem_v, c0, n0, m0, conv0, p, *, tm, cb, cl, emit_gv):
    b, l, _ = x.shape
    q, kt, v, osig, gt, conv_new = _inproj_a(
        x, p["norm_mix_g"], p["w_qk"], p["w_v"], p["w_o"], p["w_ift"], p["b_if"],
        p["conv_w"], p["conv_b"], conv0, tm)
    outs = _inproj_b(x, p["norm_mix_g"], p["w_u"], p["w_gv"], p["w_mq"], p["w_gate"],
                     p["gmlp_norm_g"], p["gmlp_norm_b"], p["w_s"][:, :cl, :cl], p["b_st"][:cl],
                     mem_k, mem_v, tm, cl, emit_gv)
    ug, att, gates = outs[:3]
    m0b = jnp.broadcast_to(m0[..., None], m0.shape + (LANES,))
    st0 = jnp.concatenate([c0, jnp.broadcast_to(n0[..., None], n0.shape + (LANES,))], axis=-1)
    hh, st1, m1 = _mlstm(q, kt, v, gt, st0, m0b, cb)
    c1, n1 = st1[..., :HEAD_DIM], st1[..., HEAD_DIM]
    n = b * l
    h = _merge(x.reshape(n, D_MODEL), hh.reshape(n, D_MODEL), osig.reshape(n, D_MODEL), p["mlstm_norm_g"],
               ug.reshape(n, D_MODEL), att.reshape(n, D_MODEL), gates.reshape(n, 3 * D_MODEL),
               p["w_br_mlstm"], p["w_br_gmlp"], p["w_br_mem"], p["w_out"], 512)
    return h, c1, n1, m1[..., 0], conv_new, (outs[3] if emit_gv else None)


def kernel(x_prompt, x_sample, mem_prompt, cache_mem_k, cache_mem_v, state_mlstm_C, state_mlstm_n, state_mlstm_m, state_mlstm_conv, norm_mix_g, w_in, mlstm_i_b, mlstm_f_b, mlstm_conv_w, mlstm_conv_b, mlstm_norm_g, gmlp_norm_g, gmlp_norm_b, gmlp_w_s, gmlp_b_s, mem_norm_g, w_mem_k, w_mem_v, w_br_mlstm, w_br_gmlp, w_br_mem, w_out, norm_moe_g, w_router_group, b_router_group, w_router_expert, b_router_expert, w_exp_gate, w_exp_up, w_exp_down, norm_final_g):
    bp = x_prompt.shape[0]
    bs = x_sample.shape[0]
    W = D_MODEL
    wi = w_in[0]
    o_qk, o_v, o_o, o_i = 0, 2 * W, 3 * W, 4 * W
    o_f = o_i + HEADS
    o_u = o_f + HEADS
    o_gv, o_mq, o_gate = o_u + W, o_u + 2 * W, o_u + 3 * W
    row = lambda a: a.reshape(1, -1)
    pad_l = lambda a: jnp.pad(a, ((0, 0), (0, LANES - a.shape[1])))
    p = {
        "norm_mix_g": row(norm_mix_g[0]),
        "w_qk": wi[:, o_qk:o_v].astype(BF16), "w_v": wi[:, o_v:o_o].astype(BF16),
        "w_o": wi[:, o_o:o_i].astype(BF16),
        "w_ift": wi[:, o_i:o_u].T.astype(BF16),
        "b_if": jnp.concatenate([mlstm_i_b[0], mlstm_f_b[0]]).reshape(2 * HEADS, 1),
        "conv_w": mlstm_conv_w[0], "conv_b": row(mlstm_conv_b[0]),
        "w_u": wi[:, o_u:o_gv].astype(BF16), "w_gv": wi[:, o_gv:o_mq].astype(BF16),
        "w_mq": wi[:, o_mq:o_gate].astype(BF16), "w_gate": wi[:, o_gate:].astype(BF16),
        "gmlp_norm_g": row(gmlp_norm_g[0]), "gmlp_norm_b": row(gmlp_norm_b[0]),
        "w_s": gmlp_w_s[0], "b_st": gmlp_b_s[0].T,
        "mlstm_norm_g": row(mlstm_norm_g[0]),
        "w_br_mlstm": w_br_mlstm[0].astype(BF16), "w_br_gmlp": w_br_gmlp[0].astype(BF16),
        "w_br_mem": w_br_mem[0].astype(BF16), "w_out": w_out[0].astype(BF16),
        "norm_moe_g": row(norm_moe_g[0]),
        "w_rg": pad_l(w_router_group[0]).astype(BF16), "b_rg": pad_l(row(b_router_group[0])),
        "w_re": pad_l(w_router_expert[0]).astype(BF16), "b_re": pad_l(row(b_router_expert[0])),
        "w_eg": w_exp_gate[0], "w_eu": w_exp_up[0], "w_ed": w_exp_down[0],
        "norm_final_g": row(norm_final_g),
    }

    mk_p, mv_p = _memory_kv(mem_prompt.reshape(bp * N_MEM, W), row(mem_norm_g[0]),
                            w_mem_k[0].astype(BF16), w_mem_v[0].astype(BF16))
    mk_p3, mv_p3 = mk_p.reshape(bp, N_MEM, W), mv_p.reshape(bp, N_MEM, W)

    zeros = lambda *s: jnp.zeros(s, F32)
    hp, cp, np_, mp, cvp, _ = _trunk(
        x_prompt, mk_p3.astype(BF16), mv_p3.astype(BF16),
        zeros(bp, HEADS, HEAD_DIM, HEAD_DIM), zeros(bp, HEADS, HEAD_DIM), zeros(bp, HEADS),
        zeros(bp, CONV_W - 1, 2 * W), p, tm=256, cb=512, cl=GMLP_CHUNK, emit_gv=False)
    ls = x_sample.shape[1]
    hs, cs, ns, ms, cvs, gvs = _trunk(
        x_sample, cache_mem_k[0].reshape(bs, N_MEM, W).astype(BF16),
        cache_mem_v[0].reshape(bs, N_MEM, W).astype(BF16),
        state_mlstm_C[0], state_mlstm_n[0], state_mlstm_m[0], state_mlstm_conv[0], p,
        tm=ls, cb=ls, cl=min(ls, GMLP_CHUNK), emit_gv=True)

    yp, ys = _moe(hp, hs, p)
    kv_shape = (1, bp, N_MEM, HEADS, HEAD_DIM)
    return (yp.reshape(x_prompt.shape), ys.reshape(x_sample.shape), mk_p.reshape(kv_shape), mv_p.reshape(kv_shape),
            cp[None], np_[None], mp[None], cvp[None],
            cs[None], ns[None], ms[None], cvs[None], gvs[None])
```

```python
import functools

import jax
import jax.numpy as jnp
from jax import lax
from jax.experimental import pallas as pl
from jax.experimental.pallas import tpu as pltpu

D_MODEL = 1024
MLSTM_BLOCK = 128
EPS = 1e-6
HEADS = 4
HEAD_DIM = 256
CONV_W = 4
GMLP_GROUPS = 4
GMLP_GROUP_DIM = 256
GMLP_CHUNK = 128
N_MEM = 256
N_GROUPS = 4
EXPERTS_PER_GROUP = 8
N_EXPERTS = 32
D_EXPERT = 256
LANES = 128
CONV_PAD = 8

F32 = jnp.float32
BF16 = jnp.bfloat16
NEG_INF = float("-inf")

VMEM_LIMIT = 56 * 1024 * 1024


def _cparams(sem):
    return pltpu.CompilerParams(dimension_semantics=sem, vmem_limit_bytes=VMEM_LIMIT)


def _const_spec(shape):
    nd = len(shape)
    return pl.BlockSpec(shape, lambda *_: (0,) * nd, pipeline_mode=pl.Buffered(1))


def _sigmoid(x):
    return 0.5 * (jnp.tanh(0.5 * x) + 1.0)


def _log_sigmoid(x):
    return jnp.minimum(x, 0.0) - jnp.log(1.0 + jnp.exp(-jnp.abs(x)))


def _rmsnorm(x, g):
    r = lax.rsqrt(jnp.mean(x * x, axis=-1, keepdims=True) + EPS)
    return (x * r) * g


def _dot(a, b):
    return jnp.dot(a, b, preferred_element_type=F32)


def _dot_nt(a, b):
    return lax.dot_general(a, b, (((1,), (1,)), ((), ())), preferred_element_type=F32)


def _dot_tn(a, b):
    return lax.dot_general(a, b, (((0,), (0,)), ((), ())), preferred_element_type=F32)


def _memkv_kernel(mem_ref, g_ref, wk_ref, wv_ref, k_ref, v_ref):
    mn = _rmsnorm(mem_ref[...], g_ref[...]).astype(BF16)
    k_ref[...] = _dot(mn, wk_ref[...])
    v_ref[...] = _dot(mn, wv_ref[...])


def _memory_kv(mem2d, g, wk, wv):
    n = mem2d.shape[0]
    tm = 512
    row = pl.BlockSpec((tm, D_MODEL), lambda i: (i, 0))
    return pl.pallas_call(
        _memkv_kernel,
        grid=(n // tm,),
        in_specs=[row, _const_spec((1, D_MODEL)), _const_spec((D_MODEL, D_MODEL)),
                  _const_spec((D_MODEL, D_MODEL))],
        out_specs=[row, row],
        out_shape=[jax.ShapeDtypeStruct((n, D_MODEL), F32)] * 2,
        compiler_params=_cparams(("parallel",)),
        name="memory_kv",
    )(mem2d, g, wk, wv)


def _inproj_a_kernel(x_ref, g_ref, wqk_ref, wv_ref, wo_ref, wift_ref, bif_ref, cw_ref, cb_ref, cs_ref,
                     q_ref, kt_ref, v_ref, o_ref, gt_ref, cn_ref, ext_ref, *, tm, lc):
    i = pl.program_id(1)

    @pl.when(i == 0)
    def _():
        ext_ref[0:CONV_PAD, :] = jnp.zeros((CONV_PAD, 2 * D_MODEL), F32)
        ext_ref[CONV_PAD - (CONV_W - 1):CONV_PAD, :] = cs_ref[...]

    xn = _rmsnorm(x_ref[...], g_ref[...]).astype(BF16)

    ext_ref[CONV_PAD:CONV_PAD + tm, :] = _dot(xn, wqk_ref[...])
    acc = cb_ref[...] + ext_ref[CONV_PAD:CONV_PAD + tm, :] * cw_ref[CONV_W - 1:CONV_W, :]
    for j in range(CONV_W - 1):
        s = CONV_PAD - (CONV_W - 1) + j
        acc = acc + ext_ref[s:s + tm, :] * cw_ref[j:j + 1, :]
    qk = acc * _sigmoid(acc)
    q_ref[...] = (qk[:, :D_MODEL] * (HEAD_DIM ** -0.5)).astype(BF16)
    k = qk[:, D_MODEL:]
    if tm % LANES:
        k = jnp.concatenate([k, jnp.zeros((LANES - tm % LANES, D_MODEL), F32)], axis=0)
    kt = k.T.astype(BF16)
    for c in range(tm // lc):
        kt_ref[c] = kt[:, c * lc:(c + 1) * lc]
    cn_ref[...] = ext_ref[tm + CONV_PAD - (CONV_W - 1):tm + CONV_PAD, :]
    ext_ref[0:CONV_PAD, :] = ext_ref[tm:tm + CONV_PAD, :]

    v_ref[...] = _dot(xn, wv_ref[...]).astype(BF16)
    o_ref[...] = _sigmoid(_dot(xn, wo_ref[...])).astype(BF16)

    nch = tm // lc
    z = jnp.concatenate([_dot_nt(wift_ref[...], xn[c * lc:(c + 1) * lc, :]) + bif_ref[...]
                         for c in range(nch)], axis=0)
    is_ig = (lax.broadcasted_iota(jnp.int32, z.shape, 0) % (2 * HEADS)) < HEADS
    g = jnp.where(is_ig, z, _log_sigmoid(z))
    upper = jnp.where(lax.broadcasted_iota(jnp.int32, (lc, lc), 0)
                      <= lax.broadcasted_iota(jnp.int32, (lc, lc), 1), 1.0, 0.0)
    bc = jnp.dot(g, upper, preferred_element_type=F32, precision=lax.Precision.HIGHEST)
    a = g - pltpu.roll(bc, nch * 2 * HEADS - HEADS, axis=0)
    amax = jnp.broadcast_to(jnp.max(a, axis=-1, keepdims=True), z.shape)
    gb = jnp.where(is_ig, g, bc)
    for c in range(nch):
        gt_ref[c, 0:2 * HEADS, :] = gb[c * 8:(c + 1) * 8, :]
        gt_ref[c, 2 * HEADS:4 * HEADS, :] = amax[c * 8:(c + 1) * 8, :]


def _inproj_a(x, g, wqk, wv, wo, wift, bif, cw, cb, cs, tm):
    b, l, _ = x.shape
    nt = l // tm
    CHUNK = min(MLSTM_BLOCK, l)
    tok = lambda w: pl.BlockSpec((None, tm, w), lambda bi, i: (bi, i, 0))
    state = pl.BlockSpec((None, CONV_W - 1, 2 * D_MODEL), lambda bi, i: (bi, 0, 0))
    return pl.pallas_call(
        functools.partial(_inproj_a_kernel, tm=tm, lc=CHUNK),
        grid=(b, nt),
        in_specs=[tok(D_MODEL), _const_spec((1, D_MODEL)), _const_spec((D_MODEL, 2 * D_MODEL)),
                  _const_spec((D_MODEL, D_MODEL)), _const_spec((D_MODEL, D_MODEL)),
                  _const_spec((2 * HEADS, D_MODEL)), _const_spec((2 * HEADS, 1)),
                  _const_spec((CONV_W, 2 * D_MODEL)), _const_spec((1, 2 * D_MODEL)), state],
        out_specs=[tok(D_MODEL),
                   pl.BlockSpec((None, tm // CHUNK, D_MODEL, CHUNK), lambda bi, i: (bi, i, 0, 0)),
                   tok(D_MODEL), tok(D_MODEL),
                   pl.BlockSpec((None, tm // CHUNK, 4 * HEADS, CHUNK), lambda bi, i: (bi, i, 0, 0)),
                   state],
        out_shape=[jax.ShapeDtypeStruct((b, l, D_MODEL), BF16),
                   jax.ShapeDtypeStruct((b, l // CHUNK, D_MODEL, CHUNK), BF16),
                   jax.ShapeDtypeStruct((b, l, D_MODEL), BF16), jax.ShapeDtypeStruct((b, l, D_MODEL), BF16),
                   jax.ShapeDtypeStruct((b, l // CHUNK, 4 * HEADS, CHUNK), F32),
                   jax.ShapeDtypeStruct((b, CONV_W - 1, 2 * D_MODEL), F32)],
        scratch_shapes=[pltpu.VMEM((tm + CONV_PAD, 2 * D_MODEL), F32)],
        compiler_params=_cparams(("parallel", "arbitrary")),
        name="inproj_a",
    )(x, g, wqk, wv, wo, wift, bif, cw, cb, cs)


def _inproj_b_kernel(x_ref, g_ref, wu_ref, wgv_ref, wmq_ref, wgate_ref, lng_ref, lnb_ref, ws_ref, bst_ref,
                     mk_ref, mv_ref, ug_ref, att_ref, gates_ref, *rest, tm, cl, emit_gv):
    xn = _rmsnorm(x_ref[...], g_ref[...]).astype(BF16)

    gates_ref[...] = _sigmoid(_dot(xn, wgate_ref[...])).astype(BF16)

    gvr = jax.nn.gelu(_dot(xn, wgv_ref[...]))
    mu = jnp.mean(gvr, axis=-1, keepdims=True)
    xc = gvr - mu
    r = lax.rsqrt(jnp.mean(xc * xc, axis=-1, keepdims=True) + EPS)
    gv = (xc * r) * lng_ref[...] + lnb_ref[...]
    if emit_gv:
        rest[0][...] = gv
    gvb = gv.astype(BF16)
    u = jax.nn.gelu(_dot(xn, wu_ref[...]))
    tri = (lax.broadcasted_iota(jnp.int32, (cl, cl), 0) >= lax.broadcasted_iota(jnp.int32, (cl, cl), 1))
    for gi in range(GMLP_GROUPS):
        wsg = jnp.where(tri, ws_ref[gi], 0.0).astype(BF16)
        lo, hi = gi * GMLP_GROUP_DIM, (gi + 1) * GMLP_GROUP_DIM
        for c in range(tm // cl):
            sp = _dot(wsg, gvb[c * cl:(c + 1) * cl, lo:hi]) + bst_ref[:, gi:gi + 1]
            ug_ref[c * cl:(c + 1) * cl, lo:hi] = (u[c * cl:(c + 1) * cl, lo:hi] * sp).astype(BF16)

    mq = _dot(xn, wmq_ref[...]).astype(BF16)
    for h in range(HEADS):
        lo, hi = h * HEAD_DIM, (h + 1) * HEAD_DIM
        sc = _dot_nt(mq[:, lo:hi], mk_ref[:, lo:hi]) * (HEAD_DIM ** -0.5)
        e = jnp.exp(sc - jnp.max(sc, axis=-1, keepdims=True))
        a = (e / jnp.sum(e, axis=-1, keepdims=True)).astype(BF16)
        att_ref[:, lo:hi] = _dot(a, mv_ref[:, lo:hi]).astype(BF16)


def _inproj_b(x, g, wu, wgv, wmq, wgate, lng, lnb, ws, bst, mk, mv, tm, cl, emit_gv):
    b, l, _ = x.shape
    nt = l // tm
    tok = lambda w: pl.BlockSpec((None, tm, w), lambda bi, i: (bi, i, 0))
    mem = pl.BlockSpec((None, N_MEM, D_MODEL), lambda bi, i: (bi, 0, 0))
    out_specs = [tok(D_MODEL), tok(D_MODEL), tok(3 * D_MODEL)]
    out_shape = [jax.ShapeDtypeStruct((b, l, D_MODEL), BF16), jax.ShapeDtypeStruct((b, l, D_MODEL), BF16),
                 jax.ShapeDtypeStruct((b, l, 3 * D_MODEL), BF16)]
    if emit_gv:
        out_specs.append(tok(D_MODEL))
        out_shape.append(jax.ShapeDtypeStruct((b, l, D_MODEL), F32))
    return pl.pallas_call(
        functools.partial(_inproj_b_kernel, tm=tm, cl=cl, emit_gv=emit_gv),
        grid=(b, nt),
        in_specs=[tok(D_MODEL), _const_spec((1, D_MODEL)), _const_spec((D_MODEL, D_MODEL)),
                  _const_spec((D_MODEL, D_MODEL)), _const_spec((D_MODEL, D_MODEL)),
                  _const_spec((D_MODEL, 3 * D_MODEL)), _const_spec((1, D_MODEL)), _const_spec((1, D_MODEL)),
                  _const_spec((GMLP_GROUPS, cl, cl)), _const_spec((cl, GMLP_GROUPS)), mem, mem],
        out_specs=out_specs,
        out_shape=out_shape,
        compiler_params=_cparams(("parallel", "parallel")),
        name="inproj_b",
    )(x, g, wu, wgv, wmq, wgate, lng, lnb, ws, bst, mk, mv)


def _mlstm_kernel(q_ref, kt_ref, v_ref, gt_ref, c0_ref, m0_ref, hm_ref, c_ref, m_ref, *, cb):
    i = pl.program_id(1)

    @pl.when(i == 0)
    def _():
        c_ref[...] = c0_ref[...]
        m_ref[...] = m0_ref[...]

    L = kt_ref.shape[-1]
    nch = cb // L
    ti = lax.broadcasted_iota(jnp.int32, (L, L), 0)
    si = lax.broadcasted_iota(jnp.int32, (L, L), 1)
    tri = ti >= si
    eye = ti == si

    rows = 4 * HEADS
    g_all = gt_ref[...].reshape(nch * rows, L)

    m_in = [m_ref[:, 0:1]]
    for c in range(nch):
        b_last4 = g_all[c * rows + HEADS:c * rows + 2 * HEADS, L - 1:L]
        amax4 = g_all[c * rows + 2 * HEADS:c * rows + 3 * HEADS, 0:1]
        m_in.append(jnp.maximum(b_last4 + m_in[-1], b_last4 + amax4))

    ones = jnp.ones((L, LANES), BF16)
    st = [c_ref[h] for h in range(HEADS)]
    for c in range(nch):
        r0, r1 = c * L, (c + 1) * L
        for h in range(HEADS):
            lo, hi = h * HEAD_DIM, (h + 1) * HEAD_DIM
            ig_r = g_all[c * rows + h:c * rows + h + 1, :]
            bc_r = g_all[c * rows + HEADS + h:c * rows + HEADS + h + 1, :]
            a_r = ig_r - bc_r
            bc_c = jnp.sum(jnp.where(eye, bc_r, 0.0), axis=-1, keepdims=True)
            m0 = m_in[c][h:h + 1, :]
            m_last = m_in[c + 1][h:h + 1, :]
            dmat = jnp.where(tri, bc_c + a_r, NEG_INF)
            inter = bc_c + m0
            m = jnp.maximum(inter, jnp.max(dmat, axis=-1, keepdims=True))
            w_intra = jnp.exp(dmat - m)
            w_inter = jnp.exp(inter - m)
            q = q_ref[r0:r1, lo:hi]
            kt = kt_ref[c, lo:hi, :]
            v = v_ref[r0:r1, lo:hi]
            s = _dot(q, kt) * w_intra
            qs = _dot(q, st[h].astype(BF16))
            num = w_inter * qs[:, :HEAD_DIM] + _dot(s.astype(BF16), v)
            den = w_inter * qs[:, HEAD_DIM:HEAD_DIM + 1] + jnp.sum(s, axis=-1, keepdims=True)
            hh = num / jnp.maximum(jnp.abs(den), jnp.exp(-m))
            bc_last = bc_r[:, L - 1:L]
            w_last = jnp.exp(bc_last + a_r - m_last)
            decay = jnp.exp(bc_last + m0 - m_last)
            ktw = (kt.astype(F32) * w_last).astype(BF16)
            st[h] = decay * st[h] + _dot(ktw, jnp.concatenate([v, ones], axis=1))
            hm_ref[r0:r1, lo:hi] = hh.astype(BF16)

    for h in range(HEADS):
        c_ref[h] = st[h]
    m_ref[...] = jnp.broadcast_to(m_in[nch], (HEADS, LANES))


def _mlstm(q, kt, v, gt, c0, m0, cb):
    b, l, _ = q.shape
    nt = l // cb
    CHUNK = kt.shape[-1]
    tok = pl.BlockSpec((None, cb, D_MODEL), lambda bi, i: (bi, i, 0))
    cs = pl.BlockSpec((None, HEADS, HEAD_DIM, HEAD_DIM + LANES), lambda bi, i: (bi, 0, 0, 0))
    ms = pl.BlockSpec((None, HEADS, LANES), lambda bi, i: (bi, 0, 0))
    return pl.pallas_call(
        functools.partial(_mlstm_kernel, cb=cb),
        grid=(b, nt),
        in_specs=[tok, pl.BlockSpec((None, cb // CHUNK, D_MODEL, CHUNK), lambda bi, i: (bi, i, 0, 0)), tok,
                  pl.BlockSpec((None, cb // CHUNK, 4 * HEADS, CHUNK), lambda bi, i: (bi, i, 0, 0)),
                  cs, ms],
        out_specs=[tok, cs, ms],
        out_shape=[jax.ShapeDtypeStruct((b, l, D_MODEL), BF16),
                   jax.ShapeDtypeStruct((b, HEADS, HEAD_DIM, HEAD_DIM + LANES), F32),
                   jax.ShapeDtypeStruct((b, HEADS, LANES), F32)],
        compiler_params=_cparams(("parallel", "arbitrary")),
        name="mlstm",
    )(q, kt, v, gt, c0, m0)


def _merge_kernel(x_ref, hh_ref, o_ref, ng_ref, ug_ref, att_ref, gates_ref, wa_ref, wb_ref, wc_ref, wo_ref, h_ref):
    parts = []
    for h in range(HEADS):
        lo, hi = h * HEAD_DIM, (h + 1) * HEAD_DIM
        hh = hh_ref[:, lo:hi].astype(F32)
        hn = hh * lax.rsqrt(jnp.mean(hh * hh, axis=-1, keepdims=True) + EPS)
        parts.append(((hn * ng_ref[:, lo:hi]) * o_ref[:, lo:hi].astype(F32)).astype(BF16))
    hm = jnp.concatenate(parts, axis=1)
    g = gates_ref[...].astype(F32)
    mixed = g[:, :D_MODEL] * _dot(hm, wa_ref[...])
    mixed = mixed + g[:, D_MODEL:2 * D_MODEL] * _dot(ug_ref[...], wb_ref[...])
    mixed = mixed + g[:, 2 * D_MODEL:] * _dot(att_ref[...], wc_ref[...])
    h_ref[...] = x_ref[...] + _dot(mixed.astype(BF16), wo_ref[...])


def _merge(x2d, hh, osig, ng, ug, att, gates, wa, wb, wc, wo, tm):
    n = x2d.shape[0]
    row = lambda w: pl.BlockSpec((tm, w), lambda i: (i, 0))
    wspec = _const_spec((D_MODEL, D_MODEL))
    return pl.pallas_call(
        _merge_kernel,
        grid=(n // tm,),
        in_specs=[row(D_MODEL), row(D_MODEL), row(D_MODEL), _const_spec((1, D_MODEL)), row(D_MODEL), row(D_MODEL),
                  row(3 * D_MODEL), wspec, wspec, wspec, wspec],
        out_specs=row(D_MODEL),
        out_shape=jax.ShapeDtypeStruct((n, D_MODEL), F32),
        compiler_params=_cparams(("parallel",)),
        name="merge",
    )(x2d, hh, osig, ng, ug, att, gates, wa, wb, wc, wo)


MOE_T = 256
MOE_STEP_SUBS = 2
MOE_CAP = 32
MOE_S = N_EXPERTS * MOE_CAP
MOE_OVF = 512
SEG_ALIGN = 16
EXPERT_BLOCK = 128


def _moe_route_kernel(hp_ref, hs_ref, g_ref, wrg_ref, brg_ref, wre_ref, bre_ref,
                      main_ref, ovf_ref, info_ref, meta_ref, *, nsp):
    n = MOE_STEP_SUBS * MOE_T
    h = jnp.where(pl.program_id(0) < nsp // MOE_STEP_SUBS, hp_ref[...], hs_ref[...])
    xm = _rmsnorm(h, g_ref[...]).astype(BF16)
    lane = lax.broadcasted_iota(jnp.int32, (n, LANES), 1).astype(F32)
    lg = jnp.where(lane < N_GROUPS, _dot(xm, wrg_ref[...]) + brg_ref[...], NEG_INF)
    gmax = jnp.max(lg, axis=-1, keepdims=True)
    p_top = 1.0 / jnp.sum(jnp.exp(lg - gmax), axis=-1, keepdims=True)
    grp = jnp.min(jnp.where(lg == gmax, lane, float(LANES)), axis=-1, keepdims=True)
    el = _dot(xm, wre_ref[...]) + bre_ref[...]
    in_grp = (lane >= grp * EXPERTS_PER_GROUP) & (lane < (grp + 1.0) * EXPERTS_PER_GROUP)
    vals = jnp.where(in_grp, el, NEG_INF)
    v1 = jnp.max(vals, axis=-1, keepdims=True)
    i1 = jnp.min(jnp.where(vals == v1, lane, float(LANES)), axis=-1, keepdims=True)
    vals2 = jnp.where(lane == i1, NEG_INF, vals)
    v2 = jnp.max(vals2, axis=-1, keepdims=True)
    i2 = jnp.min(jnp.where(vals2 == v2, lane, float(LANES)), axis=-1, keepdims=True)
    r = jnp.exp(v2 - v1)
    p1 = p_top / (1.0 + r)
    p2 = p_top * r / (1.0 + r)
    sel1 = lane == i1
    sel2 = lane == i2
    onehot = jnp.where(sel1 | sel2, 1.0, 0.0)

    deferred = []
    for s in range(MOE_STEP_SUBS):
        rows = slice(s * MOE_T, (s + 1) * MOE_T)
        deferred.append(_route_sub_tile(
            xm[rows, :], onehot[rows, :], i1[rows, :], i2[rows, :],
            p1[rows, :], p2[rows, :], main_ref.at[s], ovf_ref.at[s], info_ref.at[rows, :], meta_ref.at[s]))
    for write_overflow in deferred:
        write_overflow()


def _route_sub_tile(xm, onehot, i1, i2, p1, p2, main_ref, ovf_ref, info_ref, meta_ref):
    t = MOE_T
    lane = lax.broadcasted_iota(jnp.int32, (t, LANES), 1)
    sel1 = lane.astype(F32) == i1
    sel2 = lane.astype(F32) == i2
    cnt = jnp.sum(onehot, axis=0, keepdims=True).astype(jnp.int32)
    pn = jnp.bitwise_and(jnp.maximum(cnt - MOE_CAP, 0) + (SEG_ALIGN - 1), -SEG_ALIGN)
    pn8 = jnp.broadcast_to(pn, (8, LANES))
    earlier = jnp.where(lax.broadcasted_iota(jnp.int32, (LANES, LANES), 0)
                        < lax.broadcasted_iota(jnp.int32, (LANES, LANES), 1), 1.0, 0.0).astype(BF16)
    off_f8 = _dot(pn8.astype(F32).astype(BF16), earlier)
    row8 = lax.broadcasted_iota(jnp.int32, (8, LANES), 0)
    meta_ref[...] = jnp.where(row8 == 0, off_f8.astype(jnp.int32), jnp.where(row8 == 1, pn8, 0))

    ti = lax.broadcasted_iota(jnp.int32, (t, t), 0)
    si = lax.broadcasted_iota(jnp.int32, (t, t), 1)
    before = jnp.where(ti > si, 1.0, 0.0).astype(BF16)
    rank = _dot(before, onehot.astype(BF16))
    off_f = off_f8[0:1, :]

    def slot_row(sel, idx):
        rk = jnp.sum(jnp.where(sel, rank, 0.0), axis=-1, keepdims=True)
        of = jnp.sum(jnp.where(sel, off_f, 0.0), axis=-1, keepdims=True)
        return jnp.where(rk < MOE_CAP, idx * MOE_CAP + rk, MOE_S - MOE_CAP + of + rk)

    pos1 = slot_row(sel1, i1)
    pos2 = slot_row(sel2, i2)
    info_ref[...] = (jnp.where(lane == 0, pos1, 0.0) + jnp.where(lane == 1, pos2, 0.0)
                     + jnp.where(lane == 2, p1, 0.0) + jnp.where(lane == 3, p2, 0.0))

    eye = ti == si
    pos1_r = jnp.sum(jnp.where(eye, pos1, 0.0), axis=0, keepdims=True)
    pos2_r = jnp.sum(jnp.where(eye, pos2, 0.0), axis=0, keepdims=True)

    def gather(rows, base):
        srow = (lax.broadcasted_iota(jnp.int32, (rows, t), 0) + base).astype(F32)
        pick = jnp.where((srow == pos1_r) | (srow == pos2_r), 1.0, 0.0).astype(BF16)
        return _dot(pick, xm).astype(BF16)

    main_ref[...] = gather(MOE_S, 0)
    has_ovf = jnp.sum(pn) > 0

    def write_overflow():
        @pl.when(has_ovf)
        def _():
            ovf_ref[...] = gather(MOE_OVF, MOE_S)

        @pl.when(jnp.logical_not(has_ovf))
        def _():
            ovf_ref[...] = jnp.zeros(ovf_ref.shape, ovf_ref.dtype)
    return write_overflow


def _two_group_specs(nsp, cols):
    rows = MOE_STEP_SUBS * MOE_T
    npb = nsp // MOE_STEP_SUBS
    return (pl.BlockSpec((rows, cols), lambda j, *_: (jnp.minimum(j, npb - 1), 0)),
            pl.BlockSpec((rows, cols), lambda j, *_: (jnp.maximum(j - npb, 0), 0)))


def _moe_route(hp, hs, g, wrg, brg, wre, bre):
    nsp = hp.shape[0] // MOE_T
    ns = nsp + hs.shape[0] // MOE_T
    sub = MOE_STEP_SUBS
    return pl.pallas_call(
        functools.partial(_moe_route_kernel, nsp=nsp),
        grid=(ns // sub,),
        in_specs=[*_two_group_specs(nsp, D_MODEL), _const_spec((1, D_MODEL)),
                  _const_spec((D_MODEL, LANES)), _const_spec((1, LANES)),
                  _const_spec((D_MODEL, LANES)), _const_spec((1, LANES))],
        out_specs=[pl.BlockSpec((sub, MOE_S, D_MODEL), lambda j: (j, 0, 0)),
                   pl.BlockSpec((sub, MOE_OVF, D_MODEL), lambda j: (j, 0, 0)),
                   pl.BlockSpec((sub * MOE_T, LANES), lambda j: (j, 0)),
                   pl.BlockSpec((sub, 8, LANES), lambda j: (j, 0, 0))],
        out_shape=[jax.ShapeDtypeStruct((ns, MOE_S, D_MODEL), BF16),
                   jax.ShapeDtypeStruct((ns, MOE_OVF, D_MODEL), BF16),
                   jax.ShapeDtypeStruct((ns * MOE_T, LANES), F32),
                   jax.ShapeDtypeStruct((ns, 8, LANES), jnp.int32)],
        compiler_params=_cparams(("arbitrary",)),
        name="moe_route",
    )(hp, hs, g, wrg, brg, wre, bre)


def _swiglu(x, wgu, wdb):
    gu = _dot(x, wgu[...])
    gate = gu[:, :D_EXPERT]
    hid = (gate * _sigmoid(gate)) * gu[:, D_EXPERT:]
    return _dot(hid.astype(BF16), wdb[...]).astype(BF16)


def _moe_expert_kernel(offs_ref, pns_ref, eflag_ref, main_ref, slots_hbm, wg_ref, wu_ref, wd_ref,
                       mout_ref, out_hbm, xbuf, obuf, wgu, wdb, sem_in, sem_out, *, n_sub, jb, n_rb):
    e = pl.program_id(0)
    rb = pl.program_id(1)

    @pl.when((e == 0) & (rb == 0))
    def _():
        xbuf[...] = jnp.zeros_like(xbuf)

    @pl.when(rb == 0)
    def _():
        wgu[:, :D_EXPERT] = wg_ref[...].astype(BF16)
        wgu[:, D_EXPERT:] = wu_ref[...].astype(BF16)
        wdb[...] = wd_ref[...].astype(BF16)

    sb = main_ref.shape[0]
    part = sb // 2 if sb % 2 == 0 else sb
    for s0 in range(0, sb, part):
        y = _swiglu(main_ref[s0:s0 + part].reshape(part * MOE_CAP, D_MODEL), wgu, wdb)
        mout_ref[s0:s0 + part] = y.reshape(part, MOE_CAP, D_MODEL)

    def copy_in(j, src, dst):
        return pltpu.make_async_copy(slots_hbm.at[j, pl.ds(src, SEG_ALIGN), :],
                                     xbuf.at[pl.ds(dst, SEG_ALIGN), :], sem_in)

    def copy_out(j, src, dst):
        return pltpu.make_async_copy(obuf.at[pl.ds(src, SEG_ALIGN), :],
                                     out_hbm.at[j, pl.ds(dst, SEG_ALIGN), :], sem_out)

    def for_each_chunk(g, fn):
        def seg(jj, cur):
            j = g * jb + jj
            n = pns_ref[j * N_EXPERTS + e]
            off = offs_ref[j * N_EXPERTS + e]

            def chunk(k, c):
                fn(j, pl.multiple_of(off + k * SEG_ALIGN, SEG_ALIGN), pl.multiple_of(cur + k * SEG_ALIGN, SEG_ALIGN))
                return c
            lax.fori_loop(0, n // SEG_ALIGN, chunk, 0)
            return cur + n
        return lax.fori_loop(0, jb, seg, 0)

    def group(g, carry):
        total = for_each_chunk(g, lambda j, r, b: copy_in(j, r, b).start())
        nchunk = total // SEG_ALIGN

        def wait_in(k, c):
            copy_in(0, 0, 0).wait()
            return c
        lax.fori_loop(0, nchunk, wait_in, 0)

        def block(bi, c):
            r0 = pl.multiple_of(bi * EXPERT_BLOCK, EXPERT_BLOCK)
            obuf[pl.ds(r0, EXPERT_BLOCK), :] = _swiglu(xbuf[pl.ds(r0, EXPERT_BLOCK), :], wgu, wdb)
            return c
        lax.fori_loop(0, (total + EXPERT_BLOCK - 1) // EXPERT_BLOCK, block, 0)

        for_each_chunk(g, lambda j, r, b: copy_out(j, b, r).start())

        def wait_out(k, c):
            copy_out(0, 0, 0).wait()
            return c
        lax.fori_loop(0, nchunk, wait_out, 0)
        return carry

    @pl.when((rb == n_rb - 1) & (eflag_ref[e] > 0))
    def _():
        lax.fori_loop(0, n_sub // jb, group, 0)


def _moe_expert(main, ovf, offs, pns, eflag, wg, wu, wd):
    ns = main.shape[0]
    n_rb = 1
    sb = ns // n_rb
    jb = max(d for d in range(1, 12) if ns % d == 0)
    rows = jb * MOE_T + EXPERT_BLOCK
    main4 = main.reshape(ns, N_EXPERTS, MOE_CAP, D_MODEL)
    mspec = pl.BlockSpec((sb, None, MOE_CAP, D_MODEL), lambda e, rb, *_: (rb, e, 0, 0))
    wspec = lambda a, b: pl.BlockSpec((None, a, b), lambda e, rb, *_: (e, 0, 0))
    grid_spec = pltpu.PrefetchScalarGridSpec(
        num_scalar_prefetch=3,
        grid=(N_EXPERTS, n_rb),
        in_specs=[mspec, pl.BlockSpec(memory_space=pl.ANY),
                  wspec(D_MODEL, D_EXPERT), wspec(D_MODEL, D_EXPERT), wspec(D_EXPERT, D_MODEL)],
        out_specs=[mspec, pl.BlockSpec(memory_space=pl.ANY)],
        scratch_shapes=[pltpu.VMEM((rows, D_MODEL), BF16), pltpu.VMEM((rows, D_MODEL), BF16),
                        pltpu.VMEM((D_MODEL, 2 * D_EXPERT), BF16), pltpu.VMEM((D_EXPERT, D_MODEL), BF16),
                        pltpu.SemaphoreType.DMA(()), pltpu.SemaphoreType.DMA(())],
    )
    mout, oout = pl.pallas_call(
        functools.partial(_moe_expert_kernel, n_sub=ns, jb=jb, n_rb=n_rb),
        grid_spec=grid_spec,
        out_shape=[jax.ShapeDtypeStruct(main4.shape, main4.dtype), jax.ShapeDtypeStruct(ovf.shape, ovf.dtype)],
        input_output_aliases={3: 0, 4: 1},
        compiler_params=_cparams(("arbitrary", "arbitrary")),
        name="moe_expert",
    )(offs, pns, eflag, main4, ovf, wg, wu, wd)
    return mout.reshape(main.shape), oout


def _moe_combine_kernel(jflag_ref, sflag_ref, hp_ref, hs_ref, main_ref, ovf_ref, info_ref, gf_ref,
                        yp_ref, ys_ref, acc_ref, *, nsp):
    j = pl.program_id(0)
    npb = nsp // MOE_STEP_SUBS

    def weights(s, cols, base):
        info = info_ref[s * MOE_T:(s + 1) * MOE_T, :]
        scol = (lax.broadcasted_iota(jnp.int32, (MOE_T, cols), 1) + base).astype(F32)
        return (jnp.where(scol == info[:, 0:1], info[:, 2:3], 0.0)
                + jnp.where(scol == info[:, 1:2], info[:, 3:4], 0.0)).astype(BF16)

    for s in range(MOE_STEP_SUBS):
        rows = slice(s * MOE_T, (s + 1) * MOE_T)
        h = jnp.where(j < npb, hp_ref[rows, :], hs_ref[rows, :])
        acc_ref[rows, :] = h + _dot(weights(s, MOE_S, 0), main_ref[s])

    for s in range(MOE_STEP_SUBS):
        @pl.when(jflag_ref[j * MOE_STEP_SUBS + s] > 0)
        def _(s=s):
            rows = slice(s * MOE_T, (s + 1) * MOE_T)
            acc_ref[rows, :] += _dot(weights(s, MOE_OVF, MOE_S), ovf_ref[s])

    y = _rmsnorm(acc_ref[...], gf_ref[...])

    @pl.when(j < npb)
    def _():
        yp_ref[...] = y

    @pl.when(j >= npb)
    def _():
        ys_ref[...] = y


def _moe_combine(hp, hs, main, ovf, info, jflag, gf):
    nsp = hp.shape[0] // MOE_T
    ns = main.shape[0]
    sub = MOE_STEP_SUBS
    sflag = jnp.max(jflag.reshape(ns // sub, sub), axis=1)
    grid_spec = pltpu.PrefetchScalarGridSpec(
        num_scalar_prefetch=2,
        grid=(ns // sub,),
        in_specs=[*_two_group_specs(nsp, D_MODEL),
                  pl.BlockSpec((sub, MOE_S, D_MODEL), lambda j, jf, sf: (j, 0, 0)),
                  pl.BlockSpec((sub, MOE_OVF, D_MODEL), lambda j, jf, sf: (jnp.where(sf[j] > 0, j, 0), 0, 0)),
                  pl.BlockSpec((sub * MOE_T, LANES), lambda j, jf, sf: (j, 0)),
                  pl.BlockSpec((1, D_MODEL), lambda j, jf, sf: (0, 0))],
        out_specs=list(_two_group_specs(nsp, D_MODEL)),
        scratch_shapes=[pltpu.VMEM((sub * MOE_T, D_MODEL), F32)],
    )
    return pl.pallas_call(
        functools.partial(_moe_combine_kernel, nsp=nsp),
        grid_spec=grid_spec,
        out_shape=[jax.ShapeDtypeStruct(hp.shape, F32), jax.ShapeDtypeStruct(hs.shape, F32)],
        compiler_params=_cparams(("arbitrary",)),
        name="moe_combine",
    )(jflag, sflag, hp, hs, main, ovf, info, gf)


def _moe(hp, hs, p):
    main, ovf, info, meta = _moe_route(hp, hs, p["norm_moe_g"], p["w_rg"], p["b_rg"], p["w_re"], p["b_re"])
    pn = meta[:, 1, :N_EXPERTS]
    offs = meta[:, 0, :N_EXPERTS].reshape(-1)
    eflag = (jnp.sum(pn, axis=0) > 0).astype(jnp.int32)
    jflag = (jnp.sum(pn, axis=1) > 0).astype(jnp.int32)
    main, ovf = _moe_expert(main, ovf, offs, pn.reshape(-1), eflag, p["w_eg"], p["w_eu"], p["w_ed"])
    return _moe_combine(hp, hs, main, ovf, info, jflag, p["norm_final_g"])


def _trunk(x, mem_k, mem_v, c0, n0, m0, conv0, p, *, tm, cb, cl, emit_gv):
    b, l, _ = x.shape
    q, kt, v, osig, gt, conv_new = _inproj_a(
        x, p["norm_mix_g"], p["w_qk"], p["w_v"], p["w_o"], p["w_ift"], p["b_if"],
        p["conv_w"], p["conv_b"], conv0, tm)
    outs = _inproj_b(x, p["norm_mix_g"], p["w_u"], p["w_gv"], p["w_mq"], p["w_gate"],
                     p["gmlp_norm_g"], p["gmlp_norm_b"], p["w_s"][:, :cl, :cl], p["b_st"][:cl],
                     mem_k, mem_v, tm, cl, emit_gv)
    ug, att, gates = outs[:3]
    m0b = jnp.broadcast_to(m0[..., None], m0.shape + (LANES,))
    st0 = jnp.concatenate([c0, jnp.broadcast_to(n0[..., None], n0.shape + (LANES,))], axis=-1)
    hh, st1, m1 = _mlstm(q, kt, v, gt, st0, m0b, cb)
    c1, n1 = st1[..., :HEAD_DIM], st1[..., HEAD_DIM]
    n = b * l
    h = _merge(x.reshape(n, D_MODEL), hh.reshape(n, D_MODEL), osig.reshape(n, D_MODEL), p["mlstm_norm_g"],
               ug.reshape(n, D_MODEL), att.reshape(n, D_MODEL), gates.reshape(n, 3 * D_MODEL),
               p["w_br_mlstm"], p["w_br_gmlp"], p["w_br_mem"], p["w_out"], 512)
    return h, c1, n1, m1[..., 0], conv_new, (outs[3] if emit_gv else None)


def kernel(x_prompt, x_sample, mem_prompt, cache_mem_k, cache_mem_v, state_mlstm_C, state_mlstm_n, state_mlstm_m, state_mlstm_conv, norm_mix_g, w_in, mlstm_i_b, mlstm_f_b, mlstm_conv_w, mlstm_conv_b, mlstm_norm_g, gmlp_norm_g, gmlp_norm_b, gmlp_w_s, gmlp_b_s, mem_norm_g, w_mem_k, w_mem_v, w_br_mlstm, w_br_gmlp, w_br_mem, w_out, norm_moe_g, w_router_group, b_router_group, w_router_expert, b_router_expert, w_exp_gate, w_exp_up, w_exp_down, norm_final_g):
    bp = x_prompt.shape[0]
    bs = x_sample.shape[0]
    W = D_MODEL
    wi = w_in[0]
    o_qk, o_v, o_o, o_i = 0, 2 * W, 3 * W, 4 * W
    o_f = o_i + HEADS
    o_u = o_f + HEADS
    o_gv, o_mq, o_gate = o_u + W, o_u + 2 * W, o_u + 3 * W
    row = lambda a: a.reshape(1, -1)
    pad_l = lambda a: jnp.pad(a, ((0, 0), (0, LANES - a.shape[1])))
    p = {
        "norm_mix_g": row(norm_mix_g[0]),
        "w_qk": wi[:, o_qk:o_v].astype(BF16), "w_v": wi[:, o_v:o_o].astype(BF16),
        "w_o": wi[:, o_o:o_i].astype(BF16),
        "w_ift": wi[:, o_i:o_u].T.astype(BF16),
        "b_if": jnp.concatenate([mlstm_i_b[0], mlstm_f_b[0]]).reshape(2 * HEADS, 1),
        "conv_w": mlstm_conv_w[0], "conv_b": row(mlstm_conv_b[0]),
        "w_u": wi[:, o_u:o_gv].astype(BF16), "w_gv": wi[:, o_gv:o_mq].astype(BF16),
        "w_mq": wi[:, o_mq:o_gate].astype(BF16), "w_gate": wi[:, o_gate:].astype(BF16),
        "gmlp_norm_g": row(gmlp_norm_g[0]), "gmlp_norm_b": row(gmlp_norm_b[0]),
        "w_s": gmlp_w_s[0], "b_st": gmlp_b_s[0].T,
        "mlstm_norm_g": row(mlstm_norm_g[0]),
        "w_br_mlstm": w_br_mlstm[0].astype(BF16), "w_br_gmlp": w_br_gmlp[0].astype(BF16),
        "w_br_mem": w_br_mem[0].astype(BF16), "w_out": w_out[0].astype(BF16),
        "norm_moe_g": row(norm_moe_g[0]),
        "w_rg": pad_l(w_router_group[0]).astype(BF16), "b_rg": pad_l(row(b_router_group[0])),
        "w_re": pad_l(w_router_expert[0]).astype(BF16), "b_re": pad_l(row(b_router_expert[0])),
        "w_eg": w_exp_gate[0], "w_eu": w_exp_up[0], "w_ed": w_exp_down[0],
        "norm_final_g": row(norm_final_g),
    }

    mk_p, mv_p = _memory_kv(mem_prompt.reshape(bp * N_MEM, W), row(mem_norm_g[0]),
                            w_mem_k[0].astype(BF16), w_mem_v[0].astype(BF16))
    mk_p3, mv_p3 = mk_p.reshape(bp, N_MEM, W), mv_p.reshape(bp, N_MEM, W)

    zeros = lambda *s: jnp.zeros(s, F32)
    hp, cp, np_, mp, cvp, _ = _trunk(
        x_prompt, mk_p3.astype(BF16), mv_p3.astype(BF16),
        zeros(bp, HEADS, HEAD_DIM, HEAD_DIM), zeros(bp, HEADS, HEAD_DIM), zeros(bp, HEADS),
        zeros(bp, CONV_W - 1, 2 * W), p, tm=256, cb=512, cl=GMLP_CHUNK, emit_gv=False)
    ls = x_sample.shape[1]
    hs, cs, ns, ms, cvs, gvs = _trunk(
        x_sample, cache_mem_k[0].reshape(bs, N_MEM, W).astype(BF16),
        cache_mem_v[0].reshape(bs, N_MEM, W).astype(BF16),
        state_mlstm_C[0], state_mlstm_n[0], state_mlstm_m[0], state_mlstm_conv[0], p,
        tm=ls, cb=ls, cl=min(ls, GMLP_CHUNK), emit_gv=True)

    yp, ys = _moe(hp, hs, p)
    kv_shape = (1, bp, N_MEM, HEADS, HEAD_DIM)
    return (yp.reshape(x_prompt.shape), ys.reshape(x_sample.shape), mk_p.reshape(kv_shape), mv_p.reshape(kv_shape),
            cp[None], np_[None], mp[None], cvp[None],
            cs[None], ns[None], ms[None], cvs[None], gvs[None])
```

```python
import functools

import jax
import jax.numpy as jnp
from jax import lax
from jax.experimental import pallas as pl
from jax.experimental.pallas import tpu as pltpu

D_MODEL = 1024
MLSTM_BLOCK = 128
TOKEN_TILE = 512
EPS = 1e-6
HEADS = 4
HEAD_DIM = 256
CONV_W = 4
GMLP_GROUPS = 4
GMLP_GROUP_DIM = 256
GMLP_CHUNK = 128
N_MEM = 256
N_GROUPS = 4
EXPERTS_PER_GROUP = 8
N_EXPERTS = 32
D_EXPERT = 256
LANES = 128
CONV_PAD = 8

F32 = jnp.float32
BF16 = jnp.bfloat16
NEG_INF = float("-inf")

VMEM_LIMIT = 56 * 1024 * 1024


def _cparams(sem):
    return pltpu.CompilerParams(dimension_semantics=sem, vmem_limit_bytes=VMEM_LIMIT)


def _const_spec(shape):
    nd = len(shape)
    return pl.BlockSpec(shape, lambda *_: (0,) * nd, pipeline_mode=pl.Buffered(1))


def _sigmoid(x):
    return 0.5 * (jnp.tanh(0.5 * x) + 1.0)


def _log_sigmoid(x):
    return jnp.minimum(x, 0.0) - jnp.log(1.0 + jnp.exp(-jnp.abs(x)))


def _rmsnorm(x, g):
    r = lax.rsqrt(jnp.mean(x * x, axis=-1, keepdims=True) + EPS)
    return (x * r) * g


def _dot(a, b):
    return jnp.dot(a, b, preferred_element_type=F32)


def _dot_nt(a, b):
    return lax.dot_general(a, b, (((1,), (1,)), ((), ())), preferred_element_type=F32)


def _dot_tn(a, b):
    return lax.dot_general(a, b, (((0,), (0,)), ((), ())), preferred_element_type=F32)


def _memkv_kernel(mem_ref, g_ref, wk_ref, wv_ref, k_ref, v_ref):
    mn = _rmsnorm(mem_ref[...], g_ref[...]).astype(BF16)
    k_ref[...] = _dot(mn, wk_ref[...])
    v_ref[...] = _dot(mn, wv_ref[...])


def _memory_kv(mem2d, g, wk, wv):
    n = mem2d.shape[0]
    tm = 512
    row = pl.BlockSpec((tm, D_MODEL), lambda i: (i, 0))
    return pl.pallas_call(
        _memkv_kernel,
        grid=(n // tm,),
        in_specs=[row, _const_spec((1, D_MODEL)), _const_spec((D_MODEL, D_MODEL)),
                  _const_spec((D_MODEL, D_MODEL))],
        out_specs=[row, row],
        out_shape=[jax.ShapeDtypeStruct((n, D_MODEL), F32)] * 2,
        compiler_params=_cparams(("parallel",)),
        name="memory_kv",
    )(mem2d, g, wk, wv)


def _inproj_a_kernel(x_ref, g_ref, wqk_ref, wv_ref, wo_ref, wift_ref, bif_ref, cw_ref, cb_ref, cs_ref,
                     q_ref, kt_ref, v_ref, o_ref, gt_ref, cn_ref, ext_ref, *, nseg, sl, lc):
    i = pl.program_id(1)
    tm = nseg * sl
    tail = CONV_PAD - (CONV_W - 1)

    @pl.when(i == 0)
    def _():
        for s in range(nseg):
            ext_ref[s, 0:CONV_PAD, :] = jnp.zeros((CONV_PAD, 2 * D_MODEL), F32)
            ext_ref[s, tail:CONV_PAD, :] = cs_ref[s]

    xn = _rmsnorm(x_ref[...], g_ref[...]).astype(BF16)

    zqk = _dot(xn, wqk_ref[...])
    ks = []
    for s in range(nseg):
        ext_ref[s, CONV_PAD:CONV_PAD + sl, :] = zqk[s * sl:(s + 1) * sl, :]
        acc = cb_ref[...] + ext_ref[s, CONV_PAD:CONV_PAD + sl, :] * cw_ref[CONV_W - 1:CONV_W, :]
        for j in range(CONV_W - 1):
            acc = acc + ext_ref[s, tail + j:tail + j + sl, :] * cw_ref[j:j + 1, :]
        qk = acc * _sigmoid(acc)
        q_ref[s * sl:(s + 1) * sl, :] = (qk[:, :D_MODEL] * (HEAD_DIM ** -0.5)).astype(BF16)
        ks.append(qk[:, D_MODEL:])
        cn_ref[s] = ext_ref[s, sl + tail:sl + CONV_PAD, :]
        ext_ref[s, 0:CONV_PAD, :] = ext_ref[s, sl:sl + CONV_PAD, :]
    if tm % LANES:
        ks.append(jnp.zeros((LANES - tm % LANES, D_MODEL), F32))
    k = jnp.concatenate(ks, axis=0) if len(ks) > 1 else ks[0]
    kt = k.T.astype(BF16)
    nch = tm // lc
    per_seq = sl // lc
    for c in range(nch):
        kt_ref[c // per_seq, c % per_seq] = kt[:, c * lc:(c + 1) * lc]

    v_ref[...] = _dot(xn, wv_ref[...]).astype(BF16)
    o_ref[...] = _sigmoid(_dot(xn, wo_ref[...])).astype(BF16)

    z = jnp.concatenate([_dot_nt(wift_ref[...], xn[c * lc:(c + 1) * lc, :]) + bif_ref[...]
                         for c in range(nch)], axis=0)
    is_ig = (lax.broadcasted_iota(jnp.int32, z.shape, 0) % (2 * HEADS)) < HEADS
    g = jnp.where(is_ig, z, _log_sigmoid(z))
    upper = jnp.where(lax.broadcasted_iota(jnp.int32, (lc, lc), 0)
                      <= lax.broadcasted_iota(jnp.int32, (lc, lc), 1), 1.0, 0.0)
    bc = jnp.dot(g, upper, preferred_element_type=F32, precision=lax.Precision.HIGHEST)
    a = g - pltpu.roll(bc, nch * 2 * HEADS - HEADS, axis=0)
    amax = jnp.broadcast_to(jnp.max(a, axis=-1, keepdims=True), z.shape)
    gb = jnp.where(is_ig, g, bc)
    for c in range(nch):
        gt_ref[c // per_seq, c % per_seq, 0:2 * HEADS, :] = gb[c * 8:(c + 1) * 8, :]
        gt_ref[c // per_seq, c % per_seq, 2 * HEADS:4 * HEADS, :] = amax[c * 8:(c + 1) * 8, :]


def _tile_geometry(b, l):
    sl = min(l, TOKEN_TILE)
    nseg = max(1, min(b, TOKEN_TILE // sl))
    return nseg, sl


def _tok_spec(nseg, sl, nt, w):
    return pl.BlockSpec((nseg * sl, w), lambda bi, i: (bi * nt + i, 0))


def _inproj_a(x2d, b, l, g, wqk, wv, wo, wift, bif, cw, cb, cs):
    nseg, sl = _tile_geometry(b, l)
    nt = l // sl
    n = b * l
    CHUNK = min(MLSTM_BLOCK, l)
    per_seq = sl // CHUNK
    tok = functools.partial(_tok_spec, nseg, sl, nt)
    state = pl.BlockSpec((nseg, CONV_W - 1, 2 * D_MODEL), lambda bi, i: (bi, 0, 0))
    return pl.pallas_call(
        functools.partial(_inproj_a_kernel, nseg=nseg, sl=sl, lc=CHUNK),
        grid=(b // nseg, nt),
        in_specs=[tok(D_MODEL), _const_spec((1, D_MODEL)), _const_spec((D_MODEL, 2 * D_MODEL)),
                  _const_spec((D_MODEL, D_MODEL)), _const_spec((D_MODEL, D_MODEL)),
                  _const_spec((2 * HEADS, D_MODEL)), _const_spec((2 * HEADS, 1)),
                  _const_spec((CONV_W, 2 * D_MODEL)), _const_spec((1, 2 * D_MODEL)), state],
        out_specs=[tok(D_MODEL),
                   pl.BlockSpec((nseg, per_seq, D_MODEL, CHUNK), lambda bi, i: (bi, i, 0, 0)),
                   tok(D_MODEL), tok(D_MODEL),
                   pl.BlockSpec((nseg, per_seq, 4 * HEADS, CHUNK), lambda bi, i: (bi, i, 0, 0)),
                   state],
        out_shape=[jax.ShapeDtypeStruct((n, D_MODEL), BF16),
                   jax.ShapeDtypeStruct((b, l // CHUNK, D_MODEL, CHUNK), BF16),
                   jax.ShapeDtypeStruct((n, D_MODEL), BF16), jax.ShapeDtypeStruct((n, D_MODEL), BF16),
                   jax.ShapeDtypeStruct((b, l // CHUNK, 4 * HEADS, CHUNK), F32),
                   jax.ShapeDtypeStruct((b, CONV_W - 1, 2 * D_MODEL), F32)],
        scratch_shapes=[pltpu.VMEM((nseg, sl + CONV_PAD, 2 * D_MODEL), F32)],
        compiler_params=_cparams(("parallel", "arbitrary")),
        name="inproj_a",
    )(x2d, g, wqk, wv, wo, wift, bif, cw, cb, cs)


def _inproj_b_kernel(x_ref, g_ref, wu_ref, wgv_ref, wmq_ref, wgate_ref, lng_ref, lnb_ref, ws_ref, bst_ref,
                     mk_ref, mv_ref, ug_ref, att_ref, gates_ref, *rest, nseg, sl, cl, emit_gv):
    tm = nseg * sl
    xn = _rmsnorm(x_ref[...], g_ref[...]).astype(BF16)

    gates_ref[...] = _sigmoid(_dot(xn, wgate_ref[...])).astype(BF16)

    gvr = jax.nn.gelu(_dot(xn, wgv_ref[...]))
    mu = jnp.mean(gvr, axis=-1, keepdims=True)
    xc = gvr - mu
    r = lax.rsqrt(jnp.mean(xc * xc, axis=-1, keepdims=True) + EPS)
    gv = (xc * r) * lng_ref[...] + lnb_ref[...]
    if emit_gv:
        rest[0][...] = gv
    gvb = gv.astype(BF16)
    u = jax.nn.gelu(_dot(xn, wu_ref[...]))
    tri = (lax.broadcasted_iota(jnp.int32, (cl, cl), 0) >= lax.broadcasted_iota(jnp.int32, (cl, cl), 1))
    for gi in range(GMLP_GROUPS):
        wsg = jnp.where(tri, ws_ref[gi], 0.0).astype(BF16)
        lo, hi = gi * GMLP_GROUP_DIM, (gi + 1) * GMLP_GROUP_DIM
        for c in range(tm // cl):
            sp = _dot(wsg, gvb[c * cl:(c + 1) * cl, lo:hi]) + bst_ref[:, gi:gi + 1]
            ug_ref[c * cl:(c + 1) * cl, lo:hi] = (u[c * cl:(c + 1) * cl, lo:hi] * sp).astype(BF16)

    mq = _dot(xn, wmq_ref[...]).astype(BF16)
    for s in range(nseg):
        r0, r1 = s * sl, (s + 1) * sl
        for h in range(HEADS):
            lo, hi = h * HEAD_DIM, (h + 1) * HEAD_DIM
            sc = _dot_nt(mq[r0:r1, lo:hi], mk_ref[s, :, lo:hi]) * (HEAD_DIM ** -0.5)
            e = jnp.exp(sc - jnp.max(sc, axis=-1, keepdims=True))
            a = (e / jnp.sum(e, axis=-1, keepdims=True)).astype(BF16)
            att_ref[r0:r1, lo:hi] = _dot(a, mv_ref[s, :, lo:hi]).astype(BF16)


def _inproj_b(x2d, b, l, g, wu, wgv, wmq, wgate, lng, lnb, ws, bst, mk, mv, cl, emit_gv):
    nseg, sl = _tile_geometry(b, l)
    nt = l // sl
    n = b * l
    tok = functools.partial(_tok_spec, nseg, sl, nt)
    mem = pl.BlockSpec((nseg, N_MEM, D_MODEL), lambda bi, i: (bi, 0, 0))
    out_specs = [tok(D_MODEL), tok(D_MODEL), tok(3 * D_MODEL)]
    out_shape = [jax.ShapeDtypeStruct((n, D_MODEL), BF16), jax.ShapeDtypeStruct((n, D_MODEL), BF16),
                 jax.ShapeDtypeStruct((n, 3 * D_MODEL), BF16)]
    if emit_gv:
        out_specs.append(tok(D_MODEL))
        out_shape.append(jax.ShapeDtypeStruct((n, D_MODEL), F32))
    return pl.pallas_call(
        functools.partial(_inproj_b_kernel, nseg=nseg, sl=sl, cl=cl, emit_gv=emit_gv),
        grid=(b // nseg, nt),
        in_specs=[tok(D_MODEL), _const_spec((1, D_MODEL)), _const_spec((D_MODEL, D_MODEL)),
                  _const_spec((D_MODEL, D_MODEL)), _const_spec((D_MODEL, D_MODEL)),
                  _const_spec((D_MODEL, 3 * D_MODEL)), _const_spec((1, D_MODEL)), _const_spec((1, D_MODEL)),
                  _const_spec((GMLP_GROUPS, cl, cl)), _const_spec((cl, GMLP_GROUPS)), mem, mem],
        out_specs=out_specs,
        out_shape=out_shape,
        compiler_params=_cparams(("parallel", "parallel")),
        name="inproj_b",
    )(x2d, g, wu, wgv, wmq, wgate, lng, lnb, ws, bst, mk, mv)


def _mlstm_kernel(q_ref, kt_ref, v_ref, gt_ref, c0_ref, m0_ref, hm_ref, c_ref, m_ref, *, nseg, cb):
    i = pl.program_id(1)

    @pl.when(i == 0)
    def _():
        c_ref[...] = c0_ref[...]
        m_ref[...] = m0_ref[...]

    for s in range(nseg):
        _mlstm_sequence(q_ref, kt_ref.at[s], v_ref, gt_ref.at[s], c_ref.at[s], m_ref.at[s], hm_ref, s * cb, cb)


def _mlstm_sequence(q_ref, kt_ref, v_ref, gt_ref, c_ref, m_ref, hm_ref, row0, cb):
    L = kt_ref.shape[-1]
    nch = cb // L
    ti = lax.broadcasted_iota(jnp.int32, (L, L), 0)
    si = lax.broadcasted_iota(jnp.int32, (L, L), 1)
    tri = ti >= si
    eye = ti == si

    rows = 4 * HEADS
    g_all = gt_ref[...].reshape(nch * rows, L)

    m_in = [m_ref[:, 0:1]]
    for c in range(nch):
        b_last4 = g_all[c * rows + HEADS:c * rows + 2 * HEADS, L - 1:L]
        amax4 = g_all[c * rows + 2 * HEADS:c * rows + 3 * HEADS, 0:1]
        m_in.append(jnp.maximum(b_last4 + m_in[-1], b_last4 + amax4))

    ones = jnp.ones((L, LANES), BF16)
    st = [c_ref[h] for h in range(HEADS)]
    for c in range(nch):
        r0, r1 = row0 + c * L, row0 + (c + 1) * L
        for h in range(HEADS):
            lo, hi = h * HEAD_DIM, (h + 1) * HEAD_DIM
            ig_r = g_all[c * rows + h:c * rows + h + 1, :]
            bc_r = g_all[c * rows + HEADS + h:c * rows + HEADS + h + 1, :]
            a_r = ig_r - bc_r
            bc_c = jnp.sum(jnp.where(eye, bc_r, 0.0), axis=-1, keepdims=True)
            m0 = m_in[c][h:h + 1, :]
            m_last = m_in[c + 1][h:h + 1, :]
            dmat = jnp.where(tri, bc_c + a_r, NEG_INF)
            inter = bc_c + m0
            m = jnp.maximum(inter, jnp.max(dmat, axis=-1, keepdims=True))
            w_intra = jnp.exp(dmat - m)
            w_inter = jnp.exp(inter - m)
            q = q_ref[r0:r1, lo:hi]
            kt = kt_ref[c, lo:hi, :]
            v = v_ref[r0:r1, lo:hi]
            s = _dot(q, kt) * w_intra
            qs = _dot(q, st[h].astype(BF16))
            num = w_inter * qs[:, :HEAD_DIM] + _dot(s.astype(BF16), v)
            den = w_inter * qs[:, HEAD_DIM:HEAD_DIM + 1] + jnp.sum(s, axis=-1, keepdims=True)
            hh = num / jnp.maximum(jnp.abs(den), jnp.exp(-m))
            bc_last = bc_r[:, L - 1:L]
            w_last = jnp.exp(bc_last + a_r - m_last)
            decay = jnp.exp(bc_last + m0 - m_last)
            ktw = (kt.astype(F32) * w_last).astype(BF16)
            st[h] = decay * st[h] + _dot(ktw, jnp.concatenate([v, ones], axis=1))
            hm_ref[r0:r1, lo:hi] = hh.astype(BF16)

    for h in range(HEADS):
        c_ref[h] = st[h]
    m_ref[...] = jnp.broadcast_to(m_in[nch], (HEADS, LANES))


def _mlstm(q, kt, v, gt, c0, m0, b, l):
    nseg, cb = _tile_geometry(b, l)
    nt = l // cb
    CHUNK = kt.shape[-1]
    tok = _tok_spec(nseg, cb, nt, D_MODEL)
    cs = pl.BlockSpec((nseg, HEADS, HEAD_DIM, HEAD_DIM + LANES), lambda bi, i: (bi, 0, 0, 0))
    ms = pl.BlockSpec((nseg, HEADS, LANES), lambda bi, i: (bi, 0, 0))
    return pl.pallas_call(
        functools.partial(_mlstm_kernel, nseg=nseg, cb=cb),
        grid=(b // nseg, nt),
        in_specs=[tok, pl.BlockSpec((nseg, cb // CHUNK, D_MODEL, CHUNK), lambda bi, i: (bi, i, 0, 0)), tok,
                  pl.BlockSpec((nseg, cb // CHUNK, 4 * HEADS, CHUNK), lambda bi, i: (bi, i, 0, 0)),
                  cs, ms],
        out_specs=[tok, cs, ms],
        out_shape=[jax.ShapeDtypeStruct((b * l, D_MODEL), BF16),
                   jax.ShapeDtypeStruct((b, HEADS, HEAD_DIM, HEAD_DIM + LANES), F32),
                   jax.ShapeDtypeStruct((b, HEADS, LANES), F32)],
        compiler_params=_cparams(("parallel", "arbitrary")),
        name="mlstm",
    )(q, kt, v, gt, c0, m0)


def _merge_kernel(x_ref, hh_ref, o_ref, ng_ref, ug_ref, att_ref, gates_ref, wa_ref, wb_ref, wc_ref, wo_ref, h_ref):
    parts = []
    for h in range(HEADS):
        lo, hi = h * HEAD_DIM, (h + 1) * HEAD_DIM
        hh = hh_ref[:, lo:hi].astype(F32)
        hn = hh * lax.rsqrt(jnp.mean(hh * hh, axis=-1, keepdims=True) + EPS)
        parts.append(((hn * ng_ref[:, lo:hi]) * o_ref[:, lo:hi].astype(F32)).astype(BF16))
    hm = jnp.concatenate(parts, axis=1)
    g = gates_ref[...].astype(F32)
    mixed = g[:, :D_MODEL] * _dot(hm, wa_ref[...])
    mixed = mixed + g[:, D_MODEL:2 * D_MODEL] * _dot(ug_ref[...], wb_ref[...])
    mixed = mixed + g[:, 2 * D_MODEL:] * _dot(att_ref[...], wc_ref[...])
    h_ref[...] = x_ref[...] + _dot(mixed.astype(BF16), wo_ref[...])


def _merge(x2d, hh, osig, ng, ug, att, gates, wa, wb, wc, wo, tm):
    n = x2d.shape[0]
    row = lambda w: pl.BlockSpec((tm, w), lambda i: (i, 0))
    wspec = _const_spec((D_MODEL, D_MODEL))
    return pl.pallas_call(
        _merge_kernel,
        grid=(n // tm,),
        in_specs=[row(D_MODEL), row(D_MODEL), row(D_MODEL), _const_spec((1, D_MODEL)), row(D_MODEL), row(D_MODEL),
                  row(3 * D_MODEL), wspec, wspec, wspec, wspec],
        out_specs=row(D_MODEL),
        out_shape=jax.ShapeDtypeStruct((n, D_MODEL), F32),
        compiler_params=_cparams(("parallel",)),
        name="merge",
    )(x2d, hh, osig, ng, ug, att, gates, wa, wb, wc, wo)


MOE_T = 256
MOE_STEP_SUBS = 2
MOE_CAP = 32
MOE_S = N_EXPERTS * MOE_CAP
MOE_OVF = 512
SEG_ALIGN = 16
EXPERT_BLOCK = 128


def _moe_route_kernel(hp_ref, hs_ref, g_ref, wrg_ref, brg_ref, wre_ref, bre_ref,
                      main_ref, ovf_ref, info_ref, meta_ref, *, nsp):
    n = MOE_STEP_SUBS * MOE_T
    h = jnp.where(pl.program_id(0) < nsp // MOE_STEP_SUBS, hp_ref[...], hs_ref[...])
    xm = _rmsnorm(h, g_ref[...]).astype(BF16)
    lane = lax.broadcasted_iota(jnp.int32, (n, LANES), 1).astype(F32)
    lg = jnp.where(lane < N_GROUPS, _dot(xm, wrg_ref[...]) + brg_ref[...], NEG_INF)
    gmax = jnp.max(lg, axis=-1, keepdims=True)
    p_top = 1.0 / jnp.sum(jnp.exp(lg - gmax), axis=-1, keepdims=True)
    grp = jnp.min(jnp.where(lg == gmax, lane, float(LANES)), axis=-1, keepdims=True)
    el = _dot(xm, wre_ref[...]) + bre_ref[...]
    in_grp = (lane >= grp * EXPERTS_PER_GROUP) & (lane < (grp + 1.0) * EXPERTS_PER_GROUP)
    vals = jnp.where(in_grp, el, NEG_INF)
    v1 = jnp.max(vals, axis=-1, keepdims=True)
    i1 = jnp.min(jnp.where(vals == v1, lane, float(LANES)), axis=-1, keepdims=True)
    vals2 = jnp.where(lane == i1, NEG_INF, vals)
    v2 = jnp.max(vals2, axis=-1, keepdims=True)
    i2 = jnp.min(jnp.where(vals2 == v2, lane, float(LANES)), axis=-1, keepdims=True)
    r = jnp.exp(v2 - v1)
    p1 = p_top / (1.0 + r)
    p2 = p_top * r / (1.0 + r)
    sel1 = lane == i1
    sel2 = lane == i2
    onehot = jnp.where(sel1 | sel2, 1.0, 0.0)

    deferred = []
    for s in range(MOE_STEP_SUBS):
        rows = slice(s * MOE_T, (s + 1) * MOE_T)
        deferred.append(_route_sub_tile(
            xm[rows, :], onehot[rows, :], i1[rows, :], i2[rows, :],
            p1[rows, :], p2[rows, :], main_ref.at[s], ovf_ref.at[s], info_ref.at[rows, :], meta_ref.at[s]))
    for write_overflow in deferred:
        write_overflow()


def _route_sub_tile(xm, onehot, i1, i2, p1, p2, main_ref, ovf_ref, info_ref, meta_ref):
    t = MOE_T
    lane = lax.broadcasted_iota(jnp.int32, (t, LANES), 1)
    sel1 = lane.astype(F32) == i1
    sel2 = lane.astype(F32) == i2
    cnt = jnp.sum(onehot, axis=0, keepdims=True).astype(jnp.int32)
    pn = jnp.bitwise_and(jnp.maximum(cnt - MOE_CAP, 0) + (SEG_ALIGN - 1), -SEG_ALIGN)
    pn8 = jnp.broadcast_to(pn, (8, LANES))
    earlier = jnp.where(lax.broadcasted_iota(jnp.int32, (LANES, LANES), 0)
                        < lax.broadcasted_iota(jnp.int32, (LANES, LANES), 1), 1.0, 0.0).astype(BF16)
    off_f8 = _dot(pn8.astype(F32).astype(BF16), earlier)
    row8 = lax.broadcasted_iota(jnp.int32, (8, LANES), 0)
    meta_ref[...] = jnp.where(row8 == 0, off_f8.astype(jnp.int32), jnp.where(row8 == 1, pn8, 0))

    ti = lax.broadcasted_iota(jnp.int32, (t, t), 0)
    si = lax.broadcasted_iota(jnp.int32, (t, t), 1)
    before = jnp.where(ti > si, 1.0, 0.0).astype(BF16)
    rank = _dot(before, onehot.astype(BF16))
    off_f = off_f8[0:1, :]

    def slot_row(sel, idx):
        rk = jnp.sum(jnp.where(sel, rank, 0.0), axis=-1, keepdims=True)
        of = jnp.sum(jnp.where(sel, off_f, 0.0), axis=-1, keepdims=True)
        return jnp.where(rk < MOE_CAP, idx * MOE_CAP + rk, MOE_S - MOE_CAP + of + rk)

    pos1 = slot_row(sel1, i1)
    pos2 = slot_row(sel2, i2)
    info_ref[...] = (jnp.where(lane == 0, pos1, 0.0) + jnp.where(lane == 1, pos2, 0.0)
                     + jnp.where(lane == 2, p1, 0.0) + jnp.where(lane == 3, p2, 0.0))

    eye = ti == si
    pos1_r = jnp.sum(jnp.where(eye, pos1, 0.0), axis=0, keepdims=True)
    pos2_r = jnp.sum(jnp.where(eye, pos2, 0.0), axis=0, keepdims=True)

    def gather(rows, base):
        srow = (lax.broadcasted_iota(jnp.int32, (rows, t), 0) + base).astype(F32)
        pick = jnp.where((srow == pos1_r) | (srow == pos2_r), 1.0, 0.0).astype(BF16)
        return _dot(pick, xm).astype(BF16)

    main_ref[...] = gather(MOE_S, 0)
    has_ovf = jnp.sum(pn) > 0

    def write_overflow():
        @pl.when(has_ovf)
        def _():
            ovf_ref[...] = gather(MOE_OVF, MOE_S)

        @pl.when(jnp.logical_not(has_ovf))
        def _():
            ovf_ref[...] = jnp.zeros(ovf_ref.shape, ovf_ref.dtype)
    return write_overflow


def _two_group_specs(nsp, cols):
    rows = MOE_STEP_SUBS * MOE_T
    npb = nsp // MOE_STEP_SUBS
    return (pl.BlockSpec((rows, cols), lambda j, *_: (jnp.minimum(j, npb - 1), 0)),
            pl.BlockSpec((rows, cols), lambda j, *_: (jnp.maximum(j - npb, 0), 0)))


def _moe_route(hp, hs, g, wrg, brg, wre, bre):
    nsp = hp.shape[0] // MOE_T
    ns = nsp + hs.shape[0] // MOE_T
    sub = MOE_STEP_SUBS
    return pl.pallas_call(
        functools.partial(_moe_route_kernel, nsp=nsp),
        grid=(ns // sub,),
        in_specs=[*_two_group_specs(nsp, D_MODEL), _const_spec((1, D_MODEL)),
                  _const_spec((D_MODEL, LANES)), _const_spec((1, LANES)),
                  _const_spec((D_MODEL, LANES)), _const_spec((1, LANES))],
        out_specs=[pl.BlockSpec((sub, MOE_S, D_MODEL), lambda j: (j, 0, 0)),
                   pl.BlockSpec((sub, MOE_OVF, D_MODEL), lambda j: (j, 0, 0)),
                   pl.BlockSpec((sub * MOE_T, LANES), lambda j: (j, 0)),
                   pl.BlockSpec((sub, 8, LANES), lambda j: (j, 0, 0))],
        out_shape=[jax.ShapeDtypeStruct((ns, MOE_S, D_MODEL), BF16),
                   jax.ShapeDtypeStruct((ns, MOE_OVF, D_MODEL), BF16),
                   jax.ShapeDtypeStruct((ns * MOE_T, LANES), F32),
                   jax.ShapeDtypeStruct((ns, 8, LANES), jnp.int32)],
        compiler_params=_cparams(("arbitrary",)),
        name="moe_route",
    )(hp, hs, g, wrg, brg, wre, bre)


def _swiglu(x, wgu, wdb):
    gu = _dot(x, wgu[...])
    gate = gu[:, :D_EXPERT]
    hid = (gate * _sigmoid(gate)) * gu[:, D_EXPERT:]
    return _dot(hid.astype(BF16), wdb[...]).astype(BF16)


def _moe_expert_kernel(offs_ref, pns_ref, eflag_ref, main_ref, slots_hbm, wg_ref, wu_ref, wd_ref,
                       mout_ref, out_hbm, xbuf, obuf, wgu, wdb, sem_in, sem_out, *, n_sub, jb, n_rb):
    e = pl.program_id(0)
    rb = pl.program_id(1)

    @pl.when((e == 0) & (rb == 0))
    def _():
        xbuf[...] = jnp.zeros_like(xbuf)

    @pl.when(rb == 0)
    def _():
        wgu[:, :D_EXPERT] = wg_ref[...].astype(BF16)
        wgu[:, D_EXPERT:] = wu_ref[...].astype(BF16)
        wdb[...] = wd_ref[...].astype(BF16)

    sb = main_ref.shape[0]
    part = sb // 2 if sb % 2 == 0 else sb
    for s0 in range(0, sb, part):
        y = _swiglu(main_ref[s0:s0 + part].reshape(part * MOE_CAP, D_MODEL), wgu, wdb)
        mout_ref[s0:s0 + part] = y.reshape(part, MOE_CAP, D_MODEL)

    def copy_in(j, src, dst):
        return pltpu.make_async_copy(slots_hbm.at[j, pl.ds(src, SEG_ALIGN), :],
                                     xbuf.at[pl.ds(dst, SEG_ALIGN), :], sem_in)

    def copy_out(j, src, dst):
        return pltpu.make_async_copy(obuf.at[pl.ds(src, SEG_ALIGN), :],
                                     out_hbm.at[j, pl.ds(dst, SEG_ALIGN), :], sem_out)

    def for_each_chunk(g, fn):
        def seg(jj, cur):
            j = g * jb + jj
            n = pns_ref[j * N_EXPERTS + e]
            off = offs_ref[j * N_EXPERTS + e]

            def chunk(k, c):
                fn(j, pl.multiple_of(off + k * SEG_ALIGN, SEG_ALIGN), pl.multiple_of(cur + k * SEG_ALIGN, SEG_ALIGN))
                return c
            lax.fori_loop(0, n // SEG_ALIGN, chunk, 0)
            return cur + n
        return lax.fori_loop(0, jb, seg, 0)

    def group(g, carry):
        total = for_each_chunk(g, lambda j, r, b: copy_in(j, r, b).start())
        nchunk = total // SEG_ALIGN

        def wait_in(k, c):
            copy_in(0, 0, 0).wait()
            return c
        lax.fori_loop(0, nchunk, wait_in, 0)

        def block(bi, c):
            r0 = pl.multiple_of(bi * EXPERT_BLOCK, EXPERT_BLOCK)
            obuf[pl.ds(r0, EXPERT_BLOCK), :] = _swiglu(xbuf[pl.ds(r0, EXPERT_BLOCK), :], wgu, wdb)
            return c
        lax.fori_loop(0, (total + EXPERT_BLOCK - 1) // EXPERT_BLOCK, block, 0)

        for_each_chunk(g, lambda j, r, b: copy_out(j, b, r).start())

        def wait_out(k, c):
            copy_out(0, 0, 0).wait()
            return c
        lax.fori_loop(0, nchunk, wait_out, 0)
        return carry

    @pl.when((rb == n_rb - 1) & (eflag_ref[e] > 0))
    def _():
        lax.fori_loop(0, n_sub // jb, group, 0)


def _moe_expert(main, ovf, offs, pns, eflag, wg, wu, wd):
    ns = main.shape[0]
    n_rb = 1
    sb = ns // n_rb
    jb = max(d for d in range(1, 12) if ns % d == 0)
    rows = jb * MOE_T + EXPERT_BLOCK
    main4 = main.reshape(ns, N_EXPERTS, MOE_CAP, D_MODEL)
    mspec = pl.BlockSpec((sb, None, MOE_CAP, D_MODEL), lambda e, rb, *_: (rb, e, 0, 0))
    wspec = lambda a, b: pl.BlockSpec((None, a, b), lambda e, rb, *_: (e, 0, 0))
    grid_spec = pltpu.PrefetchScalarGridSpec(
        num_scalar_prefetch=3,
        grid=(N_EXPERTS, n_rb),
        in_specs=[mspec, pl.BlockSpec(memory_space=pl.ANY),
                  wspec(D_MODEL, D_EXPERT), wspec(D_MODEL, D_EXPERT), wspec(D_EXPERT, D_MODEL)],
        out_specs=[mspec, pl.BlockSpec(memory_space=pl.ANY)],
        scratch_shapes=[pltpu.VMEM((rows, D_MODEL), BF16), pltpu.VMEM((rows, D_MODEL), BF16),
                        pltpu.VMEM((D_MODEL, 2 * D_EXPERT), BF16), pltpu.VMEM((D_EXPERT, D_MODEL), BF16),
                        pltpu.SemaphoreType.DMA(()), pltpu.SemaphoreType.DMA(())],
    )
    mout, oout = pl.pallas_call(
        functools.partial(_moe_expert_kernel, n_sub=ns, jb=jb, n_rb=n_rb),
        grid_spec=grid_spec,
        out_shape=[jax.ShapeDtypeStruct(main4.shape, main4.dtype), jax.ShapeDtypeStruct(ovf.shape, ovf.dtype)],
        input_output_aliases={3: 0, 4: 1},
        compiler_params=_cparams(("arbitrary", "arbitrary")),
        name="moe_expert",
    )(offs, pns, eflag, main4, ovf, wg, wu, wd)
    return mout.reshape(main.shape), oout


def _moe_combine_kernel(jflag_ref, sflag_ref, hp_ref, hs_ref, main_ref, ovf_ref, info_ref, gf_ref,
                        yp_ref, ys_ref, acc_ref, *, nsp):
    j = pl.program_id(0)
    npb = nsp // MOE_STEP_SUBS

    def weights(s, cols, base):
        info = info_ref[s * MOE_T:(s + 1) * MOE_T, :]
        scol = (lax.broadcasted_iota(jnp.int32, (MOE_T, cols), 1) + base).astype(F32)
        return (jnp.where(scol == info[:, 0:1], info[:, 2:3], 0.0)
                + jnp.where(scol == info[:, 1:2], info[:, 3:4], 0.0)).astype(BF16)

    for s in range(MOE_STEP_SUBS):
        rows = slice(s * MOE_T, (s + 1) * MOE_T)
        h = jnp.where(j < npb, hp_ref[rows, :], hs_ref[rows, :])
        acc_ref[rows, :] = h + _dot(weights(s, MOE_S, 0), main_ref[s])

    for s in range(MOE_STEP_SUBS):
        @pl.when(jflag_ref[j * MOE_STEP_SUBS + s] > 0)
        def _(s=s):
            rows = slice(s * MOE_T, (s + 1) * MOE_T)
            acc_ref[rows, :] += _dot(weights(s, MOE_OVF, MOE_S), ovf_ref[s])

    y = _rmsnorm(acc_ref[...], gf_ref[...])

    @pl.when(j < npb)
    def _():
        yp_ref[...] = y

    @pl.when(j >= npb)
    def _():
        ys_ref[...] = y


def _moe_combine(hp, hs, main, ovf, info, jflag, gf):
    nsp = hp.shape[0] // MOE_T
    ns = main.shape[0]
    sub = MOE_STEP_SUBS
    sflag = jnp.max(jflag.reshape(ns // sub, sub), axis=1)
    grid_spec = pltpu.PrefetchScalarGridSpec(
        num_scalar_prefetch=2,
        grid=(ns // sub,),
        in_specs=[*_two_group_specs(nsp, D_MODEL),
                  pl.BlockSpec((sub, MOE_S, D_MODEL), lambda j, jf, sf: (j, 0, 0)),
                  pl.BlockSpec((sub, MOE_OVF, D_MODEL), lambda j, jf, sf: (jnp.where(sf[j] > 0, j, 0), 0, 0)),
                  pl.BlockSpec((sub * MOE_T, LANES), lambda j, jf, sf: (j, 0)),
                  pl.BlockSpec((1, D_MODEL), lambda j, jf, sf: (0, 0))],
        out_specs=list(_two_group_specs(nsp, D_MODEL)),
        scratch_shapes=[pltpu.VMEM((sub * MOE_T, D_MODEL), F32)],
    )
    return pl.pallas_call(
        functools.partial(_moe_combine_kernel, nsp=nsp),
        grid_spec=grid_spec,
        out_shape=[jax.ShapeDtypeStruct(hp.shape, F32), jax.ShapeDtypeStruct(hs.shape, F32)],
        compiler_params=_cparams(("arbitrary",)),
        name="moe_combine",
    )(jflag, sflag, hp, hs, main, ovf, info, gf)


def _moe(hp, hs, p):
    main, ovf, info, meta = _moe_route(hp, hs, p["norm_moe_g"], p["w_rg"], p["b_rg"], p["w_re"], p["b_re"])
    pn = meta[:, 1, :N_EXPERTS]
    offs = meta[:, 0, :N_EXPERTS].reshape(-1)
    eflag = (jnp.sum(pn, axis=0) > 0).astype(jnp.int32)
    jflag = (jnp.sum(pn, axis=1) > 0).astype(jnp.int32)
    main, ovf = _moe_expert(main, ovf, offs, pn.reshape(-1), eflag, p["w_eg"], p["w_eu"], p["w_ed"])
    return _moe_combine(hp, hs, main, ovf, info, jflag, p["norm_final_g"])


def _trunk(x, mem_k, mem_v, c0, n0, m0, conv0, p, *, emit_gv):
    b, l, _ = x.shape
    n = b * l
    cl = min(l, GMLP_CHUNK)
    x2d = x.reshape(n, D_MODEL)
    q, kt, v, osig, gt, conv_new = _inproj_a(
        x2d, b, l, p["norm_mix_g"], p["w_qk"], p["w_v"], p["w_o"], p["w_ift"], p["b_if"],
        p["conv_w"], p["conv_b"], conv0)
    outs = _inproj_b(x2d, b, l, p["norm_mix_g"], p["w_u"], p["w_gv"], p["w_mq"], p["w_gate"],
                     p["gmlp_norm_g"], p["gmlp_norm_b"], p["w_s"][:, :cl, :cl], p["b_st"][:cl],
                     mem_k, mem_v, cl, emit_gv)
    ug, att, gates = outs[:3]
    m0b = jnp.broadcast_to(m0[..., None], m0.shape + (LANES,))
    st0 = jnp.concatenate([c0, jnp.broadcast_to(n0[..., None], n0.shape + (LANES,))], axis=-1)
    hh, st1, m1 = _mlstm(q, kt, v, gt, st0, m0b, b, l)
    c1, n1 = st1[..., :HEAD_DIM], st1[..., HEAD_DIM]
    h = _merge(x2d, hh, osig, p["mlstm_norm_g"], ug, att, gates,
               p["w_br_mlstm"], p["w_br_gmlp"], p["w_br_mem"], p["w_out"], TOKEN_TILE)
    return h, c1, n1, m1[..., 0], conv_new, (outs[3].reshape(b, l, D_MODEL) if emit_gv else None)


def kernel(x_prompt, x_sample, mem_prompt, cache_mem_k, cache_mem_v, state_mlstm_C, state_mlstm_n, state_mlstm_m, state_mlstm_conv, norm_mix_g, w_in, mlstm_i_b, mlstm_f_b, mlstm_conv_w, mlstm_conv_b, mlstm_norm_g, gmlp_norm_g, gmlp_norm_b, gmlp_w_s, gmlp_b_s, mem_norm_g, w_mem_k, w_mem_v, w_br_mlstm, w_br_gmlp, w_br_mem, w_out, norm_moe_g, w_router_group, b_router_group, w_router_expert, b_router_expert, w_exp_gate, w_exp_up, w_exp_down, norm_final_g):
    bp = x_prompt.shape[0]
    bs = x_sample.shape[0]
    W = D_MODEL
    wi = w_in[0]
    o_qk, o_v, o_o, o_i = 0, 2 * W, 3 * W, 4 * W
    o_f = o_i + HEADS
    o_u = o_f + HEADS
    o_gv, o_mq, o_gate = o_u + W, o_u + 2 * W, o_u + 3 * W
    row = lambda a: a.reshape(1, -1)
    pad_l = lambda a: jnp.pad(a, ((0, 0), (0, LANES - a.shape[1])))
    p = {
        "norm_mix_g": row(norm_mix_g[0]),
        "w_qk": wi[:, o_qk:o_v].astype(BF16), "w_v": wi[:, o_v:o_o].astype(BF16),
        "w_o": wi[:, o_o:o_i].astype(BF16),
        "w_ift": wi[:, o_i:o_u].T.astype(BF16),
        "b_if": jnp.concatenate([mlstm_i_b[0], mlstm_f_b[0]]).reshape(2 * HEADS, 1),
        "conv_w": mlstm_conv_w[0], "conv_b": row(mlstm_conv_b[0]),
        "w_u": wi[:, o_u:o_gv].astype(BF16), "w_gv": wi[:, o_gv:o_mq].astype(BF16),
        "w_mq": wi[:, o_mq:o_gate].astype(BF16), "w_gate": wi[:, o_gate:].astype(BF16),
        "gmlp_norm_g": row(gmlp_norm_g[0]), "gmlp_norm_b": row(gmlp_norm_b[0]),
        "w_s": gmlp_w_s[0], "b_st": gmlp_b_s[0].T,
        "mlstm_norm_g": row(mlstm_norm_g[0]),
        "w_br_mlstm": w_br_mlstm[0].astype(BF16), "w_br_gmlp": w_br_gmlp[0].astype(BF16),
        "w_br_mem": w_br_mem[0].astype(BF16), "w_out": w_out[0].astype(BF16),
        "norm_moe_g": row(norm_moe_g[0]),
        "w_rg": pad_l(w_router_group[0]).astype(BF16), "b_rg": pad_l(row(b_router_group[0])),
        "w_re": pad_l(w_router_expert[0]).astype(BF16), "b_re": pad_l(row(b_router_expert[0])),
        "w_eg": w_exp_gate[0], "w_eu": w_exp_up[0], "w_ed": w_exp_down[0],
        "norm_final_g": row(norm_final_g),
    }

    mk_p, mv_p = _memory_kv(mem_prompt.reshape(bp * N_MEM, W), row(mem_norm_g[0]),
                            w_mem_k[0].astype(BF16), w_mem_v[0].astype(BF16))
    mk_p3, mv_p3 = mk_p.reshape(bp, N_MEM, W), mv_p.reshape(bp, N_MEM, W)

    zeros = lambda *s: jnp.zeros(s, F32)
    hp, cp, np_, mp, cvp, _ = _trunk(
        x_prompt, mk_p3.astype(BF16), mv_p3.astype(BF16),
        zeros(bp, HEADS, HEAD_DIM, HEAD_DIM), zeros(bp, HEADS, HEAD_DIM), zeros(bp, HEADS),
        zeros(bp, CONV_W - 1, 2 * W), p, emit_gv=False)
    hs, cs, ns, ms, cvs, gvs = _trunk(
        x_sample, cache_mem_k[0].reshape(bs, N_MEM, W).astype(BF16),
        cache_mem_v[0].reshape(bs, N_MEM, W).astype(BF16),
        state_mlstm_C[0], state_mlstm_n[0], state_mlstm_m[0], state_mlstm_conv[0], p,
        emit_gv=True)

    yp, ys = _moe(hp, hs, p)
    kv_shape = (1, bp, N_MEM, HEADS, HEAD_DIM)
    return (yp.reshape(x_prompt.shape), ys.reshape(x_sample.shape), mk_p.reshape(kv_shape), mv_p.reshape(kv_shape),
            cp[None], np_[None], mp[None], cvp[None],
            cs[None], ns[None], ms[None], cvs[None], gvs[None])
```

```python
import functools

import jax
import jax.numpy as jnp
from jax import lax
from jax.experimental import pallas as pl
from jax.experimental.pallas import tpu as pltpu

D_MODEL = 1024
MLSTM_BLOCK = 128
TOKEN_TILE = 512
EPS = 1e-6
HEADS = 4
HEAD_DIM = 256
CONV_W = 4
GMLP_GROUPS = 4
GMLP_GROUP_DIM = 256
GMLP_CHUNK = 128
N_MEM = 256
N_GROUPS = 4
EXPERTS_PER_GROUP = 8
N_EXPERTS = 32
D_EXPERT = 256
LANES = 128
CONV_PAD = 8

F32 = jnp.float32
BF16 = jnp.bfloat16
NEG_INF = float("-inf")

VMEM_LIMIT = 56 * 1024 * 1024


def _cparams(sem):
    return pltpu.CompilerParams(dimension_semantics=sem, vmem_limit_bytes=VMEM_LIMIT)


def _const_spec(shape):
    nd = len(shape)
    return pl.BlockSpec(shape, lambda *_: (0,) * nd, pipeline_mode=pl.Buffered(1))


def _sigmoid(x):
    return 0.5 * (jnp.tanh(0.5 * x) + 1.0)


def _log_sigmoid(x):
    return jnp.minimum(x, 0.0) - jnp.log(1.0 + jnp.exp(-jnp.abs(x)))


def _rmsnorm(x, g):
    r = lax.rsqrt(jnp.mean(x * x, axis=-1, keepdims=True) + EPS)
    return (x * r) * g


def _dot(a, b):
    return jnp.dot(a, b, preferred_element_type=F32)


def _dot_nt(a, b):
    return lax.dot_general(a, b, (((1,), (1,)), ((), ())), preferred_element_type=F32)


def _dot_tn(a, b):
    return lax.dot_general(a, b, (((0,), (0,)), ((), ())), preferred_element_type=F32)


def _memkv_kernel(mem_ref, g_ref, wk_ref, wv_ref, k_ref, v_ref, kb_ref, vb_ref):
    mn = _rmsnorm(mem_ref[...], g_ref[...]).astype(BF16)
    k = _dot(mn, wk_ref[...].astype(BF16))
    v = _dot(mn, wv_ref[...].astype(BF16))
    k_ref[...] = k
    v_ref[...] = v
    kb_ref[...] = k.astype(BF16)
    vb_ref[...] = v.astype(BF16)


def _memory_kv(mem2d, g, wk, wv):
    n = mem2d.shape[0]
    tm = 512
    row = pl.BlockSpec((tm, D_MODEL), lambda i: (i, 0))
    return pl.pallas_call(
        _memkv_kernel,
        grid=(n // tm,),
        in_specs=[row, _const_spec((1, D_MODEL)), _const_spec((D_MODEL, D_MODEL)),
                  _const_spec((D_MODEL, D_MODEL))],
        out_specs=[row, row, row, row],
        out_shape=[jax.ShapeDtypeStruct((n, D_MODEL), F32)] * 2 + [jax.ShapeDtypeStruct((n, D_MODEL), BF16)] * 2,
        compiler_params=_cparams(("parallel",)),
        name="memory_kv",
    )(mem2d, g, wk, wv)


def _inproj_a_kernel(x_ref, g_ref, wqk_ref, wv_ref, wo_ref, wift_ref, bif_ref, cw_ref, cb_ref, cs_ref,
                     q_ref, kt_ref, v_ref, o_ref, gt_ref, cn_ref, ext_ref, *, nseg, sl, lc):
    i = pl.program_id(1)
    tm = nseg * sl
    tail = CONV_PAD - (CONV_W - 1)

    @pl.when(i == 0)
    def _():
        for s in range(nseg):
            ext_ref[s, 0:tail, :] = jnp.zeros((tail, 2 * D_MODEL), F32)
            ext_ref[s, tail:CONV_PAD, :] = cs_ref[s]

    xn = _rmsnorm(x_ref[...], g_ref[...]).astype(BF16)

    zqk = _dot(xn, wqk_ref[...])
    ks = []
    row8 = lax.broadcasted_iota(jnp.int32, (CONV_PAD, 2 * D_MODEL), 0)
    for s in range(nseg):
        cur = zqk[s * sl:(s + 1) * sl, :]
        prev = ext_ref[s]
        acc = cb_ref[...] + cur * cw_ref[CONV_W - 1:CONV_W, :]
        for d in range(1, CONV_W):
            back = pltpu.roll(cur, d, axis=0)
            head = jnp.where(row8 < d, pltpu.roll(prev, d, axis=0), back[0:CONV_PAD, :])
            back = jnp.concatenate([head, back[CONV_PAD:, :]], axis=0)
            acc = acc + back * cw_ref[CONV_W - 1 - d:CONV_W - d, :]
        qk = acc * _sigmoid(acc)
        q_ref[s * sl:(s + 1) * sl, :] = (qk[:, :D_MODEL] * (HEAD_DIM ** -0.5)).astype(BF16)
        ks.append(qk[:, D_MODEL:])
        ext_ref[s] = cur[sl - CONV_PAD:sl, :]
        cn_ref[s] = ext_ref[s, tail:CONV_PAD, :]
    if tm % LANES:
        ks.append(jnp.zeros((LANES - tm % LANES, D_MODEL), F32))
    k = jnp.concatenate(ks, axis=0) if len(ks) > 1 else ks[0]
    kt = k.T.astype(BF16)
    nch = tm // lc
    per_seq = sl // lc
    for c in range(nch):
        kt_ref[c // per_seq, c % per_seq] = kt[:, c * lc:(c + 1) * lc]

    v_ref[...] = _dot(xn, wv_ref[...]).astype(BF16)
    o_ref[...] = _sigmoid(_dot(xn, wo_ref[...])).astype(BF16)

    z = jnp.concatenate([_dot_nt(wift_ref[...], xn[c * lc:(c + 1) * lc, :]) + bif_ref[...]
                         for c in range(nch)], axis=0)
    is_ig = (lax.broadcasted_iota(jnp.int32, z.shape, 0) % (2 * HEADS)) < HEADS
    g = jnp.where(is_ig, z, _log_sigmoid(z))
    upper = jnp.where(lax.broadcasted_iota(jnp.int32, (lc, lc), 0)
                      <= lax.broadcasted_iota(jnp.int32, (lc, lc), 1), 1.0, 0.0)
    bc = jnp.dot(g, upper, preferred_element_type=F32, precision=lax.Precision.HIGHEST)
    a = g - pltpu.roll(bc, nch * 2 * HEADS - HEADS, axis=0)
    amax = jnp.broadcast_to(jnp.max(a, axis=-1, keepdims=True), z.shape)
    gb = jnp.where(is_ig, g, bc)
    for c in range(nch):
        gt_ref[c // per_seq, c % per_seq, 0:2 * HEADS, :] = gb[c * 8:(c + 1) * 8, :]
        gt_ref[c // per_seq, c % per_seq, 2 * HEADS:4 * HEADS, :] = amax[c * 8:(c + 1) * 8, :]


def _tile_geometry(b, l):
    sl = min(l, TOKEN_TILE)
    nseg = max(1, min(b, TOKEN_TILE // sl))
    return nseg, sl


def _tok_spec(nseg, sl, nt, w):
    return pl.BlockSpec((nseg * sl, w), lambda bi, i: (bi * nt + i, 0))


def _inproj_a(x2d, b, l, g, wqk, wv, wo, wift, bif, cw, cb, cs):
    nseg, sl = _tile_geometry(b, l)
    nt = l // sl
    n = b * l
    CHUNK = min(MLSTM_BLOCK, l)
    per_seq = sl // CHUNK
    tok = functools.partial(_tok_spec, nseg, sl, nt)
    state = pl.BlockSpec((nseg, CONV_W - 1, 2 * D_MODEL), lambda bi, i: (bi, 0, 0))
    return pl.pallas_call(
        functools.partial(_inproj_a_kernel, nseg=nseg, sl=sl, lc=CHUNK),
        grid=(b // nseg, nt),
        in_specs=[tok(D_MODEL), _const_spec((1, D_MODEL)), _const_spec((D_MODEL, 2 * D_MODEL)),
                  _const_spec((D_MODEL, D_MODEL)), _const_spec((D_MODEL, D_MODEL)),
                  _const_spec((2 * HEADS, D_MODEL)), _const_spec((2 * HEADS, 1)),
                  _const_spec((CONV_W, 2 * D_MODEL)), _const_spec((1, 2 * D_MODEL)), state],
        out_specs=[tok(D_MODEL),
                   pl.BlockSpec((nseg, per_seq, D_MODEL, CHUNK), lambda bi, i: (bi, i, 0, 0)),
                   tok(D_MODEL), tok(D_MODEL),
                   pl.BlockSpec((nseg, per_seq, 4 * HEADS, CHUNK), lambda bi, i: (bi, i, 0, 0)),
                   state],
        out_shape=[jax.ShapeDtypeStruct((n, D_MODEL), BF16),
                   jax.ShapeDtypeStruct((b, l // CHUNK, D_MODEL, CHUNK), BF16),
                   jax.ShapeDtypeStruct((n, D_MODEL), BF16), jax.ShapeDtypeStruct((n, D_MODEL), BF16),
                   jax.ShapeDtypeStruct((b, l // CHUNK, 4 * HEADS, CHUNK), F32),
                   jax.ShapeDtypeStruct((b, CONV_W - 1, 2 * D_MODEL), F32)],
        scratch_shapes=[pltpu.VMEM((nseg, CONV_PAD, 2 * D_MODEL), F32)],
        compiler_params=_cparams(("parallel", "arbitrary")),
        name="inproj_a",
    )(x2d, g, wqk, wv, wo, wift, bif, cw, cb, cs)


def _inproj_b_kernel(x_ref, g_ref, wu_ref, wgv_ref, wmq_ref, wgate_ref, lng_ref, lnb_ref, ws_ref, bst_ref,
                     mk_ref, mv_ref, ug_ref, att_ref, gates_ref, *rest, nseg, sl, cl, emit_gv):
    tm = nseg * sl
    xn = _rmsnorm(x_ref[...], g_ref[...]).astype(BF16)

    gates_ref[...] = _sigmoid(_dot(xn, wgate_ref[...])).astype(BF16)

    gvr = jax.nn.gelu(_dot(xn, wgv_ref[...]))
    mu = jnp.mean(gvr, axis=-1, keepdims=True)
    xc = gvr - mu
    r = lax.rsqrt(jnp.mean(xc * xc, axis=-1, keepdims=True) + EPS)
    gv = (xc * r) * lng_ref[...] + lnb_ref[...]
    if emit_gv:
        rest[0][...] = gv
    gvb = gv.astype(BF16)
    u = jax.nn.gelu(_dot(xn, wu_ref[...]))
    tri = (lax.broadcasted_iota(jnp.int32, (cl, cl), 0) >= lax.broadcasted_iota(jnp.int32, (cl, cl), 1))
    for gi in range(GMLP_GROUPS):
        wsg = jnp.where(tri, ws_ref[gi], 0.0).astype(BF16)
        lo, hi = gi * GMLP_GROUP_DIM, (gi + 1) * GMLP_GROUP_DIM
        for c in range(tm // cl):
            sp = _dot(wsg, gvb[c * cl:(c + 1) * cl, lo:hi]) + bst_ref[:, gi:gi + 1]
            ug_ref[c * cl:(c + 1) * cl, lo:hi] = (u[c * cl:(c + 1) * cl, lo:hi] * sp).astype(BF16)

    mq = _dot(xn, wmq_ref[...]).astype(BF16)
    for s in range(nseg):
        r0, r1 = s * sl, (s + 1) * sl
        for h in range(HEADS):
            lo, hi = h * HEAD_DIM, (h + 1) * HEAD_DIM
            sc = _dot_nt(mq[r0:r1, lo:hi], mk_ref[s, :, lo:hi]) * (HEAD_DIM ** -0.5)
            e = jnp.exp(sc - jnp.max(sc, axis=-1, keepdims=True))
            a = (e / jnp.sum(e, axis=-1, keepdims=True)).astype(BF16)
            att_ref[r0:r1, lo:hi] = _dot(a, mv_ref[s, :, lo:hi]).astype(BF16)


def _inproj_b(x2d, b, l, g, wu, wgv, wmq, wgate, lng, lnb, ws, bst, mk, mv, cl, emit_gv):
    nseg, sl = _tile_geometry(b, l)
    nt = l // sl
    n = b * l
    tok = functools.partial(_tok_spec, nseg, sl, nt)
    mem = pl.BlockSpec((nseg, N_MEM, D_MODEL), lambda bi, i: (bi, 0, 0))
    out_specs = [tok(D_MODEL), tok(D_MODEL), tok(3 * D_MODEL)]
    out_shape = [jax.ShapeDtypeStruct((n, D_MODEL), BF16), jax.ShapeDtypeStruct((n, D_MODEL), BF16),
                 jax.ShapeDtypeStruct((n, 3 * D_MODEL), BF16)]
    if emit_gv:
        out_specs.append(tok(D_MODEL))
        out_shape.append(jax.ShapeDtypeStruct((n, D_MODEL), F32))
    return pl.pallas_call(
        functools.partial(_inproj_b_kernel, nseg=nseg, sl=sl, cl=cl, emit_gv=emit_gv),
        grid=(b // nseg, nt),
        in_specs=[tok(D_MODEL), _const_spec((1, D_MODEL)), _const_spec((D_MODEL, D_MODEL)),
                  _const_spec((D_MODEL, D_MODEL)), _const_spec((D_MODEL, D_MODEL)),
                  _const_spec((D_MODEL, 3 * D_MODEL)), _const_spec((1, D_MODEL)), _const_spec((1, D_MODEL)),
                  _const_spec((GMLP_GROUPS, cl, cl)), _const_spec((cl, GMLP_GROUPS)), mem, mem],
        out_specs=out_specs,
        out_shape=out_shape,
        compiler_params=_cparams(("parallel", "parallel")),
        name="inproj_b",
    )(x2d, g, wu, wgv, wmq, wgate, lng, lnb, ws, bst, mk, mv)


def _mlstm_kernel(q_ref, kt_ref, v_ref, gt_ref, *rest, nseg, cb, zero_state):
    i = pl.program_id(1)

    if zero_state:
        hm_ref, c_ref, n_ref, m_ref, st_ref = rest
    else:
        c0_ref, n0_ref, m0_ref, hm_ref, c_ref, n_ref, m_ref, st_ref = rest

    @pl.when(i == 0)
    def _():
        if zero_state:
            st_ref[...] = jnp.zeros(st_ref.shape, F32)
            m_ref[...] = jnp.zeros(m_ref.shape, F32)
        else:
            st_ref[:, :, :, :HEAD_DIM] = c0_ref[...]
            st_ref[:, :, :, HEAD_DIM:] = n0_ref[...]
            m_ref[...] = m0_ref[...]

    for s in range(nseg):
        _mlstm_sequence(q_ref, kt_ref.at[s], v_ref, gt_ref.at[s], st_ref.at[s], m_ref.at[s], hm_ref, s * cb, cb)

    @pl.when(i == pl.num_programs(1) - 1)
    def _():
        c_ref[...] = st_ref[:, :, :, :HEAD_DIM]
        n_ref[...] = st_ref[:, :, :, HEAD_DIM:]


def _mlstm_sequence(q_ref, kt_ref, v_ref, gt_ref, c_ref, m_ref, hm_ref, row0, cb):
    L = kt_ref.shape[-1]
    nch = cb // L
    ti = lax.broadcasted_iota(jnp.int32, (L, L), 0)
    si = lax.broadcasted_iota(jnp.int32, (L, L), 1)
    tri = ti >= si
    eye = ti == si

    rows = 4 * HEADS
    g_all = gt_ref[...].reshape(nch * rows, L)

    m_in = [m_ref[:, 0:1]]
    for c in range(nch):
        b_last4 = g_all[c * rows + HEADS:c * rows + 2 * HEADS, L - 1:L]
        amax4 = g_all[c * rows + 2 * HEADS:c * rows + 3 * HEADS, 0:1]
        m_in.append(jnp.maximum(b_last4 + m_in[-1], b_last4 + amax4))

    ones = jnp.ones((L, LANES), BF16)
    st = [c_ref[h] for h in range(HEADS)]
    for c in range(nch):
        r0, r1 = row0 + c * L, row0 + (c + 1) * L
        for h in range(HEADS):
            lo, hi = h * HEAD_DIM, (h + 1) * HEAD_DIM
            ig_r = g_all[c * rows + h:c * rows + h + 1, :]
            bc_r = g_all[c * rows + HEADS + h:c * rows + HEADS + h + 1, :]
            a_r = ig_r - bc_r
            bc_c = jnp.sum(jnp.where(eye, bc_r, 0.0), axis=-1, keepdims=True)
            m0 = m_in[c][h:h + 1, :]
            m_last = m_in[c + 1][h:h + 1, :]
            dmat = jnp.where(tri, bc_c + a_r, NEG_INF)
            inter = bc_c + m0
            m = jnp.maximum(inter, jnp.max(dmat, axis=-1, keepdims=True))
            w_intra = jnp.exp(dmat - m)
            w_inter = jnp.exp(inter - m)
            q = q_ref[r0:r1, lo:hi]
            kt = kt_ref[c, lo:hi, :]
            v = v_ref[r0:r1, lo:hi]
            s = _dot(q, kt) * w_intra
            qs = _dot(q, st[h].astype(BF16))
            num = w_inter * qs[:, :HEAD_DIM] + _dot(s.astype(BF16), v)
            den = w_inter * qs[:, HEAD_DIM:HEAD_DIM + 1] + jnp.sum(s, axis=-1, keepdims=True)
            hh = num / jnp.maximum(jnp.abs(den), jnp.exp(-m))
            bc_last = bc_r[:, L - 1:L]
            w_last = jnp.exp(bc_last + a_r - m_last)
            decay = jnp.exp(bc_last + m0 - m_last)
            ktw = (kt.astype(F32) * w_last).astype(BF16)
            st[h] = decay * st[h] + _dot(ktw, jnp.concatenate([v, ones], axis=1))
            hm_ref[r0:r1, lo:hi] = hh.astype(BF16)

    for h in range(HEADS):
        c_ref[h] = st[h]
    m_ref[...] = jnp.broadcast_to(m_in[nch], (HEADS, LANES))


def _mlstm(q, kt, v, gt, state, b, l):
    nseg, cb = _tile_geometry(b, l)
    nt = l // cb
    CHUNK = kt.shape[-1]
    tok = _tok_spec(nseg, cb, nt, D_MODEL)
    cs = pl.BlockSpec((nseg, HEADS, HEAD_DIM, HEAD_DIM), lambda bi, i: (bi, 0, 0, 0))
    ns = pl.BlockSpec((nseg, HEADS, HEAD_DIM, LANES), lambda bi, i: (bi, 0, 0, 0))
    ms = pl.BlockSpec((nseg, HEADS, LANES), lambda bi, i: (bi, 0, 0))
    state_specs = [] if state is None else [cs, ns, ms]
    return pl.pallas_call(
        functools.partial(_mlstm_kernel, nseg=nseg, cb=cb, zero_state=state is None),
        grid=(b // nseg, nt),
        in_specs=[tok, pl.BlockSpec((nseg, cb // CHUNK, D_MODEL, CHUNK), lambda bi, i: (bi, i, 0, 0)), tok,
                  pl.BlockSpec((nseg, cb // CHUNK, 4 * HEADS, CHUNK), lambda bi, i: (bi, i, 0, 0))] + state_specs,
        out_specs=[tok, cs, ns, ms],
        out_shape=[jax.ShapeDtypeStruct((b * l, D_MODEL), BF16),
                   jax.ShapeDtypeStruct((b, HEADS, HEAD_DIM, HEAD_DIM), F32),
                   jax.ShapeDtypeStruct((b, HEADS, HEAD_DIM, LANES), F32),
                   jax.ShapeDtypeStruct((b, HEADS, LANES), F32)],
        scratch_shapes=[pltpu.VMEM((nseg, HEADS, HEAD_DIM, HEAD_DIM + LANES), F32)],
        compiler_params=_cparams(("parallel", "arbitrary")),
        name="mlstm",
    )(q, kt, v, gt, *(state or ()))


def _merge_kernel(x_ref, hh_ref, o_ref, ng_ref, ug_ref, att_ref, gates_ref, wa_ref, wb_ref, wc_ref, wo_ref, h_ref):
    parts = []
    for h in range(HEADS):
        lo, hi = h * HEAD_DIM, (h + 1) * HEAD_DIM
        hh = hh_ref[:, lo:hi].astype(F32)
        hn = hh * lax.rsqrt(jnp.mean(hh * hh, axis=-1, keepdims=True) + EPS)
        parts.append(((hn * ng_ref[:, lo:hi]) * o_ref[:, lo:hi].astype(F32)).astype(BF16))
    hm = jnp.concatenate(parts, axis=1)
    g = gates_ref[...].astype(F32)
    mixed = g[:, :D_MODEL] * _dot(hm, wa_ref[...])
    mixed = mixed + g[:, D_MODEL:2 * D_MODEL] * _dot(ug_ref[...], wb_ref[...])
    mixed = mixed + g[:, 2 * D_MODEL:] * _dot(att_ref[...], wc_ref[...])
    h_ref[...] = x_ref[...] + _dot(mixed.astype(BF16), wo_ref[...])


def _merge(x2d, hh, osig, ng, ug, att, gates, wa, wb, wc, wo, tm):
    n = x2d.shape[0]
    row = lambda w: pl.BlockSpec((tm, w), lambda i: (i, 0))
    wspec = _const_spec((D_MODEL, D_MODEL))
    return pl.pallas_call(
        _merge_kernel,
        grid=(n // tm,),
        in_specs=[row(D_MODEL), row(D_MODEL), row(D_MODEL), _const_spec((1, D_MODEL)), row(D_MODEL), row(D_MODEL),
                  row(3 * D_MODEL), wspec, wspec, wspec, wspec],
        out_specs=row(D_MODEL),
        out_shape=jax.ShapeDtypeStruct((n, D_MODEL), F32),
        compiler_params=_cparams(("parallel",)),
        name="merge",
    )(x2d, hh, osig, ng, ug, att, gates, wa, wb, wc, wo)


MOE_T = 256
MOE_STEP_SUBS = 2
MOE_CAP = 32
MOE_S = N_EXPERTS * MOE_CAP
MOE_OVF = 512
SEG_ALIGN = 16
EXPERT_BLOCK = 128


def _moe_route_kernel(hp_ref, hs_ref, g_ref, wrg_ref, brg_ref, wre_ref, bre_ref,
                      main_ref, ovf_ref, info_ref, meta_ref, *, nsp):
    n = MOE_STEP_SUBS * MOE_T
    h = jnp.where(pl.program_id(0) < nsp // MOE_STEP_SUBS, hp_ref[...], hs_ref[...])
    xm = _rmsnorm(h, g_ref[...]).astype(BF16)
    lane = lax.broadcasted_iota(jnp.int32, (n, LANES), 1).astype(F32)
    lg = jnp.where(lane < N_GROUPS, _dot(xm, wrg_ref[...]) + brg_ref[...], NEG_INF)
    gmax = jnp.max(lg, axis=-1, keepdims=True)
    p_top = 1.0 / jnp.sum(jnp.exp(lg - gmax), axis=-1, keepdims=True)
    grp = jnp.min(jnp.where(lg == gmax, lane, float(LANES)), axis=-1, keepdims=True)
    el = _dot(xm, wre_ref[...]) + bre_ref[...]
    in_grp = (lane >= grp * EXPERTS_PER_GROUP) & (lane < (grp + 1.0) * EXPERTS_PER_GROUP)
    vals = jnp.where(in_grp, el, NEG_INF)
    v1 = jnp.max(vals, axis=-1, keepdims=True)
    i1 = jnp.min(jnp.where(vals == v1, lane, float(LANES)), axis=-1, keepdims=True)
    vals2 = jnp.where(lane == i1, NEG_INF, vals)
    v2 = jnp.max(vals2, axis=-1, keepdims=True)
    i2 = jnp.min(jnp.where(vals2 == v2, lane, float(LANES)), axis=-1, keepdims=True)
    r = jnp.exp(v2 - v1)
    p1 = p_top / (1.0 + r)
    p2 = p_top * r / (1.0 + r)
    sel1 = lane == i1
    sel2 = lane == i2
    onehot = jnp.where(sel1 | sel2, 1.0, 0.0)

    deferred = []
    for s in range(MOE_STEP_SUBS):
        rows = slice(s * MOE_T, (s + 1) * MOE_T)
        deferred.append(_route_sub_tile(
            xm[rows, :], onehot[rows, :], i1[rows, :], i2[rows, :],
            p1[rows, :], p2[rows, :], main_ref.at[s], ovf_ref.at[s], info_ref.at[rows, :], meta_ref.at[s]))
    for write_overflow in deferred:
        write_overflow()


def _route_sub_tile(xm, onehot, i1, i2, p1, p2, main_ref, ovf_ref, info_ref, meta_ref):
    t = MOE_T
    lane = lax.broadcasted_iota(jnp.int32, (t, LANES), 1)
    sel1 = lane.astype(F32) == i1
    sel2 = lane.astype(F32) == i2
    cnt = jnp.sum(onehot, axis=0, keepdims=True).astype(jnp.int32)
    pn = jnp.bitwise_and(jnp.maximum(cnt - MOE_CAP, 0) + (SEG_ALIGN - 1), -SEG_ALIGN)
    pn8 = jnp.broadcast_to(pn, (8, LANES))
    earlier = jnp.where(lax.broadcasted_iota(jnp.int32, (LANES, LANES), 0)
                        < lax.broadcasted_iota(jnp.int32, (LANES, LANES), 1), 1.0, 0.0).astype(BF16)
    off_f8 = _dot(pn8.astype(F32).astype(BF16), earlier)
    row8 = lax.broadcasted_iota(jnp.int32, (8, LANES), 0)
    meta_ref[...] = jnp.where(row8 == 0, off_f8.astype(jnp.int32), jnp.where(row8 == 1, pn8, 0))

    ti = lax.broadcasted_iota(jnp.int32, (t, t), 0)
    si = lax.broadcasted_iota(jnp.int32, (t, t), 1)
    before = jnp.where(ti > si, 1.0, 0.0).astype(BF16)
    rank = _dot(before, onehot.astype(BF16))
    off_f = off_f8[0:1, :]

    def slot_row(sel, idx):
        rk = jnp.sum(jnp.where(sel, rank, 0.0), axis=-1, keepdims=True)
        of = jnp.sum(jnp.where(sel, off_f, 0.0), axis=-1, keepdims=True)
        return jnp.where(rk < MOE_CAP, idx * MOE_CAP + rk, MOE_S - MOE_CAP + of + rk)

    pos1 = slot_row(sel1, i1)
    pos2 = slot_row(sel2, i2)
    info_ref[...] = (jnp.where(lane == 0, pos1, 0.0) + jnp.where(lane == 1, pos2, 0.0)
                     + jnp.where(lane == 2, p1, 0.0) + jnp.where(lane == 3, p2, 0.0))

    eye = ti == si
    pos1_r = jnp.sum(jnp.where(eye, pos1, 0.0), axis=0, keepdims=True)
    pos2_r = jnp.sum(jnp.where(eye, pos2, 0.0), axis=0, keepdims=True)

    def gather(rows, base):
        srow = (lax.broadcasted_iota(jnp.int32, (rows, t), 0) + base).astype(F32)
        pick = jnp.where((srow == pos1_r) | (srow == pos2_r), 1.0, 0.0).astype(BF16)
        return _dot(pick, xm).astype(BF16)

    main_ref[...] = gather(MOE_S, 0)
    has_ovf = jnp.sum(pn) > 0

    def write_overflow():
        @pl.when(has_ovf)
        def _():
            ovf_ref[...] = gather(MOE_OVF, MOE_S)

        @pl.when(jnp.logical_not(has_ovf))
        def _():
            ovf_ref[...] = jnp.zeros(ovf_ref.shape, ovf_ref.dtype)
    return write_overflow


def _two_group_specs(nsp, cols):
    rows = MOE_STEP_SUBS * MOE_T
    npb = nsp // MOE_STEP_SUBS
    return (pl.BlockSpec((rows, cols), lambda j, *_: (jnp.minimum(j, npb - 1), 0)),
            pl.BlockSpec((rows, cols), lambda j, *_: (jnp.maximum(j - npb, 0), 0)))


def _moe_route(hp, hs, g, wrg, brg, wre, bre):
    nsp = hp.shape[0] // MOE_T
    ns = nsp + hs.shape[0] // MOE_T
    sub = MOE_STEP_SUBS
    return pl.pallas_call(
        functools.partial(_moe_route_kernel, nsp=nsp),
        grid=(ns // sub,),
        in_specs=[*_two_group_specs(nsp, D_MODEL), _const_spec((1, D_MODEL)),
                  _const_spec((D_MODEL, LANES)), _const_spec((1, LANES)),
                  _const_spec((D_MODEL, LANES)), _const_spec((1, LANES))],
        out_specs=[pl.BlockSpec((sub, MOE_S, D_MODEL), lambda j: (j, 0, 0)),
                   pl.BlockSpec((sub, MOE_OVF, D_MODEL), lambda j: (j, 0, 0)),
                   pl.BlockSpec((sub * MOE_T, LANES), lambda j: (j, 0)),
                   pl.BlockSpec((sub, 8, LANES), lambda j: (j, 0, 0))],
        out_shape=[jax.ShapeDtypeStruct((ns, MOE_S, D_MODEL), BF16),
                   jax.ShapeDtypeStruct((ns, MOE_OVF, D_MODEL), BF16),
                   jax.ShapeDtypeStruct((ns * MOE_T, LANES), F32),
                   jax.ShapeDtypeStruct((ns, 8, LANES), jnp.int32)],
        compiler_params=_cparams(("arbitrary",)),
        name="moe_route",
    )(hp, hs, g, wrg, brg, wre, bre)


def _swiglu(x, wgu, wdb):
    gu = _dot(x, wgu[...])
    gate = gu[:, :D_EXPERT]
    hid = (gate * _sigmoid(gate)) * gu[:, D_EXPERT:]
    return _dot(hid.astype(BF16), wdb[...]).astype(BF16)


def _moe_expert_kernel(offs_ref, pns_ref, eflag_ref, main_ref, slots_hbm, wg_ref, wu_ref, wd_ref,
                       mout_ref, out_hbm, xbuf, obuf, wgu, wdb, sem_in, sem_out, *, n_sub, jb, n_rb):
    e = pl.program_id(0)
    rb = pl.program_id(1)

    @pl.when((e == 0) & (rb == 0))
    def _():
        xbuf[...] = jnp.zeros_like(xbuf)

    @pl.when(rb == 0)
    def _():
        wgu[:, :D_EXPERT] = wg_ref[...].astype(BF16)
        wgu[:, D_EXPERT:] = wu_ref[...].astype(BF16)
        wdb[...] = wd_ref[...].astype(BF16)

    sb = main_ref.shape[0]
    part = sb // 2 if sb % 2 == 0 else sb
    for s0 in range(0, sb, part):
        y = _swiglu(main_ref[s0:s0 + part].reshape(part * MOE_CAP, D_MODEL), wgu, wdb)
        mout_ref[s0:s0 + part] = y.reshape(part, MOE_CAP, D_MODEL)

    def copy_in(j, src, dst):
        return pltpu.make_async_copy(slots_hbm.at[j, pl.ds(src, SEG_ALIGN), :],
                                     xbuf.at[pl.ds(dst, SEG_ALIGN), :], sem_in)

    def copy_out(j, src, dst):
        return pltpu.make_async_copy(obuf.at[pl.ds(src, SEG_ALIGN), :],
                                     out_hbm.at[j, pl.ds(dst, SEG_ALIGN), :], sem_out)

    def for_each_chunk(g, fn):
        def seg(jj, cur):
            j = g * jb + jj
            n = pns_ref[j * N_EXPERTS + e]
            off = offs_ref[j * N_EXPERTS + e]

            def chunk(k, c):
                fn(j, pl.multiple_of(off + k * SEG_ALIGN, SEG_ALIGN), pl.multiple_of(cur + k * SEG_ALIGN, SEG_ALIGN))
                return c
            lax.fori_loop(0, n // SEG_ALIGN, chunk, 0)
            return cur + n
        return lax.fori_loop(0, jb, seg, 0)

    def group(g, carry):
        total = for_each_chunk(g, lambda j, r, b: copy_in(j, r, b).start())
        nchunk = total // SEG_ALIGN

        def wait_in(k, c):
            copy_in(0, 0, 0).wait()
            return c
        lax.fori_loop(0, nchunk, wait_in, 0)

        def block(bi, c):
            r0 = pl.multiple_of(bi * EXPERT_BLOCK, EXPERT_BLOCK)
            obuf[pl.ds(r0, EXPERT_BLOCK), :] = _swiglu(xbuf[pl.ds(r0, EXPERT_BLOCK), :], wgu, wdb)
            return c
        lax.fori_loop(0, (total + EXPERT_BLOCK - 1) // EXPERT_BLOCK, block, 0)

        for_each_chunk(g, lambda j, r, b: copy_out(j, b, r).start())

        def wait_out(k, c):
            copy_out(0, 0, 0).wait()
            return c
        lax.fori_loop(0, nchunk, wait_out, 0)
        return carry

    @pl.when((rb == n_rb - 1) & (eflag_ref[e] > 0))
    def _():
        lax.fori_loop(0, n_sub // jb, group, 0)


def _moe_expert(main, ovf, offs, pns, eflag, wg, wu, wd):
    ns = main.shape[0]
    n_rb = 1
    sb = ns // n_rb
    jb = max(d for d in range(1, 12) if ns % d == 0)
    rows = jb * MOE_T + EXPERT_BLOCK
    main4 = main.reshape(ns, N_EXPERTS, MOE_CAP, D_MODEL)
    mspec = pl.BlockSpec((sb, None, MOE_CAP, D_MODEL), lambda e, rb, *_: (rb, e, 0, 0))
    wspec = lambda a, b: pl.BlockSpec((None, a, b), lambda e, rb, *_: (e, 0, 0))
    grid_spec = pltpu.PrefetchScalarGridSpec(
        num_scalar_prefetch=3,
        grid=(N_EXPERTS, n_rb),
        in_specs=[mspec, pl.BlockSpec(memory_space=pl.ANY),
                  wspec(D_MODEL, D_EXPERT), wspec(D_MODEL, D_EXPERT), wspec(D_EXPERT, D_MODEL)],
        out_specs=[mspec, pl.BlockSpec(memory_space=pl.ANY)],
        scratch_shapes=[pltpu.VMEM((rows, D_MODEL), BF16), pltpu.VMEM((rows, D_MODEL), BF16),
                        pltpu.VMEM((D_MODEL, 2 * D_EXPERT), BF16), pltpu.VMEM((D_EXPERT, D_MODEL), BF16),
                        pltpu.SemaphoreType.DMA(()), pltpu.SemaphoreType.DMA(())],
    )
    mout, oout = pl.pallas_call(
        functools.partial(_moe_expert_kernel, n_sub=ns, jb=jb, n_rb=n_rb),
        grid_spec=grid_spec,
        out_shape=[jax.ShapeDtypeStruct(main4.shape, main4.dtype), jax.ShapeDtypeStruct(ovf.shape, ovf.dtype)],
        input_output_aliases={3: 0, 4: 1},
        compiler_params=_cparams(("arbitrary", "arbitrary")),
        name="moe_expert",
    )(offs, pns, eflag, main4, ovf, wg, wu, wd)
    return mout.reshape(main.shape), oout


def _moe_combine_kernel(jflag_ref, sflag_ref, hp_ref, hs_ref, main_ref, ovf_ref, info_ref, gf_ref,
                        yp_ref, ys_ref, acc_ref, *, nsp):
    j = pl.program_id(0)
    npb = nsp // MOE_STEP_SUBS

    def weights(s, cols, base):
        info = info_ref[s * MOE_T:(s + 1) * MOE_T, :]
        scol = (lax.broadcasted_iota(jnp.int32, (MOE_T, cols), 1) + base).astype(F32)
        return (jnp.where(scol == info[:, 0:1], info[:, 2:3], 0.0)
                + jnp.where(scol == info[:, 1:2], info[:, 3:4], 0.0)).astype(BF16)

    for s in range(MOE_STEP_SUBS):
        rows = slice(s * MOE_T, (s + 1) * MOE_T)
        h = jnp.where(j < npb, hp_ref[rows, :], hs_ref[rows, :])
        acc_ref[rows, :] = h + _dot(weights(s, MOE_S, 0), main_ref[s])

    for s in range(MOE_STEP_SUBS):
        @pl.when(jflag_ref[j * MOE_STEP_SUBS + s] > 0)
        def _(s=s):
            rows = slice(s * MOE_T, (s + 1) * MOE_T)
            acc_ref[rows, :] += _dot(weights(s, MOE_OVF, MOE_S), ovf_ref[s])

    y = _rmsnorm(acc_ref[...], gf_ref[...])

    @pl.when(j < npb)
    def _():
        yp_ref[...] = y

    @pl.when(j >= npb)
    def _():
        ys_ref[...] = y


def _moe_combine(hp, hs, main, ovf, info, jflag, gf):
    nsp = hp.shape[0] // MOE_T
    ns = main.shape[0]
    sub = MOE_STEP_SUBS
    sflag = jnp.max(jflag.reshape(ns // sub, sub), axis=1)
    grid_spec = pltpu.PrefetchScalarGridSpec(
        num_scalar_prefetch=2,
        grid=(ns // sub,),
        in_specs=[*_two_group_specs(nsp, D_MODEL),
                  pl.BlockSpec((sub, MOE_S, D_MODEL), lambda j, jf, sf: (j, 0, 0)),
                  pl.BlockSpec((sub, MOE_OVF, D_MODEL), lambda j, jf, sf: (jnp.where(sf[j] > 0, j, 0), 0, 0)),
                  pl.BlockSpec((sub * MOE_T, LANES), lambda j, jf, sf: (j, 0)),
                  pl.BlockSpec((1, D_MODEL), lambda j, jf, sf: (0, 0))],
        out_specs=list(_two_group_specs(nsp, D_MODEL)),
        scratch_shapes=[pltpu.VMEM((sub * MOE_T, D_MODEL), F32)],
    )
    return pl.pallas_call(
        functools.partial(_moe_combine_kernel, nsp=nsp),
        grid_spec=grid_spec,
        out_shape=[jax.ShapeDtypeStruct(hp.shape, F32), jax.ShapeDtypeStruct(hs.shape, F32)],
        compiler_params=_cparams(("arbitrary",)),
        name="moe_combine",
    )(jflag, sflag, hp, hs, main, ovf, info, gf)


def _moe(hp, hs, p):
    main, ovf, info, meta = _moe_route(hp, hs, p["norm_moe_g"], p["w_rg"], p["b_rg"], p["w_re"], p["b_re"])
    pn = meta[:, 1, :N_EXPERTS]
    offs = meta[:, 0, :N_EXPERTS].reshape(-1)
    eflag = (jnp.sum(pn, axis=0) > 0).astype(jnp.int32)
    jflag = (jnp.sum(pn, axis=1) > 0).astype(jnp.int32)
    main, ovf = _moe_expert(main, ovf, offs, pn.reshape(-1), eflag, p["w_eg"], p["w_eu"], p["w_ed"])
    return _moe_combine(hp, hs, main, ovf, info, jflag, p["norm_final_g"])


def _trunk(x, mem_k, mem_v, c0, n0, m0, conv0, p, *, emit_gv):
    b, l, _ = x.shape
    n = b * l
    cl = min(l, GMLP_CHUNK)
    x2d = x.reshape(n, D_MODEL)
    q, kt, v, osig, gt, conv_new = _inproj_a(
        x2d, b, l, p["norm_mix_g"], p["w_qk"], p["w_v"], p["w_o"], p["w_ift"], p["b_if"],
        p["conv_w"], p["conv_b"], conv0)
    outs = _inproj_b(x2d, b, l, p["norm_mix_g"], p["w_u"], p["w_gv"], p["w_mq"], p["w_gate"],
                     p["gmlp_norm_g"], p["gmlp_norm_b"], p["w_s"][:, :cl, :cl], p["b_st"][:cl],
                     mem_k, mem_v, cl, emit_gv)
    ug, att, gates = outs[:3]
    on_lanes = lambda a: jnp.broadcast_to(a[..., None], a.shape + (LANES,))
    state = None if c0 is None else (c0, on_lanes(n0), on_lanes(m0))
    hh, c1, n1, m1 = _mlstm(q, kt, v, gt, state, b, l)
    n1 = n1[..., 0]
    h = _merge(x2d, hh, osig, p["mlstm_norm_g"], ug, att, gates,
               p["w_br_mlstm"], p["w_br_gmlp"], p["w_br_mem"], p["w_out"], TOKEN_TILE)
    return h, c1, n1, m1[..., 0], conv_new, (outs[3].reshape(b, l, D_MODEL) if emit_gv else None)


def kernel(x_prompt, x_sample, mem_prompt, cache_mem_k, cache_mem_v, state_mlstm_C, state_mlstm_n, state_mlstm_m, state_mlstm_conv, norm_mix_g, w_in, mlstm_i_b, mlstm_f_b, mlstm_conv_w, mlstm_conv_b, mlstm_norm_g, gmlp_norm_g, gmlp_norm_b, gmlp_w_s, gmlp_b_s, mem_norm_g, w_mem_k, w_mem_v, w_br_mlstm, w_br_gmlp, w_br_mem, w_out, norm_moe_g, w_router_group, b_router_group, w_router_expert, b_router_expert, w_exp_gate, w_exp_up, w_exp_down, norm_final_g):
    bp = x_prompt.shape[0]
    bs = x_sample.shape[0]
    W = D_MODEL
    wi = w_in[0]
    o_qk, o_v, o_o, o_i = 0, 2 * W, 3 * W, 4 * W
    o_f = o_i + HEADS
    o_u = o_f + HEADS
    o_gv, o_mq, o_gate = o_u + W, o_u + 2 * W, o_u + 3 * W
    row = lambda a: a.reshape(1, -1)
    pad_l = lambda a: jnp.pad(a, ((0, 0), (0, LANES - a.shape[1])))
    p = {
        "norm_mix_g": row(norm_mix_g[0]),
        "w_qk": wi[:, o_qk:o_v].astype(BF16), "w_v": wi[:, o_v:o_o].astype(BF16),
        "w_o": wi[:, o_o:o_i].astype(BF16),
        "w_ift": wi[:, o_i:o_u].T.astype(BF16),
        "b_if": jnp.concatenate([mlstm_i_b[0], mlstm_f_b[0]]).reshape(2 * HEADS, 1),
        "conv_w": mlstm_conv_w[0], "conv_b": row(mlstm_conv_b[0]),
        "w_u": wi[:, o_u:o_gv].astype(BF16), "w_gv": wi[:, o_gv:o_mq].astype(BF16),
        "w_mq": wi[:, o_mq:o_gate].astype(BF16), "w_gate": wi[:, o_gate:].astype(BF16),
        "gmlp_norm_g": row(gmlp_norm_g[0]), "gmlp_norm_b": row(gmlp_norm_b[0]),
        "w_s": gmlp_w_s[0], "b_st": gmlp_b_s[0].T,
        "mlstm_norm_g": row(mlstm_norm_g[0]),
        "w_br_mlstm": w_br_mlstm[0].astype(BF16), "w_br_gmlp": w_br_gmlp[0].astype(BF16),
        "w_br_mem": w_br_mem[0].astype(BF16), "w_out": w_out[0].astype(BF16),
        "norm_moe_g": row(norm_moe_g[0]),
        "w_rg": pad_l(w_router_group[0]).astype(BF16), "b_rg": pad_l(row(b_router_group[0])),
        "w_re": pad_l(w_router_expert[0]).astype(BF16), "b_re": pad_l(row(b_router_expert[0])),
        "w_eg": w_exp_gate[0], "w_eu": w_exp_up[0], "w_ed": w_exp_down[0],
        "norm_final_g": row(norm_final_g),
    }

    mk_p, mv_p, mk_pb, mv_pb = _memory_kv(mem_prompt.reshape(bp * N_MEM, W), row(mem_norm_g[0]),
                                          w_mem_k[0], w_mem_v[0])

    zeros = lambda *s: jnp.zeros(s, F32)
    hp, cp, np_, mp, cvp, _ = _trunk(
        x_prompt, mk_pb.reshape(bp, N_MEM, W), mv_pb.reshape(bp, N_MEM, W),
        None, None, None, zeros(bp, CONV_W - 1, 2 * W), p, emit_gv=False)
    hs, cs, ns, ms, cvs, gvs = _trunk(
        x_sample, cache_mem_k[0].reshape(bs, N_MEM, W).astype(BF16),
        cache_mem_v[0].reshape(bs, N_MEM, W).astype(BF16),
        state_mlstm_C[0], state_mlstm_n[0], state_mlstm_m[0], state_mlstm_conv[0], p,
        emit_gv=True)

    yp, ys = _moe(hp, hs, p)
    kv_shape = (1, bp, N_MEM, HEADS, HEAD_DIM)
    return (yp.reshape(x_prompt.shape), ys.reshape(x_sample.shape), mk_p.reshape(kv_shape), mv_p.reshape(kv_shape),
            cp[None], np_[None], mp[None], cvp[None],
            cs[None], ns[None], ms[None], cvs[None], gvs[None])
```

```python
import functools

import jax
import jax.numpy as jnp
from jax import lax
from jax.experimental import pallas as pl
from jax.experimental.pallas import tpu as pltpu

D_MODEL = 1024
MLSTM_BLOCK = 128
TOKEN_TILE = 512
MLSTM_TILE = 1024
EPS = 1e-6
HEADS = 4
HEAD_DIM = 256
CONV_W = 4
GMLP_GROUPS = 4
GMLP_GROUP_DIM = 256
GMLP_CHUNK = 128
N_MEM = 256
N_GROUPS = 4
EXPERTS_PER_GROUP = 8
N_EXPERTS = 32
D_EXPERT = 256
LANES = 128
CONV_PAD = 8

F32 = jnp.float32
BF16 = jnp.bfloat16
NEG_INF = float("-inf")

VMEM_LIMIT = 56 * 1024 * 1024


def _cparams(sem):
    return pltpu.CompilerParams(dimension_semantics=sem, vmem_limit_bytes=VMEM_LIMIT)


def _const_spec(shape):
    nd = len(shape)
    return pl.BlockSpec(shape, lambda *_: (0,) * nd, pipeline_mode=pl.Buffered(1))


def _sigmoid(x):
    return 0.5 * (jnp.tanh(0.5 * x) + 1.0)


def _log_sigmoid(x):
    return jnp.minimum(x, 0.0) - jnp.log(1.0 + jnp.exp(-jnp.abs(x)))


def _rmsnorm(x, g):
    r = lax.rsqrt(jnp.mean(x * x, axis=-1, keepdims=True) + EPS)
    return (x * r) * g


def _dot(a, b):
    return jnp.dot(a, b, preferred_element_type=F32)


def _dot_nt(a, b):
    return lax.dot_general(a, b, (((1,), (1,)), ((), ())), preferred_element_type=F32)


def _dot_tn(a, b):
    return lax.dot_general(a, b, (((0,), (0,)), ((), ())), preferred_element_type=F32)


def _memkv_kernel(mem_ref, g_ref, wk_ref, wv_ref, k_ref, v_ref, kb_ref, vb_ref):
    mn = _rmsnorm(mem_ref[...], g_ref[...]).astype(BF16)
    k = _dot(mn, wk_ref[...].astype(BF16))
    v = _dot(mn, wv_ref[...].astype(BF16))
    k_ref[...] = k
    v_ref[...] = v
    kb_ref[...] = k.astype(BF16)
    vb_ref[...] = v.astype(BF16)


def _memory_kv(mem2d, g, wk, wv):
    n = mem2d.shape[0]
    tm = 512
    row = pl.BlockSpec((tm, D_MODEL), lambda i: (i, 0))
    return pl.pallas_call(
        _memkv_kernel,
        grid=(n // tm,),
        in_specs=[row, _const_spec((1, D_MODEL)), _const_spec((D_MODEL, D_MODEL)),
                  _const_spec((D_MODEL, D_MODEL))],
        out_specs=[row, row, row, row],
        out_shape=[jax.ShapeDtypeStruct((n, D_MODEL), F32)] * 2 + [jax.ShapeDtypeStruct((n, D_MODEL), BF16)] * 2,
        compiler_params=_cparams(("parallel",)),
        name="memory_kv",
    )(mem2d, g, wk, wv)


def _inproj_a_kernel(x_ref, g_ref, wqk_ref, wv_ref, wo_ref, wif_ref, bif_ref, cw_ref, cb_ref, cs_ref,
                     q_ref, kt_ref, v_ref, o_ref, gt_ref, cn_ref, ext_ref, *, nseg, sl, lc):
    i = pl.program_id(1)
    tm = nseg * sl
    tail = CONV_PAD - (CONV_W - 1)

    @pl.when(i == 0)
    def _():
        for s in range(nseg):
            ext_ref[s, 0:tail, :] = jnp.zeros((tail, 2 * D_MODEL), F32)
            ext_ref[s, tail:CONV_PAD, :] = cs_ref[s]

    xn = _rmsnorm(x_ref[...], g_ref[...]).astype(BF16)

    zqk = _dot(xn, wqk_ref[...])
    ks = []
    row8 = lax.broadcasted_iota(jnp.int32, (CONV_PAD, 2 * D_MODEL), 0)
    for s in range(nseg):
        cur = zqk[s * sl:(s + 1) * sl, :]
        prev = ext_ref[s]
        acc = cb_ref[...] + cur * cw_ref[CONV_W - 1:CONV_W, :]
        for d in range(1, CONV_W):
            back = pltpu.roll(cur, d, axis=0)
            head = jnp.where(row8 < d, pltpu.roll(prev, d, axis=0), back[0:CONV_PAD, :])
            back = jnp.concatenate([head, back[CONV_PAD:, :]], axis=0)
            acc = acc + back * cw_ref[CONV_W - 1 - d:CONV_W - d, :]
        qk = acc * _sigmoid(acc)
        q_ref[s * sl:(s + 1) * sl, :] = (qk[:, :D_MODEL] * (HEAD_DIM ** -0.5)).astype(BF16)
        ks.append(qk[:, D_MODEL:])
        ext_ref[s] = cur[sl - CONV_PAD:sl, :]
        cn_ref[s] = ext_ref[s, tail:CONV_PAD, :]
    if tm % LANES:
        ks.append(jnp.zeros((LANES - tm % LANES, D_MODEL), F32))
    k = jnp.concatenate(ks, axis=0) if len(ks) > 1 else ks[0]
    kt = k.T.astype(BF16)
    nch = tm // lc
    per_seq = sl // lc
    for c in range(nch):
        kt_ref[c // per_seq, c % per_seq] = kt[:, c * lc:(c + 1) * lc]

    v_ref[...] = _dot(xn, wv_ref[...]).astype(BF16)
    o_ref[...] = _sigmoid(_dot(xn, wo_ref[...])).astype(BF16)

    zg = _dot(xn, wif_ref[...])
    if tm % LANES:
        zg = jnp.concatenate([zg, jnp.zeros((LANES - tm % LANES, LANES), F32)], axis=0)
    zt = zg.T[0:2 * HEADS, :]
    z = jnp.concatenate([zt[:, c * lc:(c + 1) * lc] + bif_ref[...] for c in range(nch)], axis=0)
    is_ig = (lax.broadcasted_iota(jnp.int32, z.shape, 0) % (2 * HEADS)) < HEADS
    g = jnp.where(is_ig, z, _log_sigmoid(z))
    upper = jnp.where(lax.broadcasted_iota(jnp.int32, (lc, lc), 0)
                      <= lax.broadcasted_iota(jnp.int32, (lc, lc), 1), 1.0, 0.0)
    bc = jnp.dot(g, upper, preferred_element_type=F32, precision=lax.Precision.HIGHEST)
    a = g - pltpu.roll(bc, nch * 2 * HEADS - HEADS, axis=0)
    amax = jnp.broadcast_to(jnp.max(a, axis=-1, keepdims=True), z.shape)
    gb = jnp.where(is_ig, g, bc)
    for c in range(nch):
        gt_ref[c // per_seq, c % per_seq, 0:2 * HEADS, :] = gb[c * 8:(c + 1) * 8, :]
        gt_ref[c // per_seq, c % per_seq, 2 * HEADS:4 * HEADS, :] = amax[c * 8:(c + 1) * 8, :]


def _tile_geometry(b, l, tile=TOKEN_TILE):
    sl = min(l, tile)
    nseg = max(1, min(b, TOKEN_TILE // sl))
    return nseg, sl


def _tok_spec(nseg, sl, nt, w):
    return pl.BlockSpec((nseg * sl, w), lambda bi, i: (bi * nt + i, 0))


def _inproj_a(x2d, b, l, g, wqk, wv, wo, wift, bif, cw, cb, cs):
    nseg, sl = _tile_geometry(b, l)
    nt = l // sl
    n = b * l
    CHUNK = min(MLSTM_BLOCK, l)
    per_seq = sl // CHUNK
    tok = functools.partial(_tok_spec, nseg, sl, nt)
    state = pl.BlockSpec((nseg, CONV_W - 1, 2 * D_MODEL), lambda bi, i: (bi, 0, 0))
    return pl.pallas_call(
        functools.partial(_inproj_a_kernel, nseg=nseg, sl=sl, lc=CHUNK),
        grid=(b // nseg, nt),
        in_specs=[tok(D_MODEL), _const_spec((1, D_MODEL)), _const_spec((D_MODEL, 2 * D_MODEL)),
                  _const_spec((D_MODEL, D_MODEL)), _const_spec((D_MODEL, D_MODEL)),
                  _const_spec((D_MODEL, LANES)), _const_spec((2 * HEADS, 1)),
                  _const_spec((CONV_W, 2 * D_MODEL)), _const_spec((1, 2 * D_MODEL)), state],
        out_specs=[tok(D_MODEL),
                   pl.BlockSpec((nseg, per_seq, D_MODEL, CHUNK), lambda bi, i: (bi, i, 0, 0)),
                   tok(D_MODEL), tok(D_MODEL),
                   pl.BlockSpec((nseg, per_seq, 4 * HEADS, CHUNK), lambda bi, i: (bi, i, 0, 0)),
                   state],
        out_shape=[jax.ShapeDtypeStruct((n, D_MODEL), BF16),
                   jax.ShapeDtypeStruct((b, l // CHUNK, D_MODEL, CHUNK), BF16),
                   jax.ShapeDtypeStruct((n, D_MODEL), BF16), jax.ShapeDtypeStruct((n, D_MODEL), BF16),
                   jax.ShapeDtypeStruct((b, l // CHUNK, 4 * HEADS, CHUNK), F32),
                   jax.ShapeDtypeStruct((b, CONV_W - 1, 2 * D_MODEL), F32)],
        scratch_shapes=[pltpu.VMEM((nseg, CONV_PAD, 2 * D_MODEL), F32)],
        compiler_params=_cparams(("parallel", "arbitrary")),
        name="inproj_a",
    )(x2d, g, wqk, wv, wo, wift, bif, cw, cb, cs)


def _inproj_b_kernel(x_ref, g_ref, wu_ref, wgv_ref, wmq_ref, wgate_ref, lng_ref, lnb_ref, ws_ref, bst_ref,
                     mk_ref, mv_ref, ug_ref, att_ref, gates_ref, *rest, nseg, sl, cl, emit_gv):
    tm = nseg * sl
    xn = _rmsnorm(x_ref[...], g_ref[...]).astype(BF16)

    gates_ref[...] = _sigmoid(_dot(xn, wgate_ref[...])).astype(BF16)

    gvr = jax.nn.gelu(_dot(xn, wgv_ref[...]))
    mu = jnp.mean(gvr, axis=-1, keepdims=True)
    xc = gvr - mu
    r = lax.rsqrt(jnp.mean(xc * xc, axis=-1, keepdims=True) + EPS)
    gv = (xc * r) * lng_ref[...] + lnb_ref[...]
    if emit_gv:
        rest[0][...] = gv
    gvb = gv.astype(BF16)
    u = jax.nn.gelu(_dot(xn, wu_ref[...]))
    tri = (lax.broadcasted_iota(jnp.int32, (cl, cl), 0) >= lax.broadcasted_iota(jnp.int32, (cl, cl), 1))
    for gi in range(GMLP_GROUPS):
        wsg = jnp.where(tri, ws_ref[gi], 0.0).astype(BF16)
        lo, hi = gi * GMLP_GROUP_DIM, (gi + 1) * GMLP_GROUP_DIM
        for c in range(tm // cl):
            sp = _dot(wsg, gvb[c * cl:(c + 1) * cl, lo:hi]) + bst_ref[:, gi:gi + 1]
            ug_ref[c * cl:(c + 1) * cl, lo:hi] = (u[c * cl:(c + 1) * cl, lo:hi] * sp).astype(BF16)

    mq = _dot(xn, wmq_ref[...]).astype(BF16)
    for s in range(nseg):
        r0, r1 = s * sl, (s + 1) * sl
        for h in range(HEADS):
            lo, hi = h * HEAD_DIM, (h + 1) * HEAD_DIM
            sc = _dot_nt(mq[r0:r1, lo:hi], mk_ref[s, :, lo:hi]) * (HEAD_DIM ** -0.5)
            e = jnp.exp(sc - jnp.max(sc, axis=-1, keepdims=True))
            a = (e / jnp.sum(e, axis=-1, keepdims=True)).astype(BF16)
            att_ref[r0:r1, lo:hi] = _dot(a, mv_ref[s, :, lo:hi]).astype(BF16)


def _inproj_b(x2d, b, l, g, wu, wgv, wmq, wgate, lng, lnb, ws, bst, mk, mv, cl, emit_gv):
    nseg, sl = _tile_geometry(b, l)
    nt = l // sl
    n = b * l
    tok = functools.partial(_tok_spec, nseg, sl, nt)
    mem = pl.BlockSpec((nseg, N_MEM, D_MODEL), lambda bi, i: (bi, 0, 0))
    out_specs = [tok(D_MODEL), tok(D_MODEL), tok(3 * D_MODEL)]
    out_shape = [jax.ShapeDtypeStruct((n, D_MODEL), BF16), jax.ShapeDtypeStruct((n, D_MODEL), BF16),
                 jax.ShapeDtypeStruct((n, 3 * D_MODEL), BF16)]
    if emit_gv:
        out_specs.append(tok(D_MODEL))
        out_shape.append(jax.ShapeDtypeStruct((n, D_MODEL), F32))
    return pl.pallas_call(
        functools.partial(_inproj_b_kernel, nseg=nseg, sl=sl, cl=cl, emit_gv=emit_gv),
        grid=(b // nseg, nt),
        in_specs=[tok(D_MODEL), _const_spec((1, D_MODEL)), _const_spec((D_MODEL, D_MODEL)),
                  _const_spec((D_MODEL, D_MODEL)), _const_spec((D_MODEL, D_MODEL)),
                  _const_spec((D_MODEL, 3 * D_MODEL)), _const_spec((1, D_MODEL)), _const_spec((1, D_MODEL)),
                  _const_spec((GMLP_GROUPS, cl, cl)), _const_spec((cl, GMLP_GROUPS)), mem, mem],
        out_specs=out_specs,
        out_shape=out_shape,
        compiler_params=_cparams(("parallel", "parallel")),
        name="inproj_b",
    )(x2d, g, wu, wgv, wmq, wgate, lng, lnb, ws, bst, mk, mv)


def _mlstm_kernel(q_ref, kt_ref, v_ref, gt_ref, *rest, nseg, cb, zero_state):
    i = pl.program_id(1)

    if zero_state:
        hm_ref, c_ref, n_ref, m_ref, st_ref = rest
    else:
        c0_ref, n0_ref, m0_ref, hm_ref, c_ref, n_ref, m_ref, st_ref = rest

    @pl.when(i == 0)
    def _():
        if zero_state:
            st_ref[...] = jnp.zeros(st_ref.shape, F32)
            m_ref[...] = jnp.zeros(m_ref.shape, F32)
        else:
            st_ref[:, :, :, :HEAD_DIM] = c0_ref[...]
            st_ref[:, :, :, HEAD_DIM:] = n0_ref[...]
            m_ref[...] = m0_ref[...]

    for s in range(nseg):
        _mlstm_sequence(q_ref, kt_ref.at[s], v_ref, gt_ref.at[s], st_ref.at[s], m_ref.at[s], hm_ref, s * cb, cb)

    @pl.when(i == pl.num_programs(1) - 1)
    def _():
        c_ref[...] = st_ref[:, :, :, :HEAD_DIM]
        n_ref[...] = st_ref[:, :, :, HEAD_DIM:]


def _mlstm_sequence(q_ref, kt_ref, v_ref, gt_ref, c_ref, m_ref, hm_ref, row0, cb):
    L = kt_ref.shape[-1]
    nch = cb // L
    ti = lax.broadcasted_iota(jnp.int32, (L, L), 0)
    si = lax.broadcasted_iota(jnp.int32, (L, L), 1)
    tri = ti >= si
    eye = ti == si

    rows = 4 * HEADS
    g_all = gt_ref[...].reshape(nch * rows, L)

    m_in = [m_ref[:, 0:1]]
    for c in range(nch):
        b_last4 = g_all[c * rows + HEADS:c * rows + 2 * HEADS, L - 1:L]
        amax4 = g_all[c * rows + 2 * HEADS:c * rows + 3 * HEADS, 0:1]
        m_in.append(jnp.maximum(b_last4 + m_in[-1], b_last4 + amax4))

    ones = jnp.ones((L, LANES), BF16)
    st = [c_ref[h] for h in range(HEADS)]
    for c in range(nch):
        r0, r1 = row0 + c * L, row0 + (c + 1) * L
        for h in range(HEADS):
            lo, hi = h * HEAD_DIM, (h + 1) * HEAD_DIM
            ig_r = g_all[c * rows + h:c * rows + h + 1, :]
            bc_r = g_all[c * rows + HEADS + h:c * rows + HEADS + h + 1, :]
            a_r = ig_r - bc_r
            bc_c = jnp.sum(jnp.where(eye, bc_r, 0.0), axis=-1, keepdims=True)
            m0 = m_in[c][h:h + 1, :]
            m_last = m_in[c + 1][h:h + 1, :]
            dmat = jnp.where(tri, bc_c + a_r, NEG_INF)
            inter = bc_c + m0
            m = jnp.maximum(inter, jnp.max(dmat, axis=-1, keepdims=True))
            w_intra = jnp.exp(dmat - m)
            w_inter = jnp.exp(inter - m)
            q = q_ref[r0:r1, lo:hi]
            kt = kt_ref[c, lo:hi, :]
            v = v_ref[r0:r1, lo:hi]
            s = _dot(q, kt) * w_intra
            qs = _dot(q, st[h].astype(BF16))
            num = w_inter * qs[:, :HEAD_DIM] + _dot(s.astype(BF16), v)
            den = w_inter * qs[:, HEAD_DIM:HEAD_DIM + 1] + jnp.sum(s, axis=-1, keepdims=True)
            hh = num / jnp.maximum(jnp.abs(den), jnp.exp(-m))
            bc_last = bc_r[:, L - 1:L]
            w_last = jnp.exp(bc_last + a_r - m_last)
            decay = jnp.exp(bc_last + m0 - m_last)
            ktw = (kt.astype(F32) * w_last).astype(BF16)
            st[h] = decay * st[h] + _dot(ktw, jnp.concatenate([v, ones], axis=1))
            hm_ref[r0:r1, lo:hi] = hh.astype(BF16)

    for h in range(HEADS):
        c_ref[h] = st[h]
    m_ref[...] = jnp.broadcast_to(m_in[nch], (HEADS, LANES))


def _mlstm(q, kt, v, gt, state, b, l):
    nseg, cb = _tile_geometry(b, l, MLSTM_TILE)
    nt = l // cb
    CHUNK = kt.shape[-1]
    tok = _tok_spec(nseg, cb, nt, D_MODEL)
    cs = pl.BlockSpec((nseg, HEADS, HEAD_DIM, HEAD_DIM), lambda bi, i: (bi, 0, 0, 0))
    ns = pl.BlockSpec((nseg, HEADS, HEAD_DIM, LANES), lambda bi, i: (bi, 0, 0, 0))
    ms = pl.BlockSpec((nseg, HEADS, LANES), lambda bi, i: (bi, 0, 0))
    state_specs = [] if state is None else [cs, ns, ms]
    return pl.pallas_call(
        functools.partial(_mlstm_kernel, nseg=nseg, cb=cb, zero_state=state is None),
        grid=(b // nseg, nt),
        in_specs=[tok, pl.BlockSpec((nseg, cb // CHUNK, D_MODEL, CHUNK), lambda bi, i: (bi, i, 0, 0)), tok,
                  pl.BlockSpec((nseg, cb // CHUNK, 4 * HEADS, CHUNK), lambda bi, i: (bi, i, 0, 0))] + state_specs,
        out_specs=[tok, cs, ns, ms],
        out_shape=[jax.ShapeDtypeStruct((b * l, D_MODEL), BF16),
                   jax.ShapeDtypeStruct((b, HEADS, HEAD_DIM, HEAD_DIM), F32),
                   jax.ShapeDtypeStruct((b, HEADS, HEAD_DIM, LANES), F32),
                   jax.ShapeDtypeStruct((b, HEADS, LANES), F32)],
        scratch_shapes=[pltpu.VMEM((nseg, HEADS, HEAD_DIM, HEAD_DIM + LANES), F32)],
        compiler_params=_cparams(("parallel", "arbitrary")),
        name="mlstm",
    )(q, kt, v, gt, *(state or ()))


def _merge_kernel(x_ref, hh_ref, o_ref, ng_ref, ug_ref, att_ref, gates_ref, wa_ref, wb_ref, wc_ref, wo_ref, h_ref):
    parts = []
    for h in range(HEADS):
        lo, hi = h * HEAD_DIM, (h + 1) * HEAD_DIM
        hh = hh_ref[:, lo:hi].astype(F32)
        hn = hh * lax.rsqrt(jnp.mean(hh * hh, axis=-1, keepdims=True) + EPS)
        parts.append(((hn * ng_ref[:, lo:hi]) * o_ref[:, lo:hi].astype(F32)).astype(BF16))
    hm = jnp.concatenate(parts, axis=1)
    g = gates_ref[...].astype(F32)
    mixed = g[:, :D_MODEL] * _dot(hm, wa_ref[...])
    mixed = mixed + g[:, D_MODEL:2 * D_MODEL] * _dot(ug_ref[...], wb_ref[...])
    mixed = mixed + g[:, 2 * D_MODEL:] * _dot(att_ref[...], wc_ref[...])
    h_ref[...] = x_ref[...] + _dot(mixed.astype(BF16), wo_ref[...])


def _merge(x2d, hh, osig, ng, ug, att, gates, wa, wb, wc, wo, tm):
    n = x2d.shape[0]
    row = lambda w: pl.BlockSpec((tm, w), lambda i: (i, 0))
    wspec = _const_spec((D_MODEL, D_MODEL))
    return pl.pallas_call(
        _merge_kernel,
        grid=(n // tm,),
        in_specs=[row(D_MODEL), row(D_MODEL), row(D_MODEL), _const_spec((1, D_MODEL)), row(D_MODEL), row(D_MODEL),
                  row(3 * D_MODEL), wspec, wspec, wspec, wspec],
        out_specs=row(D_MODEL),
        out_shape=jax.ShapeDtypeStruct((n, D_MODEL), F32),
        compiler_params=_cparams(("parallel",)),
        name="merge",
    )(x2d, hh, osig, ng, ug, att, gates, wa, wb, wc, wo)


MOE_T = 256
MOE_STEP_SUBS = 2
MOE_CAP = 32
MOE_S = N_EXPERTS * MOE_CAP
MOE_OVF = 512
SEG_ALIGN = 16
EXPERT_BLOCK = 128


def _moe_route_kernel(hp_ref, hs_ref, g_ref, wrg_ref, brg_ref, wre_ref, bre_ref,
                      main_ref, ovf_ref, info_ref, meta_ref, *, nsp):
    n = MOE_STEP_SUBS * MOE_T
    h = jnp.where(pl.program_id(0) < nsp // MOE_STEP_SUBS, hp_ref[...], hs_ref[...])
    xm = _rmsnorm(h, g_ref[...]).astype(BF16)
    lane = lax.broadcasted_iota(jnp.int32, (n, LANES), 1).astype(F32)
    lg = jnp.where(lane < N_GROUPS, _dot(xm, wrg_ref[...]) + brg_ref[...], NEG_INF)
    gmax = jnp.max(lg, axis=-1, keepdims=True)
    p_top = 1.0 / jnp.sum(jnp.exp(lg - gmax), axis=-1, keepdims=True)
    grp = jnp.min(jnp.where(lg == gmax, lane, float(LANES)), axis=-1, keepdims=True)
    el = _dot(xm, wre_ref[...]) + bre_ref[...]
    in_grp = (lane >= grp * EXPERTS_PER_GROUP) & (lane < (grp + 1.0) * EXPERTS_PER_GROUP)
    vals = jnp.where(in_grp, el, NEG_INF)
    v1 = jnp.max(vals, axis=-1, keepdims=True)
    i1 = jnp.min(jnp.where(vals == v1, lane, float(LANES)), axis=-1, keepdims=True)
    vals2 = jnp.where(lane == i1, NEG_INF, vals)
    v2 = jnp.max(vals2, axis=-1, keepdims=True)
    i2 = jnp.min(jnp.where(vals2 == v2, lane, float(LANES)), axis=-1, keepdims=True)
    r = jnp.exp(v2 - v1)
    p1 = p_top / (1.0 + r)
    p2 = p_top * r / (1.0 + r)
    sel1 = lane == i1
    sel2 = lane == i2
    onehot = jnp.where(sel1 | sel2, 1.0, 0.0)

    deferred = []
    for s in range(MOE_STEP_SUBS):
        rows = slice(s * MOE_T, (s + 1) * MOE_T)
        deferred.append(_route_sub_tile(
            xm[rows, :], onehot[rows, :], i1[rows, :], i2[rows, :],
            p1[rows, :], p2[rows, :], main_ref.at[s], ovf_ref.at[s], info_ref.at[rows, :], meta_ref.at[s]))
    for write_overflow in deferred:
        write_overflow()


def _route_sub_tile(xm, onehot, i1, i2, p1, p2, main_ref, ovf_ref, info_ref, meta_ref):
    t = MOE_T
    lane = lax.broadcasted_iota(jnp.int32, (t, LANES), 1)
    sel1 = lane.astype(F32) == i1
    sel2 = lane.astype(F32) == i2
    cnt = jnp.sum(onehot, axis=0, keepdims=True).astype(jnp.int32)
    pn = jnp.bitwise_and(jnp.maximum(cnt - MOE_CAP, 0) + (SEG_ALIGN - 1), -SEG_ALIGN)
    pn8 = jnp.broadcast_to(pn, (8, LANES))
    earlier = jnp.where(lax.broadcasted_iota(jnp.int32, (LANES, LANES), 0)
                        < lax.broadcasted_iota(jnp.int32, (LANES, LANES), 1), 1.0, 0.0).astype(BF16)
    off_f8 = _dot(pn8.astype(F32).astype(BF16), earlier)
    row8 = lax.broadcasted_iota(jnp.int32, (8, LANES), 0)
    meta_ref[...] = jnp.where(row8 == 0, off_f8.astype(jnp.int32), jnp.where(row8 == 1, pn8, 0))

    ti = lax.broadcasted_iota(jnp.int32, (t, t), 0)
    si = lax.broadcasted_iota(jnp.int32, (t, t), 1)
    before = jnp.where(ti > si, 1.0, 0.0).astype(BF16)
    rank = _dot(before, onehot.astype(BF16))
    off_f = off_f8[0:1, :]

    def slot_row(sel, idx):
        rk = jnp.sum(jnp.where(sel, rank, 0.0), axis=-1, keepdims=True)
        of = jnp.sum(jnp.where(sel, off_f, 0.0), axis=-1, keepdims=True)
        return jnp.where(rk < MOE_CAP, idx * MOE_CAP + rk, MOE_S - MOE_CAP + of + rk)

    pos1 = slot_row(sel1, i1)
    pos2 = slot_row(sel2, i2)
    info_ref[...] = (jnp.where(lane == 0, pos1, 0.0) + jnp.where(lane == 1, pos2, 0.0)
                     + jnp.where(lane == 2, p1, 0.0) + jnp.where(lane == 3, p2, 0.0))

    eye = ti == si
    pos1_r = jnp.sum(jnp.where(eye, pos1, 0.0), axis=0, keepdims=True)
    pos2_r = jnp.sum(jnp.where(eye, pos2, 0.0), axis=0, keepdims=True)

    def gather(rows, base):
        srow = (lax.broadcasted_iota(jnp.int32, (rows, t), 0) + base).astype(F32)
        pick = jnp.where((srow == pos1_r) | (srow == pos2_r), 1.0, 0.0).astype(BF16)
        return _dot(pick, xm).astype(BF16)

    main_ref[...] = gather(MOE_S, 0)
    has_ovf = jnp.sum(pn) > 0

    def write_overflow():
        @pl.when(has_ovf)
        def _():
            ovf_ref[...] = gather(MOE_OVF, MOE_S)

        @pl.when(jnp.logical_not(has_ovf))
        def _():
            ovf_ref[...] = jnp.zeros(ovf_ref.shape, ovf_ref.dtype)
    return write_overflow


def _two_group_specs(nsp, cols):
    rows = MOE_STEP_SUBS * MOE_T
    npb = nsp // MOE_STEP_SUBS
    return (pl.BlockSpec((rows, cols), lambda j, *_: (jnp.minimum(j, npb - 1), 0)),
            pl.BlockSpec((rows, cols), lambda j, *_: (jnp.maximum(j - npb, 0), 0)))


def _moe_route(hp, hs, g, wrg, brg, wre, bre):
    nsp = hp.shape[0] // MOE_T
    ns = nsp + hs.shape[0] // MOE_T
    sub = MOE_STEP_SUBS
    return pl.pallas_call(
        functools.partial(_moe_route_kernel, nsp=nsp),
        grid=(ns // sub,),
        in_specs=[*_two_group_specs(nsp, D_MODEL), _const_spec((1, D_MODEL)),
                  _const_spec((D_MODEL, LANES)), _const_spec((1, LANES)),
                  _const_spec((D_MODEL, LANES)), _const_spec((1, LANES))],
        out_specs=[pl.BlockSpec((sub, MOE_S, D_MODEL), lambda j: (j, 0, 0)),
                   pl.BlockSpec((sub, MOE_OVF, D_MODEL), lambda j: (j, 0, 0)),
                   pl.BlockSpec((sub * MOE_T, LANES), lambda j: (j, 0)),
                   pl.BlockSpec((sub, 8, LANES), lambda j: (j, 0, 0))],
        out_shape=[jax.ShapeDtypeStruct((ns, MOE_S, D_MODEL), BF16),
                   jax.ShapeDtypeStruct((ns, MOE_OVF, D_MODEL), BF16),
                   jax.ShapeDtypeStruct((ns * MOE_T, LANES), F32),
                   jax.ShapeDtypeStruct((ns, 8, LANES), jnp.int32)],
        compiler_params=_cparams(("arbitrary",)),
        name="moe_route",
    )(hp, hs, g, wrg, brg, wre, bre)


def _swiglu(x, wgu, wdb):
    gu = _dot(x, wgu[...])
    gate = gu[:, :D_EXPERT]
    hid = (gate * _sigmoid(gate)) * gu[:, D_EXPERT:]
    return _dot(hid.astype(BF16), wdb[...]).astype(BF16)


def _moe_expert_kernel(offs_ref, pns_ref, eflag_ref, main_ref, slots_hbm, wg_ref, wu_ref, wd_ref,
                       mout_ref, out_hbm, xbuf, obuf, wgu, wdb, sem_in, sem_out, *, n_sub, jb, n_rb):
    e = pl.program_id(0)
    rb = pl.program_id(1)

    @pl.when((e == 0) & (rb == 0))
    def _():
        xbuf[...] = jnp.zeros_like(xbuf)

    @pl.when(rb == 0)
    def _():
        wgu[:, :D_EXPERT] = wg_ref[...].astype(BF16)
        wgu[:, D_EXPERT:] = wu_ref[...].astype(BF16)
        wdb[...] = wd_ref[...].astype(BF16)

    sb = main_ref.shape[0]
    part = sb // 2 if sb % 2 == 0 else sb
    for s0 in range(0, sb, part):
        y = _swiglu(main_ref[s0:s0 + part].reshape(part * MOE_CAP, D_MODEL), wgu, wdb)
        mout_ref[s0:s0 + part] = y.reshape(part, MOE_CAP, D_MODEL)

    def copy_in(j, src, dst):
        return pltpu.make_async_copy(slots_hbm.at[j, pl.ds(src, SEG_ALIGN), :],
                                     xbuf.at[pl.ds(dst, SEG_ALIGN), :], sem_in)

    def copy_out(j, src, dst):
        return pltpu.make_async_copy(obuf.at[pl.ds(src, SEG_ALIGN), :],
                                     out_hbm.at[j, pl.ds(dst, SEG_ALIGN), :], sem_out)

    def for_each_chunk(g, fn):
        def seg(jj, cur):
            j = g * jb + jj
            n = pns_ref[j * N_EXPERTS + e]
            off = offs_ref[j * N_EXPERTS + e]

            def chunk(k, c):
                fn(j, pl.multiple_of(off + k * SEG_ALIGN, SEG_ALIGN), pl.multiple_of(cur + k * SEG_ALIGN, SEG_ALIGN))
                return c
            lax.fori_loop(0, n // SEG_ALIGN, chunk, 0)
            return cur + n
        return lax.fori_loop(0, jb, seg, 0)

    def group(g, carry):
        total = for_each_chunk(g, lambda j, r, b: copy_in(j, r, b).start())
        nchunk = total // SEG_ALIGN

        def wait_in(k, c):
            copy_in(0, 0, 0).wait()
            return c
        lax.fori_loop(0, nchunk, wait_in, 0)

        def block(bi, c):
            r0 = pl.multiple_of(bi * EXPERT_BLOCK, EXPERT_BLOCK)
            obuf[pl.ds(r0, EXPERT_BLOCK), :] = _swiglu(xbuf[pl.ds(r0, EXPERT_BLOCK), :], wgu, wdb)
            return c
        lax.fori_loop(0, (total + EXPERT_BLOCK - 1) // EXPERT_BLOCK, block, 0)

        for_each_chunk(g, lambda j, r, b: copy_out(j, b, r).start())

        def wait_out(k, c):
            copy_out(0, 0, 0).wait()
            return c
        lax.fori_loop(0, nchunk, wait_out, 0)
        return carry

    @pl.when((rb == n_rb - 1) & (eflag_ref[e] > 0))
    def _():
        lax.fori_loop(0, n_sub // jb, group, 0)


def _moe_expert(main, ovf, offs, pns, eflag, wg, wu, wd):
    ns = main.shape[0]
    n_rb = 1
    sb = ns // n_rb
    jb = max(d for d in range(1, 12) if ns % d == 0)
    rows = jb * MOE_T + EXPERT_BLOCK
    main4 = main.reshape(ns, N_EXPERTS, MOE_CAP, D_MODEL)
    mspec = pl.BlockSpec((sb, None, MOE_CAP, D_MODEL), lambda e, rb, *_: (rb, e, 0, 0))
    wspec = lambda a, b: pl.BlockSpec((None, a, b), lambda e, rb, *_: (e, 0, 0))
    grid_spec = pltpu.PrefetchScalarGridSpec(
        num_scalar_prefetch=3,
        grid=(N_EXPERTS, n_rb),
        in_specs=[mspec, pl.BlockSpec(memory_space=pl.ANY),
                  wspec(D_MODEL, D_EXPERT), wspec(D_MODEL, D_EXPERT), wspec(D_EXPERT, D_MODEL)],
        out_specs=[mspec, pl.BlockSpec(memory_space=pl.ANY)],
        scratch_shapes=[pltpu.VMEM((rows, D_MODEL), BF16), pltpu.VMEM((rows, D_MODEL), BF16),
                        pltpu.VMEM((D_MODEL, 2 * D_EXPERT), BF16), pltpu.VMEM((D_EXPERT, D_MODEL), BF16),
                        pltpu.SemaphoreType.DMA(()), pltpu.SemaphoreType.DMA(())],
    )
    mout, oout = pl.pallas_call(
        functools.partial(_moe_expert_kernel, n_sub=ns, jb=jb, n_rb=n_rb),
        grid_spec=grid_spec,
        out_shape=[jax.ShapeDtypeStruct(main4.shape, main4.dtype), jax.ShapeDtypeStruct(ovf.shape, ovf.dtype)],
        input_output_aliases={4: 1},
        compiler_params=_cparams(("arbitrary", "arbitrary")),
        name="moe_expert",
    )(offs, pns, eflag, main4, ovf, wg, wu, wd)
    return mout.reshape(main.shape), oout


def _moe_combine_kernel(jflag_ref, sflag_ref, hp_ref, hs_ref, main_ref, ovf_ref, info_ref, gf_ref,
                        yp_ref, ys_ref, acc_ref, *, nsp):
    j = pl.program_id(0)
    npb = nsp // MOE_STEP_SUBS

    def weights(s, cols, base):
        info = info_ref[s * MOE_T:(s + 1) * MOE_T, :]
        scol = (lax.broadcasted_iota(jnp.int32, (MOE_T, cols), 1) + base).astype(F32)
        return (jnp.where(scol == info[:, 0:1], info[:, 2:3], 0.0)
                + jnp.where(scol == info[:, 1:2], info[:, 3:4], 0.0)).astype(BF16)

    for s in range(MOE_STEP_SUBS):
        rows = slice(s * MOE_T, (s + 1) * MOE_T)
        h = jnp.where(j < npb, hp_ref[rows, :], hs_ref[rows, :])
        acc_ref[rows, :] = h + _dot(weights(s, MOE_S, 0), main_ref[s])

    for s in range(MOE_STEP_SUBS):
        @pl.when(jflag_ref[j * MOE_STEP_SUBS + s] > 0)
        def _(s=s):
            rows = slice(s * MOE_T, (s + 1) * MOE_T)
            acc_ref[rows, :] += _dot(weights(s, MOE_OVF, MOE_S), ovf_ref[s])

    y = _rmsnorm(acc_ref[...], gf_ref[...])

    @pl.when(j < npb)
    def _():
        yp_ref[...] = y

    @pl.when(j >= npb)
    def _():
        ys_ref[...] = y


def _moe_combine(hp, hs, main, ovf, info, jflag, gf):
    nsp = hp.shape[0] // MOE_T
    ns = main.shape[0]
    sub = MOE_STEP_SUBS
    sflag = jnp.max(jflag.reshape(ns // sub, sub), axis=1)
    grid_spec = pltpu.PrefetchScalarGridSpec(
        num_scalar_prefetch=2,
        grid=(ns // sub,),
        in_specs=[*_two_group_specs(nsp, D_MODEL),
                  pl.BlockSpec((sub, MOE_S, D_MODEL), lambda j, jf, sf: (j, 0, 0)),
                  pl.BlockSpec((sub, MOE_OVF, D_MODEL), lambda j, jf, sf: (jnp.where(sf[j] > 0, j, 0), 0, 0)),
                  pl.BlockSpec((sub * MOE_T, LANES), lambda j, jf, sf: (j, 0)),
                  pl.BlockSpec((1, D_MODEL), lambda j, jf, sf: (0, 0))],
        out_specs=list(_two_group_specs(nsp, D_MODEL)),
        scratch_shapes=[pltpu.VMEM((sub * MOE_T, D_MODEL), F32)],
    )
    return pl.pallas_call(
        functools.partial(_moe_combine_kernel, nsp=nsp),
        grid_spec=grid_spec,
        out_shape=[jax.ShapeDtypeStruct(hp.shape, F32), jax.ShapeDtypeStruct(hs.shape, F32)],
        compiler_params=_cparams(("arbitrary",)),
        name="moe_combine",
    )(jflag, sflag, hp, hs, main, ovf, info, gf)


def _moe(hp, hs, p):
    main, ovf, info, meta = _moe_route(hp, hs, p["norm_moe_g"], p["w_rg"], p["b_rg"], p["w_re"], p["b_re"])
    pn = meta[:, 1, :N_EXPERTS]
    offs = meta[:, 0, :N_EXPERTS].reshape(-1)
    eflag = (jnp.sum(pn, axis=0) > 0).astype(jnp.int32)
    jflag = (jnp.sum(pn, axis=1) > 0).astype(jnp.int32)
    main, ovf = _moe_expert(main, ovf, offs, pn.reshape(-1), eflag, p["w_eg"], p["w_eu"], p["w_ed"])
    return _moe_combine(hp, hs, main, ovf, info, jflag, p["norm_final_g"])


def _trunk(x, mem_k, mem_v, c0, n0, m0, conv0, p, *, emit_gv):
    b, l, _ = x.shape
    n = b * l
    cl = min(l, GMLP_CHUNK)
    x2d = x.reshape(n, D_MODEL)
    q, kt, v, osig, gt, conv_new = _inproj_a(
        x2d, b, l, p["norm_mix_g"], p["w_qk"], p["w_v"], p["w_o"], p["w_if"], p["b_if"],
        p["conv_w"], p["conv_b"], conv0)
    outs = _inproj_b(x2d, b, l, p["norm_mix_g"], p["w_u"], p["w_gv"], p["w_mq"], p["w_gate"],
                     p["gmlp_norm_g"], p["gmlp_norm_b"], p["w_s"][:, :cl, :cl], p["b_st"][:cl],
                     mem_k, mem_v, cl, emit_gv)
    ug, att, gates = outs[:3]
    on_lanes = lambda a: jnp.broadcast_to(a[..., None], a.shape + (LANES,))
    state = None if c0 is None else (c0, on_lanes(n0), on_lanes(m0))
    hh, c1, n1, m1 = _mlstm(q, kt, v, gt, state, b, l)
    n1 = n1[..., 0]
    h = _merge(x2d, hh, osig, p["mlstm_norm_g"], ug, att, gates,
               p["w_br_mlstm"], p["w_br_gmlp"], p["w_br_mem"], p["w_out"], TOKEN_TILE)
    return h, c1, n1, m1[..., 0], conv_new, (outs[3].reshape(b, l, D_MODEL) if emit_gv else None)


def kernel(x_prompt, x_sample, mem_prompt, cache_mem_k, cache_mem_v, state_mlstm_C, state_mlstm_n, state_mlstm_m, state_mlstm_conv, norm_mix_g, w_in, mlstm_i_b, mlstm_f_b, mlstm_conv_w, mlstm_conv_b, mlstm_norm_g, gmlp_norm_g, gmlp_norm_b, gmlp_w_s, gmlp_b_s, mem_norm_g, w_mem_k, w_mem_v, w_br_mlstm, w_br_gmlp, w_br_mem, w_out, norm_moe_g, w_router_group, b_router_group, w_router_expert, b_router_expert, w_exp_gate, w_exp_up, w_exp_down, norm_final_g):
    bp = x_prompt.shape[0]
    bs = x_sample.shape[0]
    W = D_MODEL
    wi = w_in[0]
    o_qk, o_v, o_o, o_i = 0, 2 * W, 3 * W, 4 * W
    o_f = o_i + HEADS
    o_u = o_f + HEADS
    o_gv, o_mq, o_gate = o_u + W, o_u + 2 * W, o_u + 3 * W
    row = lambda a: a.reshape(1, -1)
    pad_l = lambda a: jnp.pad(a, ((0, 0), (0, LANES - a.shape[1])))
    p = {
        "norm_mix_g": row(norm_mix_g[0]),
        "w_qk": wi[:, o_qk:o_v].astype(BF16), "w_v": wi[:, o_v:o_o].astype(BF16),
        "w_o": wi[:, o_o:o_i].astype(BF16),
        "w_if": pad_l(wi[:, o_i:o_u]).astype(BF16),
        "b_if": jnp.concatenate([mlstm_i_b[0], mlstm_f_b[0]]).reshape(2 * HEADS, 1),
        "conv_w": mlstm_conv_w[0], "conv_b": row(mlstm_conv_b[0]),
        "w_u": wi[:, o_u:o_gv].astype(BF16), "w_gv": wi[:, o_gv:o_mq].astype(BF16),
        "w_mq": wi[:, o_mq:o_gate].astype(BF16), "w_gate": wi[:, o_gate:].astype(BF16),
        "gmlp_norm_g": row(gmlp_norm_g[0]), "gmlp_norm_b": row(gmlp_norm_b[0]),
        "w_s": gmlp_w_s[0], "b_st": gmlp_b_s[0].T,
        "mlstm_norm_g": row(mlstm_norm_g[0]),
        "w_br_mlstm": w_br_mlstm[0].astype(BF16), "w_br_gmlp": w_br_gmlp[0].astype(BF16),
        "w_br_mem": w_br_mem[0].astype(BF16), "w_out": w_out[0].astype(BF16),
        "norm_moe_g": row(norm_moe_g[0]),
        "w_rg": pad_l(w_router_group[0]).astype(BF16), "b_rg": pad_l(row(b_router_group[0])),
        "w_re": pad_l(w_router_expert[0]).astype(BF16), "b_re": pad_l(row(b_router_expert[0])),
        "w_eg": w_exp_gate[0], "w_eu": w_exp_up[0], "w_ed": w_exp_down[0],
        "norm_final_g": row(norm_final_g),
    }

    mk_p, mv_p, mk_pb, mv_pb = _memory_kv(mem_prompt.reshape(bp * N_MEM, W), row(mem_norm_g[0]),
                                          w_mem_k[0], w_mem_v[0])

    zeros = lambda *s: jnp.zeros(s, F32)
    hp, cp, np_, mp, cvp, _ = _trunk(
        x_prompt, mk_pb.reshape(bp, N_MEM, W), mv_pb.reshape(bp, N_MEM, W),
        None, None, None, zeros(bp, CONV_W - 1, 2 * W), p, emit_gv=False)
    hs, cs, ns, ms, cvs, gvs = _trunk(
        x_sample, cache_mem_k[0].reshape(bs, N_MEM, W).astype(BF16),
        cache_mem_v[0].reshape(bs, N_MEM, W).astype(BF16),
        state_mlstm_C[0], state_mlstm_n[0], state_mlstm_m[0], state_mlstm_conv[0], p,
        emit_gv=True)

    yp, ys = _moe(hp, hs, p)
    kv_shape = (1, bp, N_MEM, HEADS, HEAD_DIM)
    return (yp.reshape(x_prompt.shape), ys.reshape(x_sample.shape), mk_p.reshape(kv_shape), mv_p.reshape(kv_shape),
            cp[None], np_[None], mp[None], cvp[None],
            cs[None], ns[None], ms[None], cvs[None], gvs[None])
```

```python
import functools

import jax
import jax.numpy as jnp
from jax import lax
from jax.experimental import pallas as pl
from jax.experimental.pallas import tpu as pltpu

D_MODEL = 1024
MLSTM_BLOCK = 128
TOKEN_TILE = 512
MLSTM_TILE = 1024
EPS = 1e-6
HEADS = 4
HEAD_DIM = 256
CONV_W = 4
GMLP_GROUPS = 4
GMLP_GROUP_DIM = 256
GMLP_CHUNK = 128
N_MEM = 256
N_GROUPS = 4
EXPERTS_PER_GROUP = 8
N_EXPERTS = 32
D_EXPERT = 256
LANES = 128
CONV_PAD = 8
NORM_ROWS = 128

F32 = jnp.float32
BF16 = jnp.bfloat16
NEG_INF = float("-inf")

VMEM_LIMIT = 56 * 1024 * 1024


def _cparams(sem):
    return pltpu.CompilerParams(dimension_semantics=sem, vmem_limit_bytes=VMEM_LIMIT)


def _const_spec(shape):
    nd = len(shape)
    return pl.BlockSpec(shape, lambda *_: (0,) * nd, pipeline_mode=pl.Buffered(1))


def _sigmoid(x):
    return 0.5 * (jnp.tanh(0.5 * x) + 1.0)


def _log_sigmoid(x):
    return jnp.minimum(x, 0.0) - jnp.log(1.0 + jnp.exp(-jnp.abs(x)))


def _rmsnorm(x, g):
    r = lax.rsqrt(jnp.mean(x * x, axis=-1, keepdims=True) + EPS)
    return (x * r) * g


def _dot(a, b):
    return jnp.dot(a, b, preferred_element_type=F32)


def _dot_nt(a, b):
    return lax.dot_general(a, b, (((1,), (1,)), ((), ())), preferred_element_type=F32)


def _dot_tn(a, b):
    return lax.dot_general(a, b, (((0,), (0,)), ((), ())), preferred_element_type=F32)


def _memkv_kernel(mem_ref, g_ref, wk_ref, wv_ref, k_ref, v_ref, kb_ref, vb_ref):
    mn = _rmsnorm(mem_ref[...], g_ref[...]).astype(BF16)
    k = _dot(mn, wk_ref[...].astype(BF16))
    v = _dot(mn, wv_ref[...].astype(BF16))
    k_ref[...] = k
    v_ref[...] = v
    kb_ref[...] = k.astype(BF16)
    vb_ref[...] = v.astype(BF16)


def _memory_kv(mem2d, g, wk, wv):
    n = mem2d.shape[0]
    tm = 512
    row = pl.BlockSpec((tm, D_MODEL), lambda i: (i, 0))
    return pl.pallas_call(
        _memkv_kernel,
        grid=(n // tm,),
        in_specs=[row, _const_spec((1, D_MODEL)), _const_spec((D_MODEL, D_MODEL)),
                  _const_spec((D_MODEL, D_MODEL))],
        out_specs=[row, row, row, row],
        out_shape=[jax.ShapeDtypeStruct((n, D_MODEL), F32)] * 2 + [jax.ShapeDtypeStruct((n, D_MODEL), BF16)] * 2,
        compiler_params=_cparams(("parallel",)),
        name="memory_kv",
    )(mem2d, g, wk, wv)


def _inproj_a_kernel(x_ref, g_ref, wqk_ref, wv_ref, wo_ref, wif_ref, bif_ref, cw_ref, cb_ref, cs_ref,
                     xn_ref, q_ref, kt_ref, v_ref, o_ref, gt_ref, cn_ref, ext_ref, *, nseg, sl, lc):
    i = pl.program_id(1)
    tm = nseg * sl
    tail = CONV_PAD - (CONV_W - 1)

    @pl.when(i == 0)
    def _():
        for s in range(nseg):
            ext_ref[s, 0:tail, :] = jnp.zeros((tail, 2 * D_MODEL), F32)
            ext_ref[s, tail:CONV_PAD, :] = cs_ref[s]

    parts = [_rmsnorm(x_ref[r:r + NORM_ROWS, :], g_ref[...]).astype(BF16) for r in range(0, tm, NORM_ROWS)]
    xn = jnp.concatenate(parts, axis=0)
    xn_ref[...] = xn

    zqk = jnp.concatenate([_dot(part, wqk_ref[...]) for part in parts], axis=0)
    ks = []
    row8 = lax.broadcasted_iota(jnp.int32, (CONV_PAD, 2 * D_MODEL), 0)
    for s in range(nseg):
        cur = zqk[s * sl:(s + 1) * sl, :]
        prev = ext_ref[s]
        acc = cb_ref[...] + cur * cw_ref[CONV_W - 1:CONV_W, :]
        for d in range(1, CONV_W):
            back = pltpu.roll(cur, d, axis=0)
            head = jnp.where(row8 < d, pltpu.roll(prev, d, axis=0), back[0:CONV_PAD, :])
            back = jnp.concatenate([head, back[CONV_PAD:, :]], axis=0)
            acc = acc + back * cw_ref[CONV_W - 1 - d:CONV_W - d, :]
        qk = acc * _sigmoid(acc)
        q_ref[s * sl:(s + 1) * sl, :] = (qk[:, :D_MODEL] * (HEAD_DIM ** -0.5)).astype(BF16)
        ks.append(qk[:, D_MODEL:])
        ext_ref[s] = cur[sl - CONV_PAD:sl, :]
        cn_ref[s] = ext_ref[s, tail:CONV_PAD, :]
    if tm % LANES:
        ks.append(jnp.zeros((LANES - tm % LANES, D_MODEL), F32))
    k = jnp.concatenate(ks, axis=0) if len(ks) > 1 else ks[0]
    kt = k.T.astype(BF16)
    nch = tm // lc
    per_seq = sl // lc
    for c in range(nch):
        kt_ref[c // per_seq, c % per_seq] = kt[:, c * lc:(c + 1) * lc]

    v_ref[...] = _dot(xn, wv_ref[...]).astype(BF16)
    o_ref[...] = _sigmoid(_dot(xn, wo_ref[...])).astype(BF16)

    zg = _dot(xn, wif_ref[...])
    if tm % LANES:
        zg = jnp.concatenate([zg, jnp.zeros((LANES - tm % LANES, LANES), F32)], axis=0)
    zt = zg.T[0:2 * HEADS, :]
    z = jnp.concatenate([zt[:, c * lc:(c + 1) * lc] + bif_ref[...] for c in range(nch)], axis=0)
    is_ig = (lax.broadcasted_iota(jnp.int32, z.shape, 0) % (2 * HEADS)) < HEADS
    g = jnp.where(is_ig, z, _log_sigmoid(z))
    upper = jnp.where(lax.broadcasted_iota(jnp.int32, (lc, lc), 0)
                      <= lax.broadcasted_iota(jnp.int32, (lc, lc), 1), 1.0, 0.0)
    bc = jnp.dot(g, upper, preferred_element_type=F32, precision=lax.Precision.HIGHEST)
    a = g - pltpu.roll(bc, nch * 2 * HEADS - HEADS, axis=0)
    amax = jnp.broadcast_to(jnp.max(a, axis=-1, keepdims=True), z.shape)
    gb = jnp.where(is_ig, g, bc)
    for c in range(nch):
        gt_ref[c // per_seq, c % per_seq, 0:2 * HEADS, :] = gb[c * 8:(c + 1) * 8, :]
        gt_ref[c // per_seq, c % per_seq, 2 * HEADS:4 * HEADS, :] = amax[c * 8:(c + 1) * 8, :]


def _tile_geometry(b, l, tile=TOKEN_TILE):
    sl = min(l, tile)
    nseg = max(1, min(b, TOKEN_TILE // sl))
    return nseg, sl


def _tok_spec(nseg, sl, nt, w):
    return pl.BlockSpec((nseg * sl, w), lambda bi, i: (bi * nt + i, 0))


def _inproj_a(x2d, b, l, g, wqk, wv, wo, wift, bif, cw, cb, cs):
    nseg, sl = _tile_geometry(b, l)
    nt = l // sl
    n = b * l
    CHUNK = min(MLSTM_BLOCK, l)
    per_seq = sl // CHUNK
    tok = functools.partial(_tok_spec, nseg, sl, nt)
    state = pl.BlockSpec((nseg, CONV_W - 1, 2 * D_MODEL), lambda bi, i: (bi, 0, 0))
    return pl.pallas_call(
        functools.partial(_inproj_a_kernel, nseg=nseg, sl=sl, lc=CHUNK),
        grid=(b // nseg, nt),
        in_specs=[tok(D_MODEL), _const_spec((1, D_MODEL)), _const_spec((D_MODEL, 2 * D_MODEL)),
                  _const_spec((D_MODEL, D_MODEL)), _const_spec((D_MODEL, D_MODEL)),
                  _const_spec((D_MODEL, LANES)), _const_spec((2 * HEADS, 1)),
                  _const_spec((CONV_W, 2 * D_MODEL)), _const_spec((1, 2 * D_MODEL)), state],
        out_specs=[tok(D_MODEL), tok(D_MODEL),
                   pl.BlockSpec((nseg, per_seq, D_MODEL, CHUNK), lambda bi, i: (bi, i, 0, 0)),
                   tok(D_MODEL), tok(D_MODEL),
                   pl.BlockSpec((nseg, per_seq, 4 * HEADS, CHUNK), lambda bi, i: (bi, i, 0, 0)),
                   state],
        out_shape=[jax.ShapeDtypeStruct((n, D_MODEL), BF16), jax.ShapeDtypeStruct((n, D_MODEL), BF16),
                   jax.ShapeDtypeStruct((b, l // CHUNK, D_MODEL, CHUNK), BF16),
                   jax.ShapeDtypeStruct((n, D_MODEL), BF16), jax.ShapeDtypeStruct((n, D_MODEL), BF16),
                   jax.ShapeDtypeStruct((b, l // CHUNK, 4 * HEADS, CHUNK), F32),
                   jax.ShapeDtypeStruct((b, CONV_W - 1, 2 * D_MODEL), F32)],
        scratch_shapes=[pltpu.VMEM((nseg, CONV_PAD, 2 * D_MODEL), F32)],
        compiler_params=_cparams(("parallel", "arbitrary")),
        name="inproj_a",
    )(x2d, g, wqk, wv, wo, wift, bif, cw, cb, cs)


def _inproj_b_kernel(x_ref, wu_ref, wgv_ref, wmq_ref, wgate_ref, lng_ref, lnb_ref, ws_ref, bst_ref,
                     mk_ref, mv_ref, ug_ref, att_ref, gates_ref, *rest, nseg, sl, cl, emit_gv):
    tm = nseg * sl
    xn = x_ref[...]

    gates_ref[...] = _sigmoid(_dot(xn, wgate_ref[...])).astype(BF16)

    gvr = jax.nn.gelu(_dot(xn, wgv_ref[...]))
    mu = jnp.mean(gvr, axis=-1, keepdims=True)
    xc = gvr - mu
    r = lax.rsqrt(jnp.mean(xc * xc, axis=-1, keepdims=True) + EPS)
    gv = (xc * r) * lng_ref[...] + lnb_ref[...]
    if emit_gv:
        rest[0][...] = gv
    gvb = gv.astype(BF16)
    u = jax.nn.gelu(_dot(xn, wu_ref[...]))
    tri = (lax.broadcasted_iota(jnp.int32, (cl, cl), 0) >= lax.broadcasted_iota(jnp.int32, (cl, cl), 1))
    for gi in range(GMLP_GROUPS):
        wsg = jnp.where(tri, ws_ref[gi], 0.0).astype(BF16)
        lo, hi = gi * GMLP_GROUP_DIM, (gi + 1) * GMLP_GROUP_DIM
        for c in range(tm // cl):
            sp = _dot(wsg, gvb[c * cl:(c + 1) * cl, lo:hi]) + bst_ref[:, gi:gi + 1]
            ug_ref[c * cl:(c + 1) * cl, lo:hi] = (u[c * cl:(c + 1) * cl, lo:hi] * sp).astype(BF16)

    mq = _dot(xn, wmq_ref[...]).astype(BF16)
    for s in range(nseg):
        r0, r1 = s * sl, (s + 1) * sl
        for h in range(HEADS):
            lo, hi = h * HEAD_DIM, (h + 1) * HEAD_DIM
            sc = _dot_nt(mq[r0:r1, lo:hi], mk_ref[s, :, lo:hi]) * (HEAD_DIM ** -0.5)
            e = jnp.exp(sc - jnp.max(sc, axis=-1, keepdims=True))
            a = (e / jnp.sum(e, axis=-1, keepdims=True)).astype(BF16)
            att_ref[r0:r1, lo:hi] = _dot(a, mv_ref[s, :, lo:hi]).astype(BF16)


def _inproj_b(xn2d, b, l, wu, wgv, wmq, wgate, lng, lnb, ws, bst, mk, mv, cl, emit_gv):
    nseg, sl = _tile_geometry(b, l)
    nt = l // sl
    n = b * l
    tok = functools.partial(_tok_spec, nseg, sl, nt)
    mem = pl.BlockSpec((nseg, N_MEM, D_MODEL), lambda bi, i: (bi, 0, 0))
    out_specs = [tok(D_MODEL), tok(D_MODEL), tok(3 * D_MODEL)]
    out_shape = [jax.ShapeDtypeStruct((n, D_MODEL), BF16), jax.ShapeDtypeStruct((n, D_MODEL), BF16),
                 jax.ShapeDtypeStruct((n, 3 * D_MODEL), BF16)]
    if emit_gv:
        out_specs.append(tok(D_MODEL))
        out_shape.append(jax.ShapeDtypeStruct((n, D_MODEL), F32))
    return pl.pallas_call(
        functools.partial(_inproj_b_kernel, nseg=nseg, sl=sl, cl=cl, emit_gv=emit_gv),
        grid=(b // nseg, nt),
        in_specs=[tok(D_MODEL), _const_spec((D_MODEL, D_MODEL)),
                  _const_spec((D_MODEL, D_MODEL)), _const_spec((D_MODEL, D_MODEL)),
                  _const_spec((D_MODEL, 3 * D_MODEL)), _const_spec((1, D_MODEL)), _const_spec((1, D_MODEL)),
                  _const_spec((GMLP_GROUPS, cl, cl)), _const_spec((cl, GMLP_GROUPS)), mem, mem],
        out_specs=out_specs,
        out_shape=out_shape,
        compiler_params=_cparams(("parallel", "parallel")),
        name="inproj_b",
    )(xn2d, wu, wgv, wmq, wgate, lng, lnb, ws, bst, mk, mv)


def _mlstm_kernel(q_ref, kt_ref, v_ref, gt_ref, *rest, nseg, cb, zero_state):
    i = pl.program_id(1)

    if zero_state:
        hm_ref, c_ref, n_ref, m_ref, st_ref = rest
    else:
        c0_ref, n0_ref, m0_ref, hm_ref, c_ref, n_ref, m_ref, st_ref = rest

    @pl.when(i == 0)
    def _():
        if zero_state:
            st_ref[...] = jnp.zeros(st_ref.shape, F32)
            m_ref[...] = jnp.zeros(m_ref.shape, F32)
        else:
            st_ref[:, :, :, :HEAD_DIM] = c0_ref[...]
            st_ref[:, :, :, HEAD_DIM:] = n0_ref[...]
            m_ref[...] = m0_ref[...]

    for s in range(nseg):
        _mlstm_sequence(q_ref, kt_ref.at[s], v_ref, gt_ref.at[s], st_ref.at[s], m_ref.at[s], hm_ref, s * cb, cb)

    @pl.when(i == pl.num_programs(1) - 1)
    def _():
        c_ref[...] = st_ref[:, :, :, :HEAD_DIM]
        n_ref[...] = st_ref[:, :, :, HEAD_DIM:]


def _mlstm_sequence(q_ref, kt_ref, v_ref, gt_ref, c_ref, m_ref, hm_ref, row0, cb):
    L = kt_ref.shape[-1]
    nch = cb // L
    ti = lax.broadcasted_iota(jnp.int32, (L, L), 0)
    si = lax.broadcasted_iota(jnp.int32, (L, L), 1)
    tri = ti >= si
    eye = ti == si

    rows = 4 * HEADS
    g_all = gt_ref[...].reshape(nch * rows, L)

    m_in = [m_ref[:, 0:1]]
    for c in range(nch):
        b_last4 = g_all[c * rows + HEADS:c * rows + 2 * HEADS, L - 1:L]
        amax4 = g_all[c * rows + 2 * HEADS:c * rows + 3 * HEADS, 0:1]
        m_in.append(jnp.maximum(b_last4 + m_in[-1], b_last4 + amax4))

    ones = jnp.ones((L, LANES), BF16)
    st = [c_ref[h] for h in range(HEADS)]
    for c in range(nch):
        r0, r1 = row0 + c * L, row0 + (c + 1) * L
        for h in range(HEADS):
            lo, hi = h * HEAD_DIM, (h + 1) * HEAD_DIM
            ig_r = g_all[c * rows + h:c * rows + h + 1, :]
            bc_r = g_all[c * rows + HEADS + h:c * rows + HEADS + h + 1, :]
            a_r = ig_r - bc_r
            bc_c = jnp.sum(jnp.where(eye, bc_r, 0.0), axis=-1, keepdims=True)
            m0 = m_in[c][h:h + 1, :]
            m_last = m_in[c + 1][h:h + 1, :]
            dmat = jnp.where(tri, bc_c + a_r, NEG_INF)
            inter = bc_c + m0
            m = jnp.maximum(inter, jnp.max(dmat, axis=-1, keepdims=True))
            w_intra = jnp.exp(dmat - m)
            w_inter = jnp.exp(inter - m)
            q = q_ref[r0:r1, lo:hi]
            kt = kt_ref[c, lo:hi, :]
            v = v_ref[r0:r1, lo:hi]
            s = _dot(q, kt) * w_intra
            qs = _dot(q, st[h].astype(BF16))
            num = w_inter * qs[:, :HEAD_DIM] + _dot(s.astype(BF16), v)
            den = w_inter * qs[:, HEAD_DIM:HEAD_DIM + 1] + jnp.sum(s, axis=-1, keepdims=True)
            hh = num / jnp.maximum(jnp.abs(den), jnp.exp(-m))
            bc_last = bc_r[:, L - 1:L]
            w_last = jnp.exp(bc_last + a_r - m_last)
            decay = jnp.exp(bc_last + m0 - m_last)
            ktw = (kt.astype(F32) * w_last).astype(BF16)
            st[h] = decay * st[h] + _dot(ktw, jnp.concatenate([v, ones], axis=1))
            hm_ref[r0:r1, lo:hi] = hh.astype(BF16)

    for h in range(HEADS):
        c_ref[h] = st[h]
    m_ref[...] = jnp.broadcast_to(m_in[nch], (HEADS, LANES))


def _mlstm(q, kt, v, gt, state, b, l):
    nseg, cb = _tile_geometry(b, l, MLSTM_TILE)
    nt = l // cb
    CHUNK = kt.shape[-1]
    tok = _tok_spec(nseg, cb, nt, D_MODEL)
    cs = pl.BlockSpec((nseg, HEADS, HEAD_DIM, HEAD_DIM), lambda bi, i: (bi, 0, 0, 0))
    ns = pl.BlockSpec((nseg, HEADS, HEAD_DIM, LANES), lambda bi, i: (bi, 0, 0, 0))
    ms = pl.BlockSpec((nseg, HEADS, LANES), lambda bi, i: (bi, 0, 0))
    state_specs = [] if state is None else [cs, ns, ms]
    return pl.pallas_call(
        functools.partial(_mlstm_kernel, nseg=nseg, cb=cb, zero_state=state is None),
        grid=(b // nseg, nt),
        in_specs=[tok, pl.BlockSpec((nseg, cb // CHUNK, D_MODEL, CHUNK), lambda bi, i: (bi, i, 0, 0)), tok,
                  pl.BlockSpec((nseg, cb // CHUNK, 4 * HEADS, CHUNK), lambda bi, i: (bi, i, 0, 0))] + state_specs,
        out_specs=[tok, cs, ns, ms],
        out_shape=[jax.ShapeDtypeStruct((b * l, D_MODEL), BF16),
                   jax.ShapeDtypeStruct((b, HEADS, HEAD_DIM, HEAD_DIM), F32),
                   jax.ShapeDtypeStruct((b, HEADS, HEAD_DIM, LANES), F32),
                   jax.ShapeDtypeStruct((b, HEADS, LANES), F32)],
        scratch_shapes=[pltpu.VMEM((nseg, HEADS, HEAD_DIM, HEAD_DIM + LANES), F32)],
        compiler_params=_cparams(("parallel", "arbitrary")),
        name="mlstm",
    )(q, kt, v, gt, *(state or ()))


def _merge_kernel(x_ref, hh_ref, o_ref, ng_ref, ug_ref, att_ref, gates_ref, wa_ref, wb_ref, wc_ref, wo_ref, h_ref):
    parts = []
    for h in range(HEADS):
        lo, hi = h * HEAD_DIM, (h + 1) * HEAD_DIM
        hh = hh_ref[:, lo:hi].astype(F32)
        hn = hh * lax.rsqrt(jnp.mean(hh * hh, axis=-1, keepdims=True) + EPS)
        parts.append(((hn * ng_ref[:, lo:hi]) * o_ref[:, lo:hi].astype(F32)).astype(BF16))
    br = _dot(parts[0], wa_ref[0:HEAD_DIM, :])
    for h in range(1, HEADS):
        br = br + _dot(parts[h], wa_ref[h * HEAD_DIM:(h + 1) * HEAD_DIM, :])
    g = gates_ref[...].astype(F32)
    mixed = g[:, :D_MODEL] * br
    mixed = mixed + g[:, D_MODEL:2 * D_MODEL] * _dot(ug_ref[...], wb_ref[...])
    mixed = mixed + g[:, 2 * D_MODEL:] * _dot(att_ref[...], wc_ref[...])
    h_ref[...] = x_ref[...] + _dot(mixed.astype(BF16), wo_ref[...])


def _merge(x2d, hh, osig, ng, ug, att, gates, wa, wb, wc, wo, tm):
    n = x2d.shape[0]
    row = lambda w: pl.BlockSpec((tm, w), lambda i: (i, 0))
    wspec = _const_spec((D_MODEL, D_MODEL))
    return pl.pallas_call(
        _merge_kernel,
        grid=(n // tm,),
        in_specs=[row(D_MODEL), row(D_MODEL), row(D_MODEL), _const_spec((1, D_MODEL)), row(D_MODEL), row(D_MODEL),
                  row(3 * D_MODEL), wspec, wspec, wspec, wspec],
        out_specs=row(D_MODEL),
        out_shape=jax.ShapeDtypeStruct((n, D_MODEL), F32),
        compiler_params=_cparams(("parallel",)),
        name="merge",
    )(x2d, hh, osig, ng, ug, att, gates, wa, wb, wc, wo)


MOE_T = 256
MOE_STEP_SUBS = 2
MOE_CAP = 32
MOE_S = N_EXPERTS * MOE_CAP
MOE_OVF = 512
SEG_ALIGN = 16
EXPERT_BLOCK = 128


def _moe_route_kernel(hp_ref, hs_ref, g_ref, wrg_ref, brg_ref, wre_ref, bre_ref,
                      main_ref, ovf_ref, info_ref, meta_ref, *, nsp):
    n = MOE_STEP_SUBS * MOE_T
    h = jnp.where(pl.program_id(0) < nsp // MOE_STEP_SUBS, hp_ref[...], hs_ref[...])
    xm = _rmsnorm(h, g_ref[...]).astype(BF16)
    lane = lax.broadcasted_iota(jnp.int32, (n, LANES), 1).astype(F32)
    lg = jnp.where(lane < N_GROUPS, _dot(xm, wrg_ref[...]) + brg_ref[...], NEG_INF)
    gmax = jnp.max(lg, axis=-1, keepdims=True)
    p_top = 1.0 / jnp.sum(jnp.exp(lg - gmax), axis=-1, keepdims=True)
    grp = jnp.min(jnp.where(lg == gmax, lane, float(LANES)), axis=-1, keepdims=True)
    el = _dot(xm, wre_ref[...]) + bre_ref[...]
    in_grp = (lane >= grp * EXPERTS_PER_GROUP) & (lane < (grp + 1.0) * EXPERTS_PER_GROUP)
    vals = jnp.where(in_grp, el, NEG_INF)
    v1 = jnp.max(vals, axis=-1, keepdims=True)
    i1 = jnp.min(jnp.where(vals == v1, lane, float(LANES)), axis=-1, keepdims=True)
    vals2 = jnp.where(lane == i1, NEG_INF, vals)
    v2 = jnp.max(vals2, axis=-1, keepdims=True)
    i2 = jnp.min(jnp.where(vals2 == v2, lane, float(LANES)), axis=-1, keepdims=True)
    r = jnp.exp(v2 - v1)
    p1 = p_top / (1.0 + r)
    p2 = p_top * r / (1.0 + r)
    sel1 = lane == i1
    sel2 = lane == i2
    onehot = jnp.where(sel1 | sel2, 1.0, 0.0)

    deferred = []
    for s in range(MOE_STEP_SUBS):
        rows = slice(s * MOE_T, (s + 1) * MOE_T)
        deferred.append(_route_sub_tile(
            xm[rows, :], onehot[rows, :], i1[rows, :], i2[rows, :],
            p1[rows, :], p2[rows, :], main_ref.at[s], ovf_ref.at[s], info_ref.at[rows, :], meta_ref.at[s]))
    for write_overflow in deferred:
        write_overflow()


def _route_sub_tile(xm, onehot, i1, i2, p1, p2, main_ref, ovf_ref, info_ref, meta_ref):
    t = MOE_T
    lane = lax.broadcasted_iota(jnp.int32, (t, LANES), 1)
    sel1 = lane.astype(F32) == i1
    sel2 = lane.astype(F32) == i2
    cnt = jnp.sum(onehot, axis=0, keepdims=True).astype(jnp.int32)
    pn = jnp.bitwise_and(jnp.maximum(cnt - MOE_CAP, 0) + (SEG_ALIGN - 1), -SEG_ALIGN)
    pn8 = jnp.broadcast_to(pn, (8, LANES))
    earlier = jnp.where(lax.broadcasted_iota(jnp.int32, (LANES, LANES), 0)
                        < lax.broadcasted_iota(jnp.int32, (LANES, LANES), 1), 1.0, 0.0).astype(BF16)
    off_f8 = _dot(pn8.astype(F32).astype(BF16), earlier)
    row8 = lax.broadcasted_iota(jnp.int32, (8, LANES), 0)
    meta_ref[...] = jnp.where(row8 == 0, off_f8.astype(jnp.int32), jnp.where(row8 == 1, pn8, 0))

    ti = lax.broadcasted_iota(jnp.int32, (t, t), 0)
    si = lax.broadcasted_iota(jnp.int32, (t, t), 1)
    before = jnp.where(ti > si, 1.0, 0.0).astype(BF16)
    rank = _dot(before, onehot.astype(BF16))
    off_f = off_f8[0:1, :]

    def slot_row(sel, idx):
        rk = jnp.sum(jnp.where(sel, rank, 0.0), axis=-1, keepdims=True)
        of = jnp.sum(jnp.where(sel, off_f, 0.0), axis=-1, keepdims=True)
        return jnp.where(rk < MOE_CAP, idx * MOE_CAP + rk, MOE_S - MOE_CAP + of + rk)

    pos1 = slot_row(sel1, i1)
    pos2 = slot_row(sel2, i2)
    info_ref[...] = (jnp.where(lane == 0, pos1, 0.0) + jnp.where(lane == 1, pos2, 0.0)
                     + jnp.where(lane == 2, p1, 0.0) + jnp.where(lane == 3, p2, 0.0))

    eye = ti == si
    pos1_r = jnp.sum(jnp.where(eye, pos1, 0.0), axis=0, keepdims=True)
    pos2_r = jnp.sum(jnp.where(eye, pos2, 0.0), axis=0, keepdims=True)

    def gather(rows, base):
        srow = (lax.broadcasted_iota(jnp.int32, (rows, t), 0) + base).astype(F32)
        pick = jnp.where((srow == pos1_r) | (srow == pos2_r), 1.0, 0.0).astype(BF16)
        return _dot(pick, xm).astype(BF16)

    main_ref[...] = gather(MOE_S, 0)
    has_ovf = jnp.sum(pn) > 0

    def write_overflow():
        @pl.when(has_ovf)
        def _():
            ovf_ref[...] = gather(MOE_OVF, MOE_S)

        @pl.when(jnp.logical_not(has_ovf))
        def _():
            ovf_ref[...] = jnp.zeros(ovf_ref.shape, ovf_ref.dtype)
    return write_overflow


def _two_group_specs(nsp, cols):
    rows = MOE_STEP_SUBS * MOE_T
    npb = nsp // MOE_STEP_SUBS
    return (pl.BlockSpec((rows, cols), lambda j, *_: (jnp.minimum(j, npb - 1), 0)),
            pl.BlockSpec((rows, cols), lambda j, *_: (jnp.maximum(j - npb, 0), 0)))


def _moe_route(hp, hs, g, wrg, brg, wre, bre):
    nsp = hp.shape[0] // MOE_T
    ns = nsp + hs.shape[0] // MOE_T
    sub = MOE_STEP_SUBS
    return pl.pallas_call(
        functools.partial(_moe_route_kernel, nsp=nsp),
        grid=(ns // sub,),
        in_specs=[*_two_group_specs(nsp, D_MODEL), _const_spec((1, D_MODEL)),
                  _const_spec((D_MODEL, LANES)), _const_spec((1, LANES)),
                  _const_spec((D_MODEL, LANES)), _const_spec((1, LANES))],
        out_specs=[pl.BlockSpec((sub, MOE_S, D_MODEL), lambda j: (j, 0, 0)),
                   pl.BlockSpec((sub, MOE_OVF, D_MODEL), lambda j: (j, 0, 0)),
                   pl.BlockSpec((sub * MOE_T, LANES), lambda j: (j, 0)),
                   pl.BlockSpec((sub, 8, LANES), lambda j: (j, 0, 0))],
        out_shape=[jax.ShapeDtypeStruct((ns, MOE_S, D_MODEL), BF16),
                   jax.ShapeDtypeStruct((ns, MOE_OVF, D_MODEL), BF16),
                   jax.ShapeDtypeStruct((ns * MOE_T, LANES), F32),
                   jax.ShapeDtypeStruct((ns, 8, LANES), jnp.int32)],
        compiler_params=_cparams(("arbitrary",)),
        name="moe_route",
    )(hp, hs, g, wrg, brg, wre, bre)


def _swiglu(x, wgu, wdb):
    gu = _dot(x, wgu[...])
    gate = gu[:, :D_EXPERT]
    hid = (gate * _sigmoid(gate)) * gu[:, D_EXPERT:]
    return _dot(hid.astype(BF16), wdb[...]).astype(BF16)


def _moe_expert_kernel(offs_ref, pns_ref, eflag_ref, main_ref, slots_hbm, wg_ref, wu_ref, wd_ref,
                       mout_ref, out_hbm, xbuf, obuf, wgu, wdb, sem_in, sem_out, *, n_sub, jb, n_rb):
    e = pl.program_id(0)
    rb = pl.program_id(1)

    @pl.when((e == 0) & (rb == 0))
    def _():
        xbuf[...] = jnp.zeros_like(xbuf)

    @pl.when(rb == 0)
    def _():
        wgu[:, :D_EXPERT] = wg_ref[...].astype(BF16)
        wgu[:, D_EXPERT:] = wu_ref[...].astype(BF16)
        wdb[...] = wd_ref[...].astype(BF16)

    sb = main_ref.shape[0]
    part = sb // 2 if sb % 2 == 0 else sb
    for s0 in range(0, sb, part):
        y = _swiglu(main_ref[s0:s0 + part].reshape(part * MOE_CAP, D_MODEL), wgu, wdb)
        mout_ref[s0:s0 + part] = y.reshape(part, MOE_CAP, D_MODEL)

    def copy_in(j, src, dst):
        return pltpu.make_async_copy(slots_hbm.at[j, pl.ds(src, SEG_ALIGN), :],
                                     xbuf.at[pl.ds(dst, SEG_ALIGN), :], sem_in)

    def copy_out(j, src, dst):
        return pltpu.make_async_copy(obuf.at[pl.ds(src, SEG_ALIGN), :],
                                     out_hbm.at[j, pl.ds(dst, SEG_ALIGN), :], sem_out)

    def for_each_chunk(g, fn):
        def seg(jj, cur):
            j = g * jb + jj
            n = pns_ref[j * N_EXPERTS + e]
            off = offs_ref[j * N_EXPERTS + e]

            def chunk(k, c):
                fn(j, pl.multiple_of(off + k * SEG_ALIGN, SEG_ALIGN), pl.multiple_of(cur + k * SEG_ALIGN, SEG_ALIGN))
                return c
            lax.fori_loop(0, n // SEG_ALIGN, chunk, 0)
            return cur + n
        return lax.fori_loop(0, jb, seg, 0)

    def group(g, carry):
        total = for_each_chunk(g, lambda j, r, b: copy_in(j, r, b).start())
        nchunk = total // SEG_ALIGN

        def wait_in(k, c):
            copy_in(0, 0, 0).wait()
            return c
        lax.fori_loop(0, nchunk, wait_in, 0)

        def block(bi, c):
            r0 = pl.multiple_of(bi * EXPERT_BLOCK, EXPERT_BLOCK)
            obuf[pl.ds(r0, EXPERT_BLOCK), :] = _swiglu(xbuf[pl.ds(r0, EXPERT_BLOCK), :], wgu, wdb)
            return c
        lax.fori_loop(0, (total + EXPERT_BLOCK - 1) // EXPERT_BLOCK, block, 0)

        for_each_chunk(g, lambda j, r, b: copy_out(j, b, r).start())

        def wait_out(k, c):
            copy_out(0, 0, 0).wait()
            return c
        lax.fori_loop(0, nchunk, wait_out, 0)
        return carry

    @pl.when((rb == n_rb - 1) & (eflag_ref[e] > 0))
    def _():
        lax.fori_loop(0, n_sub // jb, group, 0)


def _moe_expert(main, ovf, offs, pns, eflag, wg, wu, wd):
    ns = main.shape[0]
    n_rb = 1
    sb = ns // n_rb
    jb = max(d for d in range(1, 12) if ns % d == 0)
    rows = jb * MOE_T + EXPERT_BLOCK
    main4 = main.reshape(ns, N_EXPERTS, MOE_CAP, D_MODEL)
    mspec = pl.BlockSpec((sb, None, MOE_CAP, D_MODEL), lambda e, rb, *_: (rb, e, 0, 0))
    wspec = lambda a, b: pl.BlockSpec((None, a, b), lambda e, rb, *_: (e, 0, 0))
    grid_spec = pltpu.PrefetchScalarGridSpec(
        num_scalar_prefetch=3,
        grid=(N_EXPERTS, n_rb),
        in_specs=[mspec, pl.BlockSpec(memory_space=pl.ANY),
                  wspec(D_MODEL, D_EXPERT), wspec(D_MODEL, D_EXPERT), wspec(D_EXPERT, D_MODEL)],
        out_specs=[mspec, pl.BlockSpec(memory_space=pl.ANY)],
        scratch_shapes=[pltpu.VMEM((rows, D_MODEL), BF16), pltpu.VMEM((rows, D_MODEL), BF16),
                        pltpu.VMEM((D_MODEL, 2 * D_EXPERT), BF16), pltpu.VMEM((D_EXPERT, D_MODEL), BF16),
                        pltpu.SemaphoreType.DMA(()), pltpu.SemaphoreType.DMA(())],
    )
    mout, oout = pl.pallas_call(
        functools.partial(_moe_expert_kernel, n_sub=ns, jb=jb, n_rb=n_rb),
        grid_spec=grid_spec,
        out_shape=[jax.ShapeDtypeStruct(main4.shape, main4.dtype), jax.ShapeDtypeStruct(ovf.shape, ovf.dtype)],
        input_output_aliases={4: 1},
        compiler_params=_cparams(("arbitrary", "arbitrary")),
        name="moe_expert",
    )(offs, pns, eflag, main4, ovf, wg, wu, wd)
    return mout.reshape(main.shape), oout


def _moe_combine_kernel(jflag_ref, sflag_ref, hp_ref, hs_ref, main_ref, ovf_ref, info_ref, gf_ref,
                        yp_ref, ys_ref, acc_ref, *, nsp):
    j = pl.program_id(0)
    npb = nsp // MOE_STEP_SUBS

    def weights(s, cols, base):
        info = info_ref[s * MOE_T:(s + 1) * MOE_T, :]
        scol = (lax.broadcasted_iota(jnp.int32, (MOE_T, cols), 1) + base).astype(F32)
        return (jnp.where(scol == info[:, 0:1], info[:, 2:3], 0.0)
                + jnp.where(scol == info[:, 1:2], info[:, 3:4], 0.0)).astype(BF16)

    for s in range(MOE_STEP_SUBS):
        rows = slice(s * MOE_T, (s + 1) * MOE_T)
        h = jnp.where(j < npb, hp_ref[rows, :], hs_ref[rows, :])
        acc_ref[rows, :] = h + _dot(weights(s, MOE_S, 0), main_ref[s])

    for s in range(MOE_STEP_SUBS):
        @pl.when(jflag_ref[j * MOE_STEP_SUBS + s] > 0)
        def _(s=s):
            rows = slice(s * MOE_T, (s + 1) * MOE_T)
            acc_ref[rows, :] += _dot(weights(s, MOE_OVF, MOE_S), ovf_ref[s])

    y = _rmsnorm(acc_ref[...], gf_ref[...])

    @pl.when(j < npb)
    def _():
        yp_ref[...] = y

    @pl.when(j >= npb)
    def _():
        ys_ref[...] = y


def _moe_combine(hp, hs, main, ovf, info, jflag, gf):
    nsp = hp.shape[0] // MOE_T
    ns = main.shape[0]
    sub = MOE_STEP_SUBS
    sflag = jnp.max(jflag.reshape(ns // sub, sub), axis=1)
    grid_spec = pltpu.PrefetchScalarGridSpec(
        num_scalar_prefetch=2,
        grid=(ns // sub,),
        in_specs=[*_two_group_specs(nsp, D_MODEL),
                  pl.BlockSpec((sub, MOE_S, D_MODEL), lambda j, jf, sf: (j, 0, 0)),
                  pl.BlockSpec((sub, MOE_OVF, D_MODEL), lambda j, jf, sf: (jnp.where(sf[j] > 0, j, 0), 0, 0)),
                  pl.BlockSpec((sub * MOE_T, LANES), lambda j, jf, sf: (j, 0)),
                  pl.BlockSpec((1, D_MODEL), lambda j, jf, sf: (0, 0))],
        out_specs=list(_two_group_specs(nsp, D_MODEL)),
        scratch_shapes=[pltpu.VMEM((sub * MOE_T, D_MODEL), F32)],
    )
    return pl.pallas_call(
        functools.partial(_moe_combine_kernel, nsp=nsp),
        grid_spec=grid_spec,
        out_shape=[jax.ShapeDtypeStruct(hp.shape, F32), jax.ShapeDtypeStruct(hs.shape, F32)],
        compiler_params=_cparams(("arbitrary",)),
        name="moe_combine",
    )(jflag, sflag, hp, hs, main, ovf, info, gf)


def _moe(hp, hs, p):
    main, ovf, info, meta = _moe_route(hp, hs, p["norm_moe_g"], p["w_rg"], p["b_rg"], p["w_re"], p["b_re"])
    pn = meta[:, 1, :N_EXPERTS]
    offs = meta[:, 0, :N_EXPERTS].reshape(-1)
    eflag = (jnp.sum(pn, axis=0) > 0).astype(jnp.int32)
    jflag = (jnp.sum(pn, axis=1) > 0).astype(jnp.int32)
    main, ovf = _moe_expert(main, ovf, offs, pn.reshape(-1), eflag, p["w_eg"], p["w_eu"], p["w_ed"])
    return _moe_combine(hp, hs, main, ovf, info, jflag, p["norm_final_g"])


def _trunk(x, mem_k, mem_v, c0, n0, m0, conv0, p, *, emit_gv):
    b, l, _ = x.shape
    n = b * l
    cl = min(l, GMLP_CHUNK)
    x2d = x.reshape(n, D_MODEL)
    xn, q, kt, v, osig, gt, conv_new = _inproj_a(
        x2d, b, l, p["norm_mix_g"], p["w_qk"], p["w_v"], p["w_o"], p["w_if"], p["b_if"],
        p["conv_w"], p["conv_b"], conv0)
    outs = _inproj_b(xn, b, l, p["w_u"], p["w_gv"], p["w_mq"], p["w_gate"],
                     p["gmlp_norm_g"], p["gmlp_norm_b"], p["w_s"][:, :cl, :cl], p["b_st"][:cl],
                     mem_k, mem_v, cl, emit_gv)
    ug, att, gates = outs[:3]
    on_lanes = lambda a: jnp.broadcast_to(a[..., None], a.shape + (LANES,))
    state = None if c0 is None else (c0, on_lanes(n0), on_lanes(m0))
    hh, c1, n1, m1 = _mlstm(q, kt, v, gt, state, b, l)
    n1 = n1[..., 0]
    h = _merge(x2d, hh, osig, p["mlstm_norm_g"], ug, att, gates,
               p["w_br_mlstm"], p["w_br_gmlp"], p["w_br_mem"], p["w_out"], TOKEN_TILE)
    return h, c1, n1, m1[..., 0], conv_new, (outs[3].reshape(b, l, D_MODEL) if emit_gv else None)


def kernel(x_prompt, x_sample, mem_prompt, cache_mem_k, cache_mem_v, state_mlstm_C, state_mlstm_n, state_mlstm_m, state_mlstm_conv, norm_mix_g, w_in, mlstm_i_b, mlstm_f_b, mlstm_conv_w, mlstm_conv_b, mlstm_norm_g, gmlp_norm_g, gmlp_norm_b, gmlp_w_s, gmlp_b_s, mem_norm_g, w_mem_k, w_mem_v, w_br_mlstm, w_br_gmlp, w_br_mem, w_out, norm_moe_g, w_router_group, b_router_group, w_router_expert, b_router_expert, w_exp_gate, w_exp_up, w_exp_down, norm_final_g):
    bp = x_prompt.shape[0]
    bs = x_sample.shape[0]
    W = D_MODEL
    wi = w_in[0]
    o_qk, o_v, o_o, o_i = 0, 2 * W, 3 * W, 4 * W
    o_f = o_i + HEADS
    o_u = o_f + HEADS
    o_gv, o_mq, o_gate = o_u + W, o_u + 2 * W, o_u + 3 * W
    row = lambda a: a.reshape(1, -1)
    pad_l = lambda a: jnp.pad(a, ((0, 0), (0, LANES - a.shape[1])))
    p = {
        "norm_mix_g": row(norm_mix_g[0]),
        "w_qk": wi[:, o_qk:o_v].astype(BF16), "w_v": wi[:, o_v:o_o].astype(BF16),
        "w_o": wi[:, o_o:o_i].astype(BF16),
        "w_if": pad_l(wi[:, o_i:o_u]).astype(BF16),
        "b_if": jnp.concatenate([mlstm_i_b[0], mlstm_f_b[0]]).reshape(2 * HEADS, 1),
        "conv_w": mlstm_conv_w[0], "conv_b": row(mlstm_conv_b[0]),
        "w_u": wi[:, o_u:o_gv].astype(BF16), "w_gv": wi[:, o_gv:o_mq].astype(BF16),
        "w_mq": wi[:, o_mq:o_gate].astype(BF16), "w_gate": wi[:, o_gate:].astype(BF16),
        "gmlp_norm_g": row(gmlp_norm_g[0]), "gmlp_norm_b": row(gmlp_norm_b[0]),
        "w_s": gmlp_w_s[0], "b_st": gmlp_b_s[0].T,
        "mlstm_norm_g": row(mlstm_norm_g[0]),
        "w_br_mlstm": w_br_mlstm[0].astype(BF16), "w_br_gmlp": w_br_gmlp[0].astype(BF16),
        "w_br_mem": w_br_mem[0].astype(BF16), "w_out": w_out[0].astype(BF16),
        "norm_moe_g": row(norm_moe_g[0]),
        "w_rg": pad_l(w_router_group[0]).astype(BF16), "b_rg": pad_l(row(b_router_group[0])),
        "w_re": pad_l(w_router_expert[0]).astype(BF16), "b_re": pad_l(row(b_router_expert[0])),
        "w_eg": w_exp_gate[0], "w_eu": w_exp_up[0], "w_ed": w_exp_down[0],
        "norm_final_g": row(norm_final_g),
    }

    mk_p, mv_p, mk_pb, mv_pb = _memory_kv(mem_prompt.reshape(bp * N_MEM, W), row(mem_norm_g[0]),
                                          w_mem_k[0], w_mem_v[0])

    zeros = lambda *s: jnp.zeros(s, F32)
    hp, cp, np_, mp, cvp, _ = _trunk(
        x_prompt, mk_pb.reshape(bp, N_MEM, W), mv_pb.reshape(bp, N_MEM, W),
        None, None, None, zeros(bp, CONV_W - 1, 2 * W), p, emit_gv=False)
    hs, cs, ns, ms, cvs, gvs = _trunk(
        x_sample, cache_mem_k[0].reshape(bs, N_MEM, W).astype(BF16),
        cache_mem_v[0].reshape(bs, N_MEM, W).astype(BF16),
        state_mlstm_C[0], state_mlstm_n[0], state_mlstm_m[0], state_mlstm_conv[0], p,
        emit_gv=True)

    yp, ys = _moe(hp, hs, p)
    kv_shape = (1, bp, N_MEM, HEADS, HEAD_DIM)
    return (yp.reshape(x_prompt.shape), ys.reshape(x_sample.shape), mk_p.reshape(kv_shape), mv_p.reshape(kv_shape),
            cp[None], np_[None], mp[None], cvp[None],
            cs[None], ns[None], ms[None], cvs[None], gvs[None])
```

```python
import functools

import jax
import jax.numpy as jnp
from jax import lax
from jax.experimental import pallas as pl
from jax.experimental.pallas import tpu as pltpu

D_MODEL = 1024
MLSTM_BLOCK = 512
TOKEN_TILE = 512
MLSTM_TILE = 2048
EPS = 1e-6
HEADS = 4
HEAD_DIM = 256
CONV_W = 4
GMLP_GROUPS = 4
GMLP_GROUP_DIM = 256
GMLP_CHUNK = 128
N_MEM = 256
N_GROUPS = 4
EXPERTS_PER_GROUP = 8
N_EXPERTS = 32
D_EXPERT = 256
LANES = 128
CONV_PAD = 8
NORM_ROWS = 128

F32 = jnp.float32
BF16 = jnp.bfloat16
NEG_INF = float("-inf")

VMEM_LIMIT = 56 * 1024 * 1024


def _cparams(sem):
    return pltpu.CompilerParams(dimension_semantics=sem, vmem_limit_bytes=VMEM_LIMIT)


def _const_spec(shape):
    nd = len(shape)
    return pl.BlockSpec(shape, lambda *_: (0,) * nd, pipeline_mode=pl.Buffered(1))


def _sigmoid(x):
    return 0.5 * (jnp.tanh(0.5 * x) + 1.0)


def _log_sigmoid(x):
    return jnp.minimum(x, 0.0) - jnp.log(1.0 + jnp.exp(-jnp.abs(x)))


def _rmsnorm(x, g):
    r = lax.rsqrt(jnp.mean(x * x, axis=-1, keepdims=True) + EPS)
    return (x * r) * g


def _dot(a, b):
    return jnp.dot(a, b, preferred_element_type=F32)


def _dot_nt(a, b):
    return lax.dot_general(a, b, (((1,), (1,)), ((), ())), preferred_element_type=F32)


def _dot_tn(a, b):
    return lax.dot_general(a, b, (((0,), (0,)), ((), ())), preferred_element_type=F32)


def _memkv_kernel(mem_ref, g_ref, wk_ref, wv_ref, k_ref, v_ref, kb_ref, vb_ref):
    mn = _rmsnorm(mem_ref[...], g_ref[...]).astype(BF16)
    k = _dot(mn, wk_ref[...].astype(BF16))
    v = _dot(mn, wv_ref[...].astype(BF16))
    k_ref[...] = k
    v_ref[...] = v
    kb_ref[...] = k.astype(BF16)
    vb_ref[...] = v.astype(BF16)


def _memory_kv(mem2d, g, wk, wv):
    n = mem2d.shape[0]
    tm = 512
    row = pl.BlockSpec((tm, D_MODEL), lambda i: (i, 0))
    return pl.pallas_call(
        _memkv_kernel,
        grid=(n // tm,),
        in_specs=[row, _const_spec((1, D_MODEL)), _const_spec((D_MODEL, D_MODEL)),
                  _const_spec((D_MODEL, D_MODEL))],
        out_specs=[row, row, row, row],
        out_shape=[jax.ShapeDtypeStruct((n, D_MODEL), F32)] * 2 + [jax.ShapeDtypeStruct((n, D_MODEL), BF16)] * 2,
        compiler_params=_cparams(("parallel",)),
        name="memory_kv",
    )(mem2d, g, wk, wv)


def _inproj_a_kernel(x_ref, g_ref, wqk_ref, wv_ref, wo_ref, wif_ref, bif_ref, cw_ref, cb_ref, cs_ref,
                     xn_ref, q_ref, kt_ref, v_ref, o_ref, gt_ref, cn_ref, ext_ref, *, nseg, sl, lc):
    i = pl.program_id(1)
    tm = nseg * sl
    tail = CONV_PAD - (CONV_W - 1)

    @pl.when(i == 0)
    def _():
        for s in range(nseg):
            ext_ref[s, 0:tail, :] = jnp.zeros((tail, 2 * D_MODEL), F32)
            ext_ref[s, tail:CONV_PAD, :] = cs_ref[s]

    parts = [_rmsnorm(x_ref[r:r + NORM_ROWS, :], g_ref[...]).astype(BF16) for r in range(0, tm, NORM_ROWS)]
    xn = jnp.concatenate(parts, axis=0)
    xn_ref[...] = xn

    zqk = jnp.concatenate([_dot(part, wqk_ref[...]) for part in parts], axis=0)
    ks = []
    row8 = lax.broadcasted_iota(jnp.int32, (CONV_PAD, 2 * D_MODEL), 0)
    for s in range(nseg):
        cur = zqk[s * sl:(s + 1) * sl, :]
        prev = ext_ref[s]
        acc = cb_ref[...] + cur * cw_ref[CONV_W - 1:CONV_W, :]
        for d in range(1, CONV_W):
            back = pltpu.roll(cur, d, axis=0)
            head = jnp.where(row8 < d, pltpu.roll(prev, d, axis=0), back[0:CONV_PAD, :])
            back = jnp.concatenate([head, back[CONV_PAD:, :]], axis=0)
            acc = acc + back * cw_ref[CONV_W - 1 - d:CONV_W - d, :]
        qk = acc * _sigmoid(acc)
        q_ref[s * sl:(s + 1) * sl, :] = (qk[:, :D_MODEL] * (HEAD_DIM ** -0.5)).astype(BF16)
        ks.append(qk[:, D_MODEL:])
        ext_ref[s] = cur[sl - CONV_PAD:sl, :]
        cn_ref[s] = ext_ref[s, tail:CONV_PAD, :]
    if tm % LANES:
        ks.append(jnp.zeros((LANES - tm % LANES, D_MODEL), F32))
    k = jnp.concatenate(ks, axis=0) if len(ks) > 1 else ks[0]
    kt = k.T.astype(BF16)
    nch = tm // lc
    per_seq = sl // lc
    for c in range(nch):
        kt_ref[c // per_seq, c % per_seq] = kt[:, c * lc:(c + 1) * lc]

    v_ref[...] = _dot(xn, wv_ref[...]).astype(BF16)
    o_ref[...] = _sigmoid(_dot(xn, wo_ref[...])).astype(BF16)

    zg = _dot(xn, wif_ref[...])
    if tm % LANES:
        zg = jnp.concatenate([zg, jnp.zeros((LANES - tm % LANES, LANES), F32)], axis=0)
    zt = zg.T[0:2 * HEADS, :]
    z = jnp.concatenate([zt[:, c * lc:(c + 1) * lc] + bif_ref[...] for c in range(nch)], axis=0)
    is_ig = (lax.broadcasted_iota(jnp.int32, z.shape, 0) % (2 * HEADS)) < HEADS
    g = jnp.where(is_ig, z, _log_sigmoid(z))
    upper = jnp.where(lax.broadcasted_iota(jnp.int32, (lc, lc), 0)
                      <= lax.broadcasted_iota(jnp.int32, (lc, lc), 1), 1.0, 0.0)
    bc = jnp.dot(g, upper, preferred_element_type=F32, precision=lax.Precision.HIGHEST)
    a = g - pltpu.roll(bc, nch * 2 * HEADS - HEADS, axis=0)
    amax = jnp.broadcast_to(jnp.max(a, axis=-1, keepdims=True), z.shape)
    gb = jnp.where(is_ig, g, bc)
    for c in range(nch):
        gt_ref[c // per_seq, c % per_seq, 0:2 * HEADS, :] = gb[c * 8:(c + 1) * 8, :]
        gt_ref[c // per_seq, c % per_seq, 2 * HEADS:4 * HEADS, :] = amax[c * 8:(c + 1) * 8, :]


def _tile_geometry(b, l, tile=TOKEN_TILE):
    sl = min(l, tile)
    nseg = max(1, min(b, TOKEN_TILE // sl))
    return nseg, sl


def _tok_spec(nseg, sl, nt, w):
    return pl.BlockSpec((nseg * sl, w), lambda bi, i: (bi * nt + i, 0))


def _inproj_a(x2d, b, l, g, wqk, wv, wo, wift, bif, cw, cb, cs):
    nseg, sl = _tile_geometry(b, l)
    nt = l // sl
    n = b * l
    CHUNK = min(MLSTM_BLOCK, l)
    per_seq = sl // CHUNK
    tok = functools.partial(_tok_spec, nseg, sl, nt)
    state = pl.BlockSpec((nseg, CONV_W - 1, 2 * D_MODEL), lambda bi, i: (bi, 0, 0))
    return pl.pallas_call(
        functools.partial(_inproj_a_kernel, nseg=nseg, sl=sl, lc=CHUNK),
        grid=(b // nseg, nt),
        in_specs=[tok(D_MODEL), _const_spec((1, D_MODEL)), _const_spec((D_MODEL, 2 * D_MODEL)),
                  _const_spec((D_MODEL, D_MODEL)), _const_spec((D_MODEL, D_MODEL)),
                  _const_spec((D_MODEL, LANES)), _const_spec((2 * HEADS, 1)),
                  _const_spec((CONV_W, 2 * D_MODEL)), _const_spec((1, 2 * D_MODEL)), state],
        out_specs=[tok(D_MODEL), tok(D_MODEL),
                   pl.BlockSpec((nseg, per_seq, D_MODEL, CHUNK), lambda bi, i: (bi, i, 0, 0)),
                   tok(D_MODEL), tok(D_MODEL),
                   pl.BlockSpec((nseg, per_seq, 4 * HEADS, CHUNK), lambda bi, i: (bi, i, 0, 0)),
                   state],
        out_shape=[jax.ShapeDtypeStruct((n, D_MODEL), BF16), jax.ShapeDtypeStruct((n, D_MODEL), BF16),
                   jax.ShapeDtypeStruct((b, l // CHUNK, D_MODEL, CHUNK), BF16),
                   jax.ShapeDtypeStruct((n, D_MODEL), BF16), jax.ShapeDtypeStruct((n, D_MODEL), BF16),
                   jax.ShapeDtypeStruct((b, l // CHUNK, 4 * HEADS, CHUNK), F32),
                   jax.ShapeDtypeStruct((b, CONV_W - 1, 2 * D_MODEL), F32)],
        scratch_shapes=[pltpu.VMEM((nseg, CONV_PAD, 2 * D_MODEL), F32)],
        compiler_params=_cparams(("parallel", "arbitrary")),
        name="inproj_a",
    )(x2d, g, wqk, wv, wo, wift, bif, cw, cb, cs)


def _inproj_b_kernel(x_ref, wu_ref, wgv_ref, wmq_ref, wgate_ref, lng_ref, lnb_ref, ws_ref, bst_ref,
                     mk_ref, mv_ref, ug_ref, att_ref, gates_ref, *rest, nseg, sl, cl, emit_gv):
    tm = nseg * sl
    xn = x_ref[...]

    gates_ref[...] = _sigmoid(_dot(xn, wgate_ref[...])).astype(BF16)

    gvr = jax.nn.gelu(_dot(xn, wgv_ref[...]))
    mu = jnp.mean(gvr, axis=-1, keepdims=True)
    xc = gvr - mu
    r = lax.rsqrt(jnp.mean(xc * xc, axis=-1, keepdims=True) + EPS)
    gv = (xc * r) * lng_ref[...] + lnb_ref[...]
    if emit_gv:
        rest[0][...] = gv
    gvb = gv.astype(BF16)
    u = jax.nn.gelu(_dot(xn, wu_ref[...]))
    tri = (lax.broadcasted_iota(jnp.int32, (cl, cl), 0) >= lax.broadcasted_iota(jnp.int32, (cl, cl), 1))
    for gi in range(GMLP_GROUPS):
        wsg = jnp.where(tri, ws_ref[gi], 0.0).astype(BF16)
        lo, hi = gi * GMLP_GROUP_DIM, (gi + 1) * GMLP_GROUP_DIM
        for c in range(tm // cl):
            sp = _dot(wsg, gvb[c * cl:(c + 1) * cl, lo:hi]) + bst_ref[:, gi:gi + 1]
            ug_ref[c * cl:(c + 1) * cl, lo:hi] = (u[c * cl:(c + 1) * cl, lo:hi] * sp).astype(BF16)

    mq = _dot(xn, wmq_ref[...]).astype(BF16)
    for s in range(nseg):
        r0, r1 = s * sl, (s + 1) * sl
        for h in range(HEADS):
            lo, hi = h * HEAD_DIM, (h + 1) * HEAD_DIM
            sc = _dot_nt(mq[r0:r1, lo:hi], mk_ref[s, :, lo:hi]) * (HEAD_DIM ** -0.5)
            e = jnp.exp(sc - jnp.max(sc, axis=-1, keepdims=True))
            a = (e / jnp.sum(e, axis=-1, keepdims=True)).astype(BF16)
            att_ref[r0:r1, lo:hi] = _dot(a, mv_ref[s, :, lo:hi]).astype(BF16)


def _inproj_b(xn2d, b, l, wu, wgv, wmq, wgate, lng, lnb, ws, bst, mk, mv, cl, emit_gv):
    nseg, sl = _tile_geometry(b, l)
    nt = l // sl
    n = b * l
    tok = functools.partial(_tok_spec, nseg, sl, nt)
    mem = pl.BlockSpec((nseg, N_MEM, D_MODEL), lambda bi, i: (bi, 0, 0))
    out_specs = [tok(D_MODEL), tok(D_MODEL), tok(3 * D_MODEL)]
    out_shape = [jax.ShapeDtypeStruct((n, D_MODEL), BF16), jax.ShapeDtypeStruct((n, D_MODEL), BF16),
                 jax.ShapeDtypeStruct((n, 3 * D_MODEL), BF16)]
    if emit_gv:
        out_specs.append(tok(D_MODEL))
        out_shape.append(jax.ShapeDtypeStruct((n, D_MODEL), F32))
    return pl.pallas_call(
        functools.partial(_inproj_b_kernel, nseg=nseg, sl=sl, cl=cl, emit_gv=emit_gv),
        grid=(b // nseg, nt),
        in_specs=[tok(D_MODEL), _const_spec((D_MODEL, D_MODEL)),
                  _const_spec((D_MODEL, D_MODEL)), _const_spec((D_MODEL, D_MODEL)),
                  _const_spec((D_MODEL, 3 * D_MODEL)), _const_spec((1, D_MODEL)), _const_spec((1, D_MODEL)),
                  _const_spec((GMLP_GROUPS, cl, cl)), _const_spec((cl, GMLP_GROUPS)), mem, mem],
        out_specs=out_specs,
        out_shape=out_shape,
        compiler_params=_cparams(("parallel", "parallel")),
        name="inproj_b",
    )(xn2d, wu, wgv, wmq, wgate, lng, lnb, ws, bst, mk, mv)


def _mlstm_kernel(q_ref, kt_ref, v_ref, gt_ref, *rest, nseg, cb, zero_state):
    i = pl.program_id(1)

    if zero_state:
        hm_ref, c_ref, n_ref, m_ref, st_ref = rest
    else:
        c0_ref, n0_ref, m0_ref, hm_ref, c_ref, n_ref, m_ref, st_ref = rest

    @pl.when(i == 0)
    def _():
        if zero_state:
            st_ref[...] = jnp.zeros(st_ref.shape, F32)
            m_ref[...] = jnp.zeros(m_ref.shape, F32)
        else:
            st_ref[:, :, :, :HEAD_DIM] = c0_ref[...]
            st_ref[:, :, :, HEAD_DIM:] = n0_ref[...]
            m_ref[...] = m0_ref[...]

    for s in range(nseg):
        _mlstm_sequence(q_ref, kt_ref.at[s], v_ref, gt_ref.at[s], st_ref.at[s], m_ref.at[s], hm_ref, s * cb, cb)

    @pl.when(i == pl.num_programs(1) - 1)
    def _():
        c_ref[...] = st_ref[:, :, :, :HEAD_DIM]
        n_ref[...] = st_ref[:, :, :, HEAD_DIM:]


def _mlstm_sequence(q_ref, kt_ref, v_ref, gt_ref, c_ref, m_ref, hm_ref, row0, cb):
    L = kt_ref.shape[-1]
    nch = cb // L
    ti = lax.broadcasted_iota(jnp.int32, (L, L), 0)
    si = lax.broadcasted_iota(jnp.int32, (L, L), 1)
    tri = ti >= si
    eye = ti == si

    rows = 4 * HEADS
    g_all = gt_ref[...].reshape(nch * rows, L)

    m_in = [m_ref[:, 0:1]]
    for c in range(nch):
        b_last4 = g_all[c * rows + HEADS:c * rows + 2 * HEADS, L - 1:L]
        amax4 = g_all[c * rows + 2 * HEADS:c * rows + 3 * HEADS, 0:1]
        m_in.append(jnp.maximum(b_last4 + m_in[-1], b_last4 + amax4))

    ones = jnp.ones((L, LANES), BF16)
    st = [c_ref[h] for h in range(HEADS)]
    for c in range(nch):
        r0, r1 = row0 + c * L, row0 + (c + 1) * L
        for h in range(HEADS):
            lo, hi = h * HEAD_DIM, (h + 1) * HEAD_DIM
            ig_r = g_all[c * rows + h:c * rows + h + 1, :]
            bc_r = g_all[c * rows + HEADS + h:c * rows + HEADS + h + 1, :]
            a_r = ig_r - bc_r
            bc_c = jnp.sum(jnp.where(eye, bc_r, 0.0), axis=-1, keepdims=True)
            m0 = m_in[c][h:h + 1, :]
            m_last = m_in[c + 1][h:h + 1, :]
            dmat = jnp.where(tri, bc_c + a_r, NEG_INF)
            inter = bc_c + m0
            m = jnp.maximum(inter, jnp.max(dmat, axis=-1, keepdims=True))
            w_intra = jnp.exp(dmat - m)
            w_inter = jnp.exp(inter - m)
            q = q_ref[r0:r1, lo:hi]
            kt = kt_ref[c, lo:hi, :]
            v = v_ref[r0:r1, lo:hi]
            s = _dot(q, kt) * w_intra
            qs = _dot(q, st[h].astype(BF16))
            num = w_inter * qs[:, :HEAD_DIM] + _dot(s.astype(BF16), v)
            den = w_inter * qs[:, HEAD_DIM:HEAD_DIM + 1] + jnp.sum(s, axis=-1, keepdims=True)
            hh = num / jnp.maximum(jnp.abs(den), jnp.exp(-m))
            bc_last = bc_r[:, L - 1:L]
            w_last = jnp.exp(bc_last + a_r - m_last)
            decay = jnp.exp(bc_last + m0 - m_last)
            ktw = (kt.astype(F32) * w_last).astype(BF16)
            st[h] = decay * st[h] + _dot(ktw, jnp.concatenate([v, ones], axis=1))
            hm_ref[r0:r1, lo:hi] = hh.astype(BF16)

    for h in range(HEADS):
        c_ref[h] = st[h]
    m_ref[...] = jnp.broadcast_to(m_in[nch], (HEADS, LANES))


def _mlstm(q, kt, v, gt, state, b, l):
    nseg, cb = _tile_geometry(b, l, MLSTM_TILE)
    nt = l // cb
    CHUNK = kt.shape[-1]
    tok = _tok_spec(nseg, cb, nt, D_MODEL)
    cs = pl.BlockSpec((nseg, HEADS, HEAD_DIM, HEAD_DIM), lambda bi, i: (bi, 0, 0, 0))
    ns = pl.BlockSpec((nseg, HEADS, HEAD_DIM, LANES), lambda bi, i: (bi, 0, 0, 0))
    ms = pl.BlockSpec((nseg, HEADS, LANES), lambda bi, i: (bi, 0, 0))
    state_specs = [] if state is None else [cs, ns, ms]
    return pl.pallas_call(
        functools.partial(_mlstm_kernel, nseg=nseg, cb=cb, zero_state=state is None),
        grid=(b // nseg, nt),
        in_specs=[tok, pl.BlockSpec((nseg, cb // CHUNK, D_MODEL, CHUNK), lambda bi, i: (bi, i, 0, 0)), tok,
                  pl.BlockSpec((nseg, cb // CHUNK, 4 * HEADS, CHUNK), lambda bi, i: (bi, i, 0, 0))] + state_specs,
        out_specs=[tok, cs, ns, ms],
        out_shape=[jax.ShapeDtypeStruct((b * l, D_MODEL), BF16),
                   jax.ShapeDtypeStruct((b, HEADS, HEAD_DIM, HEAD_DIM), F32),
                   jax.ShapeDtypeStruct((b, HEADS, HEAD_DIM, LANES), F32),
                   jax.ShapeDtypeStruct((b, HEADS, LANES), F32)],
        scratch_shapes=[pltpu.VMEM((nseg, HEADS, HEAD_DIM, HEAD_DIM + LANES), F32)],
        compiler_params=_cparams(("parallel", "arbitrary")),
        name="mlstm",
    )(q, kt, v, gt, *(state or ()))


def _merge_kernel(x_ref, hh_ref, o_ref, ng_ref, ug_ref, att_ref, gates_ref, wa_ref, wb_ref, wc_ref, wo_ref, h_ref):
    parts = []
    for h in range(HEADS):
        lo, hi = h * HEAD_DIM, (h + 1) * HEAD_DIM
        hh = hh_ref[:, lo:hi].astype(F32)
        hn = hh * lax.rsqrt(jnp.mean(hh * hh, axis=-1, keepdims=True) + EPS)
        parts.append(((hn * ng_ref[:, lo:hi]) * o_ref[:, lo:hi].astype(F32)).astype(BF16))
    br = _dot(parts[0], wa_ref[0:HEAD_DIM, :])
    for h in range(1, HEADS):
        br = br + _dot(parts[h], wa_ref[h * HEAD_DIM:(h + 1) * HEAD_DIM, :])
    g = gates_ref[...].astype(F32)
    mixed = g[:, :D_MODEL] * br
    mixed = mixed + g[:, D_MODEL:2 * D_MODEL] * _dot(ug_ref[...], wb_ref[...])
    mixed = mixed + g[:, 2 * D_MODEL:] * _dot(att_ref[...], wc_ref[...])
    h_ref[...] = x_ref[...] + _dot(mixed.astype(BF16), wo_ref[...])


def _merge(x2d, hh, osig, ng, ug, att, gates, wa, wb, wc, wo, tm):
    n = x2d.shape[0]
    row = lambda w: pl.BlockSpec((tm, w), lambda i: (i, 0))
    wspec = _const_spec((D_MODEL, D_MODEL))
    return pl.pallas_call(
        _merge_kernel,
        grid=(n // tm,),
        in_specs=[row(D_MODEL), row(D_MODEL), row(D_MODEL), _const_spec((1, D_MODEL)), row(D_MODEL), row(D_MODEL),
                  row(3 * D_MODEL), wspec, wspec, wspec, wspec],
        out_specs=row(D_MODEL),
        out_shape=jax.ShapeDtypeStruct((n, D_MODEL), F32),
        compiler_params=_cparams(("parallel",)),
        name="merge",
    )(x2d, hh, osig, ng, ug, att, gates, wa, wb, wc, wo)


MOE_T = 256
MOE_STEP_SUBS = 2
MOE_CAP = 32
MOE_S = N_EXPERTS * MOE_CAP
MOE_OVF = 512
SEG_ALIGN = 16
EXPERT_BLOCK = 128


def _moe_route_kernel(hp_ref, hs_ref, g_ref, wrg_ref, brg_ref, wre_ref, bre_ref,
                      main_ref, ovf_ref, info_ref, meta_ref, *, nsp):
    n = MOE_STEP_SUBS * MOE_T
    h = jnp.where(pl.program_id(0) < nsp // MOE_STEP_SUBS, hp_ref[...], hs_ref[...])
    xm = _rmsnorm(h, g_ref[...]).astype(BF16)
    lane = lax.broadcasted_iota(jnp.int32, (n, LANES), 1).astype(F32)
    lg = jnp.where(lane < N_GROUPS, _dot(xm, wrg_ref[...]) + brg_ref[...], NEG_INF)
    gmax = jnp.max(lg, axis=-1, keepdims=True)
    p_top = 1.0 / jnp.sum(jnp.exp(lg - gmax), axis=-1, keepdims=True)
    grp = jnp.min(jnp.where(lg == gmax, lane, float(LANES)), axis=-1, keepdims=True)
    el = _dot(xm, wre_ref[...]) + bre_ref[...]
    in_grp = (lane >= grp * EXPERTS_PER_GROUP) & (lane < (grp + 1.0) * EXPERTS_PER_GROUP)
    vals = jnp.where(in_grp, el, NEG_INF)
    v1 = jnp.max(vals, axis=-1, keepdims=True)
    i1 = jnp.min(jnp.where(vals == v1, lane, float(LANES)), axis=-1, keepdims=True)
    vals2 = jnp.where(lane == i1, NEG_INF, vals)
    v2 = jnp.max(vals2, axis=-1, keepdims=True)
    i2 = jnp.min(jnp.where(vals2 == v2, lane, float(LANES)), axis=-1, keepdims=True)
    r = jnp.exp(v2 - v1)
    p1 = p_top / (1.0 + r)
    p2 = p_top * r / (1.0 + r)
    sel1 = lane == i1
    sel2 = lane == i2
    onehot = jnp.where(sel1 | sel2, 1.0, 0.0)

    deferred = []
    for s in range(MOE_STEP_SUBS):
        rows = slice(s * MOE_T, (s + 1) * MOE_T)
        deferred.append(_route_sub_tile(
            xm[rows, :], onehot[rows, :], i1[rows, :], i2[rows, :],
            p1[rows, :], p2[rows, :], main_ref.at[s], ovf_ref.at[s], info_ref.at[rows, :], meta_ref.at[s]))
    for write_overflow in deferred:
        write_overflow()


def _route_sub_tile(xm, onehot, i1, i2, p1, p2, main_ref, ovf_ref, info_ref, meta_ref):
    t = MOE_T
    lane = lax.broadcasted_iota(jnp.int32, (t, LANES), 1)
    sel1 = lane.astype(F32) == i1
    sel2 = lane.astype(F32) == i2
    cnt = jnp.sum(onehot, axis=0, keepdims=True).astype(jnp.int32)
    pn = jnp.bitwise_and(jnp.maximum(cnt - MOE_CAP, 0) + (SEG_ALIGN - 1), -SEG_ALIGN)
    pn8 = jnp.broadcast_to(pn, (8, LANES))
    earlier = jnp.where(lax.broadcasted_iota(jnp.int32, (LANES, LANES), 0)
                        < lax.broadcasted_iota(jnp.int32, (LANES, LANES), 1), 1.0, 0.0).astype(BF16)
    off_f8 = _dot(pn8.astype(F32).astype(BF16), earlier)
    row8 = lax.broadcasted_iota(jnp.int32, (8, LANES), 0)
    meta_ref[...] = jnp.where(row8 == 0, off_f8.astype(jnp.int32), jnp.where(row8 == 1, pn8, 0))

    ti = lax.broadcasted_iota(jnp.int32, (t, t), 0)
    si = lax.broadcasted_iota(jnp.int32, (t, t), 1)
    before = jnp.where(ti > si, 1.0, 0.0).astype(BF16)
    rank = _dot(before, onehot.astype(BF16))
    off_f = off_f8[0:1, :]

    def slot_row(sel, idx):
        rk = jnp.sum(jnp.where(sel, rank, 0.0), axis=-1, keepdims=True)
        of = jnp.sum(jnp.where(sel, off_f, 0.0), axis=-1, keepdims=True)
        return jnp.where(rk < MOE_CAP, idx * MOE_CAP + rk, MOE_S - MOE_CAP + of + rk)

    pos1 = slot_row(sel1, i1)
    pos2 = slot_row(sel2, i2)
    info_ref[...] = (jnp.where(lane == 0, pos1, 0.0) + jnp.where(lane == 1, pos2, 0.0)
                     + jnp.where(lane == 2, p1, 0.0) + jnp.where(lane == 3, p2, 0.0))

    eye = ti == si
    pos1_r = jnp.sum(jnp.where(eye, pos1, 0.0), axis=0, keepdims=True)
    pos2_r = jnp.sum(jnp.where(eye, pos2, 0.0), axis=0, keepdims=True)

    def gather(rows, base):
        srow = (lax.broadcasted_iota(jnp.int32, (rows, t), 0) + base).astype(F32)
        pick = jnp.where((srow == pos1_r) | (srow == pos2_r), 1.0, 0.0).astype(BF16)
        return _dot(pick, xm).astype(BF16)

    main_ref[...] = gather(MOE_S, 0)
    has_ovf = jnp.sum(pn) > 0

    def write_overflow():
        @pl.when(has_ovf)
        def _():
            ovf_ref[...] = gather(MOE_OVF, MOE_S)

        @pl.when(jnp.logical_not(has_ovf))
        def _():
            ovf_ref[...] = jnp.zeros(ovf_ref.shape, ovf_ref.dtype)
    return write_overflow


def _two_group_specs(nsp, cols):
    rows = MOE_STEP_SUBS * MOE_T
    npb = nsp // MOE_STEP_SUBS
    return (pl.BlockSpec((rows, cols), lambda j, *_: (jnp.minimum(j, npb - 1), 0)),
            pl.BlockSpec((rows, cols), lambda j, *_: (jnp.maximum(j - npb, 0), 0)))


def _moe_route(hp, hs, g, wrg, brg, wre, bre):
    nsp = hp.shape[0] // MOE_T
    ns = nsp + hs.shape[0] // MOE_T
    sub = MOE_STEP_SUBS
    return pl.pallas_call(
        functools.partial(_moe_route_kernel, nsp=nsp),
        grid=(ns // sub,),
        in_specs=[*_two_group_specs(nsp, D_MODEL), _const_spec((1, D_MODEL)),
                  _const_spec((D_MODEL, LANES)), _const_spec((1, LANES)),
                  _const_spec((D_MODEL, LANES)), _const_spec((1, LANES))],
        out_specs=[pl.BlockSpec((sub, MOE_S, D_MODEL), lambda j: (j, 0, 0)),
                   pl.BlockSpec((sub, MOE_OVF, D_MODEL), lambda j: (j, 0, 0)),
                   pl.BlockSpec((sub * MOE_T, LANES), lambda j: (j, 0)),
                   pl.BlockSpec((sub, 8, LANES), lambda j: (j, 0, 0))],
        out_shape=[jax.ShapeDtypeStruct((ns, MOE_S, D_MODEL), BF16),
                   jax.ShapeDtypeStruct((ns, MOE_OVF, D_MODEL), BF16),
                   jax.ShapeDtypeStruct((ns * MOE_T, LANES), F32),
                   jax.ShapeDtypeStruct((ns, 8, LANES), jnp.int32)],
        compiler_params=_cparams(("arbitrary",)),
        name="moe_route",
    )(hp, hs, g, wrg, brg, wre, bre)


def _swiglu(x, wgu, wdb):
    gu = _dot(x, wgu[...])
    gate = gu[:, :D_EXPERT]
    hid = (gate * _sigmoid(gate)) * gu[:, D_EXPERT:]
    return _dot(hid.astype(BF16), wdb[...]).astype(BF16)


def _moe_expert_kernel(offs_ref, pns_ref, eflag_ref, main_ref, slots_hbm, wg_ref, wu_ref, wd_ref,
                       mout_ref, out_hbm, xbuf, obuf, wgu, wdb, sem_in, sem_out, *, n_sub, jb, n_rb):
    e = pl.program_id(0)
    rb = pl.program_id(1)

    @pl.when((e == 0) & (rb == 0))
    def _():
        xbuf[...] = jnp.zeros_like(xbuf)

    @pl.when(rb == 0)
    def _():
        wgu[:, :D_EXPERT] = wg_ref[...].astype(BF16)
        wgu[:, D_EXPERT:] = wu_ref[...].astype(BF16)
        wdb[...] = wd_ref[...].astype(BF16)

    sb = main_ref.shape[0]
    part = sb // 2 if sb % 2 == 0 else sb
    for s0 in range(0, sb, part):
        y = _swiglu(main_ref[s0:s0 + part].reshape(part * MOE_CAP, D_MODEL), wgu, wdb)
        mout_ref[s0:s0 + part] = y.reshape(part, MOE_CAP, D_MODEL)

    def copy_in(j, src, dst):
        return pltpu.make_async_copy(slots_hbm.at[j, pl.ds(src, SEG_ALIGN), :],
                                     xbuf.at[pl.ds(dst, SEG_ALIGN), :], sem_in)

    def copy_out(j, src, dst):
        return pltpu.make_async_copy(obuf.at[pl.ds(src, SEG_ALIGN), :],
                                     out_hbm.at[j, pl.ds(dst, SEG_ALIGN), :], sem_out)

    def for_each_chunk(g, fn):
        def seg(jj, cur):
            j = g * jb + jj
            n = pns_ref[j * N_EXPERTS + e]
            off = offs_ref[j * N_EXPERTS + e]

            def chunk(k, c):
                fn(j, pl.multiple_of(off + k * SEG_ALIGN, SEG_ALIGN), pl.multiple_of(cur + k * SEG_ALIGN, SEG_ALIGN))
                return c
            lax.fori_loop(0, n // SEG_ALIGN, chunk, 0)
            return cur + n
        return lax.fori_loop(0, jb, seg, 0)

    def group(g, carry):
        total = for_each_chunk(g, lambda j, r, b: copy_in(j, r, b).start())
        nchunk = total // SEG_ALIGN

        def wait_in(k, c):
            copy_in(0, 0, 0).wait()
            return c
        lax.fori_loop(0, nchunk, wait_in, 0)

        def block(bi, c):
            r0 = pl.multiple_of(bi * EXPERT_BLOCK, EXPERT_BLOCK)
            obuf[pl.ds(r0, EXPERT_BLOCK), :] = _swiglu(xbuf[pl.ds(r0, EXPERT_BLOCK), :], wgu, wdb)
            return c
        lax.fori_loop(0, (total + EXPERT_BLOCK - 1) // EXPERT_BLOCK, block, 0)

        for_each_chunk(g, lambda j, r, b: copy_out(j, b, r).start())

        def wait_out(k, c):
            copy_out(0, 0, 0).wait()
            return c
        lax.fori_loop(0, nchunk, wait_out, 0)
        return carry

    @pl.when((rb == n_rb - 1) & (eflag_ref[e] > 0))
    def _():
        lax.fori_loop(0, n_sub // jb, group, 0)


def _moe_expert(main, ovf, offs, pns, eflag, wg, wu, wd):
    ns = main.shape[0]
    n_rb = 1
    sb = ns // n_rb
    jb = max(d for d in range(1, 12) if ns % d == 0)
    rows = jb * MOE_T + EXPERT_BLOCK
    main4 = main.reshape(ns, N_EXPERTS, MOE_CAP, D_MODEL)
    mspec = pl.BlockSpec((sb, None, MOE_CAP, D_MODEL), lambda e, rb, *_: (rb, e, 0, 0))
    wspec = lambda a, b: pl.BlockSpec((None, a, b), lambda e, rb, *_: (e, 0, 0))
    grid_spec = pltpu.PrefetchScalarGridSpec(
        num_scalar_prefetch=3,
        grid=(N_EXPERTS, n_rb),
        in_specs=[mspec, pl.BlockSpec(memory_space=pl.ANY),
                  wspec(D_MODEL, D_EXPERT), wspec(D_MODEL, D_EXPERT), wspec(D_EXPERT, D_MODEL)],
        out_specs=[mspec, pl.BlockSpec(memory_space=pl.ANY)],
        scratch_shapes=[pltpu.VMEM((rows, D_MODEL), BF16), pltpu.VMEM((rows, D_MODEL), BF16),
                        pltpu.VMEM((D_MODEL, 2 * D_EXPERT), BF16), pltpu.VMEM((D_EXPERT, D_MODEL), BF16),
                        pltpu.SemaphoreType.DMA(()), pltpu.SemaphoreType.DMA(())],
    )
    mout, oout = pl.pallas_call(
        functools.partial(_moe_expert_kernel, n_sub=ns, jb=jb, n_rb=n_rb),
        grid_spec=grid_spec,
        out_shape=[jax.ShapeDtypeStruct(main4.shape, main4.dtype), jax.ShapeDtypeStruct(ovf.shape, ovf.dtype)],
        input_output_aliases={4: 1},
        compiler_params=_cparams(("arbitrary", "arbitrary")),
        name="moe_expert",
    )(offs, pns, eflag, main4, ovf, wg, wu, wd)
    return mout.reshape(main.shape), oout


def _moe_combine_kernel(jflag_ref, sflag_ref, hp_ref, hs_ref, main_ref, ovf_ref, info_ref, gf_ref,
                        yp_ref, ys_ref, acc_ref, *, nsp):
    j = pl.program_id(0)
    npb = nsp // MOE_STEP_SUBS

    def weights(s, cols, base):
        info = info_ref[s * MOE_T:(s + 1) * MOE_T, :]
        scol = (lax.broadcasted_iota(jnp.int32, (MOE_T, cols), 1) + base).astype(F32)
        return (jnp.where(scol == info[:, 0:1], info[:, 2:3], 0.0)
                + jnp.where(scol == info[:, 1:2], info[:, 3:4], 0.0)).astype(BF16)

    for s in range(MOE_STEP_SUBS):
        rows = slice(s * MOE_T, (s + 1) * MOE_T)
        h = jnp.where(j < npb, hp_ref[rows, :], hs_ref[rows, :])
        acc_ref[rows, :] = h + _dot(weights(s, MOE_S, 0), main_ref[s])

    for s in range(MOE_STEP_SUBS):
        @pl.when(jflag_ref[j * MOE_STEP_SUBS + s] > 0)
        def _(s=s):
            rows = slice(s * MOE_T, (s + 1) * MOE_T)
            acc_ref[rows, :] += _dot(weights(s, MOE_OVF, MOE_S), ovf_ref[s])

    y = _rmsnorm(acc_ref[...], gf_ref[...])

    @pl.when(j < npb)
    def _():
        yp_ref[...] = y

    @pl.when(j >= npb)
    def _():
        ys_ref[...] = y


def _moe_combine(hp, hs, main, ovf, info, jflag, gf):
    nsp = hp.shape[0] // MOE_T
    ns = main.shape[0]
    sub = MOE_STEP_SUBS
    sflag = jnp.max(jflag.reshape(ns // sub, sub), axis=1)
    grid_spec = pltpu.PrefetchScalarGridSpec(
        num_scalar_prefetch=2,
        grid=(ns // sub,),
        in_specs=[*_two_group_specs(nsp, D_MODEL),
                  pl.BlockSpec((sub, MOE_S, D_MODEL), lambda j, jf, sf: (j, 0, 0)),
                  pl.BlockSpec((sub, MOE_OVF, D_MODEL), lambda j, jf, sf: (jnp.where(sf[j] > 0, j, 0), 0, 0)),
                  pl.BlockSpec((sub * MOE_T, LANES), lambda j, jf, sf: (j, 0)),
                  pl.BlockSpec((1, D_MODEL), lambda j, jf, sf: (0, 0))],
        out_specs=list(_two_group_specs(nsp, D_MODEL)),
        scratch_shapes=[pltpu.VMEM((sub * MOE_T, D_MODEL), F32)],
    )
    return pl.pallas_call(
        functools.partial(_moe_combine_kernel, nsp=nsp),
        grid_spec=grid_spec,
        out_shape=[jax.ShapeDtypeStruct(hp.shape, F32), jax.ShapeDtypeStruct(hs.shape, F32)],
        compiler_params=_cparams(("arbitrary",)),
        name="moe_combine",
    )(jflag, sflag, hp, hs, main, ovf, info, gf)


def _moe(hp, hs, p):
    main, ovf, info, meta = _moe_route(hp, hs, p["norm_moe_g"], p["w_rg"], p["b_rg"], p["w_re"], p["b_re"])
    pn = meta[:, 1, :N_EXPERTS]
    offs = meta[:, 0, :N_EXPERTS].reshape(-1)
    eflag = (jnp.sum(pn, axis=0) > 0).astype(jnp.int32)
    jflag = (jnp.sum(pn, axis=1) > 0).astype(jnp.int32)
    main, ovf = _moe_expert(main, ovf, offs, pn.reshape(-1), eflag, p["w_eg"], p["w_eu"], p["w_ed"])
    return _moe_combine(hp, hs, main, ovf, info, jflag, p["norm_final_g"])


def _trunk(x, mem_k, mem_v, c0, n0, m0, conv0, p, *, emit_gv):
    b, l, _ = x.shape
    n = b * l
    cl = min(l, GMLP_CHUNK)
    x2d = x.reshape(n, D_MODEL)
    xn, q, kt, v, osig, gt, conv_new = _inproj_a(
        x2d, b, l, p["norm_mix_g"], p["w_qk"], p["w_v"], p["w_o"], p["w_if"], p["b_if"],
        p["conv_w"], p["conv_b"], conv0)
    outs = _inproj_b(xn, b, l, p["w_u"], p["w_gv"], p["w_mq"], p["w_gate"],
                     p["gmlp_norm_g"], p["gmlp_norm_b"], p["w_s"][:, :cl, :cl], p["b_st"][:cl],
                     mem_k, mem_v, cl, emit_gv)
    ug, att, gates = outs[:3]
    on_lanes = lambda a: jnp.broadcast_to(a[..., None], a.shape + (LANES,))
    state = None if c0 is None else (c0, on_lanes(n0), on_lanes(m0))
    hh, c1, n1, m1 = _mlstm(q, kt, v, gt, state, b, l)
    n1 = n1[..., 0]
    h = _merge(x2d, hh, osig, p["mlstm_norm_g"], ug, att, gates,
               p["w_br_mlstm"], p["w_br_gmlp"], p["w_br_mem"], p["w_out"], TOKEN_TILE)
    return h, c1, n1, m1[..., 0], conv_new, (outs[3].reshape(b, l, D_MODEL) if emit_gv else None)


def kernel(x_prompt, x_sample, mem_prompt, cache_mem_k, cache_mem_v, state_mlstm_C, state_mlstm_n, state_mlstm_m, state_mlstm_conv, norm_mix_g, w_in, mlstm_i_b, mlstm_f_b, mlstm_conv_w, mlstm_conv_b, mlstm_norm_g, gmlp_norm_g, gmlp_norm_b, gmlp_w_s, gmlp_b_s, mem_norm_g, w_mem_k, w_mem_v, w_br_mlstm, w_br_gmlp, w_br_mem, w_out, norm_moe_g, w_router_group, b_router_group, w_router_expert, b_router_expert, w_exp_gate, w_exp_up, w_exp_down, norm_final_g):
    bp = x_prompt.shape[0]
    bs = x_sample.shape[0]
    W = D_MODEL
    wi = w_in[0]
    o_qk, o_v, o_o, o_i = 0, 2 * W, 3 * W, 4 * W
    o_f = o_i + HEADS
    o_u = o_f + HEADS
    o_gv, o_mq, o_gate = o_u + W, o_u + 2 * W, o_u + 3 * W
    row = lambda a: a.reshape(1, -1)
    pad_l = lambda a: jnp.pad(a, ((0, 0), (0, LANES - a.shape[1])))
    p = {
        "norm_mix_g": row(norm_mix_g[0]),
        "w_qk": wi[:, o_qk:o_v].astype(BF16), "w_v": wi[:, o_v:o_o].astype(BF16),
        "w_o": wi[:, o_o:o_i].astype(BF16),
        "w_if": pad_l(wi[:, o_i:o_u]).astype(BF16),
        "b_if": jnp.concatenate([mlstm_i_b[0], mlstm_f_b[0]]).reshape(2 * HEADS, 1),
        "conv_w": mlstm_conv_w[0], "conv_b": row(mlstm_conv_b[0]),
        "w_u": wi[:, o_u:o_gv].astype(BF16), "w_gv": wi[:, o_gv:o_mq].astype(BF16),
        "w_mq": wi[:, o_mq:o_gate].astype(BF16), "w_gate": wi[:, o_gate:].astype(BF16),
        "gmlp_norm_g": row(gmlp_norm_g[0]), "gmlp_norm_b": row(gmlp_norm_b[0]),
        "w_s": gmlp_w_s[0], "b_st": gmlp_b_s[0].T,
        "mlstm_norm_g": row(mlstm_norm_g[0]),
        "w_br_mlstm": w_br_mlstm[0].astype(BF16), "w_br_gmlp": w_br_gmlp[0].astype(BF16),
        "w_br_mem": w_br_mem[0].astype(BF16), "w_out": w_out[0].astype(BF16),
        "norm_moe_g": row(norm_moe_g[0]),
        "w_rg": pad_l(w_router_group[0]).astype(BF16), "b_rg": pad_l(row(b_router_group[0])),
        "w_re": pad_l(w_router_expert[0]).astype(BF16), "b_re": pad_l(row(b_router_expert[0])),
        "w_eg": w_exp_gate[0], "w_eu": w_exp_up[0], "w_ed": w_exp_down[0],
        "norm_final_g": row(norm_final_g),
    }

    mk_p, mv_p, mk_pb, mv_pb = _memory_kv(mem_prompt.reshape(bp * N_MEM, W), row(mem_norm_g[0]),
                                          w_mem_k[0], w_mem_v[0])

    zeros = lambda *s: jnp.zeros(s, F32)
    hp, cp, np_, mp, cvp, _ = _trunk(
        x_prompt, mk_pb.reshape(bp, N_MEM, W), mv_pb.reshape(bp, N_MEM, W),
        None, None, None, zeros(bp, CONV_W - 1, 2 * W), p, emit_gv=False)
    hs, cs, ns, ms, cvs, gvs = _trunk(
        x_sample, cache_mem_k[0].reshape(bs, N_MEM, W).astype(BF16),
        cache_mem_v[0].reshape(bs, N_MEM, W).astype(BF16),
        state_mlstm_C[0], state_mlstm_n[0], state_mlstm_m[0], state_mlstm_conv[0], p,
        emit_gv=True)

    yp, ys = _moe(hp, hs, p)
    kv_shape = (1, bp, N_MEM, HEADS, HEAD_DIM)
    return (yp.reshape(x_prompt.shape), ys.reshape(x_sample.shape), mk_p.reshape(kv_shape), mv_p.reshape(kv_shape),
            cp[None], np_[None], mp[None], cvp[None],
            cs[None], ns[None], ms[None], cvs[None], gvs[None])
```

```python
import functools

import jax
import jax.numpy as jnp
from jax import lax
from jax.experimental import pallas as pl
from jax.experimental.pallas import tpu as pltpu

D_MODEL = 1024
MLSTM_BLOCK = 512
TOKEN_TILE = 512
MLSTM_TILE = 2048
EPS = 1e-6
HEADS = 4
HEAD_DIM = 256
CONV_W = 4
GMLP_GROUPS = 4
GMLP_GROUP_DIM = 256
GMLP_CHUNK = 128
N_MEM = 256
N_GROUPS = 4
EXPERTS_PER_GROUP = 8
N_EXPERTS = 32
D_EXPERT = 256
LANES = 128
CONV_PAD = 8
NORM_ROWS = 128

F32 = jnp.float32
BF16 = jnp.bfloat16
NEG_INF = float("-inf")

VMEM_LIMIT = 56 * 1024 * 1024


def _cparams(sem):
    return pltpu.CompilerParams(dimension_semantics=sem, vmem_limit_bytes=VMEM_LIMIT)


def _const_spec(shape):
    nd = len(shape)
    return pl.BlockSpec(shape, lambda *_: (0,) * nd, pipeline_mode=pl.Buffered(1))


def _sigmoid(x):
    return 0.5 * (jnp.tanh(0.5 * x) + 1.0)


def _log_sigmoid(x):
    return jnp.minimum(x, 0.0) - jnp.log(1.0 + jnp.exp(-jnp.abs(x)))


def _rmsnorm(x, g):
    r = lax.rsqrt(jnp.mean(x * x, axis=-1, keepdims=True) + EPS)
    return (x * r) * g


def _dot(a, b):
    return jnp.dot(a, b, preferred_element_type=F32)


def _dot_nt(a, b):
    return lax.dot_general(a, b, (((1,), (1,)), ((), ())), preferred_element_type=F32)


def _dot_tn(a, b):
    return lax.dot_general(a, b, (((0,), (0,)), ((), ())), preferred_element_type=F32)


def _memkv_kernel(mem_ref, g_ref, wk_ref, wv_ref, k_ref, v_ref, kb_ref, vb_ref):
    mn = _rmsnorm(mem_ref[...], g_ref[...]).astype(BF16)
    k = _dot(mn, wk_ref[...].astype(BF16))
    v = _dot(mn, wv_ref[...].astype(BF16))
    k_ref[...] = k
    v_ref[...] = v
    kb_ref[...] = k.astype(BF16)
    vb_ref[...] = v.astype(BF16)


def _memory_kv(mem2d, g, wk, wv):
    n = mem2d.shape[0]
    tm = 512
    row = pl.BlockSpec((tm, D_MODEL), lambda i: (i, 0))
    return pl.pallas_call(
        _memkv_kernel,
        grid=(n // tm,),
        in_specs=[row, _const_spec((1, D_MODEL)), _const_spec((D_MODEL, D_MODEL)),
                  _const_spec((D_MODEL, D_MODEL))],
        out_specs=[row, row, row, row],
        out_shape=[jax.ShapeDtypeStruct((n, D_MODEL), F32)] * 2 + [jax.ShapeDtypeStruct((n, D_MODEL), BF16)] * 2,
        compiler_params=_cparams(("parallel",)),
        name="memory_kv",
    )(mem2d, g, wk, wv)


def _inproj_a_kernel(x_ref, g_ref, wqk_ref, wv_ref, wo_ref, wif_ref, bif_ref, cw_ref, cb_ref, cs_ref,
                     xn_ref, q_ref, kt_ref, v_ref, o_ref, gt_ref, cn_ref, ext_ref, *, nseg, sl, lc):
    i = pl.program_id(1)
    tm = nseg * sl
    tail = CONV_PAD - (CONV_W - 1)

    @pl.when(i == 0)
    def _():
        for s in range(nseg):
            ext_ref[s, 0:tail, :] = jnp.zeros((tail, 2 * D_MODEL), F32)
            ext_ref[s, tail:CONV_PAD, :] = cs_ref[s]

    parts = [_rmsnorm(x_ref[r:r + NORM_ROWS, :], g_ref[...]).astype(BF16) for r in range(0, tm, NORM_ROWS)]
    xn = jnp.concatenate(parts, axis=0)
    xn_ref[...] = xn

    zqk = jnp.concatenate([_dot(part, wqk_ref[...]) for part in parts], axis=0)
    ks = []
    row8 = lax.broadcasted_iota(jnp.int32, (CONV_PAD, 2 * D_MODEL), 0)
    for s in range(nseg):
        cur = zqk[s * sl:(s + 1) * sl, :]
        prev = ext_ref[s]
        acc = cb_ref[...] + cur * cw_ref[CONV_W - 1:CONV_W, :]
        for d in range(1, CONV_W):
            back = pltpu.roll(cur, d, axis=0)
            head = jnp.where(row8 < d, pltpu.roll(prev, d, axis=0), back[0:CONV_PAD, :])
            back = jnp.concatenate([head, back[CONV_PAD:, :]], axis=0)
            acc = acc + back * cw_ref[CONV_W - 1 - d:CONV_W - d, :]
        qk = acc * _sigmoid(acc)
        q_ref[s * sl:(s + 1) * sl, :] = (qk[:, :D_MODEL] * (HEAD_DIM ** -0.5)).astype(BF16)
        ks.append(qk[:, D_MODEL:])
        ext_ref[s] = cur[sl - CONV_PAD:sl, :]
        cn_ref[s] = ext_ref[s, tail:CONV_PAD, :]
    if tm % LANES:
        ks.append(jnp.zeros((LANES - tm % LANES, D_MODEL), F32))
    k = jnp.concatenate(ks, axis=0) if len(ks) > 1 else ks[0]
    kt = k.T.astype(BF16)
    nch = tm // lc
    per_seq = sl // lc
    for c in range(nch):
        kt_ref[c // per_seq, c % per_seq] = kt[:, c * lc:(c + 1) * lc]

    v_ref[...] = _dot(xn, wv_ref[...]).astype(BF16)
    o_ref[...] = _sigmoid(_dot(xn, wo_ref[...])).astype(BF16)

    zg = _dot(xn, wif_ref[...])
    if tm % LANES:
        zg = jnp.concatenate([zg, jnp.zeros((LANES - tm % LANES, LANES), F32)], axis=0)
    zt = zg.T[0:2 * HEADS, :]
    z = jnp.concatenate([zt[:, c * lc:(c + 1) * lc] + bif_ref[...] for c in range(nch)], axis=0)
    is_ig = (lax.broadcasted_iota(jnp.int32, z.shape, 0) % (2 * HEADS)) < HEADS
    g = jnp.where(is_ig, z, _log_sigmoid(z))
    upper = jnp.where(lax.broadcasted_iota(jnp.int32, (lc, lc), 0)
                      <= lax.broadcasted_iota(jnp.int32, (lc, lc), 1), 1.0, 0.0)
    bc = jnp.dot(g, upper, preferred_element_type=F32, precision=lax.Precision.HIGHEST)
    a = g - pltpu.roll(bc, nch * 2 * HEADS - HEADS, axis=0)
    amax = jnp.broadcast_to(jnp.max(a, axis=-1, keepdims=True), z.shape)
    gb = jnp.where(is_ig, g, bc)
    for c in range(nch):
        gt_ref[c // per_seq, c % per_seq, 0:2 * HEADS, :] = gb[c * 8:(c + 1) * 8, :]
        gt_ref[c // per_seq, c % per_seq, 2 * HEADS:4 * HEADS, :] = amax[c * 8:(c + 1) * 8, :]


def _tile_geometry(b, l, tile=TOKEN_TILE):
    sl = min(l, tile)
    nseg = max(1, min(b, TOKEN_TILE // sl))
    return nseg, sl


def _tok_spec(nseg, sl, nt, w):
    return pl.BlockSpec((nseg * sl, w), lambda bi, i: (bi * nt + i, 0))


def _inproj_a(x2d, b, l, g, wqk, wv, wo, wift, bif, cw, cb, cs):
    nseg, sl = _tile_geometry(b, l)
    nt = l // sl
    n = b * l
    CHUNK = min(MLSTM_BLOCK, l)
    per_seq = sl // CHUNK
    tok = functools.partial(_tok_spec, nseg, sl, nt)
    state = pl.BlockSpec((nseg, CONV_W - 1, 2 * D_MODEL), lambda bi, i: (bi, 0, 0))
    return pl.pallas_call(
        functools.partial(_inproj_a_kernel, nseg=nseg, sl=sl, lc=CHUNK),
        grid=(b // nseg, nt),
        in_specs=[tok(D_MODEL), _const_spec((1, D_MODEL)), _const_spec((D_MODEL, 2 * D_MODEL)),
                  _const_spec((D_MODEL, D_MODEL)), _const_spec((D_MODEL, D_MODEL)),
                  _const_spec((D_MODEL, LANES)), _const_spec((2 * HEADS, 1)),
                  _const_spec((CONV_W, 2 * D_MODEL)), _const_spec((1, 2 * D_MODEL)), state],
        out_specs=[tok(D_MODEL), tok(D_MODEL),
                   pl.BlockSpec((nseg, per_seq, D_MODEL, CHUNK), lambda bi, i: (bi, i, 0, 0)),
                   tok(D_MODEL), tok(D_MODEL),
                   pl.BlockSpec((nseg, per_seq, 4 * HEADS, CHUNK), lambda bi, i: (bi, i, 0, 0)),
                   state],
        out_shape=[jax.ShapeDtypeStruct((n, D_MODEL), BF16), jax.ShapeDtypeStruct((n, D_MODEL), BF16),
                   jax.ShapeDtypeStruct((b, l // CHUNK, D_MODEL, CHUNK), BF16),
                   jax.ShapeDtypeStruct((n, D_MODEL), BF16), jax.ShapeDtypeStruct((n, D_MODEL), BF16),
                   jax.ShapeDtypeStruct((b, l // CHUNK, 4 * HEADS, CHUNK), F32),
                   jax.ShapeDtypeStruct((b, CONV_W - 1, 2 * D_MODEL), F32)],
        scratch_shapes=[pltpu.VMEM((nseg, CONV_PAD, 2 * D_MODEL), F32)],
        compiler_params=_cparams(("parallel", "arbitrary")),
        name="inproj_a",
    )(x2d, g, wqk, wv, wo, wift, bif, cw, cb, cs)


def _inproj_b_kernel(x_ref, wu_ref, wgv_ref, wmq_ref, wgate_ref, lng_ref, lnb_ref, ws_ref, bst_ref,
                     mk_ref, mv_ref, ug_ref, att_ref, gates_ref, *rest, nseg, sl, cl, emit_gv):
    tm = nseg * sl
    xn = x_ref[...]

    gates_ref[...] = _sigmoid(_dot(xn, wgate_ref[...])).astype(BF16)

    gvr = jax.nn.gelu(_dot(xn, wgv_ref[...]))
    mu = jnp.mean(gvr, axis=-1, keepdims=True)
    xc = gvr - mu
    r = lax.rsqrt(jnp.mean(xc * xc, axis=-1, keepdims=True) + EPS)
    gv = (xc * r) * lng_ref[...] + lnb_ref[...]
    if emit_gv:
        rest[0][...] = gv
    gvb = gv.astype(BF16)
    u = jax.nn.gelu(_dot(xn, wu_ref[...]))
    tri = (lax.broadcasted_iota(jnp.int32, (cl, cl), 0) >= lax.broadcasted_iota(jnp.int32, (cl, cl), 1))
    for gi in range(GMLP_GROUPS):
        wsg = jnp.where(tri, ws_ref[gi], 0.0).astype(BF16)
        lo, hi = gi * GMLP_GROUP_DIM, (gi + 1) * GMLP_GROUP_DIM
        for c in range(tm // cl):
            sp = _dot(wsg, gvb[c * cl:(c + 1) * cl, lo:hi]) + bst_ref[:, gi:gi + 1]
            ug_ref[c * cl:(c + 1) * cl, lo:hi] = (u[c * cl:(c + 1) * cl, lo:hi] * sp).astype(BF16)

    mq = _dot(xn, wmq_ref[...]).astype(BF16)
    for s in range(nseg):
        r0, r1 = s * sl, (s + 1) * sl
        for h in range(HEADS):
            lo, hi = h * HEAD_DIM, (h + 1) * HEAD_DIM
            sc = _dot_nt(mq[r0:r1, lo:hi], mk_ref[s, :, lo:hi]) * (HEAD_DIM ** -0.5)
            e = jnp.exp(sc - jnp.max(sc, axis=-1, keepdims=True))
            a = (e / jnp.sum(e, axis=-1, keepdims=True)).astype(BF16)
            att_ref[r0:r1, lo:hi] = _dot(a, mv_ref[s, :, lo:hi]).astype(BF16)


def _inproj_b(xn2d, b, l, wu, wgv, wmq, wgate, lng, lnb, ws, bst, mk, mv, cl, emit_gv):
    nseg, sl = _tile_geometry(b, l)
    nt = l // sl
    n = b * l
    tok = functools.partial(_tok_spec, nseg, sl, nt)
    mem = pl.BlockSpec((nseg, N_MEM, D_MODEL), lambda bi, i: (bi, 0, 0))
    out_specs = [tok(D_MODEL), tok(D_MODEL), tok(3 * D_MODEL)]
    out_shape = [jax.ShapeDtypeStruct((n, D_MODEL), BF16), jax.ShapeDtypeStruct((n, D_MODEL), BF16),
                 jax.ShapeDtypeStruct((n, 3 * D_MODEL), BF16)]
    if emit_gv:
        out_specs.append(tok(D_MODEL))
        out_shape.append(jax.ShapeDtypeStruct((n, D_MODEL), F32))
    return pl.pallas_call(
        functools.partial(_inproj_b_kernel, nseg=nseg, sl=sl, cl=cl, emit_gv=emit_gv),
        grid=(b // nseg, nt),
        in_specs=[tok(D_MODEL), _const_spec((D_MODEL, D_MODEL)),
                  _const_spec((D_MODEL, D_MODEL)), _const_spec((D_MODEL, D_MODEL)),
                  _const_spec((D_MODEL, 3 * D_MODEL)), _const_spec((1, D_MODEL)), _const_spec((1, D_MODEL)),
                  _const_spec((GMLP_GROUPS, cl, cl)), _const_spec((cl, GMLP_GROUPS)), mem, mem],
        out_specs=out_specs,
        out_shape=out_shape,
        compiler_params=_cparams(("parallel", "parallel")),
        name="inproj_b",
    )(xn2d, wu, wgv, wmq, wgate, lng, lnb, ws, bst, mk, mv)


def _mlstm_kernel(q_ref, kt_ref, v_ref, gt_ref, *rest, nseg, cb, zero_state):
    i = pl.program_id(1)

    if zero_state:
        hm_ref, c_ref, n_ref, m_ref, st_ref = rest
    else:
        c0_ref, n0_ref, m0_ref, hm_ref, c_ref, n_ref, m_ref, st_ref = rest

    @pl.when(i == 0)
    def _():
        if zero_state:
            st_ref[...] = jnp.zeros(st_ref.shape, F32)
            m_ref[...] = jnp.zeros(m_ref.shape, F32)
        else:
            st_ref[:, :, :, :HEAD_DIM] = c0_ref[...]
            st_ref[:, :, :, HEAD_DIM:] = n0_ref[...]
            m_ref[...] = m0_ref[...]

    for s in range(nseg):
        _mlstm_sequence(q_ref, kt_ref.at[s], v_ref, gt_ref.at[s], st_ref.at[s], m_ref.at[s], hm_ref, s * cb, cb)

    @pl.when(i == pl.num_programs(1) - 1)
    def _():
        c_ref[...] = st_ref[:, :, :, :HEAD_DIM]
        n_ref[...] = st_ref[:, :, :, HEAD_DIM:]


def _mlstm_sequence(q_ref, kt_ref, v_ref, gt_ref, c_ref, m_ref, hm_ref, row0, cb):
    L = kt_ref.shape[-1]
    nch = cb // L
    ti = lax.broadcasted_iota(jnp.int32, (L, L), 0)
    si = lax.broadcasted_iota(jnp.int32, (L, L), 1)
    tri = ti >= si
    eye = ti == si

    rows = 4 * HEADS
    g_all = gt_ref[...].reshape(nch * rows, L)

    m_in = [m_ref[:, 0:1]]
    for c in range(nch):
        b_last4 = g_all[c * rows + HEADS:c * rows + 2 * HEADS, L - 1:L]
        amax4 = g_all[c * rows + 2 * HEADS:c * rows + 3 * HEADS, 0:1]
        m_in.append(jnp.maximum(b_last4 + m_in[-1], b_last4 + amax4))

    ones = jnp.ones((L, LANES), BF16)
    st = [c_ref[h] for h in range(HEADS)]
    for c in range(nch):
        r0, r1 = row0 + c * L, row0 + (c + 1) * L
        for h in range(HEADS):
            lo, hi = h * HEAD_DIM, (h + 1) * HEAD_DIM
            ig_r = g_all[c * rows + h:c * rows + h + 1, :]
            bc_r = g_all[c * rows + HEADS + h:c * rows + HEADS + h + 1, :]
            a_r = ig_r - bc_r
            bc_c = jnp.sum(jnp.where(eye, bc_r, 0.0), axis=-1, keepdims=True)
            m0 = m_in[c][h:h + 1, :]
            m_last = m_in[c + 1][h:h + 1, :]
            dmat = jnp.where(tri, bc_c + a_r, NEG_INF)
            inter = bc_c + m0
            m = jnp.maximum(inter, jnp.max(dmat, axis=-1, keepdims=True))
            w_intra = jnp.exp(dmat - m)
            w_inter = jnp.exp(inter - m)
            q = q_ref[r0:r1, lo:hi]
            kt = kt_ref[c, lo:hi, :]
            v = v_ref[r0:r1, lo:hi]
            s = _dot(q, kt) * w_intra
            qs = _dot(q, st[h].astype(BF16))
            num = w_inter * qs[:, :HEAD_DIM] + _dot(s.astype(BF16), v)
            den = w_inter * qs[:, HEAD_DIM:HEAD_DIM + 1] + jnp.sum(s, axis=-1, keepdims=True)
            hh = num / jnp.maximum(jnp.abs(den), jnp.exp(-m))
            bc_last = bc_r[:, L - 1:L]
            w_last = jnp.exp(bc_last + a_r - m_last)
            decay = jnp.exp(bc_last + m0 - m_last)
            ktw = (kt.astype(F32) * w_last).astype(BF16)
            st[h] = decay * st[h] + _dot(ktw, jnp.concatenate([v, ones], axis=1))
            hm_ref[r0:r1, lo:hi] = hh.astype(BF16)

    for h in range(HEADS):
        c_ref[h] = st[h]
    m_ref[...] = jnp.broadcast_to(m_in[nch], (HEADS, LANES))


def _mlstm(q, kt, v, gt, state, b, l):
    nseg, cb = _tile_geometry(b, l, MLSTM_TILE)
    nt = l // cb
    CHUNK = kt.shape[-1]
    tok = _tok_spec(nseg, cb, nt, D_MODEL)
    cs = pl.BlockSpec((nseg, HEADS, HEAD_DIM, HEAD_DIM), lambda bi, i: (bi, 0, 0, 0))
    ns = pl.BlockSpec((nseg, HEADS, HEAD_DIM, LANES), lambda bi, i: (bi, 0, 0, 0))
    ms = pl.BlockSpec((nseg, HEADS, LANES), lambda bi, i: (bi, 0, 0))
    state_specs = [] if state is None else [cs, ns, ms]
    return pl.pallas_call(
        functools.partial(_mlstm_kernel, nseg=nseg, cb=cb, zero_state=state is None),
        grid=(b // nseg, nt),
        in_specs=[tok, pl.BlockSpec((nseg, cb // CHUNK, D_MODEL, CHUNK), lambda bi, i: (bi, i, 0, 0)), tok,
                  pl.BlockSpec((nseg, cb // CHUNK, 4 * HEADS, CHUNK), lambda bi, i: (bi, i, 0, 0))] + state_specs,
        out_specs=[tok, cs, ns, ms],
        out_shape=[jax.ShapeDtypeStruct((b * l, D_MODEL), BF16),
                   jax.ShapeDtypeStruct((b, HEADS, HEAD_DIM, HEAD_DIM), F32),
                   jax.ShapeDtypeStruct((b, HEADS, HEAD_DIM, LANES), F32),
                   jax.ShapeDtypeStruct((b, HEADS, LANES), F32)],
        scratch_shapes=[pltpu.VMEM((nseg, HEADS, HEAD_DIM, HEAD_DIM + LANES), F32)],
        compiler_params=_cparams(("parallel", "arbitrary")),
        name="mlstm",
    )(q, kt, v, gt, *(state or ()))


def _merge_kernel(x_ref, hh_ref, o_ref, ng_ref, ug_ref, att_ref, gates_ref, wa_ref, wb_ref, wc_ref, wo_ref, h_ref):
    parts = []
    for h in range(HEADS):
        lo, hi = h * HEAD_DIM, (h + 1) * HEAD_DIM
        hh = hh_ref[:, lo:hi].astype(F32)
        hn = hh * lax.rsqrt(jnp.mean(hh * hh, axis=-1, keepdims=True) + EPS)
        parts.append(((hn * ng_ref[:, lo:hi]) * o_ref[:, lo:hi].astype(F32)).astype(BF16))
    br = _dot(parts[0], wa_ref[0:HEAD_DIM, :])
    for h in range(1, HEADS):
        br = br + _dot(parts[h], wa_ref[h * HEAD_DIM:(h + 1) * HEAD_DIM, :])
    g = gates_ref[...].astype(F32)
    mixed = g[:, :D_MODEL] * br
    mixed = mixed + g[:, D_MODEL:2 * D_MODEL] * _dot(ug_ref[...], wb_ref[...])
    mixed = mixed + g[:, 2 * D_MODEL:] * _dot(att_ref[...], wc_ref[...])
    h_ref[...] = x_ref[...] + _dot(mixed.astype(BF16), wo_ref[...])


def _merge(x2d, hh, osig, ng, ug, att, gates, wa, wb, wc, wo, tm):
    n = x2d.shape[0]
    row = lambda w: pl.BlockSpec((tm, w), lambda i: (i, 0))
    wspec = _const_spec((D_MODEL, D_MODEL))
    return pl.pallas_call(
        _merge_kernel,
        grid=(n // tm,),
        in_specs=[row(D_MODEL), row(D_MODEL), row(D_MODEL), _const_spec((1, D_MODEL)), row(D_MODEL), row(D_MODEL),
                  row(3 * D_MODEL), wspec, wspec, wspec, wspec],
        out_specs=row(D_MODEL),
        out_shape=jax.ShapeDtypeStruct((n, D_MODEL), F32),
        compiler_params=_cparams(("parallel",)),
        name="merge",
    )(x2d, hh, osig, ng, ug, att, gates, wa, wb, wc, wo)


MOE_T = 256
MOE_STEP_SUBS = 2
MOE_CAP = 32
MOE_S = N_EXPERTS * MOE_CAP
MOE_OVF = 512
SEG_ALIGN = 16
EXPERT_BLOCK = 128


def _moe_route_kernel(hp_ref, hs_ref, g_ref, wrg_ref, brg_ref, wre_ref, bre_ref,
                      main_ref, ovf_ref, info_ref, meta_ref, *, nsp):
    n = MOE_STEP_SUBS * MOE_T
    h = jnp.where(pl.program_id(0) < nsp // MOE_STEP_SUBS, hp_ref[...], hs_ref[...])
    xm = _rmsnorm(h, g_ref[...]).astype(BF16)
    lane = lax.broadcasted_iota(jnp.int32, (n, LANES), 1).astype(F32)
    lg = jnp.where(lane < N_GROUPS, _dot(xm, wrg_ref[...]) + brg_ref[...], NEG_INF)
    gmax = jnp.max(lg, axis=-1, keepdims=True)
    p_top = 1.0 / jnp.sum(jnp.exp(lg - gmax), axis=-1, keepdims=True)
    grp = jnp.min(jnp.where(lg == gmax, lane, float(LANES)), axis=-1, keepdims=True)
    el = _dot(xm, wre_ref[...]) + bre_ref[...]
    in_grp = (lane >= grp * EXPERTS_PER_GROUP) & (lane < (grp + 1.0) * EXPERTS_PER_GROUP)
    vals = jnp.where(in_grp, el, NEG_INF)
    v1 = jnp.max(vals, axis=-1, keepdims=True)
    i1 = jnp.min(jnp.where(vals == v1, lane, float(LANES)), axis=-1, keepdims=True)
    vals2 = jnp.where(lane == i1, NEG_INF, vals)
    v2 = jnp.max(vals2, axis=-1, keepdims=True)
    i2 = jnp.min(jnp.where(vals2 == v2, lane, float(LANES)), axis=-1, keepdims=True)
    r = jnp.exp(v2 - v1)
    p1 = p_top / (1.0 + r)
    p2 = p_top * r / (1.0 + r)
    sel1 = lane == i1
    sel2 = lane == i2
    onehot = jnp.where(sel1 | sel2, 1.0, 0.0)

    deferred = []
    for s in range(MOE_STEP_SUBS):
        rows = slice(s * MOE_T, (s + 1) * MOE_T)
        deferred.append(_route_sub_tile(
            xm[rows, :], onehot[rows, :], i1[rows, :], i2[rows, :],
            p1[rows, :], p2[rows, :], main_ref.at[:, s], ovf_ref.at[s], info_ref.at[rows, :], meta_ref.at[s]))
    for write_overflow in deferred:
        write_overflow()


def _route_sub_tile(xm, onehot, i1, i2, p1, p2, main_ref, ovf_ref, info_ref, meta_ref):
    t = MOE_T
    lane = lax.broadcasted_iota(jnp.int32, (t, LANES), 1)
    sel1 = lane.astype(F32) == i1
    sel2 = lane.astype(F32) == i2
    cnt = jnp.sum(onehot, axis=0, keepdims=True).astype(jnp.int32)
    pn = jnp.bitwise_and(jnp.maximum(cnt - MOE_CAP, 0) + (SEG_ALIGN - 1), -SEG_ALIGN)
    pn8 = jnp.broadcast_to(pn, (8, LANES))
    earlier = jnp.where(lax.broadcasted_iota(jnp.int32, (LANES, LANES), 0)
                        < lax.broadcasted_iota(jnp.int32, (LANES, LANES), 1), 1.0, 0.0).astype(BF16)
    off_f8 = _dot(pn8.astype(F32).astype(BF16), earlier)
    row8 = lax.broadcasted_iota(jnp.int32, (8, LANES), 0)
    meta_ref[...] = jnp.where(row8 == 0, off_f8.astype(jnp.int32), jnp.where(row8 == 1, pn8, 0))

    ti = lax.broadcasted_iota(jnp.int32, (t, t), 0)
    si = lax.broadcasted_iota(jnp.int32, (t, t), 1)
    before = jnp.where(ti > si, 1.0, 0.0).astype(BF16)
    rank = _dot(before, onehot.astype(BF16))
    off_f = off_f8[0:1, :]

    def slot_row(sel, idx):
        rk = jnp.sum(jnp.where(sel, rank, 0.0), axis=-1, keepdims=True)
        of = jnp.sum(jnp.where(sel, off_f, 0.0), axis=-1, keepdims=True)
        return jnp.where(rk < MOE_CAP, idx * MOE_CAP + rk, MOE_S - MOE_CAP + of + rk)

    pos1 = slot_row(sel1, i1)
    pos2 = slot_row(sel2, i2)
    info_ref[...] = (jnp.where(lane == 0, pos1, 0.0) + jnp.where(lane == 1, pos2, 0.0)
                     + jnp.where(lane == 2, p1, 0.0) + jnp.where(lane == 3, p2, 0.0))

    eye = ti == si
    pos1_r = jnp.sum(jnp.where(eye, pos1, 0.0), axis=0, keepdims=True)
    pos2_r = jnp.sum(jnp.where(eye, pos2, 0.0), axis=0, keepdims=True)

    def gather(rows, base):
        srow = (lax.broadcasted_iota(jnp.int32, (rows, t), 0) + base).astype(F32)
        pick = jnp.where((srow == pos1_r) | (srow == pos2_r), 1.0, 0.0).astype(BF16)
        return _dot(pick, xm).astype(BF16)

    main_ref[...] = gather(MOE_S, 0).reshape(N_EXPERTS, MOE_CAP, D_MODEL)
    has_ovf = jnp.sum(pn) > 0

    def write_overflow():
        @pl.when(has_ovf)
        def _():
            ovf_ref[...] = gather(MOE_OVF, MOE_S)

        @pl.when(jnp.logical_not(has_ovf))
        def _():
            ovf_ref[...] = jnp.zeros(ovf_ref.shape, ovf_ref.dtype)
    return write_overflow


def _two_group_specs(nsp, cols):
    rows = MOE_STEP_SUBS * MOE_T
    npb = nsp // MOE_STEP_SUBS
    return (pl.BlockSpec((rows, cols), lambda j, *_: (jnp.minimum(j, npb - 1), 0)),
            pl.BlockSpec((rows, cols), lambda j, *_: (jnp.maximum(j - npb, 0), 0)))


def _moe_route(hp, hs, g, wrg, brg, wre, bre):
    nsp = hp.shape[0] // MOE_T
    ns = nsp + hs.shape[0] // MOE_T
    sub = MOE_STEP_SUBS
    return pl.pallas_call(
        functools.partial(_moe_route_kernel, nsp=nsp),
        grid=(ns // sub,),
        in_specs=[*_two_group_specs(nsp, D_MODEL), _const_spec((1, D_MODEL)),
                  _const_spec((D_MODEL, LANES)), _const_spec((1, LANES)),
                  _const_spec((D_MODEL, LANES)), _const_spec((1, LANES))],
        out_specs=[pl.BlockSpec((N_EXPERTS, sub, MOE_CAP, D_MODEL), lambda j: (0, j, 0, 0)),
                   pl.BlockSpec((sub, MOE_OVF, D_MODEL), lambda j: (j, 0, 0)),
                   pl.BlockSpec((sub * MOE_T, LANES), lambda j: (j, 0)),
                   pl.BlockSpec((sub, 8, LANES), lambda j: (j, 0, 0))],
        out_shape=[jax.ShapeDtypeStruct((N_EXPERTS, ns, MOE_CAP, D_MODEL), BF16),
                   jax.ShapeDtypeStruct((ns, MOE_OVF, D_MODEL), BF16),
                   jax.ShapeDtypeStruct((ns * MOE_T, LANES), F32),
                   jax.ShapeDtypeStruct((ns, 8, LANES), jnp.int32)],
        compiler_params=_cparams(("arbitrary",)),
        name="moe_route",
    )(hp, hs, g, wrg, brg, wre, bre)


def _swiglu(x, wgu, wdb):
    gu = _dot(x, wgu[...])
    gate = gu[:, :D_EXPERT]
    hid = (gate * _sigmoid(gate)) * gu[:, D_EXPERT:]
    return _dot(hid.astype(BF16), wdb[...]).astype(BF16)


def _moe_expert_kernel(offs_ref, pns_ref, eflag_ref, main_ref, slots_hbm, wg_ref, wu_ref, wd_ref,
                       mout_ref, out_hbm, xbuf, obuf, wgu, wdb, sem_in, sem_out, *, n_sub, jb, n_rb):
    e = pl.program_id(0)
    rb = pl.program_id(1)

    @pl.when((e == 0) & (rb == 0))
    def _():
        xbuf[...] = jnp.zeros_like(xbuf)

    @pl.when(rb == 0)
    def _():
        wgu[:, :D_EXPERT] = wg_ref[...].astype(BF16)
        wgu[:, D_EXPERT:] = wu_ref[...].astype(BF16)
        wdb[...] = wd_ref[...].astype(BF16)

    sb = main_ref.shape[0]
    part = sb // 2 if sb % 2 == 0 else sb
    for s0 in range(0, sb, part):
        y = _swiglu(main_ref[s0:s0 + part].reshape(part * MOE_CAP, D_MODEL), wgu, wdb)
        mout_ref[s0:s0 + part] = y.reshape(part, MOE_CAP, D_MODEL)

    def copy_in(j, src, dst):
        return pltpu.make_async_copy(slots_hbm.at[j, pl.ds(src, SEG_ALIGN), :],
                                     xbuf.at[pl.ds(dst, SEG_ALIGN), :], sem_in)

    def copy_out(j, src, dst):
        return pltpu.make_async_copy(obuf.at[pl.ds(src, SEG_ALIGN), :],
                                     out_hbm.at[j, pl.ds(dst, SEG_ALIGN), :], sem_out)

    def for_each_chunk(g, fn):
        def seg(jj, cur):
            j = g * jb + jj
            n = pns_ref[j * N_EXPERTS + e]
            off = offs_ref[j * N_EXPERTS + e]

            def chunk(k, c):
                fn(j, pl.multiple_of(off + k * SEG_ALIGN, SEG_ALIGN), pl.multiple_of(cur + k * SEG_ALIGN, SEG_ALIGN))
                return c
            lax.fori_loop(0, n // SEG_ALIGN, chunk, 0)
            return cur + n
        return lax.fori_loop(0, jb, seg, 0)

    def group(g, carry):
        total = for_each_chunk(g, lambda j, r, b: copy_in(j, r, b).start())
        nchunk = total // SEG_ALIGN

        def wait_in(k, c):
            copy_in(0, 0, 0).wait()
            return c
        lax.fori_loop(0, nchunk, wait_in, 0)

        def block(bi, c):
            r0 = pl.multiple_of(bi * EXPERT_BLOCK, EXPERT_BLOCK)
            obuf[pl.ds(r0, EXPERT_BLOCK), :] = _swiglu(xbuf[pl.ds(r0, EXPERT_BLOCK), :], wgu, wdb)
            return c
        lax.fori_loop(0, (total + EXPERT_BLOCK - 1) // EXPERT_BLOCK, block, 0)

        for_each_chunk(g, lambda j, r, b: copy_out(j, b, r).start())

        def wait_out(k, c):
            copy_out(0, 0, 0).wait()
            return c
        lax.fori_loop(0, nchunk, wait_out, 0)
        return carry

    @pl.when((rb == n_rb - 1) & (eflag_ref[e] > 0))
    def _():
        lax.fori_loop(0, n_sub // jb, group, 0)


def _moe_expert(main, ovf, offs, pns, eflag, wg, wu, wd):
    ns = main.shape[1]
    n_rb = 1
    sb = ns // n_rb
    jb = max(d for d in range(1, 12) if ns % d == 0)
    rows = jb * MOE_T + EXPERT_BLOCK
    main4 = main
    mspec = pl.BlockSpec((None, sb, MOE_CAP, D_MODEL), lambda e, rb, *_: (e, rb, 0, 0))
    wspec = lambda a, b: pl.BlockSpec((None, a, b), lambda e, rb, *_: (e, 0, 0))
    grid_spec = pltpu.PrefetchScalarGridSpec(
        num_scalar_prefetch=3,
        grid=(N_EXPERTS, n_rb),
        in_specs=[mspec, pl.BlockSpec(memory_space=pl.ANY),
                  wspec(D_MODEL, D_EXPERT), wspec(D_MODEL, D_EXPERT), wspec(D_EXPERT, D_MODEL)],
        out_specs=[mspec, pl.BlockSpec(memory_space=pl.ANY)],
        scratch_shapes=[pltpu.VMEM((rows, D_MODEL), BF16), pltpu.VMEM((rows, D_MODEL), BF16),
                        pltpu.VMEM((D_MODEL, 2 * D_EXPERT), BF16), pltpu.VMEM((D_EXPERT, D_MODEL), BF16),
                        pltpu.SemaphoreType.DMA(()), pltpu.SemaphoreType.DMA(())],
    )
    mout, oout = pl.pallas_call(
        functools.partial(_moe_expert_kernel, n_sub=ns, jb=jb, n_rb=n_rb),
        grid_spec=grid_spec,
        out_shape=[jax.ShapeDtypeStruct(main4.shape, main4.dtype), jax.ShapeDtypeStruct(ovf.shape, ovf.dtype)],
        input_output_aliases={4: 1},
        compiler_params=_cparams(("arbitrary", "arbitrary")),
        name="moe_expert",
    )(offs, pns, eflag, main4, ovf, wg, wu, wd)
    return mout, oout


def _moe_combine_kernel(jflag_ref, sflag_ref, hp_ref, hs_ref, main_ref, ovf_ref, info_ref, gf_ref,
                        yp_ref, ys_ref, acc_ref, *, nsp):
    j = pl.program_id(0)
    npb = nsp // MOE_STEP_SUBS

    def weights(s, cols, base):
        info = info_ref[s * MOE_T:(s + 1) * MOE_T, :]
        scol = (lax.broadcasted_iota(jnp.int32, (MOE_T, cols), 1) + base).astype(F32)
        return (jnp.where(scol == info[:, 0:1], info[:, 2:3], 0.0)
                + jnp.where(scol == info[:, 1:2], info[:, 3:4], 0.0)).astype(BF16)

    for s in range(MOE_STEP_SUBS):
        rows = slice(s * MOE_T, (s + 1) * MOE_T)
        h = jnp.where(j < npb, hp_ref[rows, :], hs_ref[rows, :])
        acc_ref[rows, :] = h + _dot(weights(s, MOE_S, 0), main_ref[:, s].reshape(MOE_S, D_MODEL))

    for s in range(MOE_STEP_SUBS):
        @pl.when(jflag_ref[j * MOE_STEP_SUBS + s] > 0)
        def _(s=s):
            rows = slice(s * MOE_T, (s + 1) * MOE_T)
            acc_ref[rows, :] += _dot(weights(s, MOE_OVF, MOE_S), ovf_ref[s])

    y = _rmsnorm(acc_ref[...], gf_ref[...])

    @pl.when(j < npb)
    def _():
        yp_ref[...] = y

    @pl.when(j >= npb)
    def _():
        ys_ref[...] = y


def _moe_combine(hp, hs, main, ovf, info, jflag, gf):
    nsp = hp.shape[0] // MOE_T
    ns = main.shape[1]
    sub = MOE_STEP_SUBS
    sflag = jnp.max(jflag.reshape(ns // sub, sub), axis=1)
    grid_spec = pltpu.PrefetchScalarGridSpec(
        num_scalar_prefetch=2,
        grid=(ns // sub,),
        in_specs=[*_two_group_specs(nsp, D_MODEL),
                  pl.BlockSpec((N_EXPERTS, sub, MOE_CAP, D_MODEL), lambda j, jf, sf: (0, j, 0, 0)),
                  pl.BlockSpec((sub, MOE_OVF, D_MODEL), lambda j, jf, sf: (jnp.where(sf[j] > 0, j, 0), 0, 0)),
                  pl.BlockSpec((sub * MOE_T, LANES), lambda j, jf, sf: (j, 0)),
                  pl.BlockSpec((1, D_MODEL), lambda j, jf, sf: (0, 0))],
        out_specs=list(_two_group_specs(nsp, D_MODEL)),
        scratch_shapes=[pltpu.VMEM((sub * MOE_T, D_MODEL), F32)],
    )
    return pl.pallas_call(
        functools.partial(_moe_combine_kernel, nsp=nsp),
        grid_spec=grid_spec,
        out_shape=[jax.ShapeDtypeStruct(hp.shape, F32), jax.ShapeDtypeStruct(hs.shape, F32)],
        compiler_params=_cparams(("arbitrary",)),
        name="moe_combine",
    )(jflag, sflag, hp, hs, main, ovf, info, gf)


def _moe(hp, hs, p):
    main, ovf, info, meta = _moe_route(hp, hs, p["norm_moe_g"], p["w_rg"], p["b_rg"], p["w_re"], p["b_re"])
    pn = meta[:, 1, :N_EXPERTS]
    offs = meta[:, 0, :N_EXPERTS].reshape(-1)
    eflag = (jnp.sum(pn, axis=0) > 0).astype(jnp.int32)
    jflag = (jnp.sum(pn, axis=1) > 0).astype(jnp.int32)
    main, ovf = _moe_expert(main, ovf, offs, pn.reshape(-1), eflag, p["w_eg"], p["w_eu"], p["w_ed"])
    return _moe_combine(hp, hs, main, ovf, info, jflag, p["norm_final_g"])


def _trunk(x, mem_k, mem_v, c0, n0, m0, conv0, p, *, emit_gv):
    b, l, _ = x.shape
    n = b * l
    cl = min(l, GMLP_CHUNK)
    x2d = x.reshape(n, D_MODEL)
    xn, q, kt, v, osig, gt, conv_new = _inproj_a(
        x2d, b, l, p["norm_mix_g"], p["w_qk"], p["w_v"], p["w_o"], p["w_if"], p["b_if"],
        p["conv_w"], p["conv_b"], conv0)
    outs = _inproj_b(xn, b, l, p["w_u"], p["w_gv"], p["w_mq"], p["w_gate"],
                     p["gmlp_norm_g"], p["gmlp_norm_b"], p["w_s"][:, :cl, :cl], p["b_st"][:cl],
                     mem_k, mem_v, cl, emit_gv)
    ug, att, gates = outs[:3]
    on_lanes = lambda a: jnp.broadcast_to(a[..., None], a.shape + (LANES,))
    state = None if c0 is None else (c0, on_lanes(n0), on_lanes(m0))
    hh, c1, n1, m1 = _mlstm(q, kt, v, gt, state, b, l)
    n1 = n1[..., 0]
    h = _merge(x2d, hh, osig, p["mlstm_norm_g"], ug, att, gates,
               p["w_br_mlstm"], p["w_br_gmlp"], p["w_br_mem"], p["w_out"], TOKEN_TILE)
    return h, c1, n1, m1[..., 0], conv_new, (outs[3].reshape(b, l, D_MODEL) if emit_gv else None)


def kernel(x_prompt, x_sample, mem_prompt, cache_mem_k, cache_mem_v, state_mlstm_C, state_mlstm_n, state_mlstm_m, state_mlstm_conv, norm_mix_g, w_in, mlstm_i_b, mlstm_f_b, mlstm_conv_w, mlstm_conv_b, mlstm_norm_g, gmlp_norm_g, gmlp_norm_b, gmlp_w_s, gmlp_b_s, mem_norm_g, w_mem_k, w_mem_v, w_br_mlstm, w_br_gmlp, w_br_mem, w_out, norm_moe_g, w_router_group, b_router_group, w_router_expert, b_router_expert, w_exp_gate, w_exp_up, w_exp_down, norm_final_g):
    bp = x_prompt.shape[0]
    bs = x_sample.shape[0]
    W = D_MODEL
    wi = w_in[0]
    o_qk, o_v, o_o, o_i = 0, 2 * W, 3 * W, 4 * W
    o_f = o_i + HEADS
    o_u = o_f + HEADS
    o_gv, o_mq, o_gate = o_u + W, o_u + 2 * W, o_u + 3 * W
    row = lambda a: a.reshape(1, -1)
    pad_l = lambda a: jnp.pad(a, ((0, 0), (0, LANES - a.shape[1])))
    p = {
        "norm_mix_g": row(norm_mix_g[0]),
        "w_qk": wi[:, o_qk:o_v].astype(BF16), "w_v": wi[:, o_v:o_o].astype(BF16),
        "w_o": wi[:, o_o:o_i].astype(BF16),
        "w_if": pad_l(wi[:, o_i:o_u]).astype(BF16),
        "b_if": jnp.concatenate([mlstm_i_b[0], mlstm_f_b[0]]).reshape(2 * HEADS, 1),
        "conv_w": mlstm_conv_w[0], "conv_b": row(mlstm_conv_b[0]),
        "w_u": wi[:, o_u:o_gv].astype(BF16), "w_gv": wi[:, o_gv:o_mq].astype(BF16),
        "w_mq": wi[:, o_mq:o_gate].astype(BF16), "w_gate": wi[:, o_gate:].astype(BF16),
        "gmlp_norm_g": row(gmlp_norm_g[0]), "gmlp_norm_b": row(gmlp_norm_b[0]),
        "w_s": gmlp_w_s[0], "b_st": gmlp_b_s[0].T,
        "mlstm_norm_g": row(mlstm_norm_g[0]),
        "w_br_mlstm": w_br_mlstm[0].astype(BF16), "w_br_gmlp": w_br_gmlp[0].astype(BF16),
        "w_br_mem": w_br_mem[0].astype(BF16), "w_out": w_out[0].astype(BF16),
        "norm_moe_g": row(norm_moe_g[0]),
        "w_rg": pad_l(w_router_group[0]).astype(BF16), "b_rg": pad_l(row(b_router_group[0])),
        "w_re": pad_l(w_router_expert[0]).astype(BF16), "b_re": pad_l(row(b_router_expert[0])),
        "w_eg": w_exp_gate[0], "w_eu": w_exp_up[0], "w_ed": w_exp_down[0],
        "norm_final_g": row(norm_final_g),
    }

    mk_p, mv_p, mk_pb, mv_pb = _memory_kv(mem_prompt.reshape(bp * N_MEM, W), row(mem_norm_g[0]),
                                          w_mem_k[0], w_mem_v[0])

    zeros = lambda *s: jnp.zeros(s, F32)
    hp, cp, np_, mp, cvp, _ = _trunk(
        x_prompt, mk_pb.reshape(bp, N_MEM, W), mv_pb.reshape(bp, N_MEM, W),
        None, None, None, zeros(bp, CONV_W - 1, 2 * W), p, emit_gv=False)
    hs, cs, ns, ms, cvs, gvs = _trunk(
        x_sample, cache_mem_k[0].reshape(bs, N_MEM, W).astype(BF16),
        cache_mem_v[0].reshape(bs, N_MEM, W).astype(BF16),
        state_mlstm_C[0], state_mlstm_n[0], state_mlstm_m[0], state_mlstm_conv[0], p,
        emit_gv=True)

    yp, ys = _moe(hp, hs, p)
    kv_shape = (1, bp, N_MEM, HEADS, HEAD_DIM)
    return (yp.reshape(x_prompt.shape), ys.reshape(x_sample.shape), mk_p.reshape(kv_shape), mv_p.reshape(kv_shape),
            cp[None], np_[None], mp[None], cvp[None],
            cs[None], ns[None], ms[None], cvs[None], gvs[None])
```

```python
import functools

import jax
import jax.numpy as jnp
from jax import lax
from jax.experimental import pallas as pl
from jax.experimental.pallas import tpu as pltpu

D_MODEL = 1024
MLSTM_BLOCK = 512
TOKEN_TILE = 512
INPROJ_B_TILE = 1024
MLSTM_TILE = 2048
EPS = 1e-6
HEADS = 4
HEAD_DIM = 256
CONV_W = 4
GMLP_GROUPS = 4
GMLP_GROUP_DIM = 256
GMLP_CHUNK = 128
N_MEM = 256
N_GROUPS = 4
EXPERTS_PER_GROUP = 8
N_EXPERTS = 32
D_EXPERT = 256
LANES = 128
CONV_PAD = 8
NORM_ROWS = 128

F32 = jnp.float32
BF16 = jnp.bfloat16
NEG_INF = float("-inf")

VMEM_LIMIT = 56 * 1024 * 1024


def _cparams(sem):
    return pltpu.CompilerParams(dimension_semantics=sem, vmem_limit_bytes=VMEM_LIMIT)


def _const_spec(shape):
    nd = len(shape)
    return pl.BlockSpec(shape, lambda *_: (0,) * nd, pipeline_mode=pl.Buffered(1))


def _sigmoid(x):
    return 0.5 * (jnp.tanh(0.5 * x) + 1.0)


def _log_sigmoid(x):
    return jnp.minimum(x, 0.0) - jnp.log(1.0 + jnp.exp(-jnp.abs(x)))


def _rmsnorm(x, g):
    r = lax.rsqrt(jnp.mean(x * x, axis=-1, keepdims=True) + EPS)
    return (x * r) * g


def _dot(a, b):
    return jnp.dot(a, b, preferred_element_type=F32)


def _dot_nt(a, b):
    return lax.dot_general(a, b, (((1,), (1,)), ((), ())), preferred_element_type=F32)


def _memkv_kernel(mem_ref, g_ref, wk_ref, wv_ref, k_ref, v_ref, kb_ref, vb_ref):
    mn = _rmsnorm(mem_ref[...], g_ref[...]).astype(BF16)
    k = _dot(mn, wk_ref[...].astype(BF16))
    v = _dot(mn, wv_ref[...].astype(BF16))
    k_ref[...] = k
    v_ref[...] = v
    kb_ref[...] = k.astype(BF16)
    vb_ref[...] = v.astype(BF16)


def _memory_kv(mem2d, g, wk, wv):
    n = mem2d.shape[0]
    tm = TOKEN_TILE
    row = pl.BlockSpec((tm, D_MODEL), lambda i: (i, 0))
    return pl.pallas_call(
        _memkv_kernel,
        grid=(n // tm,),
        in_specs=[row, _const_spec((1, D_MODEL)), _const_spec((D_MODEL, D_MODEL)),
                  _const_spec((D_MODEL, D_MODEL))],
        out_specs=[row, row, row, row],
        out_shape=[jax.ShapeDtypeStruct((n, D_MODEL), F32)] * 2 + [jax.ShapeDtypeStruct((n, D_MODEL), BF16)] * 2,
        compiler_params=_cparams(("parallel",)),
        name="memory_kv",
    )(mem2d, g, wk, wv)


def _inproj_a_kernel(x_ref, g_ref, wqk_ref, wv_ref, wo_ref, wif_ref, bif_ref, cw_ref, cb_ref, cs_ref,
                     xn_ref, q_ref, kt_ref, v_ref, o_ref, gt_ref, cn_ref, ext_ref, *, nseg, sl, lc):
    i = pl.program_id(1)
    tm = nseg * sl
    tail = CONV_PAD - (CONV_W - 1)

    @pl.when(i == 0)
    def _():
        for s in range(nseg):
            ext_ref[s, 0:tail, :] = jnp.zeros((tail, 2 * D_MODEL), F32)
            ext_ref[s, tail:CONV_PAD, :] = cs_ref[s]

    parts = [_rmsnorm(x_ref[r:r + NORM_ROWS, :], g_ref[...]).astype(BF16) for r in range(0, tm, NORM_ROWS)]
    xn = jnp.concatenate(parts, axis=0)
    xn_ref[...] = xn

    zqk = jnp.concatenate([_dot(part, wqk_ref[...]) for part in parts], axis=0)
    ks = []
    row8 = lax.broadcasted_iota(jnp.int32, (CONV_PAD, 2 * D_MODEL), 0)
    for s in range(nseg):
        cur = zqk[s * sl:(s + 1) * sl, :]
        prev = ext_ref[s]
        acc = cb_ref[...] + cur * cw_ref[CONV_W - 1:CONV_W, :]
        for d in range(1, CONV_W):
            back = pltpu.roll(cur, d, axis=0)
            head = jnp.where(row8 < d, pltpu.roll(prev, d, axis=0), back[0:CONV_PAD, :])
            back = jnp.concatenate([head, back[CONV_PAD:, :]], axis=0)
            acc = acc + back * cw_ref[CONV_W - 1 - d:CONV_W - d, :]
        qk = acc * _sigmoid(acc)
        q_ref[s * sl:(s + 1) * sl, :] = (qk[:, :D_MODEL] * (HEAD_DIM ** -0.5)).astype(BF16)
        ks.append(qk[:, D_MODEL:])
        ext_ref[s] = cur[sl - CONV_PAD:sl, :]
        cn_ref[s] = ext_ref[s, tail:CONV_PAD, :]
    if tm % LANES:
        ks.append(jnp.zeros((LANES - tm % LANES, D_MODEL), F32))
    k = jnp.concatenate(ks, axis=0) if len(ks) > 1 else ks[0]
    kt = k.T.astype(BF16)
    nch = tm // lc
    per_seq = sl // lc
    for c in range(nch):
        kt_ref[c // per_seq, c % per_seq] = kt[:, c * lc:(c + 1) * lc]

    v_ref[...] = _dot(xn, wv_ref[...]).astype(BF16)
    o_ref[...] = _sigmoid(_dot(xn, wo_ref[...])).astype(BF16)

    zg = _dot(xn, wif_ref[...])
    if tm % LANES:
        zg = jnp.concatenate([zg, jnp.zeros((LANES - tm % LANES, LANES), F32)], axis=0)
    zt = zg.T[0:2 * HEADS, :]
    z = jnp.concatenate([zt[:, c * lc:(c + 1) * lc] + bif_ref[...] for c in range(nch)], axis=0)
    is_ig = (lax.broadcasted_iota(jnp.int32, z.shape, 0) % (2 * HEADS)) < HEADS
    g = jnp.where(is_ig, z, _log_sigmoid(z))
    upper = jnp.where(lax.broadcasted_iota(jnp.int32, (lc, lc), 0)
                      <= lax.broadcasted_iota(jnp.int32, (lc, lc), 1), 1.0, 0.0)
    bc = jnp.dot(g, upper, preferred_element_type=F32, precision=lax.Precision.HIGHEST)
    a = g - pltpu.roll(bc, nch * 2 * HEADS - HEADS, axis=0)
    amax = jnp.broadcast_to(jnp.max(a, axis=-1, keepdims=True), z.shape)
    gb = jnp.where(is_ig, g, bc)
    for c in range(nch):
        gt_ref[c // per_seq, c % per_seq, 0:2 * HEADS, :] = gb[c * 8:(c + 1) * 8, :]
        gt_ref[c // per_seq, c % per_seq, 2 * HEADS:4 * HEADS, :] = amax[c * 8:(c + 1) * 8, :]


def _tile_geometry(b, l, tile=TOKEN_TILE):
    sl = min(l, tile)
    nseg = max(1, min(b, TOKEN_TILE // sl))
    return nseg, sl


def _tok_spec(nseg, sl, nt, w):
    return pl.BlockSpec((nseg * sl, w), lambda bi, i: (bi * nt + i, 0))


def _inproj_a(x2d, b, l, g, wqk, wv, wo, wif, bif, cw, cb, cs):
    nseg, sl = _tile_geometry(b, l)
    nt = l // sl
    n = b * l
    CHUNK = min(MLSTM_BLOCK, l)
    per_seq = sl // CHUNK
    tok = functools.partial(_tok_spec, nseg, sl, nt)
    state = pl.BlockSpec((nseg, CONV_W - 1, 2 * D_MODEL), lambda bi, i: (bi, 0, 0))
    return pl.pallas_call(
        functools.partial(_inproj_a_kernel, nseg=nseg, sl=sl, lc=CHUNK),
        grid=(b // nseg, nt),
        in_specs=[tok(D_MODEL), _const_spec((1, D_MODEL)), _const_spec((D_MODEL, 2 * D_MODEL)),
                  _const_spec((D_MODEL, D_MODEL)), _const_spec((D_MODEL, D_MODEL)),
                  _const_spec((D_MODEL, LANES)), _const_spec((2 * HEADS, 1)),
                  _const_spec((CONV_W, 2 * D_MODEL)), _const_spec((1, 2 * D_MODEL)), state],
        out_specs=[tok(D_MODEL), tok(D_MODEL),
                   pl.BlockSpec((nseg, per_seq, D_MODEL, CHUNK), lambda bi, i: (bi, i, 0, 0)),
                   tok(D_MODEL), tok(D_MODEL),
                   pl.BlockSpec((nseg, per_seq, 4 * HEADS, CHUNK), lambda bi, i: (bi, i, 0, 0)),
                   state],
        out_shape=[jax.ShapeDtypeStruct((n, D_MODEL), BF16), jax.ShapeDtypeStruct((n, D_MODEL), BF16),
                   jax.ShapeDtypeStruct((b, l // CHUNK, D_MODEL, CHUNK), BF16),
                   jax.ShapeDtypeStruct((n, D_MODEL), BF16), jax.ShapeDtypeStruct((n, D_MODEL), BF16),
                   jax.ShapeDtypeStruct((b, l // CHUNK, 4 * HEADS, CHUNK), F32),
                   jax.ShapeDtypeStruct((b, CONV_W - 1, 2 * D_MODEL), F32)],
        scratch_shapes=[pltpu.VMEM((nseg, CONV_PAD, 2 * D_MODEL), F32)],
        compiler_params=_cparams(("parallel", "arbitrary")),
        name="inproj_a",
    )(x2d, g, wqk, wv, wo, wif, bif, cw, cb, cs)


def _inproj_b_kernel(x_ref, wu_ref, wgv_ref, wmq_ref, wgate_ref, lng_ref, lnb_ref, ws_ref, bst_ref,
                     mk_ref, mv_ref, ug_ref, att_ref, gates_ref, *rest, nseg, sl, cl, emit_gv):
    tm = nseg * sl
    xn = x_ref[...]

    gates_ref[...] = _sigmoid(_dot(xn, wgate_ref[...])).astype(BF16)

    gvr = jax.nn.gelu(_dot(xn, wgv_ref[...]))
    mu = jnp.mean(gvr, axis=-1, keepdims=True)
    xc = gvr - mu
    r = lax.rsqrt(jnp.mean(xc * xc, axis=-1, keepdims=True) + EPS)
    gv = (xc * r) * lng_ref[...] + lnb_ref[...]
    if emit_gv:
        rest[0][...] = gv
    gvb = gv.astype(BF16)
    u = jax.nn.gelu(_dot(xn, wu_ref[...]))
    tri = (lax.broadcasted_iota(jnp.int32, (cl, cl), 0) >= lax.broadcasted_iota(jnp.int32, (cl, cl), 1))
    for gi in range(GMLP_GROUPS):
        wsg = jnp.where(tri, ws_ref[gi], 0.0).astype(BF16)
        lo, hi = gi * GMLP_GROUP_DIM, (gi + 1) * GMLP_GROUP_DIM
        for c in range(tm // cl):
            sp = _dot(wsg, gvb[c * cl:(c + 1) * cl, lo:hi]) + bst_ref[:, gi:gi + 1]
            ug_ref[c * cl:(c + 1) * cl, lo:hi] = (u[c * cl:(c + 1) * cl, lo:hi] * sp).astype(BF16)

    mq = _dot(xn, wmq_ref[...]).astype(BF16)
    for s in range(nseg):
        r0, r1 = s * sl, (s + 1) * sl
        for h in range(HEADS):
            lo, hi = h * HEAD_DIM, (h + 1) * HEAD_DIM
            sc = _dot_nt(mq[r0:r1, lo:hi], mk_ref[s, :, lo:hi]) * (HEAD_DIM ** -0.5)
            e = jnp.exp(sc - jnp.max(sc, axis=-1, keepdims=True))
            a = (e / jnp.sum(e, axis=-1, keepdims=True)).astype(BF16)
            att_ref[r0:r1, lo:hi] = _dot(a, mv_ref[s, :, lo:hi]).astype(BF16)


def _inproj_b(xn2d, b, l, wu, wgv, wmq, wgate, lng, lnb, ws, bst, mk, mv, cl, emit_gv):
    nseg, sl = _tile_geometry(b, l, INPROJ_B_TILE)
    nt = l // sl
    n = b * l
    tok = functools.partial(_tok_spec, nseg, sl, nt)
    mem = pl.BlockSpec((nseg, N_MEM, D_MODEL), lambda bi, i: (bi, 0, 0))
    out_specs = [tok(D_MODEL), tok(D_MODEL), tok(3 * D_MODEL)]
    out_shape = [jax.ShapeDtypeStruct((n, D_MODEL), BF16), jax.ShapeDtypeStruct((n, D_MODEL), BF16),
                 jax.ShapeDtypeStruct((n, 3 * D_MODEL), BF16)]
    if emit_gv:
        out_specs.append(tok(D_MODEL))
        out_shape.append(jax.ShapeDtypeStruct((n, D_MODEL), F32))
    return pl.pallas_call(
        functools.partial(_inproj_b_kernel, nseg=nseg, sl=sl, cl=cl, emit_gv=emit_gv),
        grid=(b // nseg, nt),
        in_specs=[tok(D_MODEL), _const_spec((D_MODEL, D_MODEL)),
                  _const_spec((D_MODEL, D_MODEL)), _const_spec((D_MODEL, D_MODEL)),
                  _const_spec((D_MODEL, 3 * D_MODEL)), _const_spec((1, D_MODEL)), _const_spec((1, D_MODEL)),
                  _const_spec((GMLP_GROUPS, cl, cl)), _const_spec((cl, GMLP_GROUPS)), mem, mem],
        out_specs=out_specs,
        out_shape=out_shape,
        compiler_params=_cparams(("parallel", "parallel")),
        name="inproj_b",
    )(xn2d, wu, wgv, wmq, wgate, lng, lnb, ws, bst, mk, mv)


def _mlstm_kernel(q_ref, kt_ref, v_ref, gt_ref, *rest, nseg, cb, zero_state):
    i = pl.program_id(1)

    if zero_state:
        hm_ref, c_ref, n_ref, m_ref, st_ref = rest
    else:
        c0_ref, n0_ref, m0_ref, hm_ref, c_ref, n_ref, m_ref, st_ref = rest

    @pl.when(i == 0)
    def _():
        if zero_state:
            st_ref[...] = jnp.zeros(st_ref.shape, F32)
            m_ref[...] = jnp.zeros(m_ref.shape, F32)
        else:
            st_ref[:, :, :, :HEAD_DIM] = c0_ref[...]
            st_ref[:, :, :, HEAD_DIM:] = n0_ref[...]
            m_ref[...] = m0_ref[...]

    for s in range(nseg):
        _mlstm_sequence(q_ref, kt_ref.at[s], v_ref, gt_ref.at[s], st_ref.at[s], m_ref.at[s], hm_ref, s * cb, cb)

    @pl.when(i == pl.num_programs(1) - 1)
    def _():
        c_ref[...] = st_ref[:, :, :, :HEAD_DIM]
        n_ref[...] = st_ref[:, :, :, HEAD_DIM:]


def _mlstm_sequence(q_ref, kt_ref, v_ref, gt_ref, c_ref, m_ref, hm_ref, row0, cb):
    L = kt_ref.shape[-1]
    nch = cb // L
    ti = lax.broadcasted_iota(jnp.int32, (L, L), 0)
    si = lax.broadcasted_iota(jnp.int32, (L, L), 1)
    tri = ti >= si
    eye = ti == si

    rows = 4 * HEADS
    g_all = gt_ref[...].reshape(nch * rows, L)

    m_in = [m_ref[:, 0:1]]
    for c in range(nch):
        b_last4 = g_all[c * rows + HEADS:c * rows + 2 * HEADS, L - 1:L]
        amax4 = g_all[c * rows + 2 * HEADS:c * rows + 3 * HEADS, 0:1]
        m_in.append(jnp.maximum(b_last4 + m_in[-1], b_last4 + amax4))

    ones = jnp.ones((L, LANES), BF16)
    st = [c_ref[h] for h in range(HEADS)]
    for c in range(nch):
        r0, r1 = row0 + c * L, row0 + (c + 1) * L
        for h in range(HEADS):
            lo, hi = h * HEAD_DIM, (h + 1) * HEAD_DIM
            ig_r = g_all[c * rows + h:c * rows + h + 1, :]
            bc_r = g_all[c * rows + HEADS + h:c * rows + HEADS + h + 1, :]
            a_r = ig_r - bc_r
            bc_c = jnp.sum(jnp.where(eye, bc_r, 0.0), axis=-1, keepdims=True)
            m0 = m_in[c][h:h + 1, :]
            m_last = m_in[c + 1][h:h + 1, :]
            dmat = jnp.where(tri, bc_c + a_r, NEG_INF)
            inter = bc_c + m0
            m = jnp.maximum(inter, jnp.max(dmat, axis=-1, keepdims=True))
            w_intra = jnp.exp(dmat - m)
            w_inter = jnp.exp(inter - m)
            q = q_ref[r0:r1, lo:hi]
            kt = kt_ref[c, lo:hi, :]
            v = v_ref[r0:r1, lo:hi]
            s = _dot(q, kt) * w_intra
            qs = _dot(q, st[h].astype(BF16))
            num = w_inter * qs[:, :HEAD_DIM] + _dot(s.astype(BF16), v)
            den = w_inter * qs[:, HEAD_DIM:HEAD_DIM + 1] + jnp.sum(s, axis=-1, keepdims=True)
            hh = num / jnp.maximum(jnp.abs(den), jnp.exp(-m))
            bc_last = bc_r[:, L - 1:L]
            w_last = jnp.exp(bc_last + a_r - m_last)
            decay = jnp.exp(bc_last + m0 - m_last)
            ktw = (kt.astype(F32) * w_last).astype(BF16)
            st[h] = decay * st[h] + _dot(ktw, jnp.concatenate([v, ones], axis=1))
            hm_ref[r0:r1, lo:hi] = hh.astype(BF16)

    for h in range(HEADS):
        c_ref[h] = st[h]
    m_ref[...] = jnp.broadcast_to(m_in[nch], (HEADS, LANES))


def _mlstm(q, kt, v, gt, state, b, l):
    nseg, cb = _tile_geometry(b, l, MLSTM_TILE)
    nt = l // cb
    CHUNK = kt.shape[-1]
    tok = _tok_spec(nseg, cb, nt, D_MODEL)
    cs = pl.BlockSpec((nseg, HEADS, HEAD_DIM, HEAD_DIM), lambda bi, i: (bi, 0, 0, 0))
    ns = pl.BlockSpec((nseg, HEADS, HEAD_DIM, LANES), lambda bi, i: (bi, 0, 0, 0))
    ms = pl.BlockSpec((nseg, HEADS, LANES), lambda bi, i: (bi, 0, 0))
    state_specs = [] if state is None else [cs, ns, ms]
    return pl.pallas_call(
        functools.partial(_mlstm_kernel, nseg=nseg, cb=cb, zero_state=state is None),
        grid=(b // nseg, nt),
        in_specs=[tok, pl.BlockSpec((nseg, cb // CHUNK, D_MODEL, CHUNK), lambda bi, i: (bi, i, 0, 0)), tok,
                  pl.BlockSpec((nseg, cb // CHUNK, 4 * HEADS, CHUNK), lambda bi, i: (bi, i, 0, 0))] + state_specs,
        out_specs=[tok, cs, ns, ms],
        out_shape=[jax.ShapeDtypeStruct((b * l, D_MODEL), BF16),
                   jax.ShapeDtypeStruct((b, HEADS, HEAD_DIM, HEAD_DIM), F32),
                   jax.ShapeDtypeStruct((b, HEADS, HEAD_DIM, LANES), F32),
                   jax.ShapeDtypeStruct((b, HEADS, LANES), F32)],
        scratch_shapes=[pltpu.VMEM((nseg, HEADS, HEAD_DIM, HEAD_DIM + LANES), F32)],
        compiler_params=_cparams(("parallel", "arbitrary")),
        name="mlstm",
    )(q, kt, v, gt, *(state or ()))


def _merge_kernel(x_ref, hh_ref, o_ref, ng_ref, ug_ref, att_ref, gates_ref, wa_ref, wb_ref, wc_ref, wo_ref, h_ref):
    parts = []
    for h in range(HEADS):
        lo, hi = h * HEAD_DIM, (h + 1) * HEAD_DIM
        hh = hh_ref[:, lo:hi].astype(F32)
        hn = hh * lax.rsqrt(jnp.mean(hh * hh, axis=-1, keepdims=True) + EPS)
        parts.append(((hn * ng_ref[:, lo:hi]) * o_ref[:, lo:hi].astype(F32)).astype(BF16))
    br = _dot(parts[0], wa_ref[0:HEAD_DIM, :])
    for h in range(1, HEADS):
        br = br + _dot(parts[h], wa_ref[h * HEAD_DIM:(h + 1) * HEAD_DIM, :])
    g = gates_ref[...].astype(F32)
    mixed = g[:, :D_MODEL] * br
    mixed = mixed + g[:, D_MODEL:2 * D_MODEL] * _dot(ug_ref[...], wb_ref[...])
    mixed = mixed + g[:, 2 * D_MODEL:] * _dot(att_ref[...], wc_ref[...])
    h_ref[...] = x_ref[...] + _dot(mixed.astype(BF16), wo_ref[...])


def _merge(x2d, hh, osig, ng, ug, att, gates, wa, wb, wc, wo, tm):
    n = x2d.shape[0]
    row = lambda w: pl.BlockSpec((tm, w), lambda i: (i, 0))
    wspec = _const_spec((D_MODEL, D_MODEL))
    return pl.pallas_call(
        _merge_kernel,
        grid=(n // tm,),
        in_specs=[row(D_MODEL), row(D_MODEL), row(D_MODEL), _const_spec((1, D_MODEL)), row(D_MODEL), row(D_MODEL),
                  row(3 * D_MODEL), wspec, wspec, wspec, wspec],
        out_specs=row(D_MODEL),
        out_shape=jax.ShapeDtypeStruct((n, D_MODEL), F32),
        compiler_params=_cparams(("parallel",)),
        name="merge",
    )(x2d, hh, osig, ng, ug, att, gates, wa, wb, wc, wo)


MOE_T = 256
MOE_STEP_SUBS = 2
MOE_CAP = 32
MOE_S = N_EXPERTS * MOE_CAP
MOE_OVF = 512
SEG_ALIGN = 16
EXPERT_BLOCK = 128


def _moe_route_kernel(hp_ref, hs_ref, g_ref, wrg_ref, brg_ref, wre_ref, bre_ref,
                      main_ref, ovf_ref, info_ref, meta_ref, *, nsp):
    n = MOE_STEP_SUBS * MOE_T
    h = jnp.where(pl.program_id(0) < nsp // MOE_STEP_SUBS, hp_ref[...], hs_ref[...])
    xm = _rmsnorm(h, g_ref[...]).astype(BF16)
    lane = lax.broadcasted_iota(jnp.int32, (n, LANES), 1).astype(F32)
    lg = jnp.where(lane < N_GROUPS, _dot(xm, wrg_ref[...]) + brg_ref[...], NEG_INF)
    gmax = jnp.max(lg, axis=-1, keepdims=True)
    p_top = 1.0 / jnp.sum(jnp.exp(lg - gmax), axis=-1, keepdims=True)
    grp = jnp.min(jnp.where(lg == gmax, lane, float(LANES)), axis=-1, keepdims=True)
    el = _dot(xm, wre_ref[...]) + bre_ref[...]
    in_grp = (lane >= grp * EXPERTS_PER_GROUP) & (lane < (grp + 1.0) * EXPERTS_PER_GROUP)
    vals = jnp.where(in_grp, el, NEG_INF)
    v1 = jnp.max(vals, axis=-1, keepdims=True)
    i1 = jnp.min(jnp.where(vals == v1, lane, float(LANES)), axis=-1, keepdims=True)
    vals2 = jnp.where(lane == i1, NEG_INF, vals)
    v2 = jnp.max(vals2, axis=-1, keepdims=True)
    i2 = jnp.min(jnp.where(vals2 == v2, lane, float(LANES)), axis=-1, keepdims=True)
    r = jnp.exp(v2 - v1)
    p1 = p_top / (1.0 + r)
    p2 = p_top * r / (1.0 + r)
    sel1 = lane == i1
    sel2 = lane == i2
    onehot = jnp.where(sel1 | sel2, 1.0, 0.0)

    deferred = []
    for s in range(MOE_STEP_SUBS):
        rows = slice(s * MOE_T, (s + 1) * MOE_T)
        deferred.append(_route_sub_tile(
            xm[rows, :], onehot[rows, :], i1[rows, :], i2[rows, :],
            p1[rows, :], p2[rows, :], main_ref.at[:, s], ovf_ref.at[s], info_ref.at[rows, :], meta_ref.at[s]))
    for write_overflow in deferred:
        write_overflow()


def _route_sub_tile(xm, onehot, i1, i2, p1, p2, main_ref, ovf_ref, info_ref, meta_ref):
    t = MOE_T
    lane = lax.broadcasted_iota(jnp.int32, (t, LANES), 1)
    sel1 = lane.astype(F32) == i1
    sel2 = lane.astype(F32) == i2
    cnt = jnp.sum(onehot, axis=0, keepdims=True).astype(jnp.int32)
    pn = jnp.bitwise_and(jnp.maximum(cnt - MOE_CAP, 0) + (SEG_ALIGN - 1), -SEG_ALIGN)
    pn8 = jnp.broadcast_to(pn, (8, LANES))
    earlier = jnp.where(lax.broadcasted_iota(jnp.int32, (LANES, LANES), 0)
                        < lax.broadcasted_iota(jnp.int32, (LANES, LANES), 1), 1.0, 0.0).astype(BF16)
    off_f8 = _dot(pn8.astype(F32).astype(BF16), earlier)
    row8 = lax.broadcasted_iota(jnp.int32, (8, LANES), 0)
    meta_ref[...] = jnp.where(row8 == 0, off_f8.astype(jnp.int32), jnp.where(row8 == 1, pn8, 0))

    ti = lax.broadcasted_iota(jnp.int32, (t, t), 0)
    si = lax.broadcasted_iota(jnp.int32, (t, t), 1)
    before = jnp.where(ti > si, 1.0, 0.0).astype(BF16)
    rank = _dot(before, onehot.astype(BF16))
    off_f = off_f8[0:1, :]

    def slot_row(sel, idx):
        rk = jnp.sum(jnp.where(sel, rank, 0.0), axis=-1, keepdims=True)
        of = jnp.sum(jnp.where(sel, off_f, 0.0), axis=-1, keepdims=True)
        return jnp.where(rk < MOE_CAP, idx * MOE_CAP + rk, MOE_S - MOE_CAP + of + rk)

    pos1 = slot_row(sel1, i1)
    pos2 = slot_row(sel2, i2)
    info_ref[...] = (jnp.where(lane == 0, pos1, 0.0) + jnp.where(lane == 1, pos2, 0.0)
                     + jnp.where(lane == 2, p1, 0.0) + jnp.where(lane == 3, p2, 0.0))

    eye = ti == si
    pos1_r = jnp.sum(jnp.where(eye, pos1, 0.0), axis=0, keepdims=True)
    pos2_r = jnp.sum(jnp.where(eye, pos2, 0.0), axis=0, keepdims=True)

    def gather(rows, base):
        srow = (lax.broadcasted_iota(jnp.int32, (rows, t), 0) + base).astype(F32)
        pick = jnp.where((srow == pos1_r) | (srow == pos2_r), 1.0, 0.0).astype(BF16)
        return _dot(pick, xm).astype(BF16)

    main_ref[...] = gather(MOE_S, 0).reshape(N_EXPERTS, MOE_CAP, D_MODEL)
    has_ovf = jnp.sum(pn) > 0

    def write_overflow():
        @pl.when(has_ovf)
        def _():
            ovf_ref[...] = gather(MOE_OVF, MOE_S)

        @pl.when(jnp.logical_not(has_ovf))
        def _():
            ovf_ref[...] = jnp.zeros(ovf_ref.shape, ovf_ref.dtype)
    return write_overflow


def _two_group_specs(nsp, cols):
    rows = MOE_STEP_SUBS * MOE_T
    npb = nsp // MOE_STEP_SUBS
    return (pl.BlockSpec((rows, cols), lambda j, *_: (jnp.minimum(j, npb - 1), 0)),
            pl.BlockSpec((rows, cols), lambda j, *_: (jnp.maximum(j - npb, 0), 0)))


def _moe_route(hp, hs, g, wrg, brg, wre, bre):
    nsp = hp.shape[0] // MOE_T
    ns = nsp + hs.shape[0] // MOE_T
    sub = MOE_STEP_SUBS
    return pl.pallas_call(
        functools.partial(_moe_route_kernel, nsp=nsp),
        grid=(ns // sub,),
        in_specs=[*_two_group_specs(nsp, D_MODEL), _const_spec((1, D_MODEL)),
                  _const_spec((D_MODEL, LANES)), _const_spec((1, LANES)),
                  _const_spec((D_MODEL, LANES)), _const_spec((1, LANES))],
        out_specs=[pl.BlockSpec((N_EXPERTS, sub, MOE_CAP, D_MODEL), lambda j: (0, j, 0, 0)),
                   pl.BlockSpec((sub, MOE_OVF, D_MODEL), lambda j: (j, 0, 0)),
                   pl.BlockSpec((sub * MOE_T, LANES), lambda j: (j, 0)),
                   pl.BlockSpec((sub, 8, LANES), lambda j: (j, 0, 0))],
        out_shape=[jax.ShapeDtypeStruct((N_EXPERTS, ns, MOE_CAP, D_MODEL), BF16),
                   jax.ShapeDtypeStruct((ns, MOE_OVF, D_MODEL), BF16),
                   jax.ShapeDtypeStruct((ns * MOE_T, LANES), F32),
                   jax.ShapeDtypeStruct((ns, 8, LANES), jnp.int32)],
        compiler_params=_cparams(("arbitrary",)),
        name="moe_route",
    )(hp, hs, g, wrg, brg, wre, bre)


def _swiglu(x, wgu, wdb):
    gu = _dot(x, wgu[...])
    gate = gu[:, :D_EXPERT]
    hid = (gate * _sigmoid(gate)) * gu[:, D_EXPERT:]
    return _dot(hid.astype(BF16), wdb[...]).astype(BF16)


def _moe_expert_kernel(offs_ref, pns_ref, eflag_ref, main_ref, slots_hbm, wg_ref, wu_ref, wd_ref,
                       mout_ref, out_hbm, xbuf, obuf, wgu, wdb, sem_in, sem_out, *, n_sub, jb, n_rb):
    e = pl.program_id(0)
    rb = pl.program_id(1)

    @pl.when((e == 0) & (rb == 0))
    def _():
        xbuf[...] = jnp.zeros_like(xbuf)

    @pl.when(rb == 0)
    def _():
        wgu[:, :D_EXPERT] = wg_ref[...].astype(BF16)
        wgu[:, D_EXPERT:] = wu_ref[...].astype(BF16)
        wdb[...] = wd_ref[...].astype(BF16)

    sb = main_ref.shape[0]
    part = sb // 2 if sb % 2 == 0 else sb
    for s0 in range(0, sb, part):
        y = _swiglu(main_ref[s0:s0 + part].reshape(part * MOE_CAP, D_MODEL), wgu, wdb)
        mout_ref[s0:s0 + part] = y.reshape(part, MOE_CAP, D_MODEL)

    def copy_in(j, src, dst):
        return pltpu.make_async_copy(slots_hbm.at[j, pl.ds(src, SEG_ALIGN), :],
                                     xbuf.at[pl.ds(dst, SEG_ALIGN), :], sem_in)

    def copy_out(j, src, dst):
        return pltpu.make_async_copy(obuf.at[pl.ds(src, SEG_ALIGN), :],
                                     out_hbm.at[j, pl.ds(dst, SEG_ALIGN), :], sem_out)

    def for_each_chunk(g, fn):
        def seg(jj, cur):
            j = g * jb + jj
            n = pns_ref[j * N_EXPERTS + e]
            off = offs_ref[j * N_EXPERTS + e]

            def chunk(k, c):
                fn(j, pl.multiple_of(off + k * SEG_ALIGN, SEG_ALIGN), pl.multiple_of(cur + k * SEG_ALIGN, SEG_ALIGN))
                return c
            lax.fori_loop(0, n // SEG_ALIGN, chunk, 0)
            return cur + n
        return lax.fori_loop(0, jb, seg, 0)

    def group(g, carry):
        total = for_each_chunk(g, lambda j, r, b: copy_in(j, r, b).start())
        nchunk = total // SEG_ALIGN

        def wait_in(k, c):
            copy_in(0, 0, 0).wait()
            return c
        lax.fori_loop(0, nchunk, wait_in, 0)

        def block(bi, c):
            r0 = pl.multiple_of(bi * EXPERT_BLOCK, EXPERT_BLOCK)
            obuf[pl.ds(r0, EXPERT_BLOCK), :] = _swiglu(xbuf[pl.ds(r0, EXPERT_BLOCK), :], wgu, wdb)
            return c
        lax.fori_loop(0, (total + EXPERT_BLOCK - 1) // EXPERT_BLOCK, block, 0)

        for_each_chunk(g, lambda j, r, b: copy_out(j, b, r).start())

        def wait_out(k, c):
            copy_out(0, 0, 0).wait()
            return c
        lax.fori_loop(0, nchunk, wait_out, 0)
        return carry

    @pl.when((rb == n_rb - 1) & (eflag_ref[e] > 0))
    def _():
        lax.fori_loop(0, n_sub // jb, group, 0)


def _moe_expert(main, ovf, offs, pns, eflag, wg, wu, wd):
    ns = main.shape[1]
    n_rb = 1
    sb = ns // n_rb
    jb = max(d for d in range(1, 12) if ns % d == 0)
    rows = jb * MOE_T + EXPERT_BLOCK
    mspec = pl.BlockSpec((None, sb, MOE_CAP, D_MODEL), lambda e, rb, *_: (e, rb, 0, 0))
    wspec = lambda a, b: pl.BlockSpec((None, a, b), lambda e, rb, *_: (e, 0, 0))
    grid_spec = pltpu.PrefetchScalarGridSpec(
        num_scalar_prefetch=3,
        grid=(N_EXPERTS, n_rb),
        in_specs=[mspec, pl.BlockSpec(memory_space=pl.ANY),
                  wspec(D_MODEL, D_EXPERT), wspec(D_MODEL, D_EXPERT), wspec(D_EXPERT, D_MODEL)],
        out_specs=[mspec, pl.BlockSpec(memory_space=pl.ANY)],
        scratch_shapes=[pltpu.VMEM((rows, D_MODEL), BF16), pltpu.VMEM((rows, D_MODEL), BF16),
                        pltpu.VMEM((D_MODEL, 2 * D_EXPERT), BF16), pltpu.VMEM((D_EXPERT, D_MODEL), BF16),
                        pltpu.SemaphoreType.DMA(()), pltpu.SemaphoreType.DMA(())],
    )
    mout, oout = pl.pallas_call(
        functools.partial(_moe_expert_kernel, n_sub=ns, jb=jb, n_rb=n_rb),
        grid_spec=grid_spec,
        out_shape=[jax.ShapeDtypeStruct(main.shape, main.dtype), jax.ShapeDtypeStruct(ovf.shape, ovf.dtype)],
        input_output_aliases={4: 1},
        compiler_params=_cparams(("arbitrary", "arbitrary")),
        name="moe_expert",
    )(offs, pns, eflag, main, ovf, wg, wu, wd)
    return mout, oout


def _moe_combine_kernel(jflag_ref, sflag_ref, hp_ref, hs_ref, main_ref, ovf_ref, info_ref, gf_ref,
                        yp_ref, ys_ref, acc_ref, *, nsp):
    j = pl.program_id(0)
    npb = nsp // MOE_STEP_SUBS

    def weights(s, cols, base):
        info = info_ref[s * MOE_T:(s + 1) * MOE_T, :]
        scol = (lax.broadcasted_iota(jnp.int32, (MOE_T, cols), 1) + base).astype(F32)
        return (jnp.where(scol == info[:, 0:1], info[:, 2:3], 0.0)
                + jnp.where(scol == info[:, 1:2], info[:, 3:4], 0.0)).astype(BF16)

    for s in range(MOE_STEP_SUBS):
        rows = slice(s * MOE_T, (s + 1) * MOE_T)
        h = jnp.where(j < npb, hp_ref[rows, :], hs_ref[rows, :])
        acc_ref[rows, :] = h + _dot(weights(s, MOE_S, 0), main_ref[:, s].reshape(MOE_S, D_MODEL))

    for s in range(MOE_STEP_SUBS):
        @pl.when(jflag_ref[j * MOE_STEP_SUBS + s] > 0)
        def _(s=s):
            rows = slice(s * MOE_T, (s + 1) * MOE_T)
            acc_ref[rows, :] += _dot(weights(s, MOE_OVF, MOE_S), ovf_ref[s])

    y = _rmsnorm(acc_ref[...], gf_ref[...])

    @pl.when(j < npb)
    def _():
        yp_ref[...] = y

    @pl.when(j >= npb)
    def _():
        ys_ref[...] = y


def _moe_combine(hp, hs, main, ovf, info, jflag, gf):
    nsp = hp.shape[0] // MOE_T
    ns = main.shape[1]
    sub = MOE_STEP_SUBS
    sflag = jnp.max(jflag.reshape(ns // sub, sub), axis=1)
    grid_spec = pltpu.PrefetchScalarGridSpec(
        num_scalar_prefetch=2,
        grid=(ns // sub,),
        in_specs=[*_two_group_specs(nsp, D_MODEL),
                  pl.BlockSpec((N_EXPERTS, sub, MOE_CAP, D_MODEL), lambda j, jf, sf: (0, j, 0, 0)),
                  pl.BlockSpec((sub, MOE_OVF, D_MODEL), lambda j, jf, sf: (jnp.where(sf[j] > 0, j, 0), 0, 0)),
                  pl.BlockSpec((sub * MOE_T, LANES), lambda j, jf, sf: (j, 0)),
                  pl.BlockSpec((1, D_MODEL), lambda j, jf, sf: (0, 0))],
        out_specs=list(_two_group_specs(nsp, D_MODEL)),
        scratch_shapes=[pltpu.VMEM((sub * MOE_T, D_MODEL), F32)],
    )
    return pl.pallas_call(
        functools.partial(_moe_combine_kernel, nsp=nsp),
        grid_spec=grid_spec,
        out_shape=[jax.ShapeDtypeStruct(hp.shape, F32), jax.ShapeDtypeStruct(hs.shape, F32)],
        compiler_params=_cparams(("arbitrary",)),
        name="moe_combine",
    )(jflag, sflag, hp, hs, main, ovf, info, gf)


def _moe(hp, hs, p):
    main, ovf, info, meta = _moe_route(hp, hs, p["norm_moe_g"], p["w_rg"], p["b_rg"], p["w_re"], p["b_re"])
    pn = meta[:, 1, :N_EXPERTS]
    offs = meta[:, 0, :N_EXPERTS].reshape(-1)
    eflag = (jnp.sum(pn, axis=0) > 0).astype(jnp.int32)
    jflag = (jnp.sum(pn, axis=1) > 0).astype(jnp.int32)
    main, ovf = _moe_expert(main, ovf, offs, pn.reshape(-1), eflag, p["w_eg"], p["w_eu"], p["w_ed"])
    return _moe_combine(hp, hs, main, ovf, info, jflag, p["norm_final_g"])


def _trunk(x, mem_k, mem_v, c0, n0, m0, conv0, p, *, emit_gv):
    b, l, _ = x.shape
    n = b * l
    cl = min(l, GMLP_CHUNK)
    x2d = x.reshape(n, D_MODEL)
    xn, q, kt, v, osig, gt, conv_new = _inproj_a(
        x2d, b, l, p["norm_mix_g"], p["w_qk"], p["w_v"], p["w_o"], p["w_if"], p["b_if"],
        p["conv_w"], p["conv_b"], conv0)
    outs = _inproj_b(xn, b, l, p["w_u"], p["w_gv"], p["w_mq"], p["w_gate"],
                     p["gmlp_norm_g"], p["gmlp_norm_b"], p["w_s"][:, :cl, :cl], p["b_st"][:cl],
                     mem_k, mem_v, cl, emit_gv)
    ug, att, gates = outs[:3]
    on_lanes = lambda a: jnp.broadcast_to(a[..., None], a.shape + (LANES,))
    state = None if c0 is None else (c0, on_lanes(n0), on_lanes(m0))
    hh, c1, n1, m1 = _mlstm(q, kt, v, gt, state, b, l)
    n1 = n1[..., 0]
    h = _merge(x2d, hh, osig, p["mlstm_norm_g"], ug, att, gates,
               p["w_br_mlstm"], p["w_br_gmlp"], p["w_br_mem"], p["w_out"], TOKEN_TILE)
    return h, c1, n1, m1[..., 0], conv_new, (outs[3].reshape(b, l, D_MODEL) if emit_gv else None)


def kernel(x_prompt, x_sample, mem_prompt, cache_mem_k, cache_mem_v, state_mlstm_C, state_mlstm_n, state_mlstm_m, state_mlstm_conv, norm_mix_g, w_in, mlstm_i_b, mlstm_f_b, mlstm_conv_w, mlstm_conv_b, mlstm_norm_g, gmlp_norm_g, gmlp_norm_b, gmlp_w_s, gmlp_b_s, mem_norm_g, w_mem_k, w_mem_v, w_br_mlstm, w_br_gmlp, w_br_mem, w_out, norm_moe_g, w_router_group, b_router_group, w_router_expert, b_router_expert, w_exp_gate, w_exp_up, w_exp_down, norm_final_g):
    bp = x_prompt.shape[0]
    bs = x_sample.shape[0]
    W = D_MODEL
    wi = w_in[0]
    o_qk, o_v, o_o, o_i = 0, 2 * W, 3 * W, 4 * W
    o_f = o_i + HEADS
    o_u = o_f + HEADS
    o_gv, o_mq, o_gate = o_u + W, o_u + 2 * W, o_u + 3 * W
    row = lambda a: a.reshape(1, -1)
    pad_l = lambda a: jnp.pad(a, ((0, 0), (0, LANES - a.shape[1])))
    p = {
        "norm_mix_g": row(norm_mix_g[0]),
        "w_qk": wi[:, o_qk:o_v].astype(BF16), "w_v": wi[:, o_v:o_o].astype(BF16),
        "w_o": wi[:, o_o:o_i].astype(BF16),
        "w_if": pad_l(wi[:, o_i:o_u]).astype(BF16),
        "b_if": jnp.concatenate([mlstm_i_b[0], mlstm_f_b[0]]).reshape(2 * HEADS, 1),
        "conv_w": mlstm_conv_w[0], "conv_b": row(mlstm_conv_b[0]),
        "w_u": wi[:, o_u:o_gv].astype(BF16), "w_gv": wi[:, o_gv:o_mq].astype(BF16),
        "w_mq": wi[:, o_mq:o_gate].astype(BF16), "w_gate": wi[:, o_gate:].astype(BF16),
        "gmlp_norm_g": row(gmlp_norm_g[0]), "gmlp_norm_b": row(gmlp_norm_b[0]),
        "w_s": gmlp_w_s[0], "b_st": gmlp_b_s[0].T,
        "mlstm_norm_g": row(mlstm_norm_g[0]),
        "w_br_mlstm": w_br_mlstm[0].astype(BF16), "w_br_gmlp": w_br_gmlp[0].astype(BF16),
        "w_br_mem": w_br_mem[0].astype(BF16), "w_out": w_out[0].astype(BF16),
        "norm_moe_g": row(norm_moe_g[0]),
        "w_rg": pad_l(w_router_group[0]).astype(BF16), "b_rg": pad_l(row(b_router_group[0])),
        "w_re": pad_l(w_router_expert[0]).astype(BF16), "b_re": pad_l(row(b_router_expert[0])),
        "w_eg": w_exp_gate[0], "w_eu": w_exp_up[0], "w_ed": w_exp_down[0],
        "norm_final_g": row(norm_final_g),
    }

    mk_p, mv_p, mk_pb, mv_pb = _memory_kv(mem_prompt.reshape(bp * N_MEM, W), row(mem_norm_g[0]),
                                          w_mem_k[0], w_mem_v[0])

    zeros = lambda *s: jnp.zeros(s, F32)
    hp, cp, np_, mp, cvp, _ = _trunk(
        x_prompt, mk_pb.reshape(bp, N_MEM, W), mv_pb.reshape(bp, N_MEM, W),
        None, None, None, zeros(bp, CONV_W - 1, 2 * W), p, emit_gv=False)
    hs, cs, ns, ms, cvs, gvs = _trunk(
        x_sample, cache_mem_k[0].reshape(bs, N_MEM, W).astype(BF16),
        cache_mem_v[0].reshape(bs, N_MEM, W).astype(BF16),
        state_mlstm_C[0], state_mlstm_n[0], state_mlstm_m[0], state_mlstm_conv[0], p,
        emit_gv=True)

    yp, ys = _moe(hp, hs, p)
    kv_shape = (1, bp, N_MEM, HEADS, HEAD_DIM)
    return (yp.reshape(x_prompt.shape), ys.reshape(x_sample.shape), mk_p.reshape(kv_shape), mv_p.reshape(kv_shape),
            cp[None], np_[None], mp[None], cvp[None],
            cs[None], ns[None], ms[None], cvs[None], gvs[None])
```

```python
import functools

import jax
import jax.numpy as jnp
from jax import lax
from jax.experimental import pallas as pl
from jax.experimental.pallas import tpu as pltpu

D_MODEL = 1024
MLSTM_BLOCK = 512
TOKEN_TILE = 512
INPROJ_B_TILE = 1024
MLSTM_TILE = 2048
EPS = 1e-6
HEADS = 4
HEAD_DIM = 256
CONV_W = 4
GMLP_GROUPS = 4
GMLP_GROUP_DIM = 256
GMLP_CHUNK = 128
N_MEM = 256
N_GROUPS = 4
EXPERTS_PER_GROUP = 8
N_EXPERTS = 32
D_EXPERT = 256
LANES = 128
CONV_PAD = 8
NORM_ROWS = 128

F32 = jnp.float32
BF16 = jnp.bfloat16
NEG_INF = float("-inf")

VMEM_LIMIT = 56 * 1024 * 1024


def _cparams(sem):
    return pltpu.CompilerParams(dimension_semantics=sem, vmem_limit_bytes=VMEM_LIMIT)


def _const_spec(shape):
    nd = len(shape)
    return pl.BlockSpec(shape, lambda *_: (0,) * nd, pipeline_mode=pl.Buffered(1))


def _sigmoid(x):
    return 0.5 * (jnp.tanh(0.5 * x) + 1.0)


def _log_sigmoid(x):
    return jnp.minimum(x, 0.0) - jnp.log(1.0 + jnp.exp(-jnp.abs(x)))


def _rmsnorm(x, g):
    r = lax.rsqrt(jnp.mean(x * x, axis=-1, keepdims=True) + EPS)
    return (x * r) * g


def _dot(a, b):
    return jnp.dot(a, b, preferred_element_type=F32)


def _dot_nt(a, b):
    return lax.dot_general(a, b, (((1,), (1,)), ((), ())), preferred_element_type=F32)


def _memkv_kernel(mem_ref, g_ref, wk_ref, wv_ref, k_ref, v_ref, kb_ref, vb_ref):
    mn = _rmsnorm(mem_ref[...], g_ref[...]).astype(BF16)
    k = _dot(mn, wk_ref[...].astype(BF16))
    v = _dot(mn, wv_ref[...].astype(BF16))
    k_ref[...] = k
    v_ref[...] = v
    kb_ref[...] = k.astype(BF16)
    vb_ref[...] = v.astype(BF16)


def _memory_kv(mem2d, g, wk, wv):
    n = mem2d.shape[0]
    tm = TOKEN_TILE
    row = pl.BlockSpec((tm, D_MODEL), lambda i: (i, 0))
    return pl.pallas_call(
        _memkv_kernel,
        grid=(n // tm,),
        in_specs=[row, _const_spec((1, D_MODEL)), _const_spec((D_MODEL, D_MODEL)),
                  _const_spec((D_MODEL, D_MODEL))],
        out_specs=[row, row, row, row],
        out_shape=[jax.ShapeDtypeStruct((n, D_MODEL), F32)] * 2 + [jax.ShapeDtypeStruct((n, D_MODEL), BF16)] * 2,
        compiler_params=_cparams(("parallel",)),
        name="memory_kv",
    )(mem2d, g, wk, wv)


def _inproj_a_kernel(x_ref, g_ref, wqk_ref, wv_ref, wo_ref, wif_ref, bif_ref, cw_ref, cb_ref, cs_ref,
                     xn_ref, q_ref, kt_ref, v_ref, o_ref, gt_ref, cn_ref, ext_ref, *, nseg, sl, lc):
    i = pl.program_id(1)
    tm = nseg * sl
    tail = CONV_PAD - (CONV_W - 1)

    @pl.when(i == 0)
    def _():
        for s in range(nseg):
            ext_ref[s, 0:tail, :] = jnp.zeros((tail, 2 * D_MODEL), F32)
            ext_ref[s, tail:CONV_PAD, :] = cs_ref[s]

    parts = [_rmsnorm(x_ref[r:r + NORM_ROWS, :], g_ref[...]).astype(BF16) for r in range(0, tm, NORM_ROWS)]
    xn = jnp.concatenate(parts, axis=0)
    xn_ref[...] = xn

    zqk = jnp.concatenate([_dot_nt(part, wqk_ref[...]) for part in parts], axis=0)
    ks = []
    row8 = lax.broadcasted_iota(jnp.int32, (CONV_PAD, 2 * D_MODEL), 0)
    for s in range(nseg):
        cur = zqk[s * sl:(s + 1) * sl, :]
        prev = ext_ref[s]
        acc = cb_ref[...] + cur * cw_ref[CONV_W - 1:CONV_W, :]
        for d in range(1, CONV_W):
            back = pltpu.roll(cur, d, axis=0)
            head = jnp.where(row8 < d, pltpu.roll(prev, d, axis=0), back[0:CONV_PAD, :])
            back = jnp.concatenate([head, back[CONV_PAD:, :]], axis=0)
            acc = acc + back * cw_ref[CONV_W - 1 - d:CONV_W - d, :]
        qk = acc * _sigmoid(acc)
        q_ref[s * sl:(s + 1) * sl, :] = (qk[:, :D_MODEL] * (HEAD_DIM ** -0.5)).astype(BF16)
        ks.append(qk[:, D_MODEL:])
        ext_ref[s] = cur[sl - CONV_PAD:sl, :]
        cn_ref[s] = ext_ref[s, tail:CONV_PAD, :]
    if tm % LANES:
        ks.append(jnp.zeros((LANES - tm % LANES, D_MODEL), F32))
    k = jnp.concatenate(ks, axis=0) if len(ks) > 1 else ks[0]
    kt = k.T.astype(BF16)
    nch = tm // lc
    per_seq = sl // lc
    for c in range(nch):
        kt_ref[c // per_seq, c % per_seq] = kt[:, c * lc:(c + 1) * lc]

    v_ref[...] = _dot_nt(xn, wv_ref[...]).astype(BF16)
    o_ref[...] = _sigmoid(_dot_nt(xn, wo_ref[...])).astype(BF16)

    zg = _dot_nt(xn, wif_ref[...])
    if tm % LANES:
        zg = jnp.concatenate([zg, jnp.zeros((LANES - tm % LANES, LANES), F32)], axis=0)
    zt = zg.T[0:2 * HEADS, :]
    z = jnp.concatenate([zt[:, c * lc:(c + 1) * lc] + bif_ref[...] for c in range(nch)], axis=0)
    is_ig = (lax.broadcasted_iota(jnp.int32, z.shape, 0) % (2 * HEADS)) < HEADS
    g = jnp.where(is_ig, z, _log_sigmoid(z))
    upper = jnp.where(lax.broadcasted_iota(jnp.int32, (lc, lc), 0)
                      <= lax.broadcasted_iota(jnp.int32, (lc, lc), 1), 1.0, 0.0)
    bc = jnp.dot(g, upper, preferred_element_type=F32, precision=lax.Precision.HIGHEST)
    a = g - pltpu.roll(bc, nch * 2 * HEADS - HEADS, axis=0)
    amax = jnp.broadcast_to(jnp.max(a, axis=-1, keepdims=True), z.shape)
    gb = jnp.where(is_ig, g, bc)
    for c in range(nch):
        gt_ref[c // per_seq, c % per_seq, 0:2 * HEADS, :] = gb[c * 8:(c + 1) * 8, :]
        gt_ref[c // per_seq, c % per_seq, 2 * HEADS:4 * HEADS, :] = amax[c * 8:(c + 1) * 8, :]


def _tile_geometry(b, l, tile=TOKEN_TILE):
    sl = min(l, tile)
    nseg = max(1, min(b, TOKEN_TILE // sl))
    return nseg, sl


def _tok_spec(nseg, sl, nt, w):
    return pl.BlockSpec((nseg * sl, w), lambda bi, i: (bi * nt + i, 0))


def _inproj_a(x2d, b, l, g, wqk, wv, wo, wif, bif, cw, cb, cs):
    nseg, sl = _tile_geometry(b, l)
    nt = l // sl
    n = b * l
    CHUNK = min(MLSTM_BLOCK, l)
    per_seq = sl // CHUNK
    tok = functools.partial(_tok_spec, nseg, sl, nt)
    state = pl.BlockSpec((nseg, CONV_W - 1, 2 * D_MODEL), lambda bi, i: (bi, 0, 0))
    return pl.pallas_call(
        functools.partial(_inproj_a_kernel, nseg=nseg, sl=sl, lc=CHUNK),
        grid=(b // nseg, nt),
        in_specs=[tok(D_MODEL), _const_spec((1, D_MODEL)), _const_spec((2 * D_MODEL, D_MODEL)),
                  _const_spec((D_MODEL, D_MODEL)), _const_spec((D_MODEL, D_MODEL)),
                  _const_spec((LANES, D_MODEL)), _const_spec((2 * HEADS, 1)),
                  _const_spec((CONV_W, 2 * D_MODEL)), _const_spec((1, 2 * D_MODEL)), state],
        out_specs=[tok(D_MODEL), tok(D_MODEL),
                   pl.BlockSpec((nseg, per_seq, D_MODEL, CHUNK), lambda bi, i: (bi, i, 0, 0)),
                   tok(D_MODEL), tok(D_MODEL),
                   pl.BlockSpec((nseg, per_seq, 4 * HEADS, CHUNK), lambda bi, i: (bi, i, 0, 0)),
                   state],
        out_shape=[jax.ShapeDtypeStruct((n, D_MODEL), BF16), jax.ShapeDtypeStruct((n, D_MODEL), BF16),
                   jax.ShapeDtypeStruct((b, l // CHUNK, D_MODEL, CHUNK), BF16),
                   jax.ShapeDtypeStruct((n, D_MODEL), BF16), jax.ShapeDtypeStruct((n, D_MODEL), BF16),
                   jax.ShapeDtypeStruct((b, l // CHUNK, 4 * HEADS, CHUNK), F32),
                   jax.ShapeDtypeStruct((b, CONV_W - 1, 2 * D_MODEL), F32)],
        scratch_shapes=[pltpu.VMEM((nseg, CONV_PAD, 2 * D_MODEL), F32)],
        compiler_params=_cparams(("parallel", "arbitrary")),
        name="inproj_a",
    )(x2d, g, wqk, wv, wo, wif, bif, cw, cb, cs)


def _inproj_b_kernel(x_ref, wu_ref, wgv_ref, wmq_ref, wgate_ref, lng_ref, lnb_ref, ws_ref, bst_ref,
                     mk_ref, mv_ref, ug_ref, att_ref, gates_ref, *rest, nseg, sl, cl, emit_gv):
    tm = nseg * sl
    xn = x_ref[...]

    gates_ref[...] = _sigmoid(_dot_nt(xn, wgate_ref[...])).astype(BF16)

    gvr = jax.nn.gelu(_dot_nt(xn, wgv_ref[...]))
    mu = jnp.mean(gvr, axis=-1, keepdims=True)
    xc = gvr - mu
    r = lax.rsqrt(jnp.mean(xc * xc, axis=-1, keepdims=True) + EPS)
    gv = (xc * r) * lng_ref[...] + lnb_ref[...]
    if emit_gv:
        rest[0][...] = gv
    gvb = gv.astype(BF16)
    u = jax.nn.gelu(_dot_nt(xn, wu_ref[...]))
    tri = (lax.broadcasted_iota(jnp.int32, (cl, cl), 0) >= lax.broadcasted_iota(jnp.int32, (cl, cl), 1))
    for gi in range(GMLP_GROUPS):
        wsg = jnp.where(tri, ws_ref[gi], 0.0).astype(BF16)
        lo, hi = gi * GMLP_GROUP_DIM, (gi + 1) * GMLP_GROUP_DIM
        for c in range(tm // cl):
            sp = _dot(wsg, gvb[c * cl:(c + 1) * cl, lo:hi]) + bst_ref[:, gi:gi + 1]
            ug_ref[c * cl:(c + 1) * cl, lo:hi] = (u[c * cl:(c + 1) * cl, lo:hi] * sp).astype(BF16)

    mq = _dot_nt(xn, wmq_ref[...]).astype(BF16)
    for s in range(nseg):
        r0, r1 = s * sl, (s + 1) * sl
        for h in range(HEADS):
            lo, hi = h * HEAD_DIM, (h + 1) * HEAD_DIM
            sc = _dot_nt(mq[r0:r1, lo:hi], mk_ref[s, :, lo:hi]) * (HEAD_DIM ** -0.5)
            e = jnp.exp(sc - jnp.max(sc, axis=-1, keepdims=True))
            a = (e / jnp.sum(e, axis=-1, keepdims=True)).astype(BF16)
            att_ref[r0:r1, lo:hi] = _dot(a, mv_ref[s, :, lo:hi]).astype(BF16)


def _inproj_b(xn2d, b, l, wu, wgv, wmq, wgate, lng, lnb, ws, bst, mk, mv, cl, emit_gv):
    nseg, sl = _tile_geometry(b, l, INPROJ_B_TILE)
    nt = l // sl
    n = b * l
    tok = functools.partial(_tok_spec, nseg, sl, nt)
    mem = pl.BlockSpec((nseg, N_MEM, D_MODEL), lambda bi, i: (bi, 0, 0))
    out_specs = [tok(D_MODEL), tok(D_MODEL), tok(3 * D_MODEL)]
    out_shape = [jax.ShapeDtypeStruct((n, D_MODEL), BF16), jax.ShapeDtypeStruct((n, D_MODEL), BF16),
                 jax.ShapeDtypeStruct((n, 3 * D_MODEL), BF16)]
    if emit_gv:
        out_specs.append(tok(D_MODEL))
        out_shape.append(jax.ShapeDtypeStruct((n, D_MODEL), F32))
    return pl.pallas_call(
        functools.partial(_inproj_b_kernel, nseg=nseg, sl=sl, cl=cl, emit_gv=emit_gv),
        grid=(b // nseg, nt),
        in_specs=[tok(D_MODEL), _const_spec((D_MODEL, D_MODEL)),
                  _const_spec((D_MODEL, D_MODEL)), _const_spec((D_MODEL, D_MODEL)),
                  _const_spec((3 * D_MODEL, D_MODEL)), _const_spec((1, D_MODEL)), _const_spec((1, D_MODEL)),
                  _const_spec((GMLP_GROUPS, cl, cl)), _const_spec((cl, GMLP_GROUPS)), mem, mem],
        out_specs=out_specs,
        out_shape=out_shape,
        compiler_params=_cparams(("parallel", "parallel")),
        name="inproj_b",
    )(xn2d, wu, wgv, wmq, wgate, lng, lnb, ws, bst, mk, mv)


def _mlstm_kernel(q_ref, kt_ref, v_ref, gt_ref, *rest, nseg, cb, zero_state):
    i = pl.program_id(1)

    if zero_state:
        hm_ref, c_ref, n_ref, m_ref, st_ref = rest
    else:
        c0_ref, n0_ref, m0_ref, hm_ref, c_ref, n_ref, m_ref, st_ref = rest

    @pl.when(i == 0)
    def _():
        if zero_state:
            st_ref[...] = jnp.zeros(st_ref.shape, F32)
            m_ref[...] = jnp.zeros(m_ref.shape, F32)
        else:
            st_ref[:, :, :, :HEAD_DIM] = c0_ref[...]
            st_ref[:, :, :, HEAD_DIM:] = n0_ref[...]
            m_ref[...] = m0_ref[...]

    for s in range(nseg):
        _mlstm_sequence(q_ref, kt_ref.at[s], v_ref, gt_ref.at[s], st_ref.at[s], m_ref.at[s], hm_ref, s * cb, cb)

    @pl.when(i == pl.num_programs(1) - 1)
    def _():
        c_ref[...] = st_ref[:, :, :, :HEAD_DIM]
        n_ref[...] = st_ref[:, :, :, HEAD_DIM:]


def _mlstm_sequence(q_ref, kt_ref, v_ref, gt_ref, c_ref, m_ref, hm_ref, row0, cb):
    L = kt_ref.shape[-1]
    nch = cb // L
    ti = lax.broadcasted_iota(jnp.int32, (L, L), 0)
    si = lax.broadcasted_iota(jnp.int32, (L, L), 1)
    tri = ti >= si
    eye = ti == si

    rows = 4 * HEADS
    g_all = gt_ref[...].reshape(nch * rows, L)

    m_in = [m_ref[:, 0:1]]
    for c in range(nch):
        b_last4 = g_all[c * rows + HEADS:c * rows + 2 * HEADS, L - 1:L]
        amax4 = g_all[c * rows + 2 * HEADS:c * rows + 3 * HEADS, 0:1]
        m_in.append(jnp.maximum(b_last4 + m_in[-1], b_last4 + amax4))

    ones = jnp.ones((L, LANES), BF16)
    st = [c_ref[h] for h in range(HEADS)]
    for c in range(nch):
        r0, r1 = row0 + c * L, row0 + (c + 1) * L
        for h in range(HEADS):
            lo, hi = h * HEAD_DIM, (h + 1) * HEAD_DIM
            ig_r = g_all[c * rows + h:c * rows + h + 1, :]
            bc_r = g_all[c * rows + HEADS + h:c * rows + HEADS + h + 1, :]
            a_r = ig_r - bc_r
            bc_c = jnp.sum(jnp.where(eye, bc_r, 0.0), axis=-1, keepdims=True)
            m0 = m_in[c][h:h + 1, :]
            m_last = m_in[c + 1][h:h + 1, :]
            dmat = jnp.where(tri, bc_c + a_r, NEG_INF)
            inter = bc_c + m0
            m = jnp.maximum(inter, jnp.max(dmat, axis=-1, keepdims=True))
            w_intra = jnp.exp(dmat - m)
            w_inter = jnp.exp(inter - m)
            q = q_ref[r0:r1, lo:hi]
            kt = kt_ref[c, lo:hi, :]
            v = v_ref[r0:r1, lo:hi]
            s = _dot(q, kt) * w_intra
            qs = _dot(q, st[h].astype(BF16))
            num = w_inter * qs[:, :HEAD_DIM] + _dot(s.astype(BF16), v)
            den = w_inter * qs[:, HEAD_DIM:HEAD_DIM + 1] + jnp.sum(s, axis=-1, keepdims=True)
            hh = num / jnp.maximum(jnp.abs(den), jnp.exp(-m))
            bc_last = bc_r[:, L - 1:L]
            w_last = jnp.exp(bc_last + a_r - m_last)
            decay = jnp.exp(bc_last + m0 - m_last)
            ktw = (kt.astype(F32) * w_last).astype(BF16)
            st[h] = decay * st[h] + _dot(ktw, jnp.concatenate([v, ones], axis=1))
            hm_ref[r0:r1, lo:hi] = hh.astype(BF16)

    for h in range(HEADS):
        c_ref[h] = st[h]
    m_ref[...] = jnp.broadcast_to(m_in[nch], (HEADS, LANES))


def _mlstm(q, kt, v, gt, state, b, l):
    nseg, cb = _tile_geometry(b, l, MLSTM_TILE)
    nt = l // cb
    CHUNK = kt.shape[-1]
    tok = _tok_spec(nseg, cb, nt, D_MODEL)
    cs = pl.BlockSpec((nseg, HEADS, HEAD_DIM, HEAD_DIM), lambda bi, i: (bi, 0, 0, 0))
    ns = pl.BlockSpec((nseg, HEADS, HEAD_DIM, LANES), lambda bi, i: (bi, 0, 0, 0))
    ms = pl.BlockSpec((nseg, HEADS, LANES), lambda bi, i: (bi, 0, 0))
    state_specs = [] if state is None else [cs, ns, ms]
    return pl.pallas_call(
        functools.partial(_mlstm_kernel, nseg=nseg, cb=cb, zero_state=state is None),
        grid=(b // nseg, nt),
        in_specs=[tok, pl.BlockSpec((nseg, cb // CHUNK, D_MODEL, CHUNK), lambda bi, i: (bi, i, 0, 0)), tok,
                  pl.BlockSpec((nseg, cb // CHUNK, 4 * HEADS, CHUNK), lambda bi, i: (bi, i, 0, 0))] + state_specs,
        out_specs=[tok, cs, ns, ms],
        out_shape=[jax.ShapeDtypeStruct((b * l, D_MODEL), BF16),
                   jax.ShapeDtypeStruct((b, HEADS, HEAD_DIM, HEAD_DIM), F32),
                   jax.ShapeDtypeStruct((b, HEADS, HEAD_DIM, LANES), F32),
                   jax.ShapeDtypeStruct((b, HEADS, LANES), F32)],
        scratch_shapes=[pltpu.VMEM((nseg, HEADS, HEAD_DIM, HEAD_DIM + LANES), F32)],
        compiler_params=_cparams(("parallel", "arbitrary")),
        name="mlstm",
    )(q, kt, v, gt, *(state or ()))


def _merge_kernel(x_ref, hh_ref, o_ref, ng_ref, ug_ref, att_ref, gates_ref, wa_ref, wb_ref, wc_ref, wo_ref, h_ref):
    parts = []
    for h in range(HEADS):
        lo, hi = h * HEAD_DIM, (h + 1) * HEAD_DIM
        hh = hh_ref[:, lo:hi].astype(F32)
        hn = hh * lax.rsqrt(jnp.mean(hh * hh, axis=-1, keepdims=True) + EPS)
        parts.append(((hn * ng_ref[:, lo:hi]) * o_ref[:, lo:hi].astype(F32)).astype(BF16))
    br = _dot(parts[0], wa_ref[0:HEAD_DIM, :])
    for h in range(1, HEADS):
        br = br + _dot(parts[h], wa_ref[h * HEAD_DIM:(h + 1) * HEAD_DIM, :])
    g = gates_ref[...].astype(F32)
    mixed = g[:, :D_MODEL] * br
    mixed = mixed + g[:, D_MODEL:2 * D_MODEL] * _dot(ug_ref[...], wb_ref[...])
    mixed = mixed + g[:, 2 * D_MODEL:] * _dot(att_ref[...], wc_ref[...])
    h_ref[...] = x_ref[...] + _dot(mixed.astype(BF16), wo_ref[...])


def _merge(x2d, hh, osig, ng, ug, att, gates, wa, wb, wc, wo, tm):
    n = x2d.shape[0]
    row = lambda w: pl.BlockSpec((tm, w), lambda i: (i, 0))
    wspec = _const_spec((D_MODEL, D_MODEL))
    return pl.pallas_call(
        _merge_kernel,
        grid=(n // tm,),
        in_specs=[row(D_MODEL), row(D_MODEL), row(D_MODEL), _const_spec((1, D_MODEL)), row(D_MODEL), row(D_MODEL),
                  row(3 * D_MODEL), wspec, wspec, wspec, wspec],
        out_specs=row(D_MODEL),
        out_shape=jax.ShapeDtypeStruct((n, D_MODEL), F32),
        compiler_params=_cparams(("parallel",)),
        name="merge",
    )(x2d, hh, osig, ng, ug, att, gates, wa, wb, wc, wo)


MOE_T = 256
MOE_STEP_SUBS = 2
MOE_CAP = 32
MOE_S = N_EXPERTS * MOE_CAP
MOE_OVF = 512
SEG_ALIGN = 16
EXPERT_BLOCK = 128


def _moe_route_kernel(hp_ref, hs_ref, g_ref, wrg_ref, brg_ref, wre_ref, bre_ref,
                      main_ref, ovf_ref, info_ref, meta_ref, *, nsp):
    n = MOE_STEP_SUBS * MOE_T
    h = jnp.where(pl.program_id(0) < nsp // MOE_STEP_SUBS, hp_ref[...], hs_ref[...])
    xm = _rmsnorm(h, g_ref[...]).astype(BF16)
    lane = lax.broadcasted_iota(jnp.int32, (n, LANES), 1).astype(F32)
    lg = jnp.where(lane < N_GROUPS, _dot(xm, wrg_ref[...]) + brg_ref[...], NEG_INF)
    gmax = jnp.max(lg, axis=-1, keepdims=True)
    p_top = 1.0 / jnp.sum(jnp.exp(lg - gmax), axis=-1, keepdims=True)
    grp = jnp.min(jnp.where(lg == gmax, lane, float(LANES)), axis=-1, keepdims=True)
    el = _dot(xm, wre_ref[...]) + bre_ref[...]
    in_grp = (lane >= grp * EXPERTS_PER_GROUP) & (lane < (grp + 1.0) * EXPERTS_PER_GROUP)
    vals = jnp.where(in_grp, el, NEG_INF)
    v1 = jnp.max(vals, axis=-1, keepdims=True)
    i1 = jnp.min(jnp.where(vals == v1, lane, float(LANES)), axis=-1, keepdims=True)
    vals2 = jnp.where(lane == i1, NEG_INF, vals)
    v2 = jnp.max(vals2, axis=-1, keepdims=True)
    i2 = jnp.min(jnp.where(vals2 == v2, lane, float(LANES)), axis=-1, keepdims=True)
    r = jnp.exp(v2 - v1)
    p1 = p_top / (1.0 + r)
    p2 = p_top * r / (1.0 + r)
    sel1 = lane == i1
    sel2 = lane == i2
    onehot = jnp.where(sel1 | sel2, 1.0, 0.0)

    deferred = []
    for s in range(MOE_STEP_SUBS):
        rows = slice(s * MOE_T, (s + 1) * MOE_T)
        deferred.append(_route_sub_tile(
            xm[rows, :], onehot[rows, :], i1[rows, :], i2[rows, :],
            p1[rows, :], p2[rows, :], main_ref.at[:, s], ovf_ref.at[s], info_ref.at[rows, :], meta_ref.at[s]))
    for write_overflow in deferred:
        write_overflow()


def _route_sub_tile(xm, onehot, i1, i2, p1, p2, main_ref, ovf_ref, info_ref, meta_ref):
    t = MOE_T
    lane = lax.broadcasted_iota(jnp.int32, (t, LANES), 1)
    sel1 = lane.astype(F32) == i1
    sel2 = lane.astype(F32) == i2
    cnt = jnp.sum(onehot, axis=0, keepdims=True).astype(jnp.int32)
    pn = jnp.bitwise_and(jnp.maximum(cnt - MOE_CAP, 0) + (SEG_ALIGN - 1), -SEG_ALIGN)
    pn8 = jnp.broadcast_to(pn, (8, LANES))
    earlier = jnp.where(lax.broadcasted_iota(jnp.int32, (LANES, LANES), 0)
                        < lax.broadcasted_iota(jnp.int32, (LANES, LANES), 1), 1.0, 0.0).astype(BF16)
    off_f8 = _dot(pn8.astype(F32).astype(BF16), earlier)
    row8 = lax.broadcasted_iota(jnp.int32, (8, LANES), 0)
    meta_ref[...] = jnp.where(row8 == 0, off_f8.astype(jnp.int32), jnp.where(row8 == 1, pn8, 0))

    ti = lax.broadcasted_iota(jnp.int32, (t, t), 0)
    si = lax.broadcasted_iota(jnp.int32, (t, t), 1)
    before = jnp.where(ti > si, 1.0, 0.0).astype(BF16)
    rank = _dot(before, onehot.astype(BF16))
    off_f = off_f8[0:1, :]

    def slot_row(sel, idx):
        rk = jnp.sum(jnp.where(sel, rank, 0.0), axis=-1, keepdims=True)
        of = jnp.sum(jnp.where(sel, off_f, 0.0), axis=-1, keepdims=True)
        return jnp.where(rk < MOE_CAP, idx * MOE_CAP + rk, MOE_S - MOE_CAP + of + rk)

    pos1 = slot_row(sel1, i1)
    pos2 = slot_row(sel2, i2)
    info_ref[...] = (jnp.where(lane == 0, pos1, 0.0) + jnp.where(lane == 1, pos2, 0.0)
                     + jnp.where(lane == 2, p1, 0.0) + jnp.where(lane == 3, p2, 0.0))

    eye = ti == si
    pos1_r = jnp.sum(jnp.where(eye, pos1, 0.0), axis=0, keepdims=True)
    pos2_r = jnp.sum(jnp.where(eye, pos2, 0.0), axis=0, keepdims=True)

    def gather(rows, base):
        srow = (lax.broadcasted_iota(jnp.int32, (rows, t), 0) + base).astype(F32)
        pick = jnp.where((srow == pos1_r) | (srow == pos2_r), 1.0, 0.0).astype(BF16)
        return _dot(pick, xm).astype(BF16)

    main_ref[...] = gather(MOE_S, 0).reshape(N_EXPERTS, MOE_CAP, D_MODEL)
    has_ovf = jnp.sum(pn) > 0

    def write_overflow():
        @pl.when(has_ovf)
        def _():
            ovf_ref[...] = gather(MOE_OVF, MOE_S)

        @pl.when(jnp.logical_not(has_ovf))
        def _():
            ovf_ref[...] = jnp.zeros(ovf_ref.shape, ovf_ref.dtype)
    return write_overflow


def _two_group_specs(nsp, cols):
    rows = MOE_STEP_SUBS * MOE_T
    npb = nsp // MOE_STEP_SUBS
    return (pl.BlockSpec((rows, cols), lambda j, *_: (jnp.minimum(j, npb - 1), 0)),
            pl.BlockSpec((rows, cols), lambda j, *_: (jnp.maximum(j - npb, 0), 0)))


def _moe_route(hp, hs, g, wrg, brg, wre, bre):
    nsp = hp.shape[0] // MOE_T
    ns = nsp + hs.shape[0] // MOE_T
    sub = MOE_STEP_SUBS
    return pl.pallas_call(
        functools.partial(_moe_route_kernel, nsp=nsp),
        grid=(ns // sub,),
        in_specs=[*_two_group_specs(nsp, D_MODEL), _const_spec((1, D_MODEL)),
                  _const_spec((D_MODEL, LANES)), _const_spec((1, LANES)),
                  _const_spec((D_MODEL, LANES)), _const_spec((1, LANES))],
        out_specs=[pl.BlockSpec((N_EXPERTS, sub, MOE_CAP, D_MODEL), lambda j: (0, j, 0, 0)),
                   pl.BlockSpec((sub, MOE_OVF, D_MODEL), lambda j: (j, 0, 0)),
                   pl.BlockSpec((sub * MOE_T, LANES), lambda j: (j, 0)),
                   pl.BlockSpec((sub, 8, LANES), lambda j: (j, 0, 0))],
        out_shape=[jax.ShapeDtypeStruct((N_EXPERTS, ns, MOE_CAP, D_MODEL), BF16),
                   jax.ShapeDtypeStruct((ns, MOE_OVF, D_MODEL), BF16),
                   jax.ShapeDtypeStruct((ns * MOE_T, LANES), F32),
                   jax.ShapeDtypeStruct((ns, 8, LANES), jnp.int32)],
        compiler_params=_cparams(("arbitrary",)),
        name="moe_route",
    )(hp, hs, g, wrg, brg, wre, bre)


def _swiglu(x, wgu, wdb):
    gu = _dot(x, wgu[...])
    gate = gu[:, :D_EXPERT]
    hid = (gate * _sigmoid(gate)) * gu[:, D_EXPERT:]
    return _dot(hid.astype(BF16), wdb[...]).astype(BF16)


def _moe_expert_kernel(offs_ref, pns_ref, eflag_ref, main_ref, slots_hbm, wg_ref, wu_ref, wd_ref,
                       mout_ref, out_hbm, xbuf, obuf, wgu, wdb, sem_in, sem_out, *, n_sub, jb, n_rb):
    e = pl.program_id(0)
    rb = pl.program_id(1)

    @pl.when((e == 0) & (rb == 0))
    def _():
        xbuf[...] = jnp.zeros_like(xbuf)

    @pl.when(rb == 0)
    def _():
        wgu[:, :D_EXPERT] = wg_ref[...].astype(BF16)
        wgu[:, D_EXPERT:] = wu_ref[...].astype(BF16)
        wdb[...] = wd_ref[...].astype(BF16)

    sb = main_ref.shape[0]
    part = sb // 2 if sb % 2 == 0 else sb
    for s0 in range(0, sb, part):
        y = _swiglu(main_ref[s0:s0 + part].reshape(part * MOE_CAP, D_MODEL), wgu, wdb)
        mout_ref[s0:s0 + part] = y.reshape(part, MOE_CAP, D_MODEL)

    def copy_in(j, src, dst):
        return pltpu.make_async_copy(slots_hbm.at[j, pl.ds(src, SEG_ALIGN), :],
                                     xbuf.at[pl.ds(dst, SEG_ALIGN), :], sem_in)

    def copy_out(j, src, dst):
        return pltpu.make_async_copy(obuf.at[pl.ds(src, SEG_ALIGN), :],
                                     out_hbm.at[j, pl.ds(dst, SEG_ALIGN), :], sem_out)

    def for_each_chunk(g, fn):
        def seg(jj, cur):
            j = g * jb + jj
            n = pns_ref[j * N_EXPERTS + e]
            off = offs_ref[j * N_EXPERTS + e]

            def chunk(k, c):
                fn(j, pl.multiple_of(off + k * SEG_ALIGN, SEG_ALIGN), pl.multiple_of(cur + k * SEG_ALIGN, SEG_ALIGN))
                return c
            lax.fori_loop(0, n // SEG_ALIGN, chunk, 0)
            return cur + n
        return lax.fori_loop(0, jb, seg, 0)

    def group(g, carry):
        total = for_each_chunk(g, lambda j, r, b: copy_in(j, r, b).start())
        nchunk = total // SEG_ALIGN

        def wait_in(k, c):
            copy_in(0, 0, 0).wait()
            return c
        lax.fori_loop(0, nchunk, wait_in, 0)

        def block(bi, c):
            r0 = pl.multiple_of(bi * EXPERT_BLOCK, EXPERT_BLOCK)
            obuf[pl.ds(r0, EXPERT_BLOCK), :] = _swiglu(xbuf[pl.ds(r0, EXPERT_BLOCK), :], wgu, wdb)
            return c
        lax.fori_loop(0, (total + EXPERT_BLOCK - 1) // EXPERT_BLOCK, block, 0)

        for_each_chunk(g, lambda j, r, b: copy_out(j, b, r).start())

        def wait_out(k, c):
            copy_out(0, 0, 0).wait()
            return c
        lax.fori_loop(0, nchunk, wait_out, 0)
        return carry

    @pl.when((rb == n_rb - 1) & (eflag_ref[e] > 0))
    def _():
        lax.fori_loop(0, n_sub // jb, group, 0)


def _moe_expert(main, ovf, offs, pns, eflag, wg, wu, wd):
    ns = main.shape[1]
    n_rb = 1
    sb = ns // n_rb
    jb = max(d for d in range(1, 12) if ns % d == 0)
    rows = jb * MOE_T + EXPERT_BLOCK
    mspec = pl.BlockSpec((None, sb, MOE_CAP, D_MODEL), lambda e, rb, *_: (e, rb, 0, 0))
    wspec = lambda a, b: pl.BlockSpec((None, a, b), lambda e, rb, *_: (e, 0, 0))
    grid_spec = pltpu.PrefetchScalarGridSpec(
        num_scalar_prefetch=3,
        grid=(N_EXPERTS, n_rb),
        in_specs=[mspec, pl.BlockSpec(memory_space=pl.ANY),
                  wspec(D_MODEL, D_EXPERT), wspec(D_MODEL, D_EXPERT), wspec(D_EXPERT, D_MODEL)],
        out_specs=[mspec, pl.BlockSpec(memory_space=pl.ANY)],
        scratch_shapes=[pltpu.VMEM((rows, D_MODEL), BF16), pltpu.VMEM((rows, D_MODEL), BF16),
                        pltpu.VMEM((D_MODEL, 2 * D_EXPERT), BF16), pltpu.VMEM((D_EXPERT, D_MODEL), BF16),
                        pltpu.SemaphoreType.DMA(()), pltpu.SemaphoreType.DMA(())],
    )
    mout, oout = pl.pallas_call(
        functools.partial(_moe_expert_kernel, n_sub=ns, jb=jb, n_rb=n_rb),
        grid_spec=grid_spec,
        out_shape=[jax.ShapeDtypeStruct(main.shape, main.dtype), jax.ShapeDtypeStruct(ovf.shape, ovf.dtype)],
        input_output_aliases={4: 1},
        compiler_params=_cparams(("arbitrary", "arbitrary")),
        name="moe_expert",
    )(offs, pns, eflag, main, ovf, wg, wu, wd)
    return mout, oout


def _moe_combine_kernel(jflag_ref, sflag_ref, hp_ref, hs_ref, main_ref, ovf_ref, info_ref, gf_ref,
                        yp_ref, ys_ref, acc_ref, *, nsp):
    j = pl.program_id(0)
    npb = nsp // MOE_STEP_SUBS

    def weights(s, cols, base):
        info = info_ref[s * MOE_T:(s + 1) * MOE_T, :]
        scol = (lax.broadcasted_iota(jnp.int32, (MOE_T, cols), 1) + base).astype(F32)
        return (jnp.where(scol == info[:, 0:1], info[:, 2:3], 0.0)
                + jnp.where(scol == info[:, 1:2], info[:, 3:4], 0.0)).astype(BF16)

    for s in range(MOE_STEP_SUBS):
        rows = slice(s * MOE_T, (s + 1) * MOE_T)
        h = jnp.where(j < npb, hp_ref[rows, :], hs_ref[rows, :])
        acc_ref[rows, :] = h + _dot(weights(s, MOE_S, 0), main_ref[:, s].reshape(MOE_S, D_MODEL))

    for s in range(MOE_STEP_SUBS):
        @pl.when(jflag_ref[j * MOE_STEP_SUBS + s] > 0)
        def _(s=s):
            rows = slice(s * MOE_T, (s + 1) * MOE_T)
            acc_ref[rows, :] += _dot(weights(s, MOE_OVF, MOE_S), ovf_ref[s])

    y = _rmsnorm(acc_ref[...], gf_ref[...])

    @pl.when(j < npb)
    def _():
        yp_ref[...] = y

    @pl.when(j >= npb)
    def _():
        ys_ref[...] = y


def _moe_combine(hp, hs, main, ovf, info, jflag, gf):
    nsp = hp.shape[0] // MOE_T
    ns = main.shape[1]
    sub = MOE_STEP_SUBS
    sflag = jnp.max(jflag.reshape(ns // sub, sub), axis=1)
    grid_spec = pltpu.PrefetchScalarGridSpec(
        num_scalar_prefetch=2,
        grid=(ns // sub,),
        in_specs=[*_two_group_specs(nsp, D_MODEL),
                  pl.BlockSpec((N_EXPERTS, sub, MOE_CAP, D_MODEL), lambda j, jf, sf: (0, j, 0, 0)),
                  pl.BlockSpec((sub, MOE_OVF, D_MODEL), lambda j, jf, sf: (jnp.where(sf[j] > 0, j, 0), 0, 0)),
                  pl.BlockSpec((sub * MOE_T, LANES), lambda j, jf, sf: (j, 0)),
                  pl.BlockSpec((1, D_MODEL), lambda j, jf, sf: (0, 0))],
        out_specs=list(_two_group_specs(nsp, D_MODEL)),
        scratch_shapes=[pltpu.VMEM((sub * MOE_T, D_MODEL), F32)],
    )
    return pl.pallas_call(
        functools.partial(_moe_combine_kernel, nsp=nsp),
        grid_spec=grid_spec,
        out_shape=[jax.ShapeDtypeStruct(hp.shape, F32), jax.ShapeDtypeStruct(hs.shape, F32)],
        compiler_params=_cparams(("arbitrary",)),
        name="moe_combine",
    )(jflag, sflag, hp, hs, main, ovf, info, gf)


def _moe(hp, hs, p):
    main, ovf, info, meta = _moe_route(hp, hs, p["norm_moe_g"], p["w_rg"], p["b_rg"], p["w_re"], p["b_re"])
    pn = meta[:, 1, :N_EXPERTS]
    offs = meta[:, 0, :N_EXPERTS].reshape(-1)
    eflag = (jnp.sum(pn, axis=0) > 0).astype(jnp.int32)
    jflag = (jnp.sum(pn, axis=1) > 0).astype(jnp.int32)
    main, ovf = _moe_expert(main, ovf, offs, pn.reshape(-1), eflag, p["w_eg"], p["w_eu"], p["w_ed"])
    return _moe_combine(hp, hs, main, ovf, info, jflag, p["norm_final_g"])


def _trunk(x, mem_k, mem_v, c0, n0, m0, conv0, p, *, emit_gv):
    b, l, _ = x.shape
    n = b * l
    cl = min(l, GMLP_CHUNK)
    x2d = x.reshape(n, D_MODEL)
    xn, q, kt, v, osig, gt, conv_new = _inproj_a(
        x2d, b, l, p["norm_mix_g"], p["w_qk"], p["w_v"], p["w_o"], p["w_if"], p["b_if"],
        p["conv_w"], p["conv_b"], conv0)
    outs = _inproj_b(xn, b, l, p["w_u"], p["w_gv"], p["w_mq"], p["w_gate"],
                     p["gmlp_norm_g"], p["gmlp_norm_b"], p["w_s"][:, :cl, :cl], p["b_st"][:cl],
                     mem_k, mem_v, cl, emit_gv)
    ug, att, gates = outs[:3]
    on_lanes = lambda a: jnp.broadcast_to(a[..., None], a.shape + (LANES,))
    state = None if c0 is None else (c0, on_lanes(n0), on_lanes(m0))
    hh, c1, n1, m1 = _mlstm(q, kt, v, gt, state, b, l)
    n1 = n1[..., 0]
    h = _merge(x2d, hh, osig, p["mlstm_norm_g"], ug, att, gates,
               p["w_br_mlstm"], p["w_br_gmlp"], p["w_br_mem"], p["w_out"], TOKEN_TILE)
    return h, c1, n1, m1[..., 0], conv_new, (outs[3].reshape(b, l, D_MODEL) if emit_gv else None)


def kernel(x_prompt, x_sample, mem_prompt, cache_mem_k, cache_mem_v, state_mlstm_C, state_mlstm_n, state_mlstm_m, state_mlstm_conv, norm_mix_g, w_in, mlstm_i_b, mlstm_f_b, mlstm_conv_w, mlstm_conv_b, mlstm_norm_g, gmlp_norm_g, gmlp_norm_b, gmlp_w_s, gmlp_b_s, mem_norm_g, w_mem_k, w_mem_v, w_br_mlstm, w_br_gmlp, w_br_mem, w_out, norm_moe_g, w_router_group, b_router_group, w_router_expert, b_router_expert, w_exp_gate, w_exp_up, w_exp_down, norm_final_g):
    bp = x_prompt.shape[0]
    bs = x_sample.shape[0]
    W = D_MODEL
    wt = w_in[0].T
    o_qk, o_v, o_o, o_i = 0, 2 * W, 3 * W, 4 * W
    o_f = o_i + HEADS
    o_u = o_f + HEADS
    o_gv, o_mq, o_gate = o_u + W, o_u + 2 * W, o_u + 3 * W
    row = lambda a: a.reshape(1, -1)
    pad_l = lambda a: jnp.pad(a, ((0, 0), (0, LANES - a.shape[1])))
    p = {
        "norm_mix_g": row(norm_mix_g[0]),
        "w_qk": wt[o_qk:o_v].astype(BF16), "w_v": wt[o_v:o_o].astype(BF16),
        "w_o": wt[o_o:o_i].astype(BF16),
        "w_if": jnp.pad(wt[o_i:o_u], ((0, LANES - 2 * HEADS), (0, 0))).astype(BF16),
        "b_if": jnp.concatenate([mlstm_i_b[0], mlstm_f_b[0]]).reshape(2 * HEADS, 1),
        "conv_w": mlstm_conv_w[0], "conv_b": row(mlstm_conv_b[0]),
        "w_u": wt[o_u:o_gv].astype(BF16), "w_gv": wt[o_gv:o_mq].astype(BF16),
        "w_mq": wt[o_mq:o_gate].astype(BF16), "w_gate": wt[o_gate:].astype(BF16),
        "gmlp_norm_g": row(gmlp_norm_g[0]), "gmlp_norm_b": row(gmlp_norm_b[0]),
        "w_s": gmlp_w_s[0], "b_st": gmlp_b_s[0].T,
        "mlstm_norm_g": row(mlstm_norm_g[0]),
        "w_br_mlstm": w_br_mlstm[0].astype(BF16), "w_br_gmlp": w_br_gmlp[0].astype(BF16),
        "w_br_mem": w_br_mem[0].astype(BF16), "w_out": w_out[0].astype(BF16),
        "norm_moe_g": row(norm_moe_g[0]),
        "w_rg": pad_l(w_router_group[0]).astype(BF16), "b_rg": pad_l(row(b_router_group[0])),
        "w_re": pad_l(w_router_expert[0]).astype(BF16), "b_re": pad_l(row(b_router_expert[0])),
        "w_eg": w_exp_gate[0], "w_eu": w_exp_up[0], "w_ed": w_exp_down[0],
        "norm_final_g": row(norm_final_g),
    }

    mk_p, mv_p, mk_pb, mv_pb = _memory_kv(mem_prompt.reshape(bp * N_MEM, W), row(mem_norm_g[0]),
                                          w_mem_k[0], w_mem_v[0])

    zeros = lambda *s: jnp.zeros(s, F32)
    hp, cp, np_, mp, cvp, _ = _trunk(
        x_prompt, mk_pb.reshape(bp, N_MEM, W), mv_pb.reshape(bp, N_MEM, W),
        None, None, None, zeros(bp, CONV_W - 1, 2 * W), p, emit_gv=False)
    hs, cs, ns, ms, cvs, gvs = _trunk(
        x_sample, cache_mem_k[0].astype(BF16).reshape(bs, N_MEM, W),
        cache_mem_v[0].astype(BF16).reshape(bs, N_MEM, W),
        state_mlstm_C[0], state_mlstm_n[0], state_mlstm_m[0], state_mlstm_conv[0], p,
        emit_gv=True)

    yp, ys = _moe(hp, hs, p)
    kv_shape = (1, bp, N_MEM, HEADS, HEAD_DIM)
    return (yp.reshape(x_prompt.shape), ys.reshape(x_sample.shape), mk_p.reshape(kv_shape), mv_p.reshape(kv_shape),
            cp[None], np_[None], mp[None], cvp[None],
            cs[None], ns[None], ms[None], cvs[None], gvs[None])
```

```python
import functools

import jax
import jax.numpy as jnp
from jax import lax
from jax.experimental import pallas as pl
from jax.experimental.pallas import tpu as pltpu

D_MODEL = 1024
MLSTM_BLOCK = 512
TOKEN_TILE = 512
INPROJ_B_TILE = 1024
MLSTM_TILE = 2048
EPS = 1e-6
HEADS = 4
HEAD_DIM = 256
CONV_W = 4
GMLP_GROUPS = 4
GMLP_GROUP_DIM = 256
GMLP_CHUNK = 128
N_MEM = 256
N_GROUPS = 4
EXPERTS_PER_GROUP = 8
N_EXPERTS = 32
D_EXPERT = 256
LANES = 128
CONV_PAD = 8
NORM_ROWS = 128

F32 = jnp.float32
BF16 = jnp.bfloat16
NEG_INF = float("-inf")

VMEM_LIMIT = 56 * 1024 * 1024


def _cparams(sem):
    return pltpu.CompilerParams(dimension_semantics=sem, vmem_limit_bytes=VMEM_LIMIT)


def _const_spec(shape):
    nd = len(shape)
    return pl.BlockSpec(shape, lambda *_: (0,) * nd, pipeline_mode=pl.Buffered(1))


def _const_block(shape, idx):
    return pl.BlockSpec(shape, lambda *_: idx, pipeline_mode=pl.Buffered(1))


def _sigmoid(x):
    return 0.5 * (jnp.tanh(0.5 * x) + 1.0)


def _log_sigmoid(x):
    return jnp.minimum(x, 0.0) - jnp.log(1.0 + jnp.exp(-jnp.abs(x)))


def _rmsnorm(x, g):
    r = lax.rsqrt(jnp.mean(x * x, axis=-1, keepdims=True) + EPS)
    return (x * r) * g


def _dot(a, b):
    return jnp.dot(a, b, preferred_element_type=F32)


def _dot_nt(a, b):
    return lax.dot_general(a, b, (((1,), (1,)), ((), ())), preferred_element_type=F32)


def _memkv_kernel(mem_ref, g_ref, wk_ref, wv_ref, k_ref, v_ref, kb_ref, vb_ref):
    mn = _rmsnorm(mem_ref[...], g_ref[...]).astype(BF16)
    k = _dot(mn, wk_ref[...].astype(BF16))
    v = _dot(mn, wv_ref[...].astype(BF16))
    k_ref[...] = k
    v_ref[...] = v
    kb_ref[...] = k.astype(BF16)
    vb_ref[...] = v.astype(BF16)


def _memory_kv(mem2d, g, wk, wv):
    n = mem2d.shape[0]
    tm = TOKEN_TILE
    row = pl.BlockSpec((tm, D_MODEL), lambda i: (i, 0))
    return pl.pallas_call(
        _memkv_kernel,
        grid=(n // tm,),
        in_specs=[row, _const_spec((1, D_MODEL)), _const_spec((D_MODEL, D_MODEL)),
                  _const_spec((D_MODEL, D_MODEL))],
        out_specs=[row, row, row, row],
        out_shape=[jax.ShapeDtypeStruct((n, D_MODEL), F32)] * 2 + [jax.ShapeDtypeStruct((n, D_MODEL), BF16)] * 2,
        compiler_params=_cparams(("parallel",)),
        name="memory_kv",
    )(mem2d, g, wk, wv)


W_COLS_BLOCK = 1408


def _w_cols_kernel(w_ref, o_ref):
    o_ref[...] = w_ref[...].T.astype(BF16)


def _w_cols(wt, ncols):
    rb = W_COLS_BLOCK
    return pl.pallas_call(
        _w_cols_kernel,
        grid=(ncols // rb,),
        in_specs=[pl.BlockSpec((rb, D_MODEL), lambda j: (j, 0))],
        out_specs=pl.BlockSpec((D_MODEL, rb), lambda j: (0, j)),
        out_shape=jax.ShapeDtypeStruct((D_MODEL, ncols), BF16),
        compiler_params=_cparams(("parallel",)),
        name="w_cols",
    )(wt)


def _w_rows_kernel(w_hbm, o_ref, buf, sem, *, row0, nblk):
    i = pl.program_id(0)

    def fetch(blk, slot):
        start = pl.multiple_of(row0 + blk * D_MODEL, 8)
        return pltpu.make_async_copy(w_hbm.at[pl.ds(start, D_MODEL), :], buf.at[slot], sem.at[slot])

    @pl.when(i == 0)
    def _():
        fetch(0, 0).start()

    @pl.when(i + 1 < nblk)
    def _():
        fetch(i + 1, (i + 1) % 2).start()

    fetch(i, i % 2).wait()
    o_ref[...] = buf[i % 2].astype(BF16)


def _w_rows(wt, row0, nblk):
    return pl.pallas_call(
        functools.partial(_w_rows_kernel, row0=row0, nblk=nblk),
        grid=(nblk,),
        in_specs=[pl.BlockSpec(memory_space=pl.ANY)],
        out_specs=pl.BlockSpec((D_MODEL, D_MODEL), lambda i: (i, 0)),
        out_shape=jax.ShapeDtypeStruct((nblk * D_MODEL, D_MODEL), BF16),
        scratch_shapes=[pltpu.VMEM((2, D_MODEL, D_MODEL), F32), pltpu.SemaphoreType.DMA((2,))],
        compiler_params=_cparams(("arbitrary",)),
        name="w_rows",
    )(wt)


def _inproj_a_kernel(x_ref, g_ref, wqk_ref, wv_ref, wo_ref, wif_ref, bif_ref, cw_ref, cb_ref, cs_ref,
                     xn_ref, q_ref, kt_ref, v_ref, o_ref, gt_ref, cn_ref, ext_ref, *, nseg, sl, lc):
    i = pl.program_id(1)
    tm = nseg * sl
    tail = CONV_PAD - (CONV_W - 1)

    @pl.when(i == 0)
    def _():
        for s in range(nseg):
            ext_ref[s, 0:tail, :] = jnp.zeros((tail, 2 * D_MODEL), F32)
            ext_ref[s, tail:CONV_PAD, :] = cs_ref[s]

    parts = [_rmsnorm(x_ref[r:r + NORM_ROWS, :], g_ref[...]).astype(BF16) for r in range(0, tm, NORM_ROWS)]
    xn = jnp.concatenate(parts, axis=0)
    xn_ref[...] = xn

    zqk = jnp.concatenate([_dot(part, wqk_ref[...]) for part in parts], axis=0)
    ks = []
    row8 = lax.broadcasted_iota(jnp.int32, (CONV_PAD, 2 * D_MODEL), 0)
    for s in range(nseg):
        cur = zqk[s * sl:(s + 1) * sl, :]
        prev = ext_ref[s]
        acc = cb_ref[...] + cur * cw_ref[CONV_W - 1:CONV_W, :]
        for d in range(1, CONV_W):
            back = pltpu.roll(cur, d, axis=0)
            head = jnp.where(row8 < d, pltpu.roll(prev, d, axis=0), back[0:CONV_PAD, :])
            back = jnp.concatenate([head, back[CONV_PAD:, :]], axis=0)
            acc = acc + back * cw_ref[CONV_W - 1 - d:CONV_W - d, :]
        qk = acc * _sigmoid(acc)
        q_ref[s * sl:(s + 1) * sl, :] = (qk[:, :D_MODEL] * (HEAD_DIM ** -0.5)).astype(BF16)
        ks.append(qk[:, D_MODEL:])
        ext_ref[s] = cur[sl - CONV_PAD:sl, :]
        cn_ref[s] = ext_ref[s, tail:CONV_PAD, :]
    if tm % LANES:
        ks.append(jnp.zeros((LANES - tm % LANES, D_MODEL), F32))
    k = jnp.concatenate(ks, axis=0) if len(ks) > 1 else ks[0]
    kt = k.T.astype(BF16)
    nch = tm // lc
    per_seq = sl // lc
    for c in range(nch):
        kt_ref[c // per_seq, c % per_seq] = kt[:, c * lc:(c + 1) * lc]

    v_ref[...] = _dot(xn, wv_ref[...]).astype(BF16)
    o_ref[...] = _sigmoid(_dot(xn, wo_ref[...])).astype(BF16)

    zg = _dot(xn, wif_ref[...])
    if tm % LANES:
        zg = jnp.concatenate([zg, jnp.zeros((LANES - tm % LANES, LANES), F32)], axis=0)
    zt = zg.T[0:2 * HEADS, :]
    z = jnp.concatenate([zt[:, c * lc:(c + 1) * lc] + bif_ref[...] for c in range(nch)], axis=0)
    is_ig = (lax.broadcasted_iota(jnp.int32, z.shape, 0) % (2 * HEADS)) < HEADS
    g = jnp.where(is_ig, z, _log_sigmoid(z))
    upper = jnp.where(lax.broadcasted_iota(jnp.int32, (lc, lc), 0)
                      <= lax.broadcasted_iota(jnp.int32, (lc, lc), 1), 1.0, 0.0)
    bc = jnp.dot(g, upper, preferred_element_type=F32, precision=lax.Precision.HIGHEST)
    a = g - pltpu.roll(bc, nch * 2 * HEADS - HEADS, axis=0)
    amax = jnp.broadcast_to(jnp.max(a, axis=-1, keepdims=True), z.shape)
    gb = jnp.where(is_ig, g, bc)
    for c in range(nch):
        gt_ref[c // per_seq, c % per_seq, 0:2 * HEADS, :] = gb[c * 8:(c + 1) * 8, :]
        gt_ref[c // per_seq, c % per_seq, 2 * HEADS:4 * HEADS, :] = amax[c * 8:(c + 1) * 8, :]


def _tile_geometry(b, l, tile=TOKEN_TILE):
    sl = min(l, tile)
    nseg = max(1, min(b, TOKEN_TILE // sl))
    return nseg, sl


def _tok_spec(nseg, sl, nt, w):
    return pl.BlockSpec((nseg * sl, w), lambda bi, i: (bi * nt + i, 0))


def _inproj_a(x2d, b, l, g, wa, bif, cw, cb, cs):
    nseg, sl = _tile_geometry(b, l)
    nt = l // sl
    n = b * l
    CHUNK = min(MLSTM_BLOCK, l)
    per_seq = sl // CHUNK
    tok = functools.partial(_tok_spec, nseg, sl, nt)
    state = pl.BlockSpec((nseg, CONV_W - 1, 2 * D_MODEL), lambda bi, i: (bi, 0, 0))
    return pl.pallas_call(
        functools.partial(_inproj_a_kernel, nseg=nseg, sl=sl, lc=CHUNK),
        grid=(b // nseg, nt),
        in_specs=[tok(D_MODEL), _const_spec((1, D_MODEL)), _const_block((D_MODEL, 2 * D_MODEL), (0, 0)),
                  _const_block((D_MODEL, D_MODEL), (0, 2)), _const_block((D_MODEL, D_MODEL), (0, 3)),
                  _const_block((D_MODEL, LANES), (0, 4 * D_MODEL // LANES)), _const_spec((2 * HEADS, 1)),
                  _const_spec((CONV_W, 2 * D_MODEL)), _const_spec((1, 2 * D_MODEL)), state],
        out_specs=[tok(D_MODEL), tok(D_MODEL),
                   pl.BlockSpec((nseg, per_seq, D_MODEL, CHUNK), lambda bi, i: (bi, i, 0, 0)),
                   tok(D_MODEL), tok(D_MODEL),
                   pl.BlockSpec((nseg, per_seq, 4 * HEADS, CHUNK), lambda bi, i: (bi, i, 0, 0)),
                   state],
        out_shape=[jax.ShapeDtypeStruct((n, D_MODEL), BF16), jax.ShapeDtypeStruct((n, D_MODEL), BF16),
                   jax.ShapeDtypeStruct((b, l // CHUNK, D_MODEL, CHUNK), BF16),
                   jax.ShapeDtypeStruct((n, D_MODEL), BF16), jax.ShapeDtypeStruct((n, D_MODEL), BF16),
                   jax.ShapeDtypeStruct((b, l // CHUNK, 4 * HEADS, CHUNK), F32),
                   jax.ShapeDtypeStruct((b, CONV_W - 1, 2 * D_MODEL), F32)],
        scratch_shapes=[pltpu.VMEM((nseg, CONV_PAD, 2 * D_MODEL), F32)],
        compiler_params=_cparams(("parallel", "arbitrary")),
        name="inproj_a",
    )(x2d, g, wa, wa, wa, wa, bif, cw, cb, cs)


def _inproj_b_kernel(x_ref, wu_ref, wgv_ref, wmq_ref, wgate_ref, lng_ref, lnb_ref, ws_ref, bst_ref,
                     mk_ref, mv_ref, ug_ref, att_ref, gates_ref, *rest, nseg, sl, cl, emit_gv):
    tm = nseg * sl
    xn = x_ref[...]

    gates_ref[...] = _sigmoid(_dot_nt(xn, wgate_ref[...])).astype(BF16)

    gvr = jax.nn.gelu(_dot_nt(xn, wgv_ref[...]))
    mu = jnp.mean(gvr, axis=-1, keepdims=True)
    xc = gvr - mu
    r = lax.rsqrt(jnp.mean(xc * xc, axis=-1, keepdims=True) + EPS)
    gv = (xc * r) * lng_ref[...] + lnb_ref[...]
    if emit_gv:
        rest[0][...] = gv
    gvb = gv.astype(BF16)
    u = jax.nn.gelu(_dot_nt(xn, wu_ref[...]))
    tri = (lax.broadcasted_iota(jnp.int32, (cl, cl), 0) >= lax.broadcasted_iota(jnp.int32, (cl, cl), 1))
    for gi in range(GMLP_GROUPS):
        wsg = jnp.where(tri, ws_ref[gi], 0.0).astype(BF16)
        lo, hi = gi * GMLP_GROUP_DIM, (gi + 1) * GMLP_GROUP_DIM
        for c in range(tm // cl):
            sp = _dot(wsg, gvb[c * cl:(c + 1) * cl, lo:hi]) + bst_ref[:, gi:gi + 1]
            ug_ref[c * cl:(c + 1) * cl, lo:hi] = (u[c * cl:(c + 1) * cl, lo:hi] * sp).astype(BF16)

    mq = _dot_nt(xn, wmq_ref[...]).astype(BF16)
    for s in range(nseg):
        r0, r1 = s * sl, (s + 1) * sl
        for h in range(HEADS):
            lo, hi = h * HEAD_DIM, (h + 1) * HEAD_DIM
            sc = _dot_nt(mq[r0:r1, lo:hi], mk_ref[s, :, lo:hi]) * (HEAD_DIM ** -0.5)
            e = jnp.exp(sc - jnp.max(sc, axis=-1, keepdims=True))
            a = (e / jnp.sum(e, axis=-1, keepdims=True)).astype(BF16)
            att_ref[r0:r1, lo:hi] = _dot(a, mv_ref[s, :, lo:hi]).astype(BF16)


def _inproj_b(xn2d, b, l, wb, lng, lnb, ws, bst, mk, mv, cl, emit_gv):
    nseg, sl = _tile_geometry(b, l, INPROJ_B_TILE)
    nt = l // sl
    n = b * l
    tok = functools.partial(_tok_spec, nseg, sl, nt)
    mem = pl.BlockSpec((nseg, N_MEM, D_MODEL), lambda bi, i: (bi, 0, 0))
    out_specs = [tok(D_MODEL), tok(D_MODEL), tok(3 * D_MODEL)]
    out_shape = [jax.ShapeDtypeStruct((n, D_MODEL), BF16), jax.ShapeDtypeStruct((n, D_MODEL), BF16),
                 jax.ShapeDtypeStruct((n, 3 * D_MODEL), BF16)]
    if emit_gv:
        out_specs.append(tok(D_MODEL))
        out_shape.append(jax.ShapeDtypeStruct((n, D_MODEL), F32))
    return pl.pallas_call(
        functools.partial(_inproj_b_kernel, nseg=nseg, sl=sl, cl=cl, emit_gv=emit_gv),
        grid=(b // nseg, nt),
        in_specs=[tok(D_MODEL), _const_block((D_MODEL, D_MODEL), (0, 0)),
                  _const_block((D_MODEL, D_MODEL), (1, 0)), _const_block((D_MODEL, D_MODEL), (2, 0)),
                  _const_block((3 * D_MODEL, D_MODEL), (1, 0)), _const_spec((1, D_MODEL)), _const_spec((1, D_MODEL)),
                  _const_spec((GMLP_GROUPS, cl, cl)), _const_spec((cl, GMLP_GROUPS)), mem, mem],
        out_specs=out_specs,
        out_shape=out_shape,
        compiler_params=_cparams(("parallel", "parallel")),
        name="inproj_b",
    )(xn2d, wb, wb, wb, wb, lng, lnb, ws, bst, mk, mv)


def _mlstm_kernel(q_ref, kt_ref, v_ref, gt_ref, *rest, nseg, cb, zero_state):
    i = pl.program_id(1)

    if zero_state:
        hm_ref, c_ref, n_ref, m_ref, st_ref = rest
    else:
        c0_ref, n0_ref, m0_ref, hm_ref, c_ref, n_ref, m_ref, st_ref = rest

    @pl.when(i == 0)
    def _():
        if zero_state:
            st_ref[...] = jnp.zeros(st_ref.shape, F32)
            m_ref[...] = jnp.zeros(m_ref.shape, F32)
        else:
            st_ref[:, :, :, :HEAD_DIM] = c0_ref[...]
            st_ref[:, :, :, HEAD_DIM:] = n0_ref[...]
            m_ref[...] = m0_ref[...]

    for s in range(nseg):
        _mlstm_sequence(q_ref, kt_ref.at[s], v_ref, gt_ref.at[s], st_ref.at[s], m_ref.at[s], hm_ref, s * cb, cb)

    @pl.when(i == pl.num_programs(1) - 1)
    def _():
        c_ref[...] = st_ref[:, :, :, :HEAD_DIM]
        n_ref[...] = st_ref[:, :, :, HEAD_DIM:]


def _mlstm_sequence(q_ref, kt_ref, v_ref, gt_ref, c_ref, m_ref, hm_ref, row0, cb):
    L = kt_ref.shape[-1]
    nch = cb // L
    ti = lax.broadcasted_iota(jnp.int32, (L, L), 0)
    si = lax.broadcasted_iota(jnp.int32, (L, L), 1)
    tri = ti >= si
    eye = ti == si

    rows = 4 * HEADS
    g_all = gt_ref[...].reshape(nch * rows, L)

    m_in = [m_ref[:, 0:1]]
    for c in range(nch):
        b_last4 = g_all[c * rows + HEADS:c * rows + 2 * HEADS, L - 1:L]
        amax4 = g_all[c * rows + 2 * HEADS:c * rows + 3 * HEADS, 0:1]
        m_in.append(jnp.maximum(b_last4 + m_in[-1], b_last4 + amax4))

    ones = jnp.ones((L, LANES), BF16)
    st = [c_ref[h] for h in range(HEADS)]
    for c in range(nch):
        r0, r1 = row0 + c * L, row0 + (c + 1) * L
        for h in range(HEADS):
            lo, hi = h * HEAD_DIM, (h + 1) * HEAD_DIM
            ig_r = g_all[c * rows + h:c * rows + h + 1, :]
            bc_r = g_all[c * rows + HEADS + h:c * rows + HEADS + h + 1, :]
            a_r = ig_r - bc_r
            bc_c = jnp.sum(jnp.where(eye, bc_r, 0.0), axis=-1, keepdims=True)
            m0 = m_in[c][h:h + 1, :]
            m_last = m_in[c + 1][h:h + 1, :]
            dmat = jnp.where(tri, bc_c + a_r, NEG_INF)
            inter = bc_c + m0
            m = jnp.maximum(inter, jnp.max(dmat, axis=-1, keepdims=True))
            w_intra = jnp.exp(dmat - m)
            w_inter = jnp.exp(inter - m)
            q = q_ref[r0:r1, lo:hi]
            kt = kt_ref[c, lo:hi, :]
            v = v_ref[r0:r1, lo:hi]
            s = _dot(q, kt) * w_intra
            qs = _dot(q, st[h].astype(BF16))
            num = w_inter * qs[:, :HEAD_DIM] + _dot(s.astype(BF16), v)
            den = w_inter * qs[:, HEAD_DIM:HEAD_DIM + 1] + jnp.sum(s, axis=-1, keepdims=True)
            hh = num / jnp.maximum(jnp.abs(den), jnp.exp(-m))
            bc_last = bc_r[:, L - 1:L]
            w_last = jnp.exp(bc_last + a_r - m_last)
            decay = jnp.exp(bc_last + m0 - m_last)
            ktw = (kt.astype(F32) * w_last).astype(BF16)
            st[h] = decay * st[h] + _dot(ktw, jnp.concatenate([v, ones], axis=1))
            hm_ref[r0:r1, lo:hi] = hh.astype(BF16)

    for h in range(HEADS):
        c_ref[h] = st[h]
    m_ref[...] = jnp.broadcast_to(m_in[nch], (HEADS, LANES))


def _mlstm(q, kt, v, gt, state, b, l):
    nseg, cb = _tile_geometry(b, l, MLSTM_TILE)
    nt = l // cb
    CHUNK = kt.shape[-1]
    tok = _tok_spec(nseg, cb, nt, D_MODEL)
    cs = pl.BlockSpec((nseg, HEADS, HEAD_DIM, HEAD_DIM), lambda bi, i: (bi, 0, 0, 0))
    ns = pl.BlockSpec((nseg, HEADS, HEAD_DIM, LANES), lambda bi, i: (bi, 0, 0, 0))
    ms = pl.BlockSpec((nseg, HEADS, LANES), lambda bi, i: (bi, 0, 0))
    state_specs = [] if state is None else [cs, ns, ms]
    return pl.pallas_call(
        functools.partial(_mlstm_kernel, nseg=nseg, cb=cb, zero_state=state is None),
        grid=(b // nseg, nt),
        in_specs=[tok, pl.BlockSpec((nseg, cb // CHUNK, D_MODEL, CHUNK), lambda bi, i: (bi, i, 0, 0)), tok,
                  pl.BlockSpec((nseg, cb // CHUNK, 4 * HEADS, CHUNK), lambda bi, i: (bi, i, 0, 0))] + state_specs,
        out_specs=[tok, cs, ns, ms],
        out_shape=[jax.ShapeDtypeStruct((b * l, D_MODEL), BF16),
                   jax.ShapeDtypeStruct((b, HEADS, HEAD_DIM, HEAD_DIM), F32),
                   jax.ShapeDtypeStruct((b, HEADS, HEAD_DIM, LANES), F32),
                   jax.ShapeDtypeStruct((b, HEADS, LANES), F32)],
        scratch_shapes=[pltpu.VMEM((nseg, HEADS, HEAD_DIM, HEAD_DIM + LANES), F32)],
        compiler_params=_cparams(("parallel", "arbitrary")),
        name="mlstm",
    )(q, kt, v, gt, *(state or ()))


def _merge_kernel(x_ref, hh_ref, o_ref, ng_ref, ug_ref, att_ref, gates_ref, wa_ref, wb_ref, wc_ref, wo_ref, h_ref):
    parts = []
    for h in range(HEADS):
        lo, hi = h * HEAD_DIM, (h + 1) * HEAD_DIM
        hh = hh_ref[:, lo:hi].astype(F32)
        hn = hh * lax.rsqrt(jnp.mean(hh * hh, axis=-1, keepdims=True) + EPS)
        parts.append(((hn * ng_ref[:, lo:hi]) * o_ref[:, lo:hi].astype(F32)).astype(BF16))
    br = _dot(parts[0], wa_ref[0:HEAD_DIM, :])
    for h in range(1, HEADS):
        br = br + _dot(parts[h], wa_ref[h * HEAD_DIM:(h + 1) * HEAD_DIM, :])
    g = gates_ref[...].astype(F32)
    mixed = g[:, :D_MODEL] * br
    mixed = mixed + g[:, D_MODEL:2 * D_MODEL] * _dot(ug_ref[...], wb_ref[...])
    mixed = mixed + g[:, 2 * D_MODEL:] * _dot(att_ref[...], wc_ref[...])
    h_ref[...] = x_ref[...] + _dot(mixed.astype(BF16), wo_ref[...])


def _merge(x2d, hh, osig, ng, ug, att, gates, wa, wb, wc, wo, tm):
    n = x2d.shape[0]
    row = lambda w: pl.BlockSpec((tm, w), lambda i: (i, 0))
    wspec = _const_spec((D_MODEL, D_MODEL))
    return pl.pallas_call(
        _merge_kernel,
        grid=(n // tm,),
        in_specs=[row(D_MODEL), row(D_MODEL), row(D_MODEL), _const_spec((1, D_MODEL)), row(D_MODEL), row(D_MODEL),
                  row(3 * D_MODEL), wspec, wspec, wspec, wspec],
        out_specs=row(D_MODEL),
        out_shape=jax.ShapeDtypeStruct((n, D_MODEL), F32),
        compiler_params=_cparams(("parallel",)),
        name="merge",
    )(x2d, hh, osig, ng, ug, att, gates, wa, wb, wc, wo)


MOE_T = 256
MOE_STEP_SUBS = 2
MOE_CAP = 32
MOE_S = N_EXPERTS * MOE_CAP
MOE_OVF = 512
SEG_ALIGN = 16
EXPERT_BLOCK = 128


def _moe_route_kernel(hp_ref, hs_ref, g_ref, wrg_ref, brg_ref, wre_ref, bre_ref,
                      main_ref, ovf_ref, info_ref, meta_ref, *, nsp):
    n = MOE_STEP_SUBS * MOE_T
    h = jnp.where(pl.program_id(0) < nsp // MOE_STEP_SUBS, hp_ref[...], hs_ref[...])
    xm = _rmsnorm(h, g_ref[...]).astype(BF16)
    lane = lax.broadcasted_iota(jnp.int32, (n, LANES), 1).astype(F32)
    lg = jnp.where(lane < N_GROUPS, _dot(xm, wrg_ref[...]) + brg_ref[...], NEG_INF)
    gmax = jnp.max(lg, axis=-1, keepdims=True)
    p_top = 1.0 / jnp.sum(jnp.exp(lg - gmax), axis=-1, keepdims=True)
    grp = jnp.min(jnp.where(lg == gmax, lane, float(LANES)), axis=-1, keepdims=True)
    el = _dot(xm, wre_ref[...]) + bre_ref[...]
    in_grp = (lane >= grp * EXPERTS_PER_GROUP) & (lane < (grp + 1.0) * EXPERTS_PER_GROUP)
    vals = jnp.where(in_grp, el, NEG_INF)
    v1 = jnp.max(vals, axis=-1, keepdims=True)
    i1 = jnp.min(jnp.where(vals == v1, lane, float(LANES)), axis=-1, keepdims=True)
    vals2 = jnp.where(lane == i1, NEG_INF, vals)
    v2 = jnp.max(vals2, axis=-1, keepdims=True)
    i2 = jnp.min(jnp.where(vals2 == v2, lane, float(LANES)), axis=-1, keepdims=True)
    r = jnp.exp(v2 - v1)
    p1 = p_top / (1.0 + r)
    p2 = p_top * r / (1.0 + r)
    sel1 = lane == i1
    sel2 = lane == i2
    onehot = jnp.where(sel1 | sel2, 1.0, 0.0)

    deferred = []
    for s in range(MOE_STEP_SUBS):
        rows = slice(s * MOE_T, (s + 1) * MOE_T)
        deferred.append(_route_sub_tile(
            xm[rows, :], onehot[rows, :], i1[rows, :], i2[rows, :],
            p1[rows, :], p2[rows, :], main_ref.at[:, s], ovf_ref.at[s], info_ref.at[rows, :], meta_ref.at[s]))
    for write_overflow in deferred:
        write_overflow()


def _route_sub_tile(xm, onehot, i1, i2, p1, p2, main_ref, ovf_ref, info_ref, meta_ref):
    t = MOE_T
    lane = lax.broadcasted_iota(jnp.int32, (t, LANES), 1)
    sel1 = lane.astype(F32) == i1
    sel2 = lane.astype(F32) == i2
    cnt = jnp.sum(onehot, axis=0, keepdims=True).astype(jnp.int32)
    pn = jnp.bitwise_and(jnp.maximum(cnt - MOE_CAP, 0) + (SEG_ALIGN - 1), -SEG_ALIGN)
    pn8 = jnp.broadcast_to(pn, (8, LANES))
    earlier = jnp.where(lax.broadcasted_iota(jnp.int32, (LANES, LANES), 0)
                        < lax.broadcasted_iota(jnp.int32, (LANES, LANES), 1), 1.0, 0.0).astype(BF16)
    off_f8 = _dot(pn8.astype(F32).astype(BF16), earlier)
    row8 = lax.broadcasted_iota(jnp.int32, (8, LANES), 0)
    meta_ref[...] = jnp.where(row8 == 0, off_f8.astype(jnp.int32), jnp.where(row8 == 1, pn8, 0))

    ti = lax.broadcasted_iota(jnp.int32, (t, t), 0)
    si = lax.broadcasted_iota(jnp.int32, (t, t), 1)
    before = jnp.where(ti > si, 1.0, 0.0).astype(BF16)
    rank = _dot(before, onehot.astype(BF16))
    off_f = off_f8[0:1, :]

    def slot_row(sel, idx):
        rk = jnp.sum(jnp.where(sel, rank, 0.0), axis=-1, keepdims=True)
        of = jnp.sum(jnp.where(sel, off_f, 0.0), axis=-1, keepdims=True)
        return jnp.where(rk < MOE_CAP, idx * MOE_CAP + rk, MOE_S - MOE_CAP + of + rk)

    pos1 = slot_row(sel1, i1)
    pos2 = slot_row(sel2, i2)
    info_ref[...] = (jnp.where(lane == 0, pos1, 0.0) + jnp.where(lane == 1, pos2, 0.0)
                     + jnp.where(lane == 2, p1, 0.0) + jnp.where(lane == 3, p2, 0.0))

    eye = ti == si
    pos1_r = jnp.sum(jnp.where(eye, pos1, 0.0), axis=0, keepdims=True)
    pos2_r = jnp.sum(jnp.where(eye, pos2, 0.0), axis=0, keepdims=True)

    def gather(rows, base):
        srow = (lax.broadcasted_iota(jnp.int32, (rows, t), 0) + base).astype(F32)
        pick = jnp.where((srow == pos1_r) | (srow == pos2_r), 1.0, 0.0).astype(BF16)
        return _dot(pick, xm).astype(BF16)

    main_ref[...] = gather(MOE_S, 0).reshape(N_EXPERTS, MOE_CAP, D_MODEL)
    has_ovf = jnp.sum(pn) > 0

    def write_overflow():
        @pl.when(has_ovf)
        def _():
            ovf_ref[...] = gather(MOE_OVF, MOE_S)

        @pl.when(jnp.logical_not(has_ovf))
        def _():
            ovf_ref[...] = jnp.zeros(ovf_ref.shape, ovf_ref.dtype)
    return write_overflow


def _two_group_specs(nsp, cols):
    rows = MOE_STEP_SUBS * MOE_T
    npb = nsp // MOE_STEP_SUBS
    return (pl.BlockSpec((rows, cols), lambda j, *_: (jnp.minimum(j, npb - 1), 0)),
            pl.BlockSpec((rows, cols), lambda j, *_: (jnp.maximum(j - npb, 0), 0)))


def _moe_route(hp, hs, g, wrg, brg, wre, bre):
    nsp = hp.shape[0] // MOE_T
    ns = nsp + hs.shape[0] // MOE_T
    sub = MOE_STEP_SUBS
    return pl.pallas_call(
        functools.partial(_moe_route_kernel, nsp=nsp),
        grid=(ns // sub,),
        in_specs=[*_two_group_specs(nsp, D_MODEL), _const_spec((1, D_MODEL)),
                  _const_spec((D_MODEL, LANES)), _const_spec((1, LANES)),
                  _const_spec((D_MODEL, LANES)), _const_spec((1, LANES))],
        out_specs=[pl.BlockSpec((N_EXPERTS, sub, MOE_CAP, D_MODEL), lambda j: (0, j, 0, 0)),
                   pl.BlockSpec((sub, MOE_OVF, D_MODEL), lambda j: (j, 0, 0)),
                   pl.BlockSpec((sub * MOE_T, LANES), lambda j: (j, 0)),
                   pl.BlockSpec((sub, 8, LANES), lambda j: (j, 0, 0))],
        out_shape=[jax.ShapeDtypeStruct((N_EXPERTS, ns, MOE_CAP, D_MODEL), BF16),
                   jax.ShapeDtypeStruct((ns, MOE_OVF, D_MODEL), BF16),
                   jax.ShapeDtypeStruct((ns * MOE_T, LANES), F32),
                   jax.ShapeDtypeStruct((ns, 8, LANES), jnp.int32)],
        compiler_params=_cparams(("arbitrary",)),
        name="moe_route",
    )(hp, hs, g, wrg, brg, wre, bre)


def _swiglu(x, wgu, wdb):
    gu = _dot(x, wgu[...])
    gate = gu[:, :D_EXPERT]
    hid = (gate * _sigmoid(gate)) * gu[:, D_EXPERT:]
    return _dot(hid.astype(BF16), wdb[...]).astype(BF16)


def _moe_expert_kernel(offs_ref, pns_ref, eflag_ref, main_ref, slots_hbm, wg_ref, wu_ref, wd_ref,
                       mout_ref, out_hbm, xbuf, obuf, wgu, wdb, sem_in, sem_out, *, n_sub, jb, n_rb):
    e = pl.program_id(0)
    rb = pl.program_id(1)

    @pl.when((e == 0) & (rb == 0))
    def _():
        xbuf[...] = jnp.zeros_like(xbuf)

    @pl.when(rb == 0)
    def _():
        wgu[:, :D_EXPERT] = wg_ref[...].astype(BF16)
        wgu[:, D_EXPERT:] = wu_ref[...].astype(BF16)
        wdb[...] = wd_ref[...].astype(BF16)

    sb = main_ref.shape[0]
    part = sb // 2 if sb % 2 == 0 else sb
    for s0 in range(0, sb, part):
        y = _swiglu(main_ref[s0:s0 + part].reshape(part * MOE_CAP, D_MODEL), wgu, wdb)
        mout_ref[s0:s0 + part] = y.reshape(part, MOE_CAP, D_MODEL)

    def copy_in(j, src, dst):
        return pltpu.make_async_copy(slots_hbm.at[j, pl.ds(src, SEG_ALIGN), :],
                                     xbuf.at[pl.ds(dst, SEG_ALIGN), :], sem_in)

    def copy_out(j, src, dst):
        return pltpu.make_async_copy(obuf.at[pl.ds(src, SEG_ALIGN), :],
                                     out_hbm.at[j, pl.ds(dst, SEG_ALIGN), :], sem_out)

    def for_each_chunk(g, fn):
        def seg(jj, cur):
            j = g * jb + jj
            n = pns_ref[j * N_EXPERTS + e]
            off = offs_ref[j * N_EXPERTS + e]

            def chunk(k, c):
                fn(j, pl.multiple_of(off + k * SEG_ALIGN, SEG_ALIGN), pl.multiple_of(cur + k * SEG_ALIGN, SEG_ALIGN))
                return c
            lax.fori_loop(0, n // SEG_ALIGN, chunk, 0)
            return cur + n
        return lax.fori_loop(0, jb, seg, 0)

    def group(g, carry):
        total = for_each_chunk(g, lambda j, r, b: copy_in(j, r, b).start())
        nchunk = total // SEG_ALIGN

        def wait_in(k, c):
            copy_in(0, 0, 0).wait()
            return c
        lax.fori_loop(0, nchunk, wait_in, 0)

        def block(bi, c):
            r0 = pl.multiple_of(bi * EXPERT_BLOCK, EXPERT_BLOCK)
            obuf[pl.ds(r0, EXPERT_BLOCK), :] = _swiglu(xbuf[pl.ds(r0, EXPERT_BLOCK), :], wgu, wdb)
            return c
        lax.fori_loop(0, (total + EXPERT_BLOCK - 1) // EXPERT_BLOCK, block, 0)

        for_each_chunk(g, lambda j, r, b: copy_out(j, b, r).start())

        def wait_out(k, c):
            copy_out(0, 0, 0).wait()
            return c
        lax.fori_loop(0, nchunk, wait_out, 0)
        return carry

    @pl.when((rb == n_rb - 1) & (eflag_ref[e] > 0))
    def _():
        lax.fori_loop(0, n_sub // jb, group, 0)


def _moe_expert(main, ovf, offs, pns, eflag, wg, wu, wd):
    ns = main.shape[1]
    n_rb = 1
    sb = ns // n_rb
    jb = max(d for d in range(1, 12) if ns % d == 0)
    rows = jb * MOE_T + EXPERT_BLOCK
    mspec = pl.BlockSpec((None, sb, MOE_CAP, D_MODEL), lambda e, rb, *_: (e, rb, 0, 0))
    wspec = lambda a, b: pl.BlockSpec((None, a, b), lambda e, rb, *_: (e, 0, 0))
    grid_spec = pltpu.PrefetchScalarGridSpec(
        num_scalar_prefetch=3,
        grid=(N_EXPERTS, n_rb),
        in_specs=[mspec, pl.BlockSpec(memory_space=pl.ANY),
                  wspec(D_MODEL, D_EXPERT), wspec(D_MODEL, D_EXPERT), wspec(D_EXPERT, D_MODEL)],
        out_specs=[mspec, pl.BlockSpec(memory_space=pl.ANY)],
        scratch_shapes=[pltpu.VMEM((rows, D_MODEL), BF16), pltpu.VMEM((rows, D_MODEL), BF16),
                        pltpu.VMEM((D_MODEL, 2 * D_EXPERT), BF16), pltpu.VMEM((D_EXPERT, D_MODEL), BF16),
                        pltpu.SemaphoreType.DMA(()), pltpu.SemaphoreType.DMA(())],
    )
    mout, oout = pl.pallas_call(
        functools.partial(_moe_expert_kernel, n_sub=ns, jb=jb, n_rb=n_rb),
        grid_spec=grid_spec,
        out_shape=[jax.ShapeDtypeStruct(main.shape, main.dtype), jax.ShapeDtypeStruct(ovf.shape, ovf.dtype)],
        input_output_aliases={4: 1},
        compiler_params=_cparams(("arbitrary", "arbitrary")),
        name="moe_expert",
    )(offs, pns, eflag, main, ovf, wg, wu, wd)
    return mout, oout


def _moe_combine_kernel(jflag_ref, sflag_ref, hp_ref, hs_ref, main_ref, ovf_ref, info_ref, gf_ref,
                        yp_ref, ys_ref, acc_ref, *, nsp):
    j = pl.program_id(0)
    npb = nsp // MOE_STEP_SUBS

    def weights(s, cols, base):
        info = info_ref[s * MOE_T:(s + 1) * MOE_T, :]
        scol = (lax.broadcasted_iota(jnp.int32, (MOE_T, cols), 1) + base).astype(F32)
        return (jnp.where(scol == info[:, 0:1], info[:, 2:3], 0.0)
                + jnp.where(scol == info[:, 1:2], info[:, 3:4], 0.0)).astype(BF16)

    for s in range(MOE_STEP_SUBS):
        rows = slice(s * MOE_T, (s + 1) * MOE_T)
        h = jnp.where(j < npb, hp_ref[rows, :], hs_ref[rows, :])
        acc_ref[rows, :] = h + _dot(weights(s, MOE_S, 0), main_ref[:, s].reshape(MOE_S, D_MODEL))

    for s in range(MOE_STEP_SUBS):
        @pl.when(jflag_ref[j * MOE_STEP_SUBS + s] > 0)
        def _(s=s):
            rows = slice(s * MOE_T, (s + 1) * MOE_T)
            acc_ref[rows, :] += _dot(weights(s, MOE_OVF, MOE_S), ovf_ref[s])

    y = _rmsnorm(acc_ref[...], gf_ref[...])

    @pl.when(j < npb)
    def _():
        yp_ref[...] = y

    @pl.when(j >= npb)
    def _():
        ys_ref[...] = y


def _moe_combine(hp, hs, main, ovf, info, jflag, gf):
    nsp = hp.shape[0] // MOE_T
    ns = main.shape[1]
    sub = MOE_STEP_SUBS
    sflag = jnp.max(jflag.reshape(ns // sub, sub), axis=1)
    grid_spec = pltpu.PrefetchScalarGridSpec(
        num_scalar_prefetch=2,
        grid=(ns // sub,),
        in_specs=[*_two_group_specs(nsp, D_MODEL),
                  pl.BlockSpec((N_EXPERTS, sub, MOE_CAP, D_MODEL), lambda j, jf, sf: (0, j, 0, 0)),
                  pl.BlockSpec((sub, MOE_OVF, D_MODEL), lambda j, jf, sf: (jnp.where(sf[j] > 0, j, 0), 0, 0)),
                  pl.BlockSpec((sub * MOE_T, LANES), lambda j, jf, sf: (j, 0)),
                  pl.BlockSpec((1, D_MODEL), lambda j, jf, sf: (0, 0))],
        out_specs=list(_two_group_specs(nsp, D_MODEL)),
        scratch_shapes=[pltpu.VMEM((sub * MOE_T, D_MODEL), F32)],
    )
    return pl.pallas_call(
        functools.partial(_moe_combine_kernel, nsp=nsp),
        grid_spec=grid_spec,
        out_shape=[jax.ShapeDtypeStruct(hp.shape, F32), jax.ShapeDtypeStruct(hs.shape, F32)],
        compiler_params=_cparams(("arbitrary",)),
        name="moe_combine",
    )(jflag, sflag, hp, hs, main, ovf, info, gf)


def _moe(hp, hs, p):
    main, ovf, info, meta = _moe_route(hp, hs, p["norm_moe_g"], p["w_rg"], p["b_rg"], p["w_re"], p["b_re"])
    pn = meta[:, 1, :N_EXPERTS]
    offs = meta[:, 0, :N_EXPERTS].reshape(-1)
    eflag = (jnp.sum(pn, axis=0) > 0).astype(jnp.int32)
    jflag = (jnp.sum(pn, axis=1) > 0).astype(jnp.int32)
    main, ovf = _moe_expert(main, ovf, offs, pn.reshape(-1), eflag, p["w_eg"], p["w_eu"], p["w_ed"])
    return _moe_combine(hp, hs, main, ovf, info, jflag, p["norm_final_g"])


def _trunk(x, mem_k, mem_v, c0, n0, m0, conv0, p, *, emit_gv):
    b, l, _ = x.shape
    n = b * l
    cl = min(l, GMLP_CHUNK)
    x2d = x.reshape(n, D_MODEL)
    xn, q, kt, v, osig, gt, conv_new = _inproj_a(
        x2d, b, l, p["norm_mix_g"], p["w_a"], p["b_if"], p["conv_w"], p["conv_b"], conv0)
    outs = _inproj_b(xn, b, l, p["w_b"], p["gmlp_norm_g"], p["gmlp_norm_b"], p["w_s"][:, :cl, :cl], p["b_st"][:cl],
                     mem_k, mem_v, cl, emit_gv)
    ug, att, gates = outs[:3]
    on_lanes = lambda a: jnp.broadcast_to(a[..., None], a.shape + (LANES,))
    state = None if c0 is None else (c0, on_lanes(n0), on_lanes(m0))
    hh, c1, n1, m1 = _mlstm(q, kt, v, gt, state, b, l)
    n1 = n1[..., 0]
    h = _merge(x2d, hh, osig, p["mlstm_norm_g"], ug, att, gates,
               p["w_br_mlstm"], p["w_br_gmlp"], p["w_br_mem"], p["w_out"], TOKEN_TILE)
    return h, c1, n1, m1[..., 0], conv_new, (outs[3].reshape(b, l, D_MODEL) if emit_gv else None)


def kernel(x_prompt, x_sample, mem_prompt, cache_mem_k, cache_mem_v, state_mlstm_C, state_mlstm_n, state_mlstm_m, state_mlstm_conv, norm_mix_g, w_in, mlstm_i_b, mlstm_f_b, mlstm_conv_w, mlstm_conv_b, mlstm_norm_g, gmlp_norm_g, gmlp_norm_b, gmlp_w_s, gmlp_b_s, mem_norm_g, w_mem_k, w_mem_v, w_br_mlstm, w_br_gmlp, w_br_mem, w_out, norm_moe_g, w_router_group, b_router_group, w_router_expert, b_router_expert, w_exp_gate, w_exp_up, w_exp_down, norm_final_g):
    bp = x_prompt.shape[0]
    bs = x_sample.shape[0]
    W = D_MODEL
    wt = w_in[0].T
    o_u = 4 * W + 2 * HEADS
    w_a = _w_cols(wt, 4 * W + LANES)
    w_b = _w_rows(wt, o_u, 6)
    row = lambda a: a.reshape(1, -1)
    pad_l = lambda a: jnp.pad(a, ((0, 0), (0, LANES - a.shape[1])))
    p = {
        "norm_mix_g": row(norm_mix_g[0]),
        "w_a": w_a, "w_b": w_b,
        "b_if": jnp.concatenate([mlstm_i_b[0], mlstm_f_b[0]]).reshape(2 * HEADS, 1),
        "conv_w": mlstm_conv_w[0], "conv_b": row(mlstm_conv_b[0]),
        "gmlp_norm_g": row(gmlp_norm_g[0]), "gmlp_norm_b": row(gmlp_norm_b[0]),
        "w_s": gmlp_w_s[0], "b_st": gmlp_b_s[0].T,
        "mlstm_norm_g": row(mlstm_norm_g[0]),
        "w_br_mlstm": w_br_mlstm[0].astype(BF16), "w_br_gmlp": w_br_gmlp[0].astype(BF16),
        "w_br_mem": w_br_mem[0].astype(BF16), "w_out": w_out[0].astype(BF16),
        "norm_moe_g": row(norm_moe_g[0]),
        "w_rg": pad_l(w_router_group[0]).astype(BF16), "b_rg": pad_l(row(b_router_group[0])),
        "w_re": pad_l(w_router_expert[0]).astype(BF16), "b_re": pad_l(row(b_router_expert[0])),
        "w_eg": w_exp_gate[0], "w_eu": w_exp_up[0], "w_ed": w_exp_down[0],
        "norm_final_g": row(norm_final_g),
    }

    mk_p, mv_p, mk_pb, mv_pb = _memory_kv(mem_prompt.reshape(bp * N_MEM, W), row(mem_norm_g[0]),
                                          w_mem_k[0], w_mem_v[0])

    zeros = lambda *s: jnp.zeros(s, F32)
    hp, cp, np_, mp, cvp, _ = _trunk(
        x_prompt, mk_pb.reshape(bp, N_MEM, W), mv_pb.reshape(bp, N_MEM, W),
        None, None, None, zeros(bp, CONV_W - 1, 2 * W), p, emit_gv=False)
    hs, cs, ns, ms, cvs, gvs = _trunk(
        x_sample, cache_mem_k[0].astype(BF16).reshape(bs, N_MEM, W),
        cache_mem_v[0].astype(BF16).reshape(bs, N_MEM, W),
        state_mlstm_C[0], state_mlstm_n[0], state_mlstm_m[0], state_mlstm_conv[0], p,
        emit_gv=True)

    yp, ys = _moe(hp, hs, p)
    kv_shape = (1, bp, N_MEM, HEADS, HEAD_DIM)
    return (yp.reshape(x_prompt.shape), ys.reshape(x_sample.shape), mk_p.reshape(kv_shape), mv_p.reshape(kv_shape),
            cp[None], np_[None], mp[None], cvp[None],
            cs[None], ns[None], ms[None], cvs[None], gvs[None])
```

```python
import functools

import jax
import jax.numpy as jnp
from jax import lax
from jax.experimental import pallas as pl
from jax.experimental.pallas import tpu as pltpu

D_MODEL = 1024
MLSTM_BLOCK = 512
TOKEN_TILE = 512
INPROJ_B_TILE = 1024
MLSTM_TILE = 2048
EPS = 1e-6
HEADS = 4
HEAD_DIM = 256
CONV_W = 4
GMLP_GROUPS = 4
GMLP_GROUP_DIM = 256
GMLP_CHUNK = 128
N_MEM = 256
N_GROUPS = 4
EXPERTS_PER_GROUP = 8
N_EXPERTS = 32
D_EXPERT = 256
LANES = 128
CONV_PAD = 8
NORM_ROWS = 128

F32 = jnp.float32
BF16 = jnp.bfloat16
NEG_INF = float("-inf")

VMEM_LIMIT = 56 * 1024 * 1024


def _cparams(sem):
    return pltpu.CompilerParams(dimension_semantics=sem, vmem_limit_bytes=VMEM_LIMIT)


def _const_spec(shape):
    nd = len(shape)
    return pl.BlockSpec(shape, lambda *_: (0,) * nd, pipeline_mode=pl.Buffered(1))


def _const_block(shape, idx):
    return pl.BlockSpec(shape, lambda *_: idx, pipeline_mode=pl.Buffered(1))


def _sigmoid(x):
    return 0.5 * (jnp.tanh(0.5 * x) + 1.0)


def _log_sigmoid(x):
    return jnp.minimum(x, 0.0) - jnp.log(1.0 + jnp.exp(-jnp.abs(x)))


def _rmsnorm(x, g):
    r = lax.rsqrt(jnp.mean(x * x, axis=-1, keepdims=True) + EPS)
    return (x * r) * g


def _dot(a, b):
    return jnp.dot(a, b, preferred_element_type=F32)


def _dot_nt(a, b):
    return lax.dot_general(a, b, (((1,), (1,)), ((), ())), preferred_element_type=F32)


def _memkv_kernel(mem_ref, g_ref, wk_ref, wv_ref, k_ref, v_ref, kb_ref, vb_ref):
    mn = _rmsnorm(mem_ref[...], g_ref[...]).astype(BF16)
    k = _dot(mn, wk_ref[...].astype(BF16))
    v = _dot(mn, wv_ref[...].astype(BF16))
    for s in range(k_ref.shape[0]):
        for h in range(HEADS):
            k_ref[s, :, h, :] = k[s * N_MEM:(s + 1) * N_MEM, h * HEAD_DIM:(h + 1) * HEAD_DIM]
            v_ref[s, :, h, :] = v[s * N_MEM:(s + 1) * N_MEM, h * HEAD_DIM:(h + 1) * HEAD_DIM]
    kb_ref[...] = k.astype(BF16)
    vb_ref[...] = v.astype(BF16)


def _memory_kv(mem2d, g, wk, wv):
    n = mem2d.shape[0]
    tm = TOKEN_TILE
    row = pl.BlockSpec((tm, D_MODEL), lambda i: (i, 0))
    cache = pl.BlockSpec((tm // N_MEM, N_MEM, HEADS, HEAD_DIM), lambda i: (i, 0, 0, 0))
    return pl.pallas_call(
        _memkv_kernel,
        grid=(n // tm,),
        in_specs=[row, _const_spec((1, D_MODEL)), _const_spec((D_MODEL, D_MODEL)),
                  _const_spec((D_MODEL, D_MODEL))],
        out_specs=[cache, cache, row, row],
        out_shape=[jax.ShapeDtypeStruct((n // N_MEM, N_MEM, HEADS, HEAD_DIM), F32)] * 2
        + [jax.ShapeDtypeStruct((n, D_MODEL), BF16)] * 2,
        compiler_params=_cparams(("parallel",)),
        name="memory_kv",
    )(mem2d, g, wk, wv)


W_COLS_BLOCK = 1408


def _w_cols_kernel(w_ref, o_ref):
    o_ref[...] = w_ref[...].T.astype(BF16)


def _w_cols(wt, ncols):
    rb = W_COLS_BLOCK
    return pl.pallas_call(
        _w_cols_kernel,
        grid=(ncols // rb,),
        in_specs=[pl.BlockSpec((rb, D_MODEL), lambda j: (j, 0))],
        out_specs=pl.BlockSpec((D_MODEL, rb), lambda j: (0, j)),
        out_shape=jax.ShapeDtypeStruct((D_MODEL, ncols), BF16),
        compiler_params=_cparams(("parallel",)),
        name="w_cols",
    )(wt)


def _w_rows_kernel(w_hbm, o_ref, buf, sem, *, row0, nblk):
    i = pl.program_id(0)

    def fetch(blk, slot):
        start = pl.multiple_of(row0 + blk * D_MODEL, 8)
        return pltpu.make_async_copy(w_hbm.at[pl.ds(start, D_MODEL), :], buf.at[slot], sem.at[slot])

    @pl.when(i == 0)
    def _():
        fetch(0, 0).start()

    @pl.when(i + 1 < nblk)
    def _():
        fetch(i + 1, (i + 1) % 2).start()

    fetch(i, i % 2).wait()
    o_ref[...] = buf[i % 2].astype(BF16)


def _w_rows(wt, row0, nblk):
    return pl.pallas_call(
        functools.partial(_w_rows_kernel, row0=row0, nblk=nblk),
        grid=(nblk,),
        in_specs=[pl.BlockSpec(memory_space=pl.ANY)],
        out_specs=pl.BlockSpec((D_MODEL, D_MODEL), lambda i: (i, 0)),
        out_shape=jax.ShapeDtypeStruct((nblk * D_MODEL, D_MODEL), BF16),
        scratch_shapes=[pltpu.VMEM((2, D_MODEL, D_MODEL), F32), pltpu.SemaphoreType.DMA((2,))],
        compiler_params=_cparams(("arbitrary",)),
        name="w_rows",
    )(wt)


def _inproj_a_kernel(x_ref, g_ref, wqk_ref, wv_ref, wo_ref, wif_ref, bif_ref, cw_ref, cb_ref, cs_ref,
                     xn_ref, q_ref, kt_ref, v_ref, o_ref, gt_ref, cn_ref, ext_ref, *, nseg, sl, lc):
    i = pl.program_id(1)
    tm = nseg * sl
    tail = CONV_PAD - (CONV_W - 1)

    @pl.when(i == 0)
    def _():
        for s in range(nseg):
            ext_ref[s, 0:tail, :] = jnp.zeros((tail, 2 * D_MODEL), F32)
            ext_ref[s, tail:CONV_PAD, :] = cs_ref[s]

    parts = [_rmsnorm(x_ref[r:r + NORM_ROWS, :], g_ref[...]).astype(BF16) for r in range(0, tm, NORM_ROWS)]
    xn = jnp.concatenate(parts, axis=0)
    xn_ref[...] = xn

    zqk = jnp.concatenate([_dot(part, wqk_ref[...]) for part in parts], axis=0)
    ks = []
    row8 = lax.broadcasted_iota(jnp.int32, (CONV_PAD, 2 * D_MODEL), 0)
    for s in range(nseg):
        cur = zqk[s * sl:(s + 1) * sl, :]
        prev = ext_ref[s]
        acc = cb_ref[...] + cur * cw_ref[CONV_W - 1:CONV_W, :]
        for d in range(1, CONV_W):
            back = pltpu.roll(cur, d, axis=0)
            head = jnp.where(row8 < d, pltpu.roll(prev, d, axis=0), back[0:CONV_PAD, :])
            back = jnp.concatenate([head, back[CONV_PAD:, :]], axis=0)
            acc = acc + back * cw_ref[CONV_W - 1 - d:CONV_W - d, :]
        qk = acc * _sigmoid(acc)
        q_ref[s * sl:(s + 1) * sl, :] = (qk[:, :D_MODEL] * (HEAD_DIM ** -0.5)).astype(BF16)
        ks.append(qk[:, D_MODEL:])
        ext_ref[s] = cur[sl - CONV_PAD:sl, :]
        cn_ref[s] = ext_ref[s, tail:CONV_PAD, :]
    if tm % LANES:
        ks.append(jnp.zeros((LANES - tm % LANES, D_MODEL), F32))
    k = jnp.concatenate(ks, axis=0) if len(ks) > 1 else ks[0]
    kt = k.T.astype(BF16)
    nch = tm // lc
    per_seq = sl // lc
    for c in range(nch):
        kt_ref[c // per_seq, c % per_seq] = kt[:, c * lc:(c + 1) * lc]

    v_ref[...] = _dot(xn, wv_ref[...]).astype(BF16)
    o_ref[...] = _sigmoid(_dot(xn, wo_ref[...])).astype(BF16)

    zg = _dot(xn, wif_ref[...])
    if tm % LANES:
        zg = jnp.concatenate([zg, jnp.zeros((LANES - tm % LANES, LANES), F32)], axis=0)
    zt = zg.T[0:2 * HEADS, :]
    z = jnp.concatenate([zt[:, c * lc:(c + 1) * lc] + bif_ref[...] for c in range(nch)], axis=0)
    is_ig = (lax.broadcasted_iota(jnp.int32, z.shape, 0) % (2 * HEADS)) < HEADS
    g = jnp.where(is_ig, z, _log_sigmoid(z))
    upper = jnp.where(lax.broadcasted_iota(jnp.int32, (lc, lc), 0)
                      <= lax.broadcasted_iota(jnp.int32, (lc, lc), 1), 1.0, 0.0)
    bc = jnp.dot(g, upper, preferred_element_type=F32, precision=lax.Precision.HIGHEST)
    a = g - pltpu.roll(bc, nch * 2 * HEADS - HEADS, axis=0)
    amax = jnp.broadcast_to(jnp.max(a, axis=-1, keepdims=True), z.shape)
    gb = jnp.where(is_ig, g, bc)
    for c in range(nch):
        gt_ref[c // per_seq, c % per_seq, 0:2 * HEADS, :] = gb[c * 8:(c + 1) * 8, :]
        gt_ref[c // per_seq, c % per_seq, 2 * HEADS:4 * HEADS, :] = amax[c * 8:(c + 1) * 8, :]


def _tile_geometry(b, l, tile=TOKEN_TILE):
    sl = min(l, tile)
    nseg = max(1, min(b, TOKEN_TILE // sl))
    return nseg, sl


def _tok_spec(nseg, sl, nt, w):
    return pl.BlockSpec((nseg * sl, w), lambda bi, i: (bi * nt + i, 0))


def _inproj_a(x2d, b, l, g, wa, bif, cw, cb, cs):
    nseg, sl = _tile_geometry(b, l)
    nt = l // sl
    n = b * l
    CHUNK = min(MLSTM_BLOCK, l)
    per_seq = sl // CHUNK
    tok = functools.partial(_tok_spec, nseg, sl, nt)
    state = pl.BlockSpec((nseg, CONV_W - 1, 2 * D_MODEL), lambda bi, i: (bi, 0, 0))
    return pl.pallas_call(
        functools.partial(_inproj_a_kernel, nseg=nseg, sl=sl, lc=CHUNK),
        grid=(b // nseg, nt),
        in_specs=[tok(D_MODEL), _const_spec((1, D_MODEL)), _const_block((D_MODEL, 2 * D_MODEL), (0, 0)),
                  _const_block((D_MODEL, D_MODEL), (0, 2)), _const_block((D_MODEL, D_MODEL), (0, 3)),
                  _const_block((D_MODEL, LANES), (0, 4 * D_MODEL // LANES)), _const_spec((2 * HEADS, 1)),
                  _const_spec((CONV_W, 2 * D_MODEL)), _const_spec((1, 2 * D_MODEL)), state],
        out_specs=[tok(D_MODEL), tok(D_MODEL),
                   pl.BlockSpec((nseg, per_seq, D_MODEL, CHUNK), lambda bi, i: (bi, i, 0, 0)),
                   tok(D_MODEL), tok(D_MODEL),
                   pl.BlockSpec((nseg, per_seq, 4 * HEADS, CHUNK), lambda bi, i: (bi, i, 0, 0)),
                   state],
        out_shape=[jax.ShapeDtypeStruct((n, D_MODEL), BF16), jax.ShapeDtypeStruct((n, D_MODEL), BF16),
                   jax.ShapeDtypeStruct((b, l // CHUNK, D_MODEL, CHUNK), BF16),
                   jax.ShapeDtypeStruct((n, D_MODEL), BF16), jax.ShapeDtypeStruct((n, D_MODEL), BF16),
                   jax.ShapeDtypeStruct((b, l // CHUNK, 4 * HEADS, CHUNK), F32),
                   jax.ShapeDtypeStruct((b, CONV_W - 1, 2 * D_MODEL), F32)],
        scratch_shapes=[pltpu.VMEM((nseg, CONV_PAD, 2 * D_MODEL), F32)],
        compiler_params=_cparams(("parallel", "arbitrary")),
        name="inproj_a",
    )(x2d, g, wa, wa, wa, wa, bif, cw, cb, cs)


def _inproj_b_kernel(x_ref, wu_ref, wgv_ref, wmq_ref, wgate_ref, lng_ref, lnb_ref, ws_ref, bst_ref,
                     mk_ref, mv_ref, ug_ref, att_ref, gates_ref, *rest, nseg, sl, cl, emit_gv):
    tm = nseg * sl
    xn = x_ref[...]

    gates_ref[...] = _sigmoid(_dot_nt(xn, wgate_ref[...])).astype(BF16)

    gvr = jax.nn.gelu(_dot_nt(xn, wgv_ref[...]))
    mu = jnp.mean(gvr, axis=-1, keepdims=True)
    xc = gvr - mu
    r = lax.rsqrt(jnp.mean(xc * xc, axis=-1, keepdims=True) + EPS)
    gv = (xc * r) * lng_ref[...] + lnb_ref[...]
    if emit_gv:
        rest[0][...] = gv
    gvb = gv.astype(BF16)
    u = jax.nn.gelu(_dot_nt(xn, wu_ref[...]))
    tri = (lax.broadcasted_iota(jnp.int32, (cl, cl), 0) >= lax.broadcasted_iota(jnp.int32, (cl, cl), 1))
    for gi in range(GMLP_GROUPS):
        wsg = jnp.where(tri, ws_ref[gi], 0.0).astype(BF16)
        lo, hi = gi * GMLP_GROUP_DIM, (gi + 1) * GMLP_GROUP_DIM
        for c in range(tm // cl):
            sp = _dot(wsg, gvb[c * cl:(c + 1) * cl, lo:hi]) + bst_ref[:, gi:gi + 1]
            ug_ref[c * cl:(c + 1) * cl, lo:hi] = (u[c * cl:(c + 1) * cl, lo:hi] * sp).astype(BF16)

    mq = _dot_nt(xn, wmq_ref[...]).astype(BF16)

    def mem_head(ref, s, h):
        if len(ref.shape) == 4:
            return ref[s, :, h, :].astype(BF16)
        return ref[s, :, h * HEAD_DIM:(h + 1) * HEAD_DIM]

    for s in range(nseg):
        r0, r1 = s * sl, (s + 1) * sl
        for h in range(HEADS):
            lo, hi = h * HEAD_DIM, (h + 1) * HEAD_DIM
            sc = _dot_nt(mq[r0:r1, lo:hi], mem_head(mk_ref, s, h)) * (HEAD_DIM ** -0.5)
            e = jnp.exp(sc - jnp.max(sc, axis=-1, keepdims=True))
            a = (e / jnp.sum(e, axis=-1, keepdims=True)).astype(BF16)
            att_ref[r0:r1, lo:hi] = _dot(a, mem_head(mv_ref, s, h)).astype(BF16)


def _inproj_b(xn2d, b, l, wb, lng, lnb, ws, bst, mk, mv, cl, emit_gv):
    nseg, sl = _tile_geometry(b, l, INPROJ_B_TILE)
    nt = l // sl
    n = b * l
    tok = functools.partial(_tok_spec, nseg, sl, nt)
    if mk.ndim == 4:
        mem = pl.BlockSpec((nseg, N_MEM, HEADS, HEAD_DIM), lambda bi, i: (bi, 0, 0, 0), pipeline_mode=pl.Buffered(1))
    else:
        mem = pl.BlockSpec((nseg, N_MEM, D_MODEL), lambda bi, i: (bi, 0, 0))
    out_specs = [tok(D_MODEL), tok(D_MODEL), tok(3 * D_MODEL)]
    out_shape = [jax.ShapeDtypeStruct((n, D_MODEL), BF16), jax.ShapeDtypeStruct((n, D_MODEL), BF16),
                 jax.ShapeDtypeStruct((n, 3 * D_MODEL), BF16)]
    if emit_gv:
        out_specs.append(tok(D_MODEL))
        out_shape.append(jax.ShapeDtypeStruct((n, D_MODEL), F32))
    return pl.pallas_call(
        functools.partial(_inproj_b_kernel, nseg=nseg, sl=sl, cl=cl, emit_gv=emit_gv),
        grid=(b // nseg, nt),
        in_specs=[tok(D_MODEL), _const_block((D_MODEL, D_MODEL), (0, 0)),
                  _const_block((D_MODEL, D_MODEL), (1, 0)), _const_block((D_MODEL, D_MODEL), (2, 0)),
                  _const_block((3 * D_MODEL, D_MODEL), (1, 0)), _const_spec((1, D_MODEL)), _const_spec((1, D_MODEL)),
                  _const_spec((GMLP_GROUPS, cl, cl)), _const_spec((cl, GMLP_GROUPS)), mem, mem],
        out_specs=out_specs,
        out_shape=out_shape,
        compiler_params=_cparams(("parallel", "parallel")),
        name="inproj_b",
    )(xn2d, wb, wb, wb, wb, lng, lnb, ws, bst, mk, mv)


def _mlstm_kernel(q_ref, kt_ref, v_ref, gt_ref, *rest, nseg, cb, zero_state):
    i = pl.program_id(1)

    if zero_state:
        hm_ref, c_ref, n_ref, m_ref, st_ref = rest
    else:
        c0_ref, n0_ref, m0_ref, hm_ref, c_ref, n_ref, m_ref, st_ref = rest

    @pl.when(i == 0)
    def _():
        if zero_state:
            st_ref[...] = jnp.zeros(st_ref.shape, F32)
            m_ref[...] = jnp.zeros(m_ref.shape, F32)
        else:
            st_ref[:, :, :, :HEAD_DIM] = c0_ref[...]
            st_ref[:, :, :, HEAD_DIM:] = n0_ref[...]
            m_ref[...] = m0_ref[...]

    for s in range(nseg):
        _mlstm_sequence(q_ref, kt_ref.at[s], v_ref, gt_ref.at[s], st_ref.at[s], m_ref.at[s], hm_ref, s * cb, cb)

    @pl.when(i == pl.num_programs(1) - 1)
    def _():
        c_ref[...] = st_ref[:, :, :, :HEAD_DIM]
        n_ref[...] = st_ref[:, :, :, HEAD_DIM:]


def _mlstm_sequence(q_ref, kt_ref, v_ref, gt_ref, c_ref, m_ref, hm_ref, row0, cb):
    L = kt_ref.shape[-1]
    nch = cb // L
    ti = lax.broadcasted_iota(jnp.int32, (L, L), 0)
    si = lax.broadcasted_iota(jnp.int32, (L, L), 1)
    tri = ti >= si
    eye = ti == si

    rows = 4 * HEADS
    g_all = gt_ref[...].reshape(nch * rows, L)

    m_in = [m_ref[:, 0:1]]
    for c in range(nch):
        b_last4 = g_all[c * rows + HEADS:c * rows + 2 * HEADS, L - 1:L]
        amax4 = g_all[c * rows + 2 * HEADS:c * rows + 3 * HEADS, 0:1]
        m_in.append(jnp.maximum(b_last4 + m_in[-1], b_last4 + amax4))

    ones = jnp.ones((L, LANES), BF16)
    st = [c_ref[h] for h in range(HEADS)]
    for c in range(nch):
        r0, r1 = row0 + c * L, row0 + (c + 1) * L
        for h in range(HEADS):
            lo, hi = h * HEAD_DIM, (h + 1) * HEAD_DIM
            ig_r = g_all[c * rows + h:c * rows + h + 1, :]
            bc_r = g_all[c * rows + HEADS + h:c * rows + HEADS + h + 1, :]
            a_r = ig_r - bc_r
            bc_c = jnp.sum(jnp.where(eye, bc_r, 0.0), axis=-1, keepdims=True)
            m0 = m_in[c][h:h + 1, :]
            m_last = m_in[c + 1][h:h + 1, :]
            dmat = jnp.where(tri, bc_c + a_r, NEG_INF)
            inter = bc_c + m0
            m = jnp.maximum(inter, jnp.max(dmat, axis=-1, keepdims=True))
            w_intra = jnp.exp(dmat - m)
            w_inter = jnp.exp(inter - m)
            q = q_ref[r0:r1, lo:hi]
            kt = kt_ref[c, lo:hi, :]
            v = v_ref[r0:r1, lo:hi]
            s = _dot(q, kt) * w_intra
            qs = _dot(q, st[h].astype(BF16))
            num = w_inter * qs[:, :HEAD_DIM] + _dot(s.astype(BF16), v)
            den = w_inter * qs[:, HEAD_DIM:HEAD_DIM + 1] + jnp.sum(s, axis=-1, keepdims=True)
            hh = num / jnp.maximum(jnp.abs(den), jnp.exp(-m))
            bc_last = bc_r[:, L - 1:L]
            w_last = jnp.exp(bc_last + a_r - m_last)
            decay = jnp.exp(bc_last + m0 - m_last)
            ktw = (kt.astype(F32) * w_last).astype(BF16)
            st[h] = decay * st[h] + _dot(ktw, jnp.concatenate([v, ones], axis=1))
            hm_ref[r0:r1, lo:hi] = hh.astype(BF16)

    for h in range(HEADS):
        c_ref[h] = st[h]
    m_ref[...] = jnp.broadcast_to(m_in[nch], (HEADS, LANES))


def _mlstm(q, kt, v, gt, state, b, l):
    nseg, cb = _tile_geometry(b, l, MLSTM_TILE)
    nt = l // cb
    CHUNK = kt.shape[-1]
    tok = _tok_spec(nseg, cb, nt, D_MODEL)
    cs = pl.BlockSpec((nseg, HEADS, HEAD_DIM, HEAD_DIM), lambda bi, i: (bi, 0, 0, 0))
    ns = pl.BlockSpec((nseg, HEADS, HEAD_DIM, LANES), lambda bi, i: (bi, 0, 0, 0))
    ms = pl.BlockSpec((nseg, HEADS, LANES), lambda bi, i: (bi, 0, 0))
    state_specs = [] if state is None else [cs, ns, ms]
    return pl.pallas_call(
        functools.partial(_mlstm_kernel, nseg=nseg, cb=cb, zero_state=state is None),
        grid=(b // nseg, nt),
        in_specs=[tok, pl.BlockSpec((nseg, cb // CHUNK, D_MODEL, CHUNK), lambda bi, i: (bi, i, 0, 0)), tok,
                  pl.BlockSpec((nseg, cb // CHUNK, 4 * HEADS, CHUNK), lambda bi, i: (bi, i, 0, 0))] + state_specs,
        out_specs=[tok, cs, ns, ms],
        out_shape=[jax.ShapeDtypeStruct((b * l, D_MODEL), BF16),
                   jax.ShapeDtypeStruct((b, HEADS, HEAD_DIM, HEAD_DIM), F32),
                   jax.ShapeDtypeStruct((b, HEADS, HEAD_DIM, LANES), F32),
                   jax.ShapeDtypeStruct((b, HEADS, LANES), F32)],
        scratch_shapes=[pltpu.VMEM((nseg, HEADS, HEAD_DIM, HEAD_DIM + LANES), F32)],
        compiler_params=_cparams(("parallel", "arbitrary")),
        name="mlstm",
    )(q, kt, v, gt, *(state or ()))


def _merge_kernel(x_ref, hh_ref, o_ref, ng_ref, ug_ref, att_ref, gates_ref, wa_ref, wb_ref, wc_ref, wo_ref, h_ref):
    parts = []
    for h in range(HEADS):
        lo, hi = h * HEAD_DIM, (h + 1) * HEAD_DIM
        hh = hh_ref[:, lo:hi].astype(F32)
        hn = hh * lax.rsqrt(jnp.mean(hh * hh, axis=-1, keepdims=True) + EPS)
        parts.append(((hn * ng_ref[:, lo:hi]) * o_ref[:, lo:hi].astype(F32)).astype(BF16))
    br = _dot(parts[0], wa_ref[0:HEAD_DIM, :])
    for h in range(1, HEADS):
        br = br + _dot(parts[h], wa_ref[h * HEAD_DIM:(h + 1) * HEAD_DIM, :])
    g = gates_ref[...].astype(F32)
    mixed = g[:, :D_MODEL] * br
    mixed = mixed + g[:, D_MODEL:2 * D_MODEL] * _dot(ug_ref[...], wb_ref[...])
    mixed = mixed + g[:, 2 * D_MODEL:] * _dot(att_ref[...], wc_ref[...])
    h_ref[...] = x_ref[...] + _dot(mixed.astype(BF16), wo_ref[...])


def _merge(x2d, hh, osig, ng, ug, att, gates, wa, wb, wc, wo, tm):
    n = x2d.shape[0]
    row = lambda w: pl.BlockSpec((tm, w), lambda i: (i, 0))
    wspec = _const_spec((D_MODEL, D_MODEL))
    return pl.pallas_call(
        _merge_kernel,
        grid=(n // tm,),
        in_specs=[row(D_MODEL), row(D_MODEL), row(D_MODEL), _const_spec((1, D_MODEL)), row(D_MODEL), row(D_MODEL),
                  row(3 * D_MODEL), wspec, wspec, wspec, wspec],
        out_specs=row(D_MODEL),
        out_shape=jax.ShapeDtypeStruct((n, D_MODEL), F32),
        compiler_params=_cparams(("parallel",)),
        name="merge",
    )(x2d, hh, osig, ng, ug, att, gates, wa, wb, wc, wo)


MOE_T = 256
MOE_STEP_SUBS = 2
MOE_CAP = 32
MOE_S = N_EXPERTS * MOE_CAP
MOE_OVF = 512
SEG_ALIGN = 16
EXPERT_BLOCK = 128


def _moe_route_kernel(hp_ref, hs_ref, g_ref, wrg_ref, brg_ref, wre_ref, bre_ref,
                      main_ref, ovf_ref, info_ref, meta_ref, *, nsp):
    n = MOE_STEP_SUBS * MOE_T
    h = jnp.where(pl.program_id(0) < nsp // MOE_STEP_SUBS, hp_ref[...], hs_ref[...])
    xm = _rmsnorm(h, g_ref[...]).astype(BF16)
    lane = lax.broadcasted_iota(jnp.int32, (n, LANES), 1).astype(F32)
    lg = jnp.where(lane < N_GROUPS, _dot(xm, wrg_ref[...]) + brg_ref[...], NEG_INF)
    gmax = jnp.max(lg, axis=-1, keepdims=True)
    p_top = 1.0 / jnp.sum(jnp.exp(lg - gmax), axis=-1, keepdims=True)
    grp = jnp.min(jnp.where(lg == gmax, lane, float(LANES)), axis=-1, keepdims=True)
    el = _dot(xm, wre_ref[...]) + bre_ref[...]
    in_grp = (lane >= grp * EXPERTS_PER_GROUP) & (lane < (grp + 1.0) * EXPERTS_PER_GROUP)
    vals = jnp.where(in_grp, el, NEG_INF)
    v1 = jnp.max(vals, axis=-1, keepdims=True)
    i1 = jnp.min(jnp.where(vals == v1, lane, float(LANES)), axis=-1, keepdims=True)
    vals2 = jnp.where(lane == i1, NEG_INF, vals)
    v2 = jnp.max(vals2, axis=-1, keepdims=True)
    i2 = jnp.min(jnp.where(vals2 == v2, lane, float(LANES)), axis=-1, keepdims=True)
    r = jnp.exp(v2 - v1)
    p1 = p_top / (1.0 + r)
    p2 = p_top * r / (1.0 + r)
    sel1 = lane == i1
    sel2 = lane == i2
    onehot = jnp.where(sel1 | sel2, 1.0, 0.0)

    deferred = []
    for s in range(MOE_STEP_SUBS):
        rows = slice(s * MOE_T, (s + 1) * MOE_T)
        deferred.append(_route_sub_tile(
            xm[rows, :], onehot[rows, :], i1[rows, :], i2[rows, :],
            p1[rows, :], p2[rows, :], main_ref.at[:, s], ovf_ref.at[s], info_ref.at[rows, :], meta_ref.at[s]))
    for write_overflow in deferred:
        write_overflow()


def _route_sub_tile(xm, onehot, i1, i2, p1, p2, main_ref, ovf_ref, info_ref, meta_ref):
    t = MOE_T
    lane = lax.broadcasted_iota(jnp.int32, (t, LANES), 1)
    sel1 = lane.astype(F32) == i1
    sel2 = lane.astype(F32) == i2
    cnt = jnp.sum(onehot, axis=0, keepdims=True).astype(jnp.int32)
    pn = jnp.bitwise_and(jnp.maximum(cnt - MOE_CAP, 0) + (SEG_ALIGN - 1), -SEG_ALIGN)
    pn8 = jnp.broadcast_to(pn, (8, LANES))
    earlier = jnp.where(lax.broadcasted_iota(jnp.int32, (LANES, LANES), 0)
                        < lax.broadcasted_iota(jnp.int32, (LANES, LANES), 1), 1.0, 0.0).astype(BF16)
    off_f8 = _dot(pn8.astype(F32).astype(BF16), earlier)
    row8 = lax.broadcasted_iota(jnp.int32, (8, LANES), 0)
    meta_ref[...] = jnp.where(row8 == 0, off_f8.astype(jnp.int32), jnp.where(row8 == 1, pn8, 0))

    ti = lax.broadcasted_iota(jnp.int32, (t, t), 0)
    si = lax.broadcasted_iota(jnp.int32, (t, t), 1)
    before = jnp.where(ti > si, 1.0, 0.0).astype(BF16)
    rank = _dot(before, onehot.astype(BF16))
    off_f = off_f8[0:1, :]

    def slot_row(sel, idx):
        rk = jnp.sum(jnp.where(sel, rank, 0.0), axis=-1, keepdims=True)
        of = jnp.sum(jnp.where(sel, off_f, 0.0), axis=-1, keepdims=True)
        return jnp.where(rk < MOE_CAP, idx * MOE_CAP + rk, MOE_S - MOE_CAP + of + rk)

    pos1 = slot_row(sel1, i1)
    pos2 = slot_row(sel2, i2)
    info_ref[...] = (jnp.where(lane == 0, pos1, 0.0) + jnp.where(lane == 1, pos2, 0.0)
                     + jnp.where(lane == 2, p1, 0.0) + jnp.where(lane == 3, p2, 0.0))

    eye = ti == si
    pos1_r = jnp.sum(jnp.where(eye, pos1, 0.0), axis=0, keepdims=True)
    pos2_r = jnp.sum(jnp.where(eye, pos2, 0.0), axis=0, keepdims=True)

    def gather(rows, base):
        srow = (lax.broadcasted_iota(jnp.int32, (rows, t), 0) + base).astype(F32)
        pick = jnp.where((srow == pos1_r) | (srow == pos2_r), 1.0, 0.0).astype(BF16)
        return _dot(pick, xm).astype(BF16)

    main_ref[...] = gather(MOE_S, 0).reshape(N_EXPERTS, MOE_CAP, D_MODEL)
    has_ovf = jnp.sum(pn) > 0

    def write_overflow():
        @pl.when(has_ovf)
        def _():
            ovf_ref[...] = gather(MOE_OVF, MOE_S)

        @pl.when(jnp.logical_not(has_ovf))
        def _():
            ovf_ref[...] = jnp.zeros(ovf_ref.shape, ovf_ref.dtype)
    return write_overflow


def _two_group_specs(nsp, cols):
    rows = MOE_STEP_SUBS * MOE_T
    npb = nsp // MOE_STEP_SUBS
    return (pl.BlockSpec((rows, cols), lambda j, *_: (jnp.minimum(j, npb - 1), 0)),
            pl.BlockSpec((rows, cols), lambda j, *_: (jnp.maximum(j - npb, 0), 0)))


def _moe_route(hp, hs, g, wrg, brg, wre, bre):
    nsp = hp.shape[0] // MOE_T
    ns = nsp + hs.shape[0] // MOE_T
    sub = MOE_STEP_SUBS
    return pl.pallas_call(
        functools.partial(_moe_route_kernel, nsp=nsp),
        grid=(ns // sub,),
        in_specs=[*_two_group_specs(nsp, D_MODEL), _const_spec((1, D_MODEL)),
                  _const_spec((D_MODEL, LANES)), _const_spec((1, LANES)),
                  _const_spec((D_MODEL, LANES)), _const_spec((1, LANES))],
        out_specs=[pl.BlockSpec((N_EXPERTS, sub, MOE_CAP, D_MODEL), lambda j: (0, j, 0, 0)),
                   pl.BlockSpec((sub, MOE_OVF, D_MODEL), lambda j: (j, 0, 0)),
                   pl.BlockSpec((sub * MOE_T, LANES), lambda j: (j, 0)),
                   pl.BlockSpec((sub, 8, LANES), lambda j: (j, 0, 0))],
        out_shape=[jax.ShapeDtypeStruct((N_EXPERTS, ns, MOE_CAP, D_MODEL), BF16),
                   jax.ShapeDtypeStruct((ns, MOE_OVF, D_MODEL), BF16),
                   jax.ShapeDtypeStruct((ns * MOE_T, LANES), F32),
                   jax.ShapeDtypeStruct((ns, 8, LANES), jnp.int32)],
        compiler_params=_cparams(("arbitrary",)),
        name="moe_route",
    )(hp, hs, g, wrg, brg, wre, bre)


def _swiglu(x, wgu, wdb):
    gu = _dot(x, wgu[...])
    gate = gu[:, :D_EXPERT]
    hid = (gate * _sigmoid(gate)) * gu[:, D_EXPERT:]
    return _dot(hid.astype(BF16), wdb[...]).astype(BF16)


def _moe_expert_kernel(offs_ref, pns_ref, eflag_ref, main_ref, slots_hbm, wg_ref, wu_ref, wd_ref,
                       mout_ref, out_hbm, xbuf, obuf, wgu, wdb, sem_in, sem_out, *, n_sub, jb, n_rb):
    e = pl.program_id(0)
    rb = pl.program_id(1)

    @pl.when((e == 0) & (rb == 0))
    def _():
        xbuf[...] = jnp.zeros_like(xbuf)

    @pl.when(rb == 0)
    def _():
        wgu[:, :D_EXPERT] = wg_ref[...].astype(BF16)
        wgu[:, D_EXPERT:] = wu_ref[...].astype(BF16)
        wdb[...] = wd_ref[...].astype(BF16)

    sb = main_ref.shape[0]
    part = sb // 2 if sb % 2 == 0 else sb
    for s0 in range(0, sb, part):
        y = _swiglu(main_ref[s0:s0 + part].reshape(part * MOE_CAP, D_MODEL), wgu, wdb)
        mout_ref[s0:s0 + part] = y.reshape(part, MOE_CAP, D_MODEL)

    def copy_in(j, src, dst):
        return pltpu.make_async_copy(slots_hbm.at[j, pl.ds(src, SEG_ALIGN), :],
                                     xbuf.at[pl.ds(dst, SEG_ALIGN), :], sem_in)

    def copy_out(j, src, dst):
        return pltpu.make_async_copy(obuf.at[pl.ds(src, SEG_ALIGN), :],
                                     out_hbm.at[j, pl.ds(dst, SEG_ALIGN), :], sem_out)

    def for_each_chunk(g, fn):
        def seg(jj, cur):
            j = g * jb + jj
            n = pns_ref[j * N_EXPERTS + e]
            off = offs_ref[j * N_EXPERTS + e]

            def chunk(k, c):
                fn(j, pl.multiple_of(off + k * SEG_ALIGN, SEG_ALIGN), pl.multiple_of(cur + k * SEG_ALIGN, SEG_ALIGN))
                return c
            lax.fori_loop(0, n // SEG_ALIGN, chunk, 0)
            return cur + n
        return lax.fori_loop(0, jb, seg, 0)

    def group(g, carry):
        total = for_each_chunk(g, lambda j, r, b: copy_in(j, r, b).start())
        nchunk = total // SEG_ALIGN

        def wait_in(k, c):
            copy_in(0, 0, 0).wait()
            return c
        lax.fori_loop(0, nchunk, wait_in, 0)

        def block(bi, c):
            r0 = pl.multiple_of(bi * EXPERT_BLOCK, EXPERT_BLOCK)
            obuf[pl.ds(r0, EXPERT_BLOCK), :] = _swiglu(xbuf[pl.ds(r0, EXPERT_BLOCK), :], wgu, wdb)
            return c
        lax.fori_loop(0, (total + EXPERT_BLOCK - 1) // EXPERT_BLOCK, block, 0)

        for_each_chunk(g, lambda j, r, b: copy_out(j, b, r).start())

        def wait_out(k, c):
            copy_out(0, 0, 0).wait()
            return c
        lax.fori_loop(0, nchunk, wait_out, 0)
        return carry

    @pl.when((rb == n_rb - 1) & (eflag_ref[e] > 0))
    def _():
        lax.fori_loop(0, n_sub // jb, group, 0)


def _moe_expert(main, ovf, offs, pns, eflag, wg, wu, wd):
    ns = main.shape[1]
    n_rb = 1
    sb = ns // n_rb
    jb = max(d for d in range(1, 12) if ns % d == 0)
    rows = jb * MOE_T + EXPERT_BLOCK
    mspec = pl.BlockSpec((None, sb, MOE_CAP, D_MODEL), lambda e, rb, *_: (e, rb, 0, 0))
    wspec = lambda a, b: pl.BlockSpec((None, a, b), lambda e, rb, *_: (e, 0, 0))
    grid_spec = pltpu.PrefetchScalarGridSpec(
        num_scalar_prefetch=3,
        grid=(N_EXPERTS, n_rb),
        in_specs=[mspec, pl.BlockSpec(memory_space=pl.ANY),
                  wspec(D_MODEL, D_EXPERT), wspec(D_MODEL, D_EXPERT), wspec(D_EXPERT, D_MODEL)],
        out_specs=[mspec, pl.BlockSpec(memory_space=pl.ANY)],
        scratch_shapes=[pltpu.VMEM((rows, D_MODEL), BF16), pltpu.VMEM((rows, D_MODEL), BF16),
                        pltpu.VMEM((D_MODEL, 2 * D_EXPERT), BF16), pltpu.VMEM((D_EXPERT, D_MODEL), BF16),
                        pltpu.SemaphoreType.DMA(()), pltpu.SemaphoreType.DMA(())],
    )
    mout, oout = pl.pallas_call(
        functools.partial(_moe_expert_kernel, n_sub=ns, jb=jb, n_rb=n_rb),
        grid_spec=grid_spec,
        out_shape=[jax.ShapeDtypeStruct(main.shape, main.dtype), jax.ShapeDtypeStruct(ovf.shape, ovf.dtype)],
        input_output_aliases={4: 1},
        compiler_params=_cparams(("arbitrary", "arbitrary")),
        name="moe_expert",
    )(offs, pns, eflag, main, ovf, wg, wu, wd)
    return mout, oout


def _moe_combine_kernel(jflag_ref, sflag_ref, hp_ref, hs_ref, main_ref, ovf_ref, info_ref, gf_ref,
                        yp_ref, ys_ref, acc_ref, *, nsp):
    j = pl.program_id(0)
    npb = nsp // MOE_STEP_SUBS

    def weights(s, cols, base):
        info = info_ref[s * MOE_T:(s + 1) * MOE_T, :]
        scol = (lax.broadcasted_iota(jnp.int32, (MOE_T, cols), 1) + base).astype(F32)
        return (jnp.where(scol == info[:, 0:1], info[:, 2:3], 0.0)
                + jnp.where(scol == info[:, 1:2], info[:, 3:4], 0.0)).astype(BF16)

    for s in range(MOE_STEP_SUBS):
        rows = slice(s * MOE_T, (s + 1) * MOE_T)
        h = jnp.where(j < npb, hp_ref[rows, :], hs_ref[rows, :])
        acc_ref[rows, :] = h + _dot(weights(s, MOE_S, 0), main_ref[:, s].reshape(MOE_S, D_MODEL))

    for s in range(MOE_STEP_SUBS):
        @pl.when(jflag_ref[j * MOE_STEP_SUBS + s] > 0)
        def _(s=s):
            rows = slice(s * MOE_T, (s + 1) * MOE_T)
            acc_ref[rows, :] += _dot(weights(s, MOE_OVF, MOE_S), ovf_ref[s])

    y = _rmsnorm(acc_ref[...], gf_ref[...])

    @pl.when(j < npb)
    def _():
        yp_ref[...] = y

    @pl.when(j >= npb)
    def _():
        ys_ref[...] = y


def _moe_combine(hp, hs, main, ovf, info, jflag, gf):
    nsp = hp.shape[0] // MOE_T
    ns = main.shape[1]
    sub = MOE_STEP_SUBS
    sflag = jnp.max(jflag.reshape(ns // sub, sub), axis=1)
    grid_spec = pltpu.PrefetchScalarGridSpec(
        num_scalar_prefetch=2,
        grid=(ns // sub,),
        in_specs=[*_two_group_specs(nsp, D_MODEL),
                  pl.BlockSpec((N_EXPERTS, sub, MOE_CAP, D_MODEL), lambda j, jf, sf: (0, j, 0, 0)),
                  pl.BlockSpec((sub, MOE_OVF, D_MODEL), lambda j, jf, sf: (jnp.where(sf[j] > 0, j, 0), 0, 0)),
                  pl.BlockSpec((sub * MOE_T, LANES), lambda j, jf, sf: (j, 0)),
                  pl.BlockSpec((1, D_MODEL), lambda j, jf, sf: (0, 0))],
        out_specs=list(_two_group_specs(nsp, D_MODEL)),
        scratch_shapes=[pltpu.VMEM((sub * MOE_T, D_MODEL), F32)],
    )
    return pl.pallas_call(
        functools.partial(_moe_combine_kernel, nsp=nsp),
        grid_spec=grid_spec,
        out_shape=[jax.ShapeDtypeStruct(hp.shape, F32), jax.ShapeDtypeStruct(hs.shape, F32)],
        compiler_params=_cparams(("arbitrary",)),
        name="moe_combine",
    )(jflag, sflag, hp, hs, main, ovf, info, gf)


def _moe(hp, hs, p):
    main, ovf, info, meta = _moe_route(hp, hs, p["norm_moe_g"], p["w_rg"], p["b_rg"], p["w_re"], p["b_re"])
    pn = meta[:, 1, :N_EXPERTS]
    offs = meta[:, 0, :N_EXPERTS].reshape(-1)
    eflag = (jnp.sum(pn, axis=0) > 0).astype(jnp.int32)
    jflag = (jnp.sum(pn, axis=1) > 0).astype(jnp.int32)
    main, ovf = _moe_expert(main, ovf, offs, pn.reshape(-1), eflag, p["w_eg"], p["w_eu"], p["w_ed"])
    return _moe_combine(hp, hs, main, ovf, info, jflag, p["norm_final_g"])


def _trunk(x, mem_k, mem_v, c0, n0, m0, conv0, p, *, emit_gv):
    b, l, _ = x.shape
    n = b * l
    cl = min(l, GMLP_CHUNK)
    x2d = x.reshape(n, D_MODEL)
    xn, q, kt, v, osig, gt, conv_new = _inproj_a(
        x2d, b, l, p["norm_mix_g"], p["w_a"], p["b_if"], p["conv_w"], p["conv_b"], conv0)
    outs = _inproj_b(xn, b, l, p["w_b"], p["gmlp_norm_g"], p["gmlp_norm_b"], p["w_s"][:, :cl, :cl], p["b_st"][:cl],
                     mem_k, mem_v, cl, emit_gv)
    ug, att, gates = outs[:3]
    on_lanes = lambda a: jnp.broadcast_to(a[..., None], a.shape + (LANES,))
    state = None if c0 is None else (c0, on_lanes(n0), on_lanes(m0))
    hh, c1, n1, m1 = _mlstm(q, kt, v, gt, state, b, l)
    n1 = n1[..., 0]
    h = _merge(x2d, hh, osig, p["mlstm_norm_g"], ug, att, gates,
               p["w_br_mlstm"], p["w_br_gmlp"], p["w_br_mem"], p["w_out"], TOKEN_TILE)
    return h, c1, n1, m1[..., 0], conv_new, (outs[3].reshape(b, l, D_MODEL) if emit_gv else None)


def kernel(x_prompt, x_sample, mem_prompt, cache_mem_k, cache_mem_v, state_mlstm_C, state_mlstm_n, state_mlstm_m, state_mlstm_conv, norm_mix_g, w_in, mlstm_i_b, mlstm_f_b, mlstm_conv_w, mlstm_conv_b, mlstm_norm_g, gmlp_norm_g, gmlp_norm_b, gmlp_w_s, gmlp_b_s, mem_norm_g, w_mem_k, w_mem_v, w_br_mlstm, w_br_gmlp, w_br_mem, w_out, norm_moe_g, w_router_group, b_router_group, w_router_expert, b_router_expert, w_exp_gate, w_exp_up, w_exp_down, norm_final_g):
    bp = x_prompt.shape[0]
    bs = x_sample.shape[0]
    W = D_MODEL
    wt = w_in[0].T
    o_u = 4 * W + 2 * HEADS
    w_a = _w_cols(wt, 4 * W + LANES)
    w_b = _w_rows(wt, o_u, 6)
    row = lambda a: a.reshape(1, -1)
    pad_l = lambda a: jnp.pad(a, ((0, 0), (0, LANES - a.shape[1])))
    p = {
        "norm_mix_g": row(norm_mix_g[0]),
        "w_a": w_a, "w_b": w_b,
        "b_if": jnp.concatenate([mlstm_i_b[0], mlstm_f_b[0]]).reshape(2 * HEADS, 1),
        "conv_w": mlstm_conv_w[0], "conv_b": row(mlstm_conv_b[0]),
        "gmlp_norm_g": row(gmlp_norm_g[0]), "gmlp_norm_b": row(gmlp_norm_b[0]),
        "w_s": gmlp_w_s[0], "b_st": gmlp_b_s[0].T,
        "mlstm_norm_g": row(mlstm_norm_g[0]),
        "w_br_mlstm": w_br_mlstm[0].astype(BF16), "w_br_gmlp": w_br_gmlp[0].astype(BF16),
        "w_br_mem": w_br_mem[0].astype(BF16), "w_out": w_out[0].astype(BF16),
        "norm_moe_g": row(norm_moe_g[0]),
        "w_rg": pad_l(w_router_group[0]).astype(BF16), "b_rg": pad_l(row(b_router_group[0])),
        "w_re": pad_l(w_router_expert[0]).astype(BF16), "b_re": pad_l(row(b_router_expert[0])),
        "w_eg": w_exp_gate[0], "w_eu": w_exp_up[0], "w_ed": w_exp_down[0],
        "norm_final_g": row(norm_final_g),
    }

    mk_p, mv_p, mk_pb, mv_pb = _memory_kv(mem_prompt.reshape(bp * N_MEM, W), row(mem_norm_g[0]),
                                          w_mem_k[0], w_mem_v[0])

    zeros = lambda *s: jnp.zeros(s, F32)
    hp, cp, np_, mp, cvp, _ = _trunk(
        x_prompt, mk_pb.reshape(bp, N_MEM, W), mv_pb.reshape(bp, N_MEM, W),
        None, None, None, zeros(bp, CONV_W - 1, 2 * W), p, emit_gv=False)
    hs, cs, ns, ms, cvs, gvs = _trunk(
        x_sample, cache_mem_k[0], cache_mem_v[0],
        state_mlstm_C[0], state_mlstm_n[0], state_mlstm_m[0], state_mlstm_conv[0], p,
        emit_gv=True)

    yp, ys = _moe(hp, hs, p)
    return (yp.reshape(x_prompt.shape), ys.reshape(x_sample.shape), mk_p[None], mv_p[None],
            cp[None], np_[None], mp[None], cvp[None],
            cs[None], ns[None], ms[None], cvs[None], gvs[None])
```

```python
import functools

import jax
import jax.numpy as jnp
from jax import lax
from jax.experimental import pallas as pl
from jax.experimental.pallas import tpu as pltpu

D_MODEL = 1024
MLSTM_BLOCK = 512
TOKEN_TILE = 512
INPROJ_B_TILE = 1024
MLSTM_TILE = 2048
EPS = 1e-6
HEADS = 4
HEAD_DIM = 256
CONV_W = 4
GMLP_GROUPS = 4
GMLP_GROUP_DIM = 256
GMLP_CHUNK = 128
N_MEM = 256
N_GROUPS = 4
EXPERTS_PER_GROUP = 8
N_EXPERTS = 32
D_EXPERT = 256
LANES = 128
CONV_PAD = 8
NORM_ROWS = 128

F32 = jnp.float32
BF16 = jnp.bfloat16
NEG_INF = float("-inf")

VMEM_LIMIT = 56 * 1024 * 1024


def _cparams(sem):
    return pltpu.CompilerParams(dimension_semantics=sem, vmem_limit_bytes=VMEM_LIMIT)


def _const_spec(shape):
    nd = len(shape)
    return pl.BlockSpec(shape, lambda *_: (0,) * nd, pipeline_mode=pl.Buffered(1))


def _const_block(shape, idx):
    return pl.BlockSpec(shape, lambda *_: idx, pipeline_mode=pl.Buffered(1))


def _sigmoid(x):
    return 0.5 * (jnp.tanh(0.5 * x) + 1.0)


def _log_sigmoid(x):
    return jnp.minimum(x, 0.0) - jnp.log(1.0 + jnp.exp(-jnp.abs(x)))


def _rmsnorm(x, g):
    r = lax.rsqrt(jnp.mean(x * x, axis=-1, keepdims=True) + EPS)
    return (x * r) * g


def _dot(a, b):
    return jnp.dot(a, b, preferred_element_type=F32)


def _dot_nt(a, b):
    return lax.dot_general(a, b, (((1,), (1,)), ((), ())), preferred_element_type=F32)


def _memkv_kernel(mem_ref, g_ref, wk_ref, wv_ref, k_ref, v_ref, kb_ref, vb_ref):
    mn = _rmsnorm(mem_ref[...], g_ref[...]).astype(BF16)
    k = _dot(mn, wk_ref[...].astype(BF16))
    v = _dot(mn, wv_ref[...].astype(BF16))
    for s in range(k_ref.shape[0]):
        for h in range(HEADS):
            k_ref[s, :, h, :] = k[s * N_MEM:(s + 1) * N_MEM, h * HEAD_DIM:(h + 1) * HEAD_DIM]
            v_ref[s, :, h, :] = v[s * N_MEM:(s + 1) * N_MEM, h * HEAD_DIM:(h + 1) * HEAD_DIM]
    kb_ref[...] = k.astype(BF16)
    vb_ref[...] = v.astype(BF16)


def _memory_kv(mem2d, g, wk, wv):
    n = mem2d.shape[0]
    tm = TOKEN_TILE
    row = pl.BlockSpec((tm, D_MODEL), lambda i: (i, 0))
    cache = pl.BlockSpec((tm // N_MEM, N_MEM, HEADS, HEAD_DIM), lambda i: (i, 0, 0, 0))
    return pl.pallas_call(
        _memkv_kernel,
        grid=(n // tm,),
        in_specs=[row, _const_spec((1, D_MODEL)), _const_spec((D_MODEL, D_MODEL)),
                  _const_spec((D_MODEL, D_MODEL))],
        out_specs=[cache, cache, row, row],
        out_shape=[jax.ShapeDtypeStruct((n // N_MEM, N_MEM, HEADS, HEAD_DIM), F32)] * 2
        + [jax.ShapeDtypeStruct((n, D_MODEL), BF16)] * 2,
        compiler_params=_cparams(("parallel",)),
        name="memory_kv",
    )(mem2d, g, wk, wv)


W_COLS_BLOCK = 1408


def _w_cols_kernel(w_ref, o_ref):
    o_ref[...] = w_ref[...].T.astype(BF16)


def _w_cols(wt, ncols):
    rb = W_COLS_BLOCK
    return pl.pallas_call(
        _w_cols_kernel,
        grid=(ncols // rb,),
        in_specs=[pl.BlockSpec((rb, D_MODEL), lambda j: (j, 0))],
        out_specs=pl.BlockSpec((D_MODEL, rb), lambda j: (0, j)),
        out_shape=jax.ShapeDtypeStruct((D_MODEL, ncols), BF16),
        compiler_params=_cparams(("parallel",)),
        name="w_cols",
    )(wt)


def _w_rows_kernel(w_hbm, o_ref, buf, sem, *, row0, nblk):
    i = pl.program_id(0)

    def fetch(blk, slot):
        start = pl.multiple_of(row0 + blk * D_MODEL, 8)
        return pltpu.make_async_copy(w_hbm.at[pl.ds(start, D_MODEL), :], buf.at[slot], sem.at[slot])

    @pl.when(i == 0)
    def _():
        fetch(0, 0).start()

    @pl.when(i + 1 < nblk)
    def _():
        fetch(i + 1, (i + 1) % 2).start()

    fetch(i, i % 2).wait()
    o_ref[...] = buf[i % 2].astype(BF16)


def _w_rows(wt, row0, nblk):
    return pl.pallas_call(
        functools.partial(_w_rows_kernel, row0=row0, nblk=nblk),
        grid=(nblk,),
        in_specs=[pl.BlockSpec(memory_space=pl.ANY)],
        out_specs=pl.BlockSpec((D_MODEL, D_MODEL), lambda i: (i, 0)),
        out_shape=jax.ShapeDtypeStruct((nblk * D_MODEL, D_MODEL), BF16),
        scratch_shapes=[pltpu.VMEM((2, D_MODEL, D_MODEL), F32), pltpu.SemaphoreType.DMA((2,))],
        compiler_params=_cparams(("arbitrary",)),
        name="w_rows",
    )(wt)


def _inproj_a_kernel(x_ref, g_ref, wqk_ref, wv_ref, wo_ref, wif_ref, bif_ref, cw_ref, cb_ref, cs_ref,
                     xn_ref, q_ref, kt_ref, v_ref, o_ref, gt_ref, cn_ref, ext_ref, *, nseg, sl, lc):
    i = pl.program_id(1)
    tm = nseg * sl
    tail = CONV_PAD - (CONV_W - 1)

    @pl.when(i == 0)
    def _():
        for s in range(nseg):
            ext_ref[s, 0:tail, :] = jnp.zeros((tail, 2 * D_MODEL), F32)
            ext_ref[s, tail:CONV_PAD, :] = cs_ref[s]

    parts = [_rmsnorm(x_ref[r:r + NORM_ROWS, :], g_ref[...]).astype(BF16) for r in range(0, tm, NORM_ROWS)]
    xn = jnp.concatenate(parts, axis=0)
    xn_ref[...] = xn

    zqk = jnp.concatenate([_dot(part, wqk_ref[...]) for part in parts], axis=0)
    ks = []
    row8 = lax.broadcasted_iota(jnp.int32, (CONV_PAD, 2 * D_MODEL), 0)
    for s in range(nseg):
        cur = zqk[s * sl:(s + 1) * sl, :]
        prev = ext_ref[s]
        acc = cb_ref[...] + cur * cw_ref[CONV_W - 1:CONV_W, :]
        for d in range(1, CONV_W):
            back = pltpu.roll(cur, d, axis=0)
            head = jnp.where(row8 < d, pltpu.roll(prev, d, axis=0), back[0:CONV_PAD, :])
            back = jnp.concatenate([head, back[CONV_PAD:, :]], axis=0)
            acc = acc + back * cw_ref[CONV_W - 1 - d:CONV_W - d, :]
        qk = acc * _sigmoid(acc)
        q_ref[s * sl:(s + 1) * sl, :] = (qk[:, :D_MODEL] * (HEAD_DIM ** -0.5)).astype(BF16)
        ks.append(qk[:, D_MODEL:])
        ext_ref[s] = cur[sl - CONV_PAD:sl, :]
        cn_ref[s] = ext_ref[s, tail:CONV_PAD, :]
    if tm % LANES:
        ks.append(jnp.zeros((LANES - tm % LANES, D_MODEL), F32))
    k = jnp.concatenate(ks, axis=0) if len(ks) > 1 else ks[0]
    kt = k.T.astype(BF16)
    nch = tm // lc
    per_seq = sl // lc
    for c in range(nch):
        kt_ref[c // per_seq, c % per_seq] = kt[:, c * lc:(c + 1) * lc]

    v_ref[...] = _dot(xn, wv_ref[...]).astype(BF16)
    o_ref[...] = _sigmoid(_dot(xn, wo_ref[...])).astype(BF16)

    zg = _dot(xn, wif_ref[...])
    if tm % LANES:
        zg = jnp.concatenate([zg, jnp.zeros((LANES - tm % LANES, LANES), F32)], axis=0)
    zt = zg.T[0:2 * HEADS, :]
    z = jnp.concatenate([zt[:, c * lc:(c + 1) * lc] + bif_ref[...] for c in range(nch)], axis=0)
    is_ig = (lax.broadcasted_iota(jnp.int32, z.shape, 0) % (2 * HEADS)) < HEADS
    g = jnp.where(is_ig, z, _log_sigmoid(z))
    upper = jnp.where(lax.broadcasted_iota(jnp.int32, (lc, lc), 0)
                      <= lax.broadcasted_iota(jnp.int32, (lc, lc), 1), 1.0, 0.0)
    bc = jnp.dot(g, upper, preferred_element_type=F32, precision=lax.Precision.HIGHEST)
    a = g - pltpu.roll(bc, nch * 2 * HEADS - HEADS, axis=0)
    amax = jnp.broadcast_to(jnp.max(a, axis=-1, keepdims=True), z.shape)
    gb = jnp.where(is_ig, g, bc)
    for c in range(nch):
        gt_ref[c // per_seq, c % per_seq, 0:2 * HEADS, :] = gb[c * 8:(c + 1) * 8, :]
        gt_ref[c // per_seq, c % per_seq, 2 * HEADS:4 * HEADS, :] = amax[c * 8:(c + 1) * 8, :]


def _tile_geometry(b, l, tile=TOKEN_TILE):
    sl = min(l, tile)
    nseg = max(1, min(b, TOKEN_TILE // sl))
    return nseg, sl


def _tok_spec(nseg, sl, nt, w):
    return pl.BlockSpec((nseg * sl, w), lambda bi, i: (bi * nt + i, 0))


def _inproj_a(x2d, b, l, g, wa, bif, cw, cb, cs):
    nseg, sl = _tile_geometry(b, l)
    nt = l // sl
    n = b * l
    CHUNK = min(MLSTM_BLOCK, l)
    per_seq = sl // CHUNK
    tok = functools.partial(_tok_spec, nseg, sl, nt)
    state = pl.BlockSpec((nseg, CONV_W - 1, 2 * D_MODEL), lambda bi, i: (bi, 0, 0))
    return pl.pallas_call(
        functools.partial(_inproj_a_kernel, nseg=nseg, sl=sl, lc=CHUNK),
        grid=(b // nseg, nt),
        in_specs=[tok(D_MODEL), _const_spec((1, D_MODEL)), _const_block((D_MODEL, 2 * D_MODEL), (0, 0)),
                  _const_block((D_MODEL, D_MODEL), (0, 2)), _const_block((D_MODEL, D_MODEL), (0, 3)),
                  _const_block((D_MODEL, LANES), (0, 4 * D_MODEL // LANES)), _const_spec((2 * HEADS, 1)),
                  _const_spec((CONV_W, 2 * D_MODEL)), _const_spec((1, 2 * D_MODEL)), state],
        out_specs=[tok(D_MODEL), tok(D_MODEL),
                   pl.BlockSpec((nseg, per_seq, D_MODEL, CHUNK), lambda bi, i: (bi, i, 0, 0)),
                   tok(D_MODEL), tok(D_MODEL),
                   pl.BlockSpec((nseg, per_seq, 4 * HEADS, CHUNK), lambda bi, i: (bi, i, 0, 0)),
                   state],
        out_shape=[jax.ShapeDtypeStruct((n, D_MODEL), BF16), jax.ShapeDtypeStruct((n, D_MODEL), BF16),
                   jax.ShapeDtypeStruct((b, l // CHUNK, D_MODEL, CHUNK), BF16),
                   jax.ShapeDtypeStruct((n, D_MODEL), BF16), jax.ShapeDtypeStruct((n, D_MODEL), BF16),
                   jax.ShapeDtypeStruct((b, l // CHUNK, 4 * HEADS, CHUNK), F32),
                   jax.ShapeDtypeStruct((b, CONV_W - 1, 2 * D_MODEL), F32)],
        scratch_shapes=[pltpu.VMEM((nseg, CONV_PAD, 2 * D_MODEL), F32)],
        compiler_params=_cparams(("parallel", "arbitrary")),
        name="inproj_a",
    )(x2d, g, wa, wa, wa, wa, bif, cw, cb, cs)


def _inproj_b_kernel(x_ref, wu_ref, wgv_ref, wmq_ref, wgate_ref, lng_ref, lnb_ref, ws_ref, bst_ref,
                     mk_ref, mv_ref, ug_ref, att_ref, gates_ref, *rest, nseg, sl, cl, emit_gv):
    tm = nseg * sl
    xn = x_ref[...]

    gates_ref[...] = _sigmoid(_dot_nt(xn, wgate_ref[...])).astype(BF16)

    gvr = jax.nn.gelu(_dot_nt(xn, wgv_ref[...]))
    mu = jnp.mean(gvr, axis=-1, keepdims=True)
    xc = gvr - mu
    r = lax.rsqrt(jnp.mean(xc * xc, axis=-1, keepdims=True) + EPS)
    gv = (xc * r) * lng_ref[...] + lnb_ref[...]
    if emit_gv:
        rest[0][...] = gv
    gvb = gv.astype(BF16)
    u = jax.nn.gelu(_dot_nt(xn, wu_ref[...]))
    tri = (lax.broadcasted_iota(jnp.int32, (cl, cl), 0) >= lax.broadcasted_iota(jnp.int32, (cl, cl), 1))
    for gi in range(GMLP_GROUPS):
        wsg = jnp.where(tri, ws_ref[gi], 0.0).astype(BF16)
        lo, hi = gi * GMLP_GROUP_DIM, (gi + 1) * GMLP_GROUP_DIM
        for c in range(tm // cl):
            sp = _dot(wsg, gvb[c * cl:(c + 1) * cl, lo:hi]) + bst_ref[:, gi:gi + 1]
            ug_ref[c * cl:(c + 1) * cl, lo:hi] = (u[c * cl:(c + 1) * cl, lo:hi] * sp).astype(BF16)

    mq = _dot_nt(xn, wmq_ref[...]).astype(BF16)

    def mem_head(ref, s, h):
        if len(ref.shape) == 4:
            return ref[s, :, h, :].astype(BF16)
        return ref[s, :, h * HEAD_DIM:(h + 1) * HEAD_DIM]

    for s in range(nseg):
        r0, r1 = s * sl, (s + 1) * sl
        for h in range(HEADS):
            lo, hi = h * HEAD_DIM, (h + 1) * HEAD_DIM
            sc = _dot_nt(mq[r0:r1, lo:hi], mem_head(mk_ref, s, h)) * (HEAD_DIM ** -0.5)
            e = jnp.exp(sc - jnp.max(sc, axis=-1, keepdims=True))
            a = (e / jnp.sum(e, axis=-1, keepdims=True)).astype(BF16)
            att_ref[r0:r1, lo:hi] = _dot(a, mem_head(mv_ref, s, h)).astype(BF16)


def _inproj_b(xn2d, b, l, wb, lng, lnb, ws, bst, mk, mv, cl, emit_gv):
    nseg, sl = _tile_geometry(b, l, INPROJ_B_TILE)
    nt = l // sl
    n = b * l
    tok = functools.partial(_tok_spec, nseg, sl, nt)
    if mk.ndim == 4:
        mem = pl.BlockSpec((nseg, N_MEM, HEADS, HEAD_DIM), lambda bi, i: (bi, 0, 0, 0), pipeline_mode=pl.Buffered(1))
    else:
        mem = pl.BlockSpec((nseg, N_MEM, D_MODEL), lambda bi, i: (bi, 0, 0))
    out_specs = [tok(D_MODEL), tok(D_MODEL), tok(3 * D_MODEL)]
    out_shape = [jax.ShapeDtypeStruct((n, D_MODEL), BF16), jax.ShapeDtypeStruct((n, D_MODEL), BF16),
                 jax.ShapeDtypeStruct((n, 3 * D_MODEL), BF16)]
    if emit_gv:
        out_specs.append(tok(D_MODEL))
        out_shape.append(jax.ShapeDtypeStruct((n, D_MODEL), F32))
    return pl.pallas_call(
        functools.partial(_inproj_b_kernel, nseg=nseg, sl=sl, cl=cl, emit_gv=emit_gv),
        grid=(b // nseg, nt),
        in_specs=[tok(D_MODEL), _const_block((D_MODEL, D_MODEL), (0, 0)),
                  _const_block((D_MODEL, D_MODEL), (1, 0)), _const_block((D_MODEL, D_MODEL), (2, 0)),
                  _const_block((3 * D_MODEL, D_MODEL), (1, 0)), _const_spec((1, D_MODEL)), _const_spec((1, D_MODEL)),
                  _const_spec((GMLP_GROUPS, cl, cl)), _const_spec((cl, GMLP_GROUPS)), mem, mem],
        out_specs=out_specs,
        out_shape=out_shape,
        compiler_params=_cparams(("parallel", "parallel")),
        name="inproj_b",
    )(xn2d, wb, wb, wb, wb, lng, lnb, ws, bst, mk, mv)


def _mlstm_kernel(q_ref, kt_ref, v_ref, gt_ref, *rest, nseg, cb, zero_state):
    i = pl.program_id(1)

    if zero_state:
        hm_ref, c_ref, n_ref, m_ref, st_ref = rest
    else:
        c0_ref, n0_ref, m0_ref, hm_ref, c_ref, n_ref, m_ref, st_ref = rest

    @pl.when(i == 0)
    def _():
        if zero_state:
            st_ref[...] = jnp.zeros(st_ref.shape, F32)
            m_ref[...] = jnp.zeros(m_ref.shape, F32)
        else:
            st_ref[:, :, :, :HEAD_DIM] = c0_ref[...]
            st_ref[:, :, :, HEAD_DIM:] = n0_ref[...]
            m_ref[...] = m0_ref[...]

    for s in range(nseg):
        _mlstm_sequence(q_ref, kt_ref.at[s], v_ref, gt_ref.at[s], st_ref.at[s], m_ref.at[s], hm_ref, s * cb, cb)

    @pl.when(i == pl.num_programs(1) - 1)
    def _():
        c_ref[...] = st_ref[:, :, :, :HEAD_DIM]
        n_ref[...] = st_ref[:, :, :, HEAD_DIM:]


def _mlstm_sequence(q_ref, kt_ref, v_ref, gt_ref, c_ref, m_ref, hm_ref, row0, cb):
    L = kt_ref.shape[-1]
    nch = cb // L
    ti = lax.broadcasted_iota(jnp.int32, (L, L), 0)
    si = lax.broadcasted_iota(jnp.int32, (L, L), 1)
    tri = ti >= si
    eye = ti == si

    rows = 4 * HEADS
    g_all = gt_ref[...].reshape(nch * rows, L)

    m_in = [m_ref[:, 0:1]]
    for c in range(nch):
        b_last4 = g_all[c * rows + HEADS:c * rows + 2 * HEADS, L - 1:L]
        amax4 = g_all[c * rows + 2 * HEADS:c * rows + 3 * HEADS, 0:1]
        m_in.append(jnp.maximum(b_last4 + m_in[-1], b_last4 + amax4))

    ones = jnp.ones((L, LANES), BF16)
    st = [c_ref[h] for h in range(HEADS)]
    for c in range(nch):
        r0, r1 = row0 + c * L, row0 + (c + 1) * L
        for h in range(HEADS):
            lo, hi = h * HEAD_DIM, (h + 1) * HEAD_DIM
            ig_r = g_all[c * rows + h:c * rows + h + 1, :]
            bc_r = g_all[c * rows + HEADS + h:c * rows + HEADS + h + 1, :]
            a_r = ig_r - bc_r
            bc_c = jnp.sum(jnp.where(eye, bc_r, 0.0), axis=-1, keepdims=True)
            m0 = m_in[c][h:h + 1, :]
            m_last = m_in[c + 1][h:h + 1, :]
            dmat = jnp.where(tri, bc_c + a_r, NEG_INF)
            inter = bc_c + m0
            m = jnp.maximum(inter, jnp.max(dmat, axis=-1, keepdims=True))
            w_intra = jnp.exp(dmat - m)
            w_inter = jnp.exp(inter - m)
            q = q_ref[r0:r1, lo:hi]
            kt = kt_ref[c, lo:hi, :]
            v = v_ref[r0:r1, lo:hi]
            s = _dot(q, kt) * w_intra
            qs = _dot(q, st[h].astype(BF16))
            num = w_inter * qs[:, :HEAD_DIM] + _dot(s.astype(BF16), v)
            den = w_inter * qs[:, HEAD_DIM:HEAD_DIM + 1] + jnp.sum(s, axis=-1, keepdims=True)
            hh = num / jnp.maximum(jnp.abs(den), jnp.exp(-m))
            bc_last = bc_r[:, L - 1:L]
            w_last = jnp.exp(bc_last + a_r - m_last)
            decay = jnp.exp(bc_last + m0 - m_last)
            ktw = (kt.astype(F32) * w_last).astype(BF16)
            st[h] = decay * st[h] + _dot(ktw, jnp.concatenate([v, ones], axis=1))
            hm_ref[r0:r1, lo:hi] = hh.astype(BF16)

    for h in range(HEADS):
        c_ref[h] = st[h]
    m_ref[...] = jnp.broadcast_to(m_in[nch], (HEADS, LANES))


def _mlstm(q, kt, v, gt, state, b, l):
    nseg, cb = _tile_geometry(b, l, MLSTM_TILE)
    nt = l // cb
    CHUNK = kt.shape[-1]
    tok = _tok_spec(nseg, cb, nt, D_MODEL)
    cs = pl.BlockSpec((nseg, HEADS, HEAD_DIM, HEAD_DIM), lambda bi, i: (bi, 0, 0, 0))
    ns = pl.BlockSpec((nseg, HEADS, HEAD_DIM, LANES), lambda bi, i: (bi, 0, 0, 0))
    ms = pl.BlockSpec((nseg, HEADS, LANES), lambda bi, i: (bi, 0, 0))
    state_specs = [] if state is None else [cs, ns, ms]
    return pl.pallas_call(
        functools.partial(_mlstm_kernel, nseg=nseg, cb=cb, zero_state=state is None),
        grid=(b // nseg, nt),
        in_specs=[tok, pl.BlockSpec((nseg, cb // CHUNK, D_MODEL, CHUNK), lambda bi, i: (bi, i, 0, 0)), tok,
                  pl.BlockSpec((nseg, cb // CHUNK, 4 * HEADS, CHUNK), lambda bi, i: (bi, i, 0, 0))] + state_specs,
        out_specs=[tok, cs, ns, ms],
        out_shape=[jax.ShapeDtypeStruct((b * l, D_MODEL), BF16),
                   jax.ShapeDtypeStruct((b, HEADS, HEAD_DIM, HEAD_DIM), F32),
                   jax.ShapeDtypeStruct((b, HEADS, HEAD_DIM, LANES), F32),
                   jax.ShapeDtypeStruct((b, HEADS, LANES), F32)],
        scratch_shapes=[pltpu.VMEM((nseg, HEADS, HEAD_DIM, HEAD_DIM + LANES), F32)],
        compiler_params=_cparams(("parallel", "arbitrary")),
        name="mlstm",
    )(q, kt, v, gt, *(state or ()))


def _merge_kernel(x_ref, hh_ref, o_ref, ng_ref, ug_ref, att_ref, gates_ref, wa_ref, wb_ref, wc_ref, wo_ref, h_ref):
    parts = []
    for h in range(HEADS):
        lo, hi = h * HEAD_DIM, (h + 1) * HEAD_DIM
        hh = hh_ref[:, lo:hi].astype(F32)
        hn = hh * lax.rsqrt(jnp.mean(hh * hh, axis=-1, keepdims=True) + EPS)
        parts.append(((hn * ng_ref[:, lo:hi]) * o_ref[:, lo:hi].astype(F32)).astype(BF16))
    br = _dot(parts[0], wa_ref[0:HEAD_DIM, :])
    for h in range(1, HEADS):
        br = br + _dot(parts[h], wa_ref[h * HEAD_DIM:(h + 1) * HEAD_DIM, :])
    g = gates_ref[...].astype(F32)
    mixed = g[:, :D_MODEL] * br
    mixed = mixed + g[:, D_MODEL:2 * D_MODEL] * _dot(ug_ref[...], wb_ref[...])
    mixed = mixed + g[:, 2 * D_MODEL:] * _dot(att_ref[...], wc_ref[...])
    h_ref[...] = x_ref[...] + _dot(mixed.astype(BF16), wo_ref[...])


def _merge(x2d, hh, osig, ng, ug, att, gates, wa, wb, wc, wo, tm):
    n = x2d.shape[0]
    row = lambda w: pl.BlockSpec((tm, w), lambda i: (i, 0))
    wspec = _const_spec((D_MODEL, D_MODEL))
    return pl.pallas_call(
        _merge_kernel,
        grid=(n // tm,),
        in_specs=[row(D_MODEL), row(D_MODEL), row(D_MODEL), _const_spec((1, D_MODEL)), row(D_MODEL), row(D_MODEL),
                  row(3 * D_MODEL), wspec, wspec, wspec, wspec],
        out_specs=row(D_MODEL),
        out_shape=jax.ShapeDtypeStruct((n, D_MODEL), F32),
        compiler_params=_cparams(("parallel",)),
        name="merge",
    )(x2d, hh, osig, ng, ug, att, gates, wa, wb, wc, wo)


MOE_T = 256
MOE_STEP_SUBS = 2
MOE_CAP = 32
MOE_S = N_EXPERTS * MOE_CAP
MOE_OVF = 512
SEG_ALIGN = 16
EXPERT_BLOCK = 128


def _moe_route_kernel(hp_ref, hs_ref, g_ref, wrg_ref, brg_ref, wre_ref, bre_ref,
                      main_ref, ovf_ref, info_ref, meta_ref, *, nsp):
    n = MOE_STEP_SUBS * MOE_T
    h = jnp.where(pl.program_id(0) < nsp // MOE_STEP_SUBS, hp_ref[...], hs_ref[...])
    xm = _rmsnorm(h, g_ref[...]).astype(BF16)
    lane = lax.broadcasted_iota(jnp.int32, (n, LANES), 1).astype(F32)
    lg = jnp.where(lane < N_GROUPS, _dot(xm, wrg_ref[...]) + brg_ref[...], NEG_INF)
    gmax = jnp.max(lg, axis=-1, keepdims=True)
    p_top = 1.0 / jnp.sum(jnp.exp(lg - gmax), axis=-1, keepdims=True)
    grp = jnp.min(jnp.where(lg == gmax, lane, float(LANES)), axis=-1, keepdims=True)
    el = _dot(xm, wre_ref[...]) + bre_ref[...]
    in_grp = (lane >= grp * EXPERTS_PER_GROUP) & (lane < (grp + 1.0) * EXPERTS_PER_GROUP)
    vals = jnp.where(in_grp, el, NEG_INF)
    v1 = jnp.max(vals, axis=-1, keepdims=True)
    i1 = jnp.min(jnp.where(vals == v1, lane, float(LANES)), axis=-1, keepdims=True)
    vals2 = jnp.where(lane == i1, NEG_INF, vals)
    v2 = jnp.max(vals2, axis=-1, keepdims=True)
    i2 = jnp.min(jnp.where(vals2 == v2, lane, float(LANES)), axis=-1, keepdims=True)
    r = jnp.exp(v2 - v1)
    p1 = p_top / (1.0 + r)
    p2 = p_top * r / (1.0 + r)
    sel1 = lane == i1
    sel2 = lane == i2
    onehot = jnp.where(sel1 | sel2, 1.0, 0.0)

    deferred = []
    for s in range(MOE_STEP_SUBS):
        rows = slice(s * MOE_T, (s + 1) * MOE_T)
        deferred.append(_route_sub_tile(
            xm[rows, :], onehot[rows, :], i1[rows, :], i2[rows, :],
            p1[rows, :], p2[rows, :], main_ref.at[:, s], ovf_ref.at[s], info_ref.at[rows, :], meta_ref.at[s]))
    for write_overflow in deferred:
        write_overflow()


def _route_sub_tile(xm, onehot, i1, i2, p1, p2, main_ref, ovf_ref, info_ref, meta_ref):
    t = MOE_T
    lane = lax.broadcasted_iota(jnp.int32, (t, LANES), 1)
    sel1 = lane.astype(F32) == i1
    sel2 = lane.astype(F32) == i2
    cnt = jnp.sum(onehot, axis=0, keepdims=True).astype(jnp.int32)
    pn = jnp.bitwise_and(jnp.maximum(cnt - MOE_CAP, 0) + (SEG_ALIGN - 1), -SEG_ALIGN)
    pn8 = jnp.broadcast_to(pn, (8, LANES))
    earlier = jnp.where(lax.broadcasted_iota(jnp.int32, (LANES, LANES), 0)
                        < lax.broadcasted_iota(jnp.int32, (LANES, LANES), 1), 1.0, 0.0).astype(BF16)
    off_f8 = _dot(pn8.astype(F32).astype(BF16), earlier)
    row8 = lax.broadcasted_iota(jnp.int32, (8, LANES), 0)
    meta_ref[...] = jnp.where(row8 == 0, off_f8.astype(jnp.int32), jnp.where(row8 == 1, pn8, 0))

    ti = lax.broadcasted_iota(jnp.int32, (t, t), 0)
    si = lax.broadcasted_iota(jnp.int32, (t, t), 1)
    before = jnp.where(ti > si, 1.0, 0.0).astype(BF16)
    rank = _dot(before, onehot.astype(BF16))
    off_f = off_f8[0:1, :]

    def slot_row(sel, idx):
        rk = jnp.sum(jnp.where(sel, rank, 0.0), axis=-1, keepdims=True)
        of = jnp.sum(jnp.where(sel, off_f, 0.0), axis=-1, keepdims=True)
        return jnp.where(rk < MOE_CAP, idx * MOE_CAP + rk, MOE_S - MOE_CAP + of + rk)

    pos1 = slot_row(sel1, i1)
    pos2 = slot_row(sel2, i2)
    info_ref[...] = (jnp.where(lane == 0, pos1, 0.0) + jnp.where(lane == 1, pos2, 0.0)
                     + jnp.where(lane == 2, p1, 0.0) + jnp.where(lane == 3, p2, 0.0))

    eye = ti == si
    pos1_r = jnp.sum(jnp.where(eye, pos1, 0.0), axis=0, keepdims=True)
    pos2_r = jnp.sum(jnp.where(eye, pos2, 0.0), axis=0, keepdims=True)

    def gather(rows, base):
        srow = (lax.broadcasted_iota(jnp.int32, (rows, t), 0) + base).astype(F32)
        pick = jnp.where((srow == pos1_r) | (srow == pos2_r), 1.0, 0.0).astype(BF16)
        return _dot(pick, xm).astype(BF16)

    main_ref[...] = gather(MOE_S, 0).reshape(N_EXPERTS, MOE_CAP, D_MODEL)
    has_ovf = jnp.sum(pn) > 0

    def write_overflow():
        @pl.when(has_ovf)
        def _():
            ovf_ref[...] = gather(MOE_OVF, MOE_S)

        @pl.when(jnp.logical_not(has_ovf))
        def _():
            ovf_ref[...] = jnp.zeros(ovf_ref.shape, ovf_ref.dtype)
    return write_overflow


def _two_group_specs(nsp, cols):
    rows = MOE_STEP_SUBS * MOE_T
    npb = nsp // MOE_STEP_SUBS
    return (pl.BlockSpec((rows, cols), lambda j, *_: (jnp.minimum(j, npb - 1), 0)),
            pl.BlockSpec((rows, cols), lambda j, *_: (jnp.maximum(j - npb, 0), 0)))


def _moe_route(hp, hs, g, wrg, brg, wre, bre):
    nsp = hp.shape[0] // MOE_T
    ns = nsp + hs.shape[0] // MOE_T
    sub = MOE_STEP_SUBS
    return pl.pallas_call(
        functools.partial(_moe_route_kernel, nsp=nsp),
        grid=(ns // sub,),
        in_specs=[*_two_group_specs(nsp, D_MODEL), _const_spec((1, D_MODEL)),
                  _const_spec((D_MODEL, LANES)), _const_spec((1, LANES)),
                  _const_spec((D_MODEL, LANES)), _const_spec((1, LANES))],
        out_specs=[pl.BlockSpec((N_EXPERTS, sub, MOE_CAP, D_MODEL), lambda j: (0, j, 0, 0)),
                   pl.BlockSpec((sub, MOE_OVF, D_MODEL), lambda j: (j, 0, 0)),
                   pl.BlockSpec((sub * MOE_T, LANES), lambda j: (j, 0)),
                   pl.BlockSpec((sub, 8, LANES), lambda j: (j, 0, 0))],
        out_shape=[jax.ShapeDtypeStruct((N_EXPERTS, ns, MOE_CAP, D_MODEL), BF16),
                   jax.ShapeDtypeStruct((ns, MOE_OVF, D_MODEL), BF16),
                   jax.ShapeDtypeStruct((ns * MOE_T, LANES), F32),
                   jax.ShapeDtypeStruct((ns, 8, LANES), jnp.int32)],
        compiler_params=_cparams(("arbitrary",)),
        name="moe_route",
    )(hp, hs, g, wrg, brg, wre, bre)


def _swiglu(x, wgu, wdb):
    gu = _dot(x, wgu[...])
    gate = gu[:, :D_EXPERT]
    hid = (gate * _sigmoid(gate)) * gu[:, D_EXPERT:]
    return _dot(hid.astype(BF16), wdb[...]).astype(BF16)


def _moe_expert_kernel(offs_ref, pns_ref, eflag_ref, main_ref, slots_hbm, wg_ref, wu_ref, wd_ref,
                       mout_ref, out_hbm, xbuf, obuf, wgu, wdb, sem_in, sem_out, *, n_sub, jb, n_rb):
    e = pl.program_id(0)
    rb = pl.program_id(1)

    @pl.when((e == 0) & (rb == 0))
    def _():
        xbuf[...] = jnp.zeros_like(xbuf)

    @pl.when(rb == 0)
    def _():
        wgu[:, :D_EXPERT] = wg_ref[...].astype(BF16)
        wgu[:, D_EXPERT:] = wu_ref[...].astype(BF16)
        wdb[...] = wd_ref[...].astype(BF16)

    sb = main_ref.shape[0]
    part = sb // 2 if sb % 2 == 0 else sb
    for s0 in range(0, sb, part):
        y = _swiglu(main_ref[s0:s0 + part].reshape(part * MOE_CAP, D_MODEL), wgu, wdb)
        mout_ref[s0:s0 + part] = y.reshape(part, MOE_CAP, D_MODEL)

    def copy_in(j, src, dst):
        return pltpu.make_async_copy(slots_hbm.at[j, pl.ds(src, SEG_ALIGN), :],
                                     xbuf.at[pl.ds(dst, SEG_ALIGN), :], sem_in)

    def copy_out(j, src, dst):
        return pltpu.make_async_copy(obuf.at[pl.ds(src, SEG_ALIGN), :],
                                     out_hbm.at[j, pl.ds(dst, SEG_ALIGN), :], sem_out)

    def for_each_chunk(g, fn):
        def seg(jj, cur):
            j = g * jb + jj
            n = pns_ref[j * N_EXPERTS + e]
            off = offs_ref[j * N_EXPERTS + e]

            def chunk(k, c):
                fn(j, pl.multiple_of(off + k * SEG_ALIGN, SEG_ALIGN), pl.multiple_of(cur + k * SEG_ALIGN, SEG_ALIGN))
                return c
            lax.fori_loop(0, n // SEG_ALIGN, chunk, 0)
            return cur + n
        return lax.fori_loop(0, jb, seg, 0)

    def group(g, carry):
        total = for_each_chunk(g, lambda j, r, b: copy_in(j, r, b).start())
        nchunk = total // SEG_ALIGN

        def wait_in(k, c):
            copy_in(0, 0, 0).wait()
            return c
        lax.fori_loop(0, nchunk, wait_in, 0)

        def block(bi, c):
            r0 = pl.multiple_of(bi * EXPERT_BLOCK, EXPERT_BLOCK)
            obuf[pl.ds(r0, EXPERT_BLOCK), :] = _swiglu(xbuf[pl.ds(r0, EXPERT_BLOCK), :], wgu, wdb)
            return c
        lax.fori_loop(0, (total + EXPERT_BLOCK - 1) // EXPERT_BLOCK, block, 0)

        for_each_chunk(g, lambda j, r, b: copy_out(j, b, r).start())

        def wait_out(k, c):
            copy_out(0, 0, 0).wait()
            return c
        lax.fori_loop(0, nchunk, wait_out, 0)
        return carry

    @pl.when((rb == n_rb - 1) & (eflag_ref[e] > 0))
    def _():
        lax.fori_loop(0, n_sub // jb, group, 0)


def _moe_expert(main, ovf, offs, pns, eflag, wg, wu, wd):
    ns = main.shape[1]
    n_rb = 1
    sb = ns // n_rb
    jb = max(d for d in range(1, 12) if ns % d == 0)
    rows = jb * MOE_T + EXPERT_BLOCK
    mspec = pl.BlockSpec((None, sb, MOE_CAP, D_MODEL), lambda e, rb, *_: (e, rb, 0, 0))
    wspec = lambda a, b: pl.BlockSpec((None, a, b), lambda e, rb, *_: (e, 0, 0))
    grid_spec = pltpu.PrefetchScalarGridSpec(
        num_scalar_prefetch=3,
        grid=(N_EXPERTS, n_rb),
        in_specs=[mspec, pl.BlockSpec(memory_space=pl.ANY),
                  wspec(D_MODEL, D_EXPERT), wspec(D_MODEL, D_EXPERT), wspec(D_EXPERT, D_MODEL)],
        out_specs=[mspec, pl.BlockSpec(memory_space=pl.ANY)],
        scratch_shapes=[pltpu.VMEM((rows, D_MODEL), BF16), pltpu.VMEM((rows, D_MODEL), BF16),
                        pltpu.VMEM((D_MODEL, 2 * D_EXPERT), BF16), pltpu.VMEM((D_EXPERT, D_MODEL), BF16),
                        pltpu.SemaphoreType.DMA(()), pltpu.SemaphoreType.DMA(())],
    )
    mout, oout = pl.pallas_call(
        functools.partial(_moe_expert_kernel, n_sub=ns, jb=jb, n_rb=n_rb),
        grid_spec=grid_spec,
        out_shape=[jax.ShapeDtypeStruct(main.shape, main.dtype), jax.ShapeDtypeStruct(ovf.shape, ovf.dtype)],
        input_output_aliases={4: 1},
        compiler_params=_cparams(("arbitrary", "arbitrary")),
        name="moe_expert",
    )(offs, pns, eflag, main, ovf, wg, wu, wd)
    return mout, oout


def _moe_combine_kernel(jflag_ref, sflag_ref, hp_ref, hs_ref, main_ref, ovf_ref, info_ref, gf_ref,
                        yp_ref, ys_ref, acc_ref, *, nsp):
    j = pl.program_id(0)
    npb = nsp // MOE_STEP_SUBS

    def weights(s, cols, base):
        info = info_ref[s * MOE_T:(s + 1) * MOE_T, :]
        scol = (lax.broadcasted_iota(jnp.int32, (MOE_T, cols), 1) + base).astype(F32)
        return (jnp.where(scol == info[:, 0:1], info[:, 2:3], 0.0)
                + jnp.where(scol == info[:, 1:2], info[:, 3:4], 0.0)).astype(BF16)

    for s in range(MOE_STEP_SUBS):
        rows = slice(s * MOE_T, (s + 1) * MOE_T)
        h = jnp.where(j < npb, hp_ref[rows, :], hs_ref[rows, :])
        acc_ref[rows, :] = h + _dot(weights(s, MOE_S, 0), main_ref[:, s].reshape(MOE_S, D_MODEL))

    for s in range(MOE_STEP_SUBS):
        @pl.when(jflag_ref[j * MOE_STEP_SUBS + s] > 0)
        def _(s=s):
            rows = slice(s * MOE_T, (s + 1) * MOE_T)
            acc_ref[rows, :] += _dot(weights(s, MOE_OVF, MOE_S), ovf_ref[s])

    y = _rmsnorm(acc_ref[...], gf_ref[...])

    @pl.when(j < npb)
    def _():
        yp_ref[...] = y

    @pl.when(j >= npb)
    def _():
        ys_ref[...] = y


def _moe_combine(hp, hs, main, ovf, info, jflag, gf):
    nsp = hp.shape[0] // MOE_T
    ns = main.shape[1]
    sub = MOE_STEP_SUBS
    sflag = jnp.max(jflag.reshape(ns // sub, sub), axis=1)
    grid_spec = pltpu.PrefetchScalarGridSpec(
        num_scalar_prefetch=2,
        grid=(ns // sub,),
        in_specs=[*_two_group_specs(nsp, D_MODEL),
                  pl.BlockSpec((N_EXPERTS, sub, MOE_CAP, D_MODEL), lambda j, jf, sf: (0, j, 0, 0)),
                  pl.BlockSpec((sub, MOE_OVF, D_MODEL), lambda j, jf, sf: (jnp.where(sf[j] > 0, j, 0), 0, 0)),
                  pl.BlockSpec((sub * MOE_T, LANES), lambda j, jf, sf: (j, 0)),
                  pl.BlockSpec((1, D_MODEL), lambda j, jf, sf: (0, 0))],
        out_specs=list(_two_group_specs(nsp, D_MODEL)),
        scratch_shapes=[pltpu.VMEM((sub * MOE_T, D_MODEL), F32)],
    )
    return pl.pallas_call(
        functools.partial(_moe_combine_kernel, nsp=nsp),
        grid_spec=grid_spec,
        out_shape=[jax.ShapeDtypeStruct(hp.shape, F32), jax.ShapeDtypeStruct(hs.shape, F32)],
        compiler_params=_cparams(("arbitrary",)),
        name="moe_combine",
    )(jflag, sflag, hp, hs, main, ovf, info, gf)


def _moe(hp, hs, p):
    main, ovf, info, meta = _moe_route(hp, hs, p["norm_moe_g"], p["w_rg"], p["b_rg"], p["w_re"], p["b_re"])
    pn = meta[:, 1, :N_EXPERTS]
    offs = meta[:, 0, :N_EXPERTS].reshape(-1)
    eflag = (jnp.sum(pn, axis=0) > 0).astype(jnp.int32)
    jflag = (jnp.sum(pn, axis=1) > 0).astype(jnp.int32)
    main, ovf = _moe_expert(main, ovf, offs, pn.reshape(-1), eflag, p["w_eg"], p["w_eu"], p["w_ed"])
    return _moe_combine(hp, hs, main, ovf, info, jflag, p["norm_final_g"])


def _trunk(x, mem_k, mem_v, c0, n0, m0, conv0, p, *, emit_gv):
    b, l, _ = x.shape
    n = b * l
    cl = min(l, GMLP_CHUNK)
    x2d = x.reshape(n, D_MODEL)
    xn, q, kt, v, osig, gt, conv_new = _inproj_a(
        x2d, b, l, p["norm_mix_g"], p["w_a"], p["b_if"], p["conv_w"], p["conv_b"], conv0)
    outs = _inproj_b(xn, b, l, p["w_b"], p["gmlp_norm_g"], p["gmlp_norm_b"], p["w_s"][:, :cl, :cl], p["b_st"][:cl],
                     mem_k, mem_v, cl, emit_gv)
    ug, att, gates = outs[:3]
    on_lanes = lambda a: jnp.broadcast_to(a[..., None], a.shape + (LANES,))
    state = None if c0 is None else (c0, on_lanes(n0), on_lanes(m0))
    hh, c1, n1, m1 = _mlstm(q, kt, v, gt, state, b, l)
    n1 = n1[..., 0]
    h = _merge(x2d, hh, osig, p["mlstm_norm_g"], ug, att, gates,
               p["w_br_mlstm"], p["w_br_gmlp"], p["w_br_mem"], p["w_out"], TOKEN_TILE)
    return h, c1, n1, m1[..., 0], conv_new, (outs[3].reshape(b, l, D_MODEL) if emit_gv else None)


def kernel(x_prompt, x_sample, mem_prompt, cache_mem_k, cache_mem_v, state_mlstm_C, state_mlstm_n, state_mlstm_m, state_mlstm_conv, norm_mix_g, w_in, mlstm_i_b, mlstm_f_b, mlstm_conv_w, mlstm_conv_b, mlstm_norm_g, gmlp_norm_g, gmlp_norm_b, gmlp_w_s, gmlp_b_s, mem_norm_g, w_mem_k, w_mem_v, w_br_mlstm, w_br_gmlp, w_br_mem, w_out, norm_moe_g, w_router_group, b_router_group, w_router_expert, b_router_expert, w_exp_gate, w_exp_up, w_exp_down, norm_final_g):
    bp = x_prompt.shape[0]
    bs = x_sample.shape[0]
    W = D_MODEL
    wt = w_in[0].T
    o_u = 4 * W + 2 * HEADS
    w_a = _w_cols(wt, 4 * W + LANES)
    w_b = _w_rows(wt, o_u, (wt.shape[0] - o_u) // W)
    row = lambda a: a.reshape(1, -1)
    pad_l = lambda a: jnp.pad(a, ((0, 0), (0, LANES - a.shape[1])))
    p = {
        "norm_mix_g": row(norm_mix_g[0]),
        "w_a": w_a, "w_b": w_b,
        "b_if": jnp.concatenate([mlstm_i_b[0], mlstm_f_b[0]]).reshape(2 * HEADS, 1),
        "conv_w": mlstm_conv_w[0], "conv_b": row(mlstm_conv_b[0]),
        "gmlp_norm_g": row(gmlp_norm_g[0]), "gmlp_norm_b": row(gmlp_norm_b[0]),
        "w_s": gmlp_w_s[0], "b_st": gmlp_b_s[0].T,
        "mlstm_norm_g": row(mlstm_norm_g[0]),
        "w_br_mlstm": w_br_mlstm[0].astype(BF16), "w_br_gmlp": w_br_gmlp[0].astype(BF16),
        "w_br_mem": w_br_mem[0].astype(BF16), "w_out": w_out[0].astype(BF16),
        "norm_moe_g": row(norm_moe_g[0]),
        "w_rg": pad_l(w_router_group[0]).astype(BF16), "b_rg": pad_l(row(b_router_group[0])),
        "w_re": pad_l(w_router_expert[0]).astype(BF16), "b_re": pad_l(row(b_router_expert[0])),
        "w_eg": w_exp_gate[0], "w_eu": w_exp_up[0], "w_ed": w_exp_down[0],
        "norm_final_g": row(norm_final_g),
    }

    mk_p, mv_p, mk_pb, mv_pb = _memory_kv(mem_prompt.reshape(bp * N_MEM, W), row(mem_norm_g[0]),
                                          w_mem_k[0], w_mem_v[0])

    zeros = lambda *s: jnp.zeros(s, F32)
    hp, cp, np_, mp, cvp, _ = _trunk(
        x_prompt, mk_pb.reshape(bp, N_MEM, W), mv_pb.reshape(bp, N_MEM, W),
        None, None, None, zeros(bp, CONV_W - 1, 2 * W), p, emit_gv=False)
    hs, cs, ns, ms, cvs, gvs = _trunk(
        x_sample, cache_mem_k[0], cache_mem_v[0],
        state_mlstm_C[0], state_mlstm_n[0], state_mlstm_m[0], state_mlstm_conv[0], p,
        emit_gv=True)

    yp, ys = _moe(hp, hs, p)
    return (yp.reshape(x_prompt.shape), ys.reshape(x_sample.shape), mk_p[None], mv_p[None],
            cp[None], np_[None], mp[None], cvp[None],
            cs[None], ns[None], ms[None], cvs[None], gvs[None])
```

```python
import functools

import jax
import jax.numpy as jnp
from jax import lax
from jax.experimental import pallas as pl
from jax.experimental.pallas import tpu as pltpu

D_MODEL = 1024
MLSTM_BLOCK = 512
TOKEN_TILE = 512
INPROJ_B_TILE = 1024
MLSTM_TILE = 2048
EPS = 1e-6
HEADS = 4
HEAD_DIM = 256
CONV_W = 4
GMLP_GROUPS = 4
GMLP_GROUP_DIM = 256
GMLP_CHUNK = 128
N_MEM = 256
N_GROUPS = 4
EXPERTS_PER_GROUP = 8
N_EXPERTS = 32
D_EXPERT = 256
LANES = 128
CONV_PAD = 8
NORM_ROWS = 128

F32 = jnp.float32
BF16 = jnp.bfloat16
NEG_INF = float("-inf")

VMEM_LIMIT = 56 * 1024 * 1024


def _cparams(sem):
    return pltpu.CompilerParams(dimension_semantics=sem, vmem_limit_bytes=VMEM_LIMIT)


def _const_spec(shape):
    nd = len(shape)
    return pl.BlockSpec(shape, lambda *_: (0,) * nd, pipeline_mode=pl.Buffered(1))


def _const_block(shape, idx):
    return pl.BlockSpec(shape, lambda *_: idx, pipeline_mode=pl.Buffered(1))


def _sigmoid(x):
    return 0.5 * (jnp.tanh(0.5 * x) + 1.0)


def _log_sigmoid(x):
    return jnp.minimum(x, 0.0) - jnp.log(1.0 + jnp.exp(-jnp.abs(x)))


def _rmsnorm(x, g):
    r = lax.rsqrt(jnp.mean(x * x, axis=-1, keepdims=True) + EPS)
    return (x * r) * g


def _dot(a, b):
    return jnp.dot(a, b, preferred_element_type=F32)


def _dot_nt(a, b):
    return lax.dot_general(a, b, (((1,), (1,)), ((), ())), preferred_element_type=F32)


def _memkv_kernel(mem_ref, g_ref, wk_ref, wv_ref, k_ref, v_ref, kb_ref, vb_ref):
    mn = _rmsnorm(mem_ref[...], g_ref[...]).astype(BF16)
    k = _dot(mn, wk_ref[...].astype(BF16))
    v = _dot(mn, wv_ref[...].astype(BF16))
    for s in range(k_ref.shape[0]):
        for h in range(HEADS):
            k_ref[s, :, h, :] = k[s * N_MEM:(s + 1) * N_MEM, h * HEAD_DIM:(h + 1) * HEAD_DIM]
            v_ref[s, :, h, :] = v[s * N_MEM:(s + 1) * N_MEM, h * HEAD_DIM:(h + 1) * HEAD_DIM]
    kb_ref[...] = k.astype(BF16)
    vb_ref[...] = v.astype(BF16)


def _memory_kv(mem2d, g, wk, wv):
    n = mem2d.shape[0]
    tm = TOKEN_TILE
    row = pl.BlockSpec((tm, D_MODEL), lambda i: (i, 0))
    cache = pl.BlockSpec((tm // N_MEM, N_MEM, HEADS, HEAD_DIM), lambda i: (i, 0, 0, 0))
    return pl.pallas_call(
        _memkv_kernel,
        grid=(n // tm,),
        in_specs=[row, _const_spec((1, D_MODEL)), _const_spec((D_MODEL, D_MODEL)),
                  _const_spec((D_MODEL, D_MODEL))],
        out_specs=[cache, cache, row, row],
        out_shape=[jax.ShapeDtypeStruct((n // N_MEM, N_MEM, HEADS, HEAD_DIM), F32)] * 2
        + [jax.ShapeDtypeStruct((n, D_MODEL), BF16)] * 2,
        compiler_params=_cparams(("parallel",)),
        name="memory_kv",
    )(mem2d, g, wk, wv)


W_COLS_BLOCK = 1408


def _w_cols_kernel(w_ref, o_ref):
    o_ref[...] = w_ref[...].T.astype(BF16)


def _w_cols(wt, ncols):
    rb = W_COLS_BLOCK
    return pl.pallas_call(
        _w_cols_kernel,
        grid=(ncols // rb,),
        in_specs=[pl.BlockSpec((rb, D_MODEL), lambda j: (j, 0))],
        out_specs=pl.BlockSpec((D_MODEL, rb), lambda j: (0, j)),
        out_shape=jax.ShapeDtypeStruct((D_MODEL, ncols), BF16),
        compiler_params=_cparams(("parallel",)),
        name="w_cols",
    )(wt)


def _w_rows_kernel(w_hbm, o_ref, buf, sem, *, row0, nblk):
    i = pl.program_id(0)

    def fetch(blk, slot):
        start = pl.multiple_of(row0 + blk * D_MODEL, 8)
        return pltpu.make_async_copy(w_hbm.at[pl.ds(start, D_MODEL), :], buf.at[slot], sem.at[slot])

    @pl.when(i == 0)
    def _():
        fetch(0, 0).start()

    @pl.when(i + 1 < nblk)
    def _():
        fetch(i + 1, (i + 1) % 2).start()

    fetch(i, i % 2).wait()
    o_ref[...] = buf[i % 2].astype(BF16)


def _w_rows(wt, row0, nblk):
    return pl.pallas_call(
        functools.partial(_w_rows_kernel, row0=row0, nblk=nblk),
        grid=(nblk,),
        in_specs=[pl.BlockSpec(memory_space=pl.ANY)],
        out_specs=pl.BlockSpec((D_MODEL, D_MODEL), lambda i: (i, 0)),
        out_shape=jax.ShapeDtypeStruct((nblk * D_MODEL, D_MODEL), BF16),
        scratch_shapes=[pltpu.VMEM((2, D_MODEL, D_MODEL), F32), pltpu.SemaphoreType.DMA((2,))],
        compiler_params=_cparams(("arbitrary",)),
        name="w_rows",
    )(wt)


def _inproj_a_kernel(x_ref, g_ref, wqk_ref, wv_ref, wo_ref, wif_ref, bif_ref, cw_ref, cb_ref, cs_ref,
                     xn_ref, q_ref, kt_ref, v_ref, o_ref, gt_ref, cn_ref, ext_ref, *, nseg, sl, lc):
    i = pl.program_id(1)
    tm = nseg * sl
    tail = CONV_PAD - (CONV_W - 1)

    @pl.when(i == 0)
    def _():
        for s in range(nseg):
            ext_ref[s, 0:tail, :] = jnp.zeros((tail, 2 * D_MODEL), F32)
            ext_ref[s, tail:CONV_PAD, :] = cs_ref[s]

    parts = [_rmsnorm(x_ref[r:r + NORM_ROWS, :], g_ref[...]).astype(BF16) for r in range(0, tm, NORM_ROWS)]
    xn = jnp.concatenate(parts, axis=0)
    xn_ref[...] = xn

    zqk = jnp.concatenate([_dot(part, wqk_ref[...]) for part in parts], axis=0)
    ks = []
    row8 = lax.broadcasted_iota(jnp.int32, (CONV_PAD, 2 * D_MODEL), 0)
    for s in range(nseg):
        cur = zqk[s * sl:(s + 1) * sl, :]
        prev = ext_ref[s]
        acc = cb_ref[...] + cur * cw_ref[CONV_W - 1:CONV_W, :]
        for d in range(1, CONV_W):
            back = pltpu.roll(cur, d, axis=0)
            head = jnp.where(row8 < d, pltpu.roll(prev, d, axis=0), back[0:CONV_PAD, :])
            back = jnp.concatenate([head, back[CONV_PAD:, :]], axis=0)
            acc = acc + back * cw_ref[CONV_W - 1 - d:CONV_W - d, :]
        qk = acc * _sigmoid(acc)
        q_ref[s * sl:(s + 1) * sl, :] = (qk[:, :D_MODEL] * (HEAD_DIM ** -0.5)).astype(BF16)
        ks.append(qk[:, D_MODEL:])
        ext_ref[s] = cur[sl - CONV_PAD:sl, :]
        cn_ref[s] = ext_ref[s, tail:CONV_PAD, :]
    if tm % LANES:
        ks.append(jnp.zeros((LANES - tm % LANES, D_MODEL), F32))
    k = jnp.concatenate(ks, axis=0) if len(ks) > 1 else ks[0]
    kt = k.T.astype(BF16)
    nch = tm // lc
    per_seq = sl // lc
    for c in range(nch):
        kt_ref[c // per_seq, c % per_seq] = kt[:, c * lc:(c + 1) * lc]

    v_ref[...] = _dot(xn, wv_ref[...]).astype(BF16)
    o_ref[...] = _sigmoid(_dot(xn, wo_ref[...])).astype(BF16)

    zg = _dot(xn, wif_ref[...])
    if tm % LANES:
        zg = jnp.concatenate([zg, jnp.zeros((LANES - tm % LANES, LANES), F32)], axis=0)
    zt = zg.T[0:2 * HEADS, :]
    z = jnp.concatenate([zt[:, c * lc:(c + 1) * lc] + bif_ref[...] for c in range(nch)], axis=0)
    is_ig = (lax.broadcasted_iota(jnp.int32, z.shape, 0) % (2 * HEADS)) < HEADS
    g = jnp.where(is_ig, z, _log_sigmoid(z))
    upper = jnp.where(lax.broadcasted_iota(jnp.int32, (lc, lc), 0)
                      <= lax.broadcasted_iota(jnp.int32, (lc, lc), 1), 1.0, 0.0)
    bc = jnp.dot(g, upper, preferred_element_type=F32, precision=lax.Precision.HIGHEST)
    a = g - pltpu.roll(bc, nch * 2 * HEADS - HEADS, axis=0)
    amax = jnp.broadcast_to(jnp.max(a, axis=-1, keepdims=True), z.shape)
    gb = jnp.where(is_ig, g, bc)
    for c in range(nch):
        gt_ref[c // per_seq, c % per_seq, 0:2 * HEADS, :] = gb[c * 8:(c + 1) * 8, :]
        gt_ref[c // per_seq, c % per_seq, 2 * HEADS:4 * HEADS, :] = amax[c * 8:(c + 1) * 8, :]


def _tile_geometry(b, l, tile=TOKEN_TILE):
    sl = min(l, tile)
    nseg = max(1, min(b, TOKEN_TILE // sl))
    return nseg, sl


def _tok_spec(nseg, sl, nt, w):
    return pl.BlockSpec((nseg * sl, w), lambda bi, i: (bi * nt + i, 0))


def _inproj_a(x2d, b, l, g, wa, bif, cw, cb, cs):
    nseg, sl = _tile_geometry(b, l)
    nt = l // sl
    n = b * l
    CHUNK = min(MLSTM_BLOCK, l)
    per_seq = sl // CHUNK
    tok = functools.partial(_tok_spec, nseg, sl, nt)
    state = pl.BlockSpec((nseg, CONV_W - 1, 2 * D_MODEL), lambda bi, i: (bi, 0, 0))
    return pl.pallas_call(
        functools.partial(_inproj_a_kernel, nseg=nseg, sl=sl, lc=CHUNK),
        grid=(b // nseg, nt),
        in_specs=[tok(D_MODEL), _const_spec((1, D_MODEL)), _const_block((D_MODEL, 2 * D_MODEL), (0, 0)),
                  _const_block((D_MODEL, D_MODEL), (0, 2)), _const_block((D_MODEL, D_MODEL), (0, 3)),
                  _const_block((D_MODEL, LANES), (0, 4 * D_MODEL // LANES)), _const_spec((2 * HEADS, 1)),
                  _const_spec((CONV_W, 2 * D_MODEL)), _const_spec((1, 2 * D_MODEL)), state],
        out_specs=[tok(D_MODEL), tok(D_MODEL),
                   pl.BlockSpec((nseg, per_seq, D_MODEL, CHUNK), lambda bi, i: (bi, i, 0, 0)),
                   tok(D_MODEL), tok(D_MODEL),
                   pl.BlockSpec((nseg, per_seq, 4 * HEADS, CHUNK), lambda bi, i: (bi, i, 0, 0)),
                   state],
        out_shape=[jax.ShapeDtypeStruct((n, D_MODEL), BF16), jax.ShapeDtypeStruct((n, D_MODEL), BF16),
                   jax.ShapeDtypeStruct((b, l // CHUNK, D_MODEL, CHUNK), BF16),
                   jax.ShapeDtypeStruct((n, D_MODEL), BF16), jax.ShapeDtypeStruct((n, D_MODEL), BF16),
                   jax.ShapeDtypeStruct((b, l // CHUNK, 4 * HEADS, CHUNK), F32),
                   jax.ShapeDtypeStruct((b, CONV_W - 1, 2 * D_MODEL), F32)],
        scratch_shapes=[pltpu.VMEM((nseg, CONV_PAD, 2 * D_MODEL), F32)],
        compiler_params=_cparams(("parallel", "arbitrary")),
        name="inproj_a",
    )(x2d, g, wa, wa, wa, wa, bif, cw, cb, cs)


def _inproj_b_kernel(x_ref, wu_ref, wgv_ref, wmq_ref, wgate_ref, lng_ref, lnb_ref, ws_ref, bst_ref,
                     mk_ref, mv_ref, ug_ref, att_ref, gates_ref, *rest, nseg, sl, cl, emit_gv):
    tm = nseg * sl
    xn = x_ref[...]

    gates_ref[...] = _sigmoid(_dot_nt(xn, wgate_ref[...])).astype(BF16)

    gvr = jax.nn.gelu(_dot_nt(xn, wgv_ref[...]))
    mu = jnp.mean(gvr, axis=-1, keepdims=True)
    xc = gvr - mu
    r = lax.rsqrt(jnp.mean(xc * xc, axis=-1, keepdims=True) + EPS)
    gv = (xc * r) * lng_ref[...] + lnb_ref[...]
    if emit_gv:
        rest[0][...] = gv
    gvb = gv.astype(BF16)
    u = jax.nn.gelu(_dot_nt(xn, wu_ref[...]))
    tri = (lax.broadcasted_iota(jnp.int32, (cl, cl), 0) >= lax.broadcasted_iota(jnp.int32, (cl, cl), 1))
    for gi in range(GMLP_GROUPS):
        wsg = jnp.where(tri, ws_ref[gi], 0.0).astype(BF16)
        lo, hi = gi * GMLP_GROUP_DIM, (gi + 1) * GMLP_GROUP_DIM
        for c in range(tm // cl):
            sp = _dot(wsg, gvb[c * cl:(c + 1) * cl, lo:hi]) + bst_ref[:, gi:gi + 1]
            ug_ref[c * cl:(c + 1) * cl, lo:hi] = (u[c * cl:(c + 1) * cl, lo:hi] * sp).astype(BF16)

    mq = _dot_nt(xn, wmq_ref[...]).astype(BF16)

    def mem_head(ref, s, h):
        if len(ref.shape) == 4:
            return ref[s, :, h, :].astype(BF16)
        return ref[s, :, h * HEAD_DIM:(h + 1) * HEAD_DIM]

    for s in range(nseg):
        r0, r1 = s * sl, (s + 1) * sl
        for h in range(HEADS):
            lo, hi = h * HEAD_DIM, (h + 1) * HEAD_DIM
            sc = _dot_nt(mq[r0:r1, lo:hi], mem_head(mk_ref, s, h)) * (HEAD_DIM ** -0.5)
            e = jnp.exp(sc - jnp.max(sc, axis=-1, keepdims=True))
            a = (e / jnp.sum(e, axis=-1, keepdims=True)).astype(BF16)
            att_ref[r0:r1, lo:hi] = _dot(a, mem_head(mv_ref, s, h)).astype(BF16)


def _inproj_b(xn2d, b, l, wb, lng, lnb, ws, bst, mk, mv, cl, emit_gv):
    nseg, sl = _tile_geometry(b, l, INPROJ_B_TILE)
    nt = l // sl
    n = b * l
    tok = functools.partial(_tok_spec, nseg, sl, nt)
    if mk.ndim == 4:
        mem = pl.BlockSpec((nseg, N_MEM, HEADS, HEAD_DIM), lambda bi, i: (bi, 0, 0, 0), pipeline_mode=pl.Buffered(1))
    else:
        mem = pl.BlockSpec((nseg, N_MEM, D_MODEL), lambda bi, i: (bi, 0, 0))
    out_specs = [tok(D_MODEL), tok(D_MODEL), tok(3 * D_MODEL)]
    out_shape = [jax.ShapeDtypeStruct((n, D_MODEL), BF16), jax.ShapeDtypeStruct((n, D_MODEL), BF16),
                 jax.ShapeDtypeStruct((n, 3 * D_MODEL), BF16)]
    if emit_gv:
        out_specs.append(tok(D_MODEL))
        out_shape.append(jax.ShapeDtypeStruct((n, D_MODEL), F32))
    return pl.pallas_call(
        functools.partial(_inproj_b_kernel, nseg=nseg, sl=sl, cl=cl, emit_gv=emit_gv),
        grid=(b // nseg, nt),
        in_specs=[tok(D_MODEL), _const_block((D_MODEL, D_MODEL), (0, 0)),
                  _const_block((D_MODEL, D_MODEL), (1, 0)), _const_block((D_MODEL, D_MODEL), (2, 0)),
                  _const_block((3 * D_MODEL, D_MODEL), (1, 0)), _const_spec((1, D_MODEL)), _const_spec((1, D_MODEL)),
                  _const_spec((GMLP_GROUPS, cl, cl)), _const_spec((cl, GMLP_GROUPS)), mem, mem],
        out_specs=out_specs,
        out_shape=out_shape,
        compiler_params=_cparams(("parallel", "parallel")),
        name="inproj_b",
    )(xn2d, wb, wb, wb, wb, lng, lnb, ws, bst, mk, mv)


def _mlstm_kernel(q_ref, kt_ref, v_ref, gt_ref, *rest, nseg, cb, zero_state):
    i = pl.program_id(1)

    if zero_state:
        hm_ref, c_ref, n_ref, m_ref, st_ref = rest
    else:
        c0_ref, n0_ref, m0_ref, hm_ref, c_ref, n_ref, m_ref, st_ref = rest

    @pl.when(i == 0)
    def _():
        if zero_state:
            st_ref[...] = jnp.zeros(st_ref.shape, F32)
            m_ref[...] = jnp.zeros(m_ref.shape, F32)
        else:
            st_ref[:, :, :, :HEAD_DIM] = c0_ref[...]
            st_ref[:, :, :, HEAD_DIM:] = n0_ref[...]
            m_ref[...] = m0_ref[...]

    for s in range(nseg):
        _mlstm_sequence(q_ref, kt_ref.at[s], v_ref, gt_ref.at[s], st_ref.at[s], m_ref.at[s], hm_ref, s * cb, cb)

    @pl.when(i == pl.num_programs(1) - 1)
    def _():
        c_ref[...] = st_ref[:, :, :, :HEAD_DIM]
        n_ref[...] = st_ref[:, :, :, HEAD_DIM:]


def _mlstm_sequence(q_ref, kt_ref, v_ref, gt_ref, c_ref, m_ref, hm_ref, row0, cb):
    L = kt_ref.shape[-1]
    nch = cb // L
    ti = lax.broadcasted_iota(jnp.int32, (L, L), 0)
    si = lax.broadcasted_iota(jnp.int32, (L, L), 1)
    tri = ti >= si
    eye = ti == si

    rows = 4 * HEADS
    g_all = gt_ref[...].reshape(nch * rows, L)

    m_in = [m_ref[:, 0:1]]
    for c in range(nch):
        b_last4 = g_all[c * rows + HEADS:c * rows + 2 * HEADS, L - 1:L]
        amax4 = g_all[c * rows + 2 * HEADS:c * rows + 3 * HEADS, 0:1]
        m_in.append(jnp.maximum(b_last4 + m_in[-1], b_last4 + amax4))

    ones = jnp.ones((L, LANES), BF16)
    st = [c_ref[h] for h in range(HEADS)]
    for c in range(nch):
        r0, r1 = row0 + c * L, row0 + (c + 1) * L
        for h in range(HEADS):
            lo, hi = h * HEAD_DIM, (h + 1) * HEAD_DIM
            ig_r = g_all[c * rows + h:c * rows + h + 1, :]
            bc_r = g_all[c * rows + HEADS + h:c * rows + HEADS + h + 1, :]
            a_r = ig_r - bc_r
            bc_c = jnp.sum(jnp.where(eye, bc_r, 0.0), axis=-1, keepdims=True)
            m0 = m_in[c][h:h + 1, :]
            m_last = m_in[c + 1][h:h + 1, :]
            dmat = jnp.where(tri, bc_c + a_r, NEG_INF)
            inter = bc_c + m0
            m = jnp.maximum(inter, jnp.max(dmat, axis=-1, keepdims=True))
            w_intra = jnp.exp(dmat - m)
            w_inter = jnp.exp(inter - m)
            q = q_ref[r0:r1, lo:hi]
            kt = kt_ref[c, lo:hi, :]
            v = v_ref[r0:r1, lo:hi]
            s = _dot(q, kt) * w_intra
            qs = _dot(q, st[h].astype(BF16))
            num = w_inter * qs[:, :HEAD_DIM] + _dot(s.astype(BF16), v)
            den = w_inter * qs[:, HEAD_DIM:HEAD_DIM + 1] + jnp.sum(s, axis=-1, keepdims=True)
            hh = num / jnp.maximum(jnp.abs(den), jnp.exp(-m))
            bc_last = bc_r[:, L - 1:L]
            w_last = jnp.exp(bc_last + a_r - m_last)
            decay = jnp.exp(bc_last + m0 - m_last)
            ktw = (kt.astype(F32) * w_last).astype(BF16)
            st[h] = decay * st[h] + _dot(ktw, jnp.concatenate([v, ones], axis=1))
            hm_ref[r0:r1, lo:hi] = hh.astype(BF16)

    for h in range(HEADS):
        c_ref[h] = st[h]
    m_ref[...] = jnp.broadcast_to(m_in[nch], (HEADS, LANES))


def _mlstm(q, kt, v, gt, state, b, l):
    nseg, cb = _tile_geometry(b, l, MLSTM_TILE)
    nt = l // cb
    CHUNK = kt.shape[-1]
    tok = _tok_spec(nseg, cb, nt, D_MODEL)
    cs = pl.BlockSpec((nseg, HEADS, HEAD_DIM, HEAD_DIM), lambda bi, i: (bi, 0, 0, 0))
    ns = pl.BlockSpec((nseg, HEADS, HEAD_DIM, LANES), lambda bi, i: (bi, 0, 0, 0))
    ms = pl.BlockSpec((nseg, HEADS, LANES), lambda bi, i: (bi, 0, 0))
    state_specs = [] if state is None else [cs, ns, ms]
    return pl.pallas_call(
        functools.partial(_mlstm_kernel, nseg=nseg, cb=cb, zero_state=state is None),
        grid=(b // nseg, nt),
        in_specs=[tok, pl.BlockSpec((nseg, cb // CHUNK, D_MODEL, CHUNK), lambda bi, i: (bi, i, 0, 0)), tok,
                  pl.BlockSpec((nseg, cb // CHUNK, 4 * HEADS, CHUNK), lambda bi, i: (bi, i, 0, 0))] + state_specs,
        out_specs=[tok, cs, ns, ms],
        out_shape=[jax.ShapeDtypeStruct((b * l, D_MODEL), BF16),
                   jax.ShapeDtypeStruct((b, HEADS, HEAD_DIM, HEAD_DIM), F32),
                   jax.ShapeDtypeStruct((b, HEADS, HEAD_DIM, LANES), F32),
                   jax.ShapeDtypeStruct((b, HEADS, LANES), F32)],
        scratch_shapes=[pltpu.VMEM((nseg, HEADS, HEAD_DIM, HEAD_DIM + LANES), F32)],
        compiler_params=_cparams(("parallel", "arbitrary")),
        name="mlstm",
    )(q, kt, v, gt, *(state or ()))


def _merge_kernel(x_ref, hh_ref, o_ref, ng_ref, ug_ref, att_ref, gates_ref, wa_ref, wb_ref, wc_ref, wo_ref, h_ref):
    parts = []
    for h in range(HEADS):
        lo, hi = h * HEAD_DIM, (h + 1) * HEAD_DIM
        hh = hh_ref[:, lo:hi].astype(F32)
        hn = hh * lax.rsqrt(jnp.mean(hh * hh, axis=-1, keepdims=True) + EPS)
        parts.append(((hn * ng_ref[:, lo:hi]) * o_ref[:, lo:hi].astype(F32)).astype(BF16))
    br = _dot(parts[0], wa_ref[0:HEAD_DIM, :])
    for h in range(1, HEADS):
        br = br + _dot(parts[h], wa_ref[h * HEAD_DIM:(h + 1) * HEAD_DIM, :])
    g = gates_ref[...].astype(F32)
    mixed = g[:, :D_MODEL] * br
    mixed = mixed + g[:, D_MODEL:2 * D_MODEL] * _dot(ug_ref[...], wb_ref[...])
    mixed = mixed + g[:, 2 * D_MODEL:] * _dot(att_ref[...], wc_ref[...])
    h_ref[...] = x_ref[...] + _dot(mixed.astype(BF16), wo_ref[...])


def _merge(x2d, hh, osig, ng, ug, att, gates, wa, wb, wc, wo, tm):
    n = x2d.shape[0]
    row = lambda w: pl.BlockSpec((tm, w), lambda i: (i, 0))
    wspec = _const_spec((D_MODEL, D_MODEL))
    return pl.pallas_call(
        _merge_kernel,
        grid=(n // tm,),
        in_specs=[row(D_MODEL), row(D_MODEL), row(D_MODEL), _const_spec((1, D_MODEL)), row(D_MODEL), row(D_MODEL),
                  row(3 * D_MODEL), wspec, wspec, wspec, wspec],
        out_specs=row(D_MODEL),
        out_shape=jax.ShapeDtypeStruct((n, D_MODEL), F32),
        compiler_params=_cparams(("parallel",)),
        name="merge",
    )(x2d, hh, osig, ng, ug, att, gates, wa, wb, wc, wo)


MOE_T = 256
MOE_STEP_SUBS = 2
MOE_CAP = 32
MOE_S = N_EXPERTS * MOE_CAP
MOE_OVF = 512
SEG_ALIGN = 16
EXPERT_BLOCK = 128
MAIN_RING = 3


def _moe_route_kernel(hp_ref, hs_ref, g_ref, wrg_ref, brg_ref, wre_ref, bre_ref,
                      main_ref, ovf_ref, info_ref, meta_ref, *, nsp):
    n = MOE_STEP_SUBS * MOE_T
    h = jnp.where(pl.program_id(0) < nsp // MOE_STEP_SUBS, hp_ref[...], hs_ref[...])
    xm = _rmsnorm(h, g_ref[...]).astype(BF16)
    lane = lax.broadcasted_iota(jnp.int32, (n, LANES), 1).astype(F32)
    lg = jnp.where(lane < N_GROUPS, _dot(xm, wrg_ref[...]) + brg_ref[...], NEG_INF)
    gmax = jnp.max(lg, axis=-1, keepdims=True)
    p_top = 1.0 / jnp.sum(jnp.exp(lg - gmax), axis=-1, keepdims=True)
    grp = jnp.min(jnp.where(lg == gmax, lane, float(LANES)), axis=-1, keepdims=True)
    el = _dot(xm, wre_ref[...]) + bre_ref[...]
    in_grp = (lane >= grp * EXPERTS_PER_GROUP) & (lane < (grp + 1.0) * EXPERTS_PER_GROUP)
    vals = jnp.where(in_grp, el, NEG_INF)
    v1 = jnp.max(vals, axis=-1, keepdims=True)
    i1 = jnp.min(jnp.where(vals == v1, lane, float(LANES)), axis=-1, keepdims=True)
    vals2 = jnp.where(lane == i1, NEG_INF, vals)
    v2 = jnp.max(vals2, axis=-1, keepdims=True)
    i2 = jnp.min(jnp.where(vals2 == v2, lane, float(LANES)), axis=-1, keepdims=True)
    r = jnp.exp(v2 - v1)
    p1 = p_top / (1.0 + r)
    p2 = p_top * r / (1.0 + r)
    sel1 = lane == i1
    sel2 = lane == i2
    onehot = jnp.where(sel1 | sel2, 1.0, 0.0)

    deferred = []
    for s in range(MOE_STEP_SUBS):
        rows = slice(s * MOE_T, (s + 1) * MOE_T)
        deferred.append(_route_sub_tile(
            xm[rows, :], onehot[rows, :], i1[rows, :], i2[rows, :],
            p1[rows, :], p2[rows, :], main_ref.at[:, s], ovf_ref.at[s], info_ref.at[rows, :], meta_ref.at[s]))
    for write_overflow in deferred:
        write_overflow()


def _route_sub_tile(xm, onehot, i1, i2, p1, p2, main_ref, ovf_ref, info_ref, meta_ref):
    t = MOE_T
    lane = lax.broadcasted_iota(jnp.int32, (t, LANES), 1)
    sel1 = lane.astype(F32) == i1
    sel2 = lane.astype(F32) == i2
    cnt = jnp.sum(onehot, axis=0, keepdims=True).astype(jnp.int32)
    pn = jnp.bitwise_and(jnp.maximum(cnt - MOE_CAP, 0) + (SEG_ALIGN - 1), -SEG_ALIGN)
    pn8 = jnp.broadcast_to(pn, (8, LANES))
    earlier = jnp.where(lax.broadcasted_iota(jnp.int32, (LANES, LANES), 0)
                        < lax.broadcasted_iota(jnp.int32, (LANES, LANES), 1), 1.0, 0.0).astype(BF16)
    off_f8 = _dot(pn8.astype(F32).astype(BF16), earlier)
    row8 = lax.broadcasted_iota(jnp.int32, (8, LANES), 0)
    meta_ref[...] = jnp.where(row8 == 0, off_f8.astype(jnp.int32), jnp.where(row8 == 1, pn8, 0))

    ti = lax.broadcasted_iota(jnp.int32, (t, t), 0)
    si = lax.broadcasted_iota(jnp.int32, (t, t), 1)
    before = jnp.where(ti > si, 1.0, 0.0).astype(BF16)
    rank = _dot(before, onehot.astype(BF16))
    off_f = off_f8[0:1, :]

    def slot_row(sel, idx):
        rk = jnp.sum(jnp.where(sel, rank, 0.0), axis=-1, keepdims=True)
        of = jnp.sum(jnp.where(sel, off_f, 0.0), axis=-1, keepdims=True)
        return jnp.where(rk < MOE_CAP, idx * MOE_CAP + rk, MOE_S - MOE_CAP + of + rk)

    pos1 = slot_row(sel1, i1)
    pos2 = slot_row(sel2, i2)
    info_ref[...] = (jnp.where(lane == 0, pos1, 0.0) + jnp.where(lane == 1, pos2, 0.0)
                     + jnp.where(lane == 2, p1, 0.0) + jnp.where(lane == 3, p2, 0.0))

    eye = ti == si
    pos1_r = jnp.sum(jnp.where(eye, pos1, 0.0), axis=0, keepdims=True)
    pos2_r = jnp.sum(jnp.where(eye, pos2, 0.0), axis=0, keepdims=True)

    def gather(rows, base):
        srow = (lax.broadcasted_iota(jnp.int32, (rows, t), 0) + base).astype(F32)
        pick = jnp.where((srow == pos1_r) | (srow == pos2_r), 1.0, 0.0).astype(BF16)
        return _dot(pick, xm).astype(BF16)

    main_ref[...] = gather(MOE_S, 0).reshape(N_EXPERTS, MOE_CAP, D_MODEL)
    has_ovf = jnp.sum(pn) > 0

    def write_overflow():
        @pl.when(has_ovf)
        def _():
            ovf_ref[...] = gather(MOE_OVF, MOE_S)

        @pl.when(jnp.logical_not(has_ovf))
        def _():
            ovf_ref[...] = jnp.zeros(ovf_ref.shape, ovf_ref.dtype)
    return write_overflow


def _two_group_specs(nsp, cols):
    rows = MOE_STEP_SUBS * MOE_T
    npb = nsp // MOE_STEP_SUBS
    return (pl.BlockSpec((rows, cols), lambda j, *_: (jnp.minimum(j, npb - 1), 0)),
            pl.BlockSpec((rows, cols), lambda j, *_: (jnp.maximum(j - npb, 0), 0)))


def _moe_route(hp, hs, g, wrg, brg, wre, bre):
    nsp = hp.shape[0] // MOE_T
    ns = nsp + hs.shape[0] // MOE_T
    sub = MOE_STEP_SUBS
    return pl.pallas_call(
        functools.partial(_moe_route_kernel, nsp=nsp),
        grid=(ns // sub,),
        in_specs=[*_two_group_specs(nsp, D_MODEL), _const_spec((1, D_MODEL)),
                  _const_spec((D_MODEL, LANES)), _const_spec((1, LANES)),
                  _const_spec((D_MODEL, LANES)), _const_spec((1, LANES))],
        out_specs=[pl.BlockSpec((N_EXPERTS, sub, MOE_CAP, D_MODEL), lambda j: (0, j, 0, 0)),
                   pl.BlockSpec((sub, MOE_OVF, D_MODEL), lambda j: (j, 0, 0)),
                   pl.BlockSpec((sub * MOE_T, LANES), lambda j: (j, 0)),
                   pl.BlockSpec((sub, 8, LANES), lambda j: (j, 0, 0))],
        out_shape=[jax.ShapeDtypeStruct((N_EXPERTS, ns, MOE_CAP, D_MODEL), BF16),
                   jax.ShapeDtypeStruct((ns, MOE_OVF, D_MODEL), BF16),
                   jax.ShapeDtypeStruct((ns * MOE_T, LANES), F32),
                   jax.ShapeDtypeStruct((ns, 8, LANES), jnp.int32)],
        compiler_params=_cparams(("arbitrary",)),
        name="moe_route",
    )(hp, hs, g, wrg, brg, wre, bre)


def _swiglu(x, wgu, wdb):
    gu = _dot(x, wgu[...])
    gate = gu[:, :D_EXPERT]
    hid = (gate * _sigmoid(gate)) * gu[:, D_EXPERT:]
    return _dot(hid.astype(BF16), wdb[...]).astype(BF16)


def _moe_expert_kernel(offs_ref, pns_ref, eflag_ref, main_hbm, slots_hbm, wg_ref, wu_ref, wd_ref,
                       mout_ref, out_hbm, ring, xbuf, obuf, wgu, wdb, sem_ring, sem_in, sem_out, *, n_sub, jb, n_rb):
    e = pl.program_id(0)
    rb = pl.program_id(1)

    def fetch(ex, slot):
        return pltpu.make_async_copy(main_hbm.at[ex], ring.at[slot], sem_ring.at[slot])

    @pl.when(e == 0)
    def _():
        for ex in range(MAIN_RING - 1):
            fetch(ex, ex).start()

    @pl.when(e + MAIN_RING - 1 < N_EXPERTS)
    def _():
        fetch(e + MAIN_RING - 1, (e + MAIN_RING - 1) % MAIN_RING).start()

    slot = e % MAIN_RING
    fetch(e, slot).wait()
    main_ref = ring.at[slot]

    @pl.when((e == 0) & (rb == 0))
    def _():
        xbuf[...] = jnp.zeros_like(xbuf)

    @pl.when(rb == 0)
    def _():
        wgu[:, :D_EXPERT] = wg_ref[...].astype(BF16)
        wgu[:, D_EXPERT:] = wu_ref[...].astype(BF16)
        wdb[...] = wd_ref[...].astype(BF16)

    sb = main_ref.shape[0]
    part = sb // 2 if sb % 2 == 0 else sb
    for s0 in range(0, sb, part):
        y = _swiglu(main_ref[s0:s0 + part].reshape(part * MOE_CAP, D_MODEL), wgu, wdb)
        mout_ref[s0:s0 + part] = y.reshape(part, MOE_CAP, D_MODEL)

    def copy_in(j, src, dst):
        return pltpu.make_async_copy(slots_hbm.at[j, pl.ds(src, SEG_ALIGN), :],
                                     xbuf.at[pl.ds(dst, SEG_ALIGN), :], sem_in)

    def copy_out(j, src, dst):
        return pltpu.make_async_copy(obuf.at[pl.ds(src, SEG_ALIGN), :],
                                     out_hbm.at[j, pl.ds(dst, SEG_ALIGN), :], sem_out)

    def for_each_chunk(g, fn):
        def seg(jj, cur):
            j = g * jb + jj
            n = pns_ref[j * N_EXPERTS + e]
            off = offs_ref[j * N_EXPERTS + e]

            def chunk(k, c):
                fn(j, pl.multiple_of(off + k * SEG_ALIGN, SEG_ALIGN), pl.multiple_of(cur + k * SEG_ALIGN, SEG_ALIGN))
                return c
            lax.fori_loop(0, n // SEG_ALIGN, chunk, 0)
            return cur + n
        return lax.fori_loop(0, jb, seg, 0)

    def group(g, carry):
        total = for_each_chunk(g, lambda j, r, b: copy_in(j, r, b).start())
        nchunk = total // SEG_ALIGN

        def wait_in(k, c):
            copy_in(0, 0, 0).wait()
            return c
        lax.fori_loop(0, nchunk, wait_in, 0)

        def block(bi, c):
            r0 = pl.multiple_of(bi * EXPERT_BLOCK, EXPERT_BLOCK)
            obuf[pl.ds(r0, EXPERT_BLOCK), :] = _swiglu(xbuf[pl.ds(r0, EXPERT_BLOCK), :], wgu, wdb)
            return c
        lax.fori_loop(0, (total + EXPERT_BLOCK - 1) // EXPERT_BLOCK, block, 0)

        for_each_chunk(g, lambda j, r, b: copy_out(j, b, r).start())

        def wait_out(k, c):
            copy_out(0, 0, 0).wait()
            return c
        lax.fori_loop(0, nchunk, wait_out, 0)
        return carry

    @pl.when((rb == n_rb - 1) & (eflag_ref[e] > 0))
    def _():
        lax.fori_loop(0, n_sub // jb, group, 0)


def _moe_expert(main, ovf, offs, pns, eflag, wg, wu, wd):
    ns = main.shape[1]
    n_rb = 1
    sb = ns // n_rb
    jb = max(d for d in range(1, 12) if ns % d == 0)
    rows = jb * MOE_T + EXPERT_BLOCK
    mspec = pl.BlockSpec((None, sb, MOE_CAP, D_MODEL), lambda e, rb, *_: (e, rb, 0, 0))
    wspec = lambda a, b: pl.BlockSpec((None, a, b), lambda e, rb, *_: (e, 0, 0))
    grid_spec = pltpu.PrefetchScalarGridSpec(
        num_scalar_prefetch=3,
        grid=(N_EXPERTS, n_rb),
        in_specs=[pl.BlockSpec(memory_space=pl.ANY), pl.BlockSpec(memory_space=pl.ANY),
                  wspec(D_MODEL, D_EXPERT), wspec(D_MODEL, D_EXPERT), wspec(D_EXPERT, D_MODEL)],
        out_specs=[mspec, pl.BlockSpec(memory_space=pl.ANY)],
        scratch_shapes=[pltpu.VMEM((MAIN_RING, sb, MOE_CAP, D_MODEL), BF16),
                        pltpu.VMEM((rows, D_MODEL), BF16), pltpu.VMEM((rows, D_MODEL), BF16),
                        pltpu.VMEM((D_MODEL, 2 * D_EXPERT), BF16), pltpu.VMEM((D_EXPERT, D_MODEL), BF16),
                        pltpu.SemaphoreType.DMA((MAIN_RING,)),
                        pltpu.SemaphoreType.DMA(()), pltpu.SemaphoreType.DMA(())],
    )
    mout, oout = pl.pallas_call(
        functools.partial(_moe_expert_kernel, n_sub=ns, jb=jb, n_rb=n_rb),
        grid_spec=grid_spec,
        out_shape=[jax.ShapeDtypeStruct(main.shape, main.dtype), jax.ShapeDtypeStruct(ovf.shape, ovf.dtype)],
        input_output_aliases={4: 1},
        compiler_params=_cparams(("arbitrary", "arbitrary")),
        name="moe_expert",
    )(offs, pns, eflag, main, ovf, wg, wu, wd)
    return mout, oout


def _moe_combine_kernel(jflag_ref, sflag_ref, hp_ref, hs_ref, main_ref, ovf_ref, info_ref, gf_ref,
                        yp_ref, ys_ref, acc_ref, *, nsp):
    j = pl.program_id(0)
    npb = nsp // MOE_STEP_SUBS

    def weights(s, cols, base):
        info = info_ref[s * MOE_T:(s + 1) * MOE_T, :]
        scol = (lax.broadcasted_iota(jnp.int32, (MOE_T, cols), 1) + base).astype(F32)
        return (jnp.where(scol == info[:, 0:1], info[:, 2:3], 0.0)
                + jnp.where(scol == info[:, 1:2], info[:, 3:4], 0.0)).astype(BF16)

    for s in range(MOE_STEP_SUBS):
        rows = slice(s * MOE_T, (s + 1) * MOE_T)
        h = jnp.where(j < npb, hp_ref[rows, :], hs_ref[rows, :])
        acc_ref[rows, :] = h + _dot(weights(s, MOE_S, 0), main_ref[:, s].reshape(MOE_S, D_MODEL))

    for s in range(MOE_STEP_SUBS):
        @pl.when(jflag_ref[j * MOE_STEP_SUBS + s] > 0)
        def _(s=s):
            rows = slice(s * MOE_T, (s + 1) * MOE_T)
            acc_ref[rows, :] += _dot(weights(s, MOE_OVF, MOE_S), ovf_ref[s])

    y = _rmsnorm(acc_ref[...], gf_ref[...])

    @pl.when(j < npb)
    def _():
        yp_ref[...] = y

    @pl.when(j >= npb)
    def _():
        ys_ref[...] = y


def _moe_combine(hp, hs, main, ovf, info, jflag, gf):
    nsp = hp.shape[0] // MOE_T
    ns = main.shape[1]
    sub = MOE_STEP_SUBS
    sflag = jnp.max(jflag.reshape(ns // sub, sub), axis=1)
    grid_spec = pltpu.PrefetchScalarGridSpec(
        num_scalar_prefetch=2,
        grid=(ns // sub,),
        in_specs=[*_two_group_specs(nsp, D_MODEL),
                  pl.BlockSpec((N_EXPERTS, sub, MOE_CAP, D_MODEL), lambda j, jf, sf: (0, j, 0, 0)),
                  pl.BlockSpec((sub, MOE_OVF, D_MODEL), lambda j, jf, sf: (jnp.where(sf[j] > 0, j, 0), 0, 0)),
                  pl.BlockSpec((sub * MOE_T, LANES), lambda j, jf, sf: (j, 0)),
                  pl.BlockSpec((1, D_MODEL), lambda j, jf, sf: (0, 0))],
        out_specs=list(_two_group_specs(nsp, D_MODEL)),
        scratch_shapes=[pltpu.VMEM((sub * MOE_T, D_MODEL), F32)],
    )
    return pl.pallas_call(
        functools.partial(_moe_combine_kernel, nsp=nsp),
        grid_spec=grid_spec,
        out_shape=[jax.ShapeDtypeStruct(hp.shape, F32), jax.ShapeDtypeStruct(hs.shape, F32)],
        compiler_params=_cparams(("arbitrary",)),
        name="moe_combine",
    )(jflag, sflag, hp, hs, main, ovf, info, gf)


def _moe(hp, hs, p):
    main, ovf, info, meta = _moe_route(hp, hs, p["norm_moe_g"], p["w_rg"], p["b_rg"], p["w_re"], p["b_re"])
    pn = meta[:, 1, :N_EXPERTS]
    offs = meta[:, 0, :N_EXPERTS].reshape(-1)
    eflag = (jnp.sum(pn, axis=0) > 0).astype(jnp.int32)
    jflag = (jnp.sum(pn, axis=1) > 0).astype(jnp.int32)
    main, ovf = _moe_expert(main, ovf, offs, pn.reshape(-1), eflag, p["w_eg"], p["w_eu"], p["w_ed"])
    return _moe_combine(hp, hs, main, ovf, info, jflag, p["norm_final_g"])


def _trunk(x, mem_k, mem_v, c0, n0, m0, conv0, p, *, emit_gv):
    b, l, _ = x.shape
    n = b * l
    cl = min(l, GMLP_CHUNK)
    x2d = x.reshape(n, D_MODEL)
    xn, q, kt, v, osig, gt, conv_new = _inproj_a(
        x2d, b, l, p["norm_mix_g"], p["w_a"], p["b_if"], p["conv_w"], p["conv_b"], conv0)
    outs = _inproj_b(xn, b, l, p["w_b"], p["gmlp_norm_g"], p["gmlp_norm_b"], p["w_s"][:, :cl, :cl], p["b_st"][:cl],
                     mem_k, mem_v, cl, emit_gv)
    ug, att, gates = outs[:3]
    on_lanes = lambda a: jnp.broadcast_to(a[..., None], a.shape + (LANES,))
    state = None if c0 is None else (c0, on_lanes(n0), on_lanes(m0))
    hh, c1, n1, m1 = _mlstm(q, kt, v, gt, state, b, l)
    n1 = n1[..., 0]
    h = _merge(x2d, hh, osig, p["mlstm_norm_g"], ug, att, gates,
               p["w_br_mlstm"], p["w_br_gmlp"], p["w_br_mem"], p["w_out"], TOKEN_TILE)
    return h, c1, n1, m1[..., 0], conv_new, (outs[3].reshape(b, l, D_MODEL) if emit_gv else None)


def kernel(x_prompt, x_sample, mem_prompt, cache_mem_k, cache_mem_v, state_mlstm_C, state_mlstm_n, state_mlstm_m, state_mlstm_conv, norm_mix_g, w_in, mlstm_i_b, mlstm_f_b, mlstm_conv_w, mlstm_conv_b, mlstm_norm_g, gmlp_norm_g, gmlp_norm_b, gmlp_w_s, gmlp_b_s, mem_norm_g, w_mem_k, w_mem_v, w_br_mlstm, w_br_gmlp, w_br_mem, w_out, norm_moe_g, w_router_group, b_router_group, w_router_expert, b_router_expert, w_exp_gate, w_exp_up, w_exp_down, norm_final_g):
    bp = x_prompt.shape[0]
    bs = x_sample.shape[0]
    W = D_MODEL
    wt = w_in[0].T
    o_u = 4 * W + 2 * HEADS
    w_a = _w_cols(wt, 4 * W + LANES)
    w_b = _w_rows(wt, o_u, (wt.shape[0] - o_u) // W)
    row = lambda a: a.reshape(1, -1)
    pad_l = lambda a: jnp.pad(a, ((0, 0), (0, LANES - a.shape[1])))
    p = {
        "norm_mix_g": row(norm_mix_g[0]),
        "w_a": w_a, "w_b": w_b,
        "b_if": jnp.concatenate([mlstm_i_b[0], mlstm_f_b[0]]).reshape(2 * HEADS, 1),
        "conv_w": mlstm_conv_w[0], "conv_b": row(mlstm_conv_b[0]),
        "gmlp_norm_g": row(gmlp_norm_g[0]), "gmlp_norm_b": row(gmlp_norm_b[0]),
        "w_s": gmlp_w_s[0], "b_st": gmlp_b_s[0].T,
        "mlstm_norm_g": row(mlstm_norm_g[0]),
        "w_br_mlstm": w_br_mlstm[0].astype(BF16), "w_br_gmlp": w_br_gmlp[0].astype(BF16),
        "w_br_mem": w_br_mem[0].astype(BF16), "w_out": w_out[0].astype(BF16),
        "norm_moe_g": row(norm_moe_g[0]),
        "w_rg": pad_l(w_router_group[0]).astype(BF16), "b_rg": pad_l(row(b_router_group[0])),
        "w_re": pad_l(w_router_expert[0]).astype(BF16), "b_re": pad_l(row(b_router_expert[0])),
        "w_eg": w_exp_gate[0], "w_eu": w_exp_up[0], "w_ed": w_exp_down[0],
        "norm_final_g": row(norm_final_g),
    }

    mk_p, mv_p, mk_pb, mv_pb = _memory_kv(mem_prompt.reshape(bp * N_MEM, W), row(mem_norm_g[0]),
                                          w_mem_k[0], w_mem_v[0])

    zeros = lambda *s: jnp.zeros(s, F32)
    hp, cp, np_, mp, cvp, _ = _trunk(
        x_prompt, mk_pb.reshape(bp, N_MEM, W), mv_pb.reshape(bp, N_MEM, W),
        None, None, None, zeros(bp, CONV_W - 1, 2 * W), p, emit_gv=False)
    hs, cs, ns, ms, cvs, gvs = _trunk(
        x_sample, cache_mem_k[0], cache_mem_v[0],
        state_mlstm_C[0], state_mlstm_n[0], state_mlstm_m[0], state_mlstm_conv[0], p,
        emit_gv=True)

    yp, ys = _moe(hp, hs, p)
    return (yp.reshape(x_prompt.shape), ys.reshape(x_sample.shape), mk_p[None], mv_p[None],
            cp[None], np_[None], mp[None], cvp[None],
            cs[None], ns[None], ms[None], cvs[None], gvs[None])
```

```python
import functools

import jax
import jax.numpy as jnp
from jax import lax
from jax.experimental import pallas as pl
from jax.experimental.pallas import tpu as pltpu

D_MODEL = 1024
MLSTM_BLOCK = 512
TOKEN_TILE = 512
INPROJ_B_TILE = 1024
MLSTM_TILE = 2048
EPS = 1e-6
HEADS = 4
HEAD_DIM = 256
CONV_W = 4
GMLP_GROUPS = 4
GMLP_GROUP_DIM = 256
GMLP_CHUNK = 128
N_MEM = 256
N_GROUPS = 4
EXPERTS_PER_GROUP = 8
N_EXPERTS = 32
D_EXPERT = 256
LANES = 128
CONV_PAD = 8
NORM_ROWS = 128

F32 = jnp.float32
BF16 = jnp.bfloat16
NEG_INF = float("-inf")

VMEM_LIMIT = 56 * 1024 * 1024


def _cparams(sem):
    return pltpu.CompilerParams(dimension_semantics=sem, vmem_limit_bytes=VMEM_LIMIT)


def _const_spec(shape):
    nd = len(shape)
    return pl.BlockSpec(shape, lambda *_: (0,) * nd, pipeline_mode=pl.Buffered(1))


def _const_block(shape, idx):
    return pl.BlockSpec(shape, lambda *_: idx, pipeline_mode=pl.Buffered(1))


def _sigmoid(x):
    return 0.5 * (jnp.tanh(0.5 * x) + 1.0)


def _log_sigmoid(x):
    return jnp.minimum(x, 0.0) - jnp.log(1.0 + jnp.exp(-jnp.abs(x)))


def _rmsnorm(x, g):
    r = lax.rsqrt(jnp.mean(x * x, axis=-1, keepdims=True) + EPS)
    return (x * r) * g


def _dot(a, b):
    return jnp.dot(a, b, preferred_element_type=F32)


def _dot_nt(a, b):
    return lax.dot_general(a, b, (((1,), (1,)), ((), ())), preferred_element_type=F32)


def _memkv_kernel(mem_ref, g_ref, wk_ref, wv_ref, k_ref, v_ref, kb_ref, vb_ref):
    mn = _rmsnorm(mem_ref[...], g_ref[...]).astype(BF16)
    k = _dot(mn, wk_ref[...].astype(BF16))
    v = _dot(mn, wv_ref[...].astype(BF16))
    for s in range(k_ref.shape[0]):
        for h in range(HEADS):
            k_ref[s, :, h, :] = k[s * N_MEM:(s + 1) * N_MEM, h * HEAD_DIM:(h + 1) * HEAD_DIM]
            v_ref[s, :, h, :] = v[s * N_MEM:(s + 1) * N_MEM, h * HEAD_DIM:(h + 1) * HEAD_DIM]
    kb_ref[...] = k.astype(BF16)
    vb_ref[...] = v.astype(BF16)


def _memory_kv(mem2d, g, wk, wv):
    n = mem2d.shape[0]
    tm = TOKEN_TILE
    row = pl.BlockSpec((tm, D_MODEL), lambda i: (i, 0))
    cache = pl.BlockSpec((tm // N_MEM, N_MEM, HEADS, HEAD_DIM), lambda i: (i, 0, 0, 0))
    return pl.pallas_call(
        _memkv_kernel,
        grid=(n // tm,),
        in_specs=[row, _const_spec((1, D_MODEL)), _const_spec((D_MODEL, D_MODEL)),
                  _const_spec((D_MODEL, D_MODEL))],
        out_specs=[cache, cache, row, row],
        out_shape=[jax.ShapeDtypeStruct((n // N_MEM, N_MEM, HEADS, HEAD_DIM), F32)] * 2
        + [jax.ShapeDtypeStruct((n, D_MODEL), BF16)] * 2,
        compiler_params=_cparams(("parallel",)),
        name="memory_kv",
    )(mem2d, g, wk, wv)


W_COLS_BLOCK = 1408


def _w_cols_kernel(w_ref, o_ref):
    o_ref[...] = w_ref[...].T.astype(BF16)


def _w_cols(wt, ncols):
    rb = W_COLS_BLOCK
    return pl.pallas_call(
        _w_cols_kernel,
        grid=(ncols // rb,),
        in_specs=[pl.BlockSpec((rb, D_MODEL), lambda j: (j, 0))],
        out_specs=pl.BlockSpec((D_MODEL, rb), lambda j: (0, j)),
        out_shape=jax.ShapeDtypeStruct((D_MODEL, ncols), BF16),
        compiler_params=_cparams(("parallel",)),
        name="w_cols",
    )(wt)


def _w_rows_kernel(w_hbm, o_ref, buf, sem, *, row0, nblk):
    i = pl.program_id(0)

    def fetch(blk, slot):
        start = pl.multiple_of(row0 + blk * D_MODEL, 8)
        return pltpu.make_async_copy(w_hbm.at[pl.ds(start, D_MODEL), :], buf.at[slot], sem.at[slot])

    @pl.when(i == 0)
    def _():
        fetch(0, 0).start()

    @pl.when(i + 1 < nblk)
    def _():
        fetch(i + 1, (i + 1) % 2).start()

    fetch(i, i % 2).wait()
    o_ref[...] = buf[i % 2].astype(BF16)


def _w_rows(wt, row0, nblk):
    return pl.pallas_call(
        functools.partial(_w_rows_kernel, row0=row0, nblk=nblk),
        grid=(nblk,),
        in_specs=[pl.BlockSpec(memory_space=pl.ANY)],
        out_specs=pl.BlockSpec((D_MODEL, D_MODEL), lambda i: (i, 0)),
        out_shape=jax.ShapeDtypeStruct((nblk * D_MODEL, D_MODEL), BF16),
        scratch_shapes=[pltpu.VMEM((2, D_MODEL, D_MODEL), F32), pltpu.SemaphoreType.DMA((2,))],
        compiler_params=_cparams(("arbitrary",)),
        name="w_rows",
    )(wt)


def _inproj_a_kernel(x_ref, g_ref, wqk_ref, wv_ref, wo_ref, wif_ref, bif_ref, cw_ref, cb_ref, cs_ref,
                     xn_ref, q_ref, kt_ref, v_ref, o_ref, gt_ref, cn_ref, ext_ref, *, nseg, sl, lc):
    i = pl.program_id(1)
    tm = nseg * sl
    tail = CONV_PAD - (CONV_W - 1)

    @pl.when(i == 0)
    def _():
        for s in range(nseg):
            ext_ref[s, 0:tail, :] = jnp.zeros((tail, 2 * D_MODEL), F32)
            ext_ref[s, tail:CONV_PAD, :] = cs_ref[s]

    parts = [_rmsnorm(x_ref[r:r + NORM_ROWS, :], g_ref[...]).astype(BF16) for r in range(0, tm, NORM_ROWS)]
    xn = jnp.concatenate(parts, axis=0)
    xn_ref[...] = xn

    zqk = jnp.concatenate([_dot(part, wqk_ref[...]) for part in parts], axis=0)
    ks = []
    row8 = lax.broadcasted_iota(jnp.int32, (CONV_PAD, 2 * D_MODEL), 0)
    for s in range(nseg):
        cur = zqk[s * sl:(s + 1) * sl, :]
        prev = ext_ref[s]
        acc = cb_ref[...] + cur * cw_ref[CONV_W - 1:CONV_W, :]
        for d in range(1, CONV_W):
            back = pltpu.roll(cur, d, axis=0)
            head = jnp.where(row8 < d, pltpu.roll(prev, d, axis=0), back[0:CONV_PAD, :])
            back = jnp.concatenate([head, back[CONV_PAD:, :]], axis=0)
            acc = acc + back * cw_ref[CONV_W - 1 - d:CONV_W - d, :]
        qk = acc * _sigmoid(acc)
        q_ref[s * sl:(s + 1) * sl, :] = (qk[:, :D_MODEL] * (HEAD_DIM ** -0.5)).astype(BF16)
        ks.append(qk[:, D_MODEL:])
        ext_ref[s] = cur[sl - CONV_PAD:sl, :]
        cn_ref[s] = ext_ref[s, tail:CONV_PAD, :]
    if tm % LANES:
        ks.append(jnp.zeros((LANES - tm % LANES, D_MODEL), F32))
    k = jnp.concatenate(ks, axis=0) if len(ks) > 1 else ks[0]
    kt = k.T.astype(BF16)
    nch = tm // lc
    per_seq = sl // lc
    for c in range(nch):
        kt_ref[c // per_seq, c % per_seq] = kt[:, c * lc:(c + 1) * lc]

    v_ref[...] = _dot(xn, wv_ref[...]).astype(BF16)
    o_ref[...] = _sigmoid(_dot(xn, wo_ref[...])).astype(BF16)

    zg = _dot(xn, wif_ref[...])
    if tm % LANES:
        zg = jnp.concatenate([zg, jnp.zeros((LANES - tm % LANES, LANES), F32)], axis=0)
    zt = zg.T[0:2 * HEADS, :]
    z = jnp.concatenate([zt[:, c * lc:(c + 1) * lc] + bif_ref[...] for c in range(nch)], axis=0)
    is_ig = (lax.broadcasted_iota(jnp.int32, z.shape, 0) % (2 * HEADS)) < HEADS
    g = jnp.where(is_ig, z, _log_sigmoid(z))
    upper = jnp.where(lax.broadcasted_iota(jnp.int32, (lc, lc), 0)
                      <= lax.broadcasted_iota(jnp.int32, (lc, lc), 1), 1.0, 0.0)
    bc = jnp.dot(g, upper, preferred_element_type=F32, precision=lax.Precision.HIGHEST)
    a = g - pltpu.roll(bc, nch * 2 * HEADS - HEADS, axis=0)
    amax = jnp.broadcast_to(jnp.max(a, axis=-1, keepdims=True), z.shape)
    gb = jnp.where(is_ig, g, bc)
    for c in range(nch):
        gt_ref[c // per_seq, c % per_seq, 0:2 * HEADS, :] = gb[c * 8:(c + 1) * 8, :]
        gt_ref[c // per_seq, c % per_seq, 2 * HEADS:4 * HEADS, :] = amax[c * 8:(c + 1) * 8, :]


def _tile_geometry(b, l, tile=TOKEN_TILE):
    sl = min(l, tile)
    nseg = max(1, min(b, TOKEN_TILE // sl))
    return nseg, sl


def _tok_spec(nseg, sl, nt, w):
    return pl.BlockSpec((nseg * sl, w), lambda bi, i: (bi * nt + i, 0))


def _inproj_a(x2d, b, l, g, wa, bif, cw, cb, cs):
    nseg, sl = _tile_geometry(b, l)
    nt = l // sl
    n = b * l
    CHUNK = min(MLSTM_BLOCK, l)
    per_seq = sl // CHUNK
    tok = functools.partial(_tok_spec, nseg, sl, nt)
    state = pl.BlockSpec((nseg, CONV_W - 1, 2 * D_MODEL), lambda bi, i: (bi, 0, 0))
    return pl.pallas_call(
        functools.partial(_inproj_a_kernel, nseg=nseg, sl=sl, lc=CHUNK),
        grid=(b // nseg, nt),
        in_specs=[tok(D_MODEL), _const_spec((1, D_MODEL)), _const_block((D_MODEL, 2 * D_MODEL), (0, 0)),
                  _const_block((D_MODEL, D_MODEL), (0, 2)), _const_block((D_MODEL, D_MODEL), (0, 3)),
                  _const_block((D_MODEL, LANES), (0, 4 * D_MODEL // LANES)), _const_spec((2 * HEADS, 1)),
                  _const_spec((CONV_W, 2 * D_MODEL)), _const_spec((1, 2 * D_MODEL)), state],
        out_specs=[tok(D_MODEL), tok(D_MODEL),
                   pl.BlockSpec((nseg, per_seq, D_MODEL, CHUNK), lambda bi, i: (bi, i, 0, 0)),
                   tok(D_MODEL), tok(D_MODEL),
                   pl.BlockSpec((nseg, per_seq, 4 * HEADS, CHUNK), lambda bi, i: (bi, i, 0, 0)),
                   state],
        out_shape=[jax.ShapeDtypeStruct((n, D_MODEL), BF16), jax.ShapeDtypeStruct((n, D_MODEL), BF16),
                   jax.ShapeDtypeStruct((b, l // CHUNK, D_MODEL, CHUNK), BF16),
                   jax.ShapeDtypeStruct((n, D_MODEL), BF16), jax.ShapeDtypeStruct((n, D_MODEL), BF16),
                   jax.ShapeDtypeStruct((b, l // CHUNK, 4 * HEADS, CHUNK), F32),
                   jax.ShapeDtypeStruct((b, CONV_W - 1, 2 * D_MODEL), F32)],
        scratch_shapes=[pltpu.VMEM((nseg, CONV_PAD, 2 * D_MODEL), F32)],
        compiler_params=_cparams(("parallel", "arbitrary")),
        name="inproj_a",
    )(x2d, g, wa, wa, wa, wa, bif, cw, cb, cs)


def _inproj_b_kernel(x_ref, wu_ref, wgv_ref, wmq_ref, wgate_ref, lng_ref, lnb_ref, ws_ref, bst_ref,
                     mk_ref, mv_ref, ug_ref, att_ref, gates_ref, *rest, nseg, sl, cl, emit_gv):
    tm = nseg * sl
    xn = x_ref[...]

    gates_ref[...] = _sigmoid(_dot_nt(xn, wgate_ref[...])).astype(BF16)

    gvr = jax.nn.gelu(_dot_nt(xn, wgv_ref[...]))
    mu = jnp.mean(gvr, axis=-1, keepdims=True)
    xc = gvr - mu
    r = lax.rsqrt(jnp.mean(xc * xc, axis=-1, keepdims=True) + EPS)
    gv = (xc * r) * lng_ref[...] + lnb_ref[...]
    if emit_gv:
        rest[0][...] = gv
    gvb = gv.astype(BF16)
    u = jax.nn.gelu(_dot_nt(xn, wu_ref[...]))
    tri = (lax.broadcasted_iota(jnp.int32, (cl, cl), 0) >= lax.broadcasted_iota(jnp.int32, (cl, cl), 1))
    for gi in range(GMLP_GROUPS):
        wsg = jnp.where(tri, ws_ref[gi], 0.0).astype(BF16)
        lo, hi = gi * GMLP_GROUP_DIM, (gi + 1) * GMLP_GROUP_DIM
        for c in range(tm // cl):
            sp = _dot(wsg, gvb[c * cl:(c + 1) * cl, lo:hi]) + bst_ref[:, gi:gi + 1]
            ug_ref[c * cl:(c + 1) * cl, lo:hi] = (u[c * cl:(c + 1) * cl, lo:hi] * sp).astype(BF16)

    mq = _dot_nt(xn, wmq_ref[...]).astype(BF16)

    def mem_head(ref, s, h):
        if len(ref.shape) == 4:
            return ref[s, :, h, :].astype(BF16)
        return ref[s, :, h * HEAD_DIM:(h + 1) * HEAD_DIM]

    for s in range(nseg):
        r0, r1 = s * sl, (s + 1) * sl
        for h in range(HEADS):
            lo, hi = h * HEAD_DIM, (h + 1) * HEAD_DIM
            sc = _dot_nt(mq[r0:r1, lo:hi], mem_head(mk_ref, s, h)) * (HEAD_DIM ** -0.5)
            e = jnp.exp(sc - jnp.max(sc, axis=-1, keepdims=True))
            a = (e / jnp.sum(e, axis=-1, keepdims=True)).astype(BF16)
            att_ref[r0:r1, lo:hi] = _dot(a, mem_head(mv_ref, s, h)).astype(BF16)


def _inproj_b(xn2d, b, l, wb, lng, lnb, ws, bst, mk, mv, cl, emit_gv):
    nseg, sl = _tile_geometry(b, l, INPROJ_B_TILE)
    nt = l // sl
    n = b * l
    tok = functools.partial(_tok_spec, nseg, sl, nt)
    if mk.ndim == 4:
        mem = pl.BlockSpec((nseg, N_MEM, HEADS, HEAD_DIM), lambda bi, i: (bi, 0, 0, 0), pipeline_mode=pl.Buffered(1))
    else:
        mem = pl.BlockSpec((nseg, N_MEM, D_MODEL), lambda bi, i: (bi, 0, 0))
    out_specs = [tok(D_MODEL), tok(D_MODEL), tok(3 * D_MODEL)]
    out_shape = [jax.ShapeDtypeStruct((n, D_MODEL), BF16), jax.ShapeDtypeStruct((n, D_MODEL), BF16),
                 jax.ShapeDtypeStruct((n, 3 * D_MODEL), BF16)]
    if emit_gv:
        out_specs.append(tok(D_MODEL))
        out_shape.append(jax.ShapeDtypeStruct((n, D_MODEL), F32))
    return pl.pallas_call(
        functools.partial(_inproj_b_kernel, nseg=nseg, sl=sl, cl=cl, emit_gv=emit_gv),
        grid=(b // nseg, nt),
        in_specs=[tok(D_MODEL), _const_block((D_MODEL, D_MODEL), (0, 0)),
                  _const_block((D_MODEL, D_MODEL), (1, 0)), _const_block((D_MODEL, D_MODEL), (2, 0)),
                  _const_block((3 * D_MODEL, D_MODEL), (1, 0)), _const_spec((1, D_MODEL)), _const_spec((1, D_MODEL)),
                  _const_spec((GMLP_GROUPS, cl, cl)), _const_spec((cl, GMLP_GROUPS)), mem, mem],
        out_specs=out_specs,
        out_shape=out_shape,
        compiler_params=_cparams(("parallel", "parallel")),
        name="inproj_b",
    )(xn2d, wb, wb, wb, wb, lng, lnb, ws, bst, mk, mv)


def _mlstm_kernel(q_ref, kt_ref, v_ref, gt_ref, *rest, nseg, cb, zero_state):
    i = pl.program_id(1)

    if zero_state:
        hm_ref, c_ref, n_ref, m_ref, st_ref = rest
    else:
        c0_ref, n0_ref, m0_ref, hm_ref, c_ref, n_ref, m_ref, st_ref = rest

    @pl.when(i == 0)
    def _():
        if zero_state:
            st_ref[...] = jnp.zeros(st_ref.shape, F32)
            m_ref[...] = jnp.zeros(m_ref.shape, F32)
        else:
            st_ref[:, :, :, :HEAD_DIM] = c0_ref[...]
            st_ref[:, :, :, HEAD_DIM:] = n0_ref[...]
            m_ref[...] = m0_ref[...]

    for s in range(nseg):
        _mlstm_sequence(q_ref, kt_ref.at[s], v_ref, gt_ref.at[s], st_ref.at[s], m_ref.at[s], hm_ref, s * cb, cb)

    @pl.when(i == pl.num_programs(1) - 1)
    def _():
        c_ref[...] = st_ref[:, :, :, :HEAD_DIM]
        n_ref[...] = st_ref[:, :, :, HEAD_DIM:]


def _mlstm_sequence(q_ref, kt_ref, v_ref, gt_ref, c_ref, m_ref, hm_ref, row0, cb):
    L = kt_ref.shape[-1]
    nch = cb // L
    ti = lax.broadcasted_iota(jnp.int32, (L, L), 0)
    si = lax.broadcasted_iota(jnp.int32, (L, L), 1)
    tri = ti >= si
    eye = ti == si

    rows = 4 * HEADS
    g_all = gt_ref[...].reshape(nch * rows, L)

    m_in = [m_ref[:, 0:1]]
    for c in range(nch):
        b_last4 = g_all[c * rows + HEADS:c * rows + 2 * HEADS, L - 1:L]
        amax4 = g_all[c * rows + 2 * HEADS:c * rows + 3 * HEADS, 0:1]
        m_in.append(jnp.maximum(b_last4 + m_in[-1], b_last4 + amax4))

    ones = jnp.ones((L, LANES), BF16)
    st = [c_ref[h] for h in range(HEADS)]
    for c in range(nch):
        r0, r1 = row0 + c * L, row0 + (c + 1) * L
        for h in range(HEADS):
            lo, hi = h * HEAD_DIM, (h + 1) * HEAD_DIM
            ig_r = g_all[c * rows + h:c * rows + h + 1, :]
            bc_r = g_all[c * rows + HEADS + h:c * rows + HEADS + h + 1, :]
            a_r = ig_r - bc_r
            bc_c = jnp.sum(jnp.where(eye, bc_r, 0.0), axis=-1, keepdims=True)
            m0 = m_in[c][h:h + 1, :]
            m_last = m_in[c + 1][h:h + 1, :]
            dmat = jnp.where(tri, bc_c + a_r, NEG_INF)
            inter = bc_c + m0
            m = jnp.maximum(inter, jnp.max(dmat, axis=-1, keepdims=True))
            w_intra = jnp.exp(dmat - m)
            w_inter = jnp.exp(inter - m)
            q = q_ref[r0:r1, lo:hi]
            kt = kt_ref[c, lo:hi, :]
            v = v_ref[r0:r1, lo:hi]
            s = _dot(q, kt) * w_intra
            qs = _dot(q, st[h].astype(BF16))
            num = w_inter * qs[:, :HEAD_DIM] + _dot(s.astype(BF16), v)
            den = w_inter * qs[:, HEAD_DIM:HEAD_DIM + 1] + jnp.sum(s, axis=-1, keepdims=True)
            hh = num / jnp.maximum(jnp.abs(den), jnp.exp(-m))
            bc_last = bc_r[:, L - 1:L]
            w_last = jnp.exp(bc_last + a_r - m_last)
            decay = jnp.exp(bc_last + m0 - m_last)
            ktw = (kt.astype(F32) * w_last).astype(BF16)
            st[h] = decay * st[h] + _dot(ktw, jnp.concatenate([v, ones], axis=1))
            hm_ref[r0:r1, lo:hi] = hh.astype(BF16)

    for h in range(HEADS):
        c_ref[h] = st[h]
    m_ref[...] = jnp.broadcast_to(m_in[nch], (HEADS, LANES))


def _mlstm(q, kt, v, gt, state, b, l):
    nseg, cb = _tile_geometry(b, l, MLSTM_TILE)
    nt = l // cb
    CHUNK = kt.shape[-1]
    tok = _tok_spec(nseg, cb, nt, D_MODEL)
    cs = pl.BlockSpec((nseg, HEADS, HEAD_DIM, HEAD_DIM), lambda bi, i: (bi, 0, 0, 0))
    ns = pl.BlockSpec((nseg, HEADS, HEAD_DIM, LANES), lambda bi, i: (bi, 0, 0, 0))
    ms = pl.BlockSpec((nseg, HEADS, LANES), lambda bi, i: (bi, 0, 0))
    state_specs = [] if state is None else [cs, ns, ms]
    return pl.pallas_call(
        functools.partial(_mlstm_kernel, nseg=nseg, cb=cb, zero_state=state is None),
        grid=(b // nseg, nt),
        in_specs=[tok, pl.BlockSpec((nseg, cb // CHUNK, D_MODEL, CHUNK), lambda bi, i: (bi, i, 0, 0)), tok,
                  pl.BlockSpec((nseg, cb // CHUNK, 4 * HEADS, CHUNK), lambda bi, i: (bi, i, 0, 0))] + state_specs,
        out_specs=[tok, cs, ns, ms],
        out_shape=[jax.ShapeDtypeStruct((b * l, D_MODEL), BF16),
                   jax.ShapeDtypeStruct((b, HEADS, HEAD_DIM, HEAD_DIM), F32),
                   jax.ShapeDtypeStruct((b, HEADS, HEAD_DIM, LANES), F32),
                   jax.ShapeDtypeStruct((b, HEADS, LANES), F32)],
        scratch_shapes=[pltpu.VMEM((nseg, HEADS, HEAD_DIM, HEAD_DIM + LANES), F32)],
        compiler_params=_cparams(("parallel", "arbitrary")),
        name="mlstm",
    )(q, kt, v, gt, *(state or ()))


def _merge_kernel(x_ref, hh_ref, o_ref, ng_ref, ug_ref, att_ref, gates_ref, wa_ref, wb_ref, wc_ref, wo_ref, h_ref):
    parts = []
    for h in range(HEADS):
        lo, hi = h * HEAD_DIM, (h + 1) * HEAD_DIM
        hh = hh_ref[:, lo:hi].astype(F32)
        hn = hh * lax.rsqrt(jnp.mean(hh * hh, axis=-1, keepdims=True) + EPS)
        parts.append(((hn * ng_ref[:, lo:hi]) * o_ref[:, lo:hi].astype(F32)).astype(BF16))
    br = _dot(parts[0], wa_ref[0:HEAD_DIM, :])
    for h in range(1, HEADS):
        br = br + _dot(parts[h], wa_ref[h * HEAD_DIM:(h + 1) * HEAD_DIM, :])
    g = gates_ref[...].astype(F32)
    mixed = g[:, :D_MODEL] * br
    mixed = mixed + g[:, D_MODEL:2 * D_MODEL] * _dot(ug_ref[...], wb_ref[...])
    mixed = mixed + g[:, 2 * D_MODEL:] * _dot(att_ref[...], wc_ref[...])
    h_ref[...] = x_ref[...] + _dot(mixed.astype(BF16), wo_ref[...])


def _merge(x2d, hh, osig, ng, ug, att, gates, wa, wb, wc, wo, tm):
    n = x2d.shape[0]
    row = lambda w: pl.BlockSpec((tm, w), lambda i: (i, 0))
    wspec = _const_spec((D_MODEL, D_MODEL))
    return pl.pallas_call(
        _merge_kernel,
        grid=(n // tm,),
        in_specs=[row(D_MODEL), row(D_MODEL), row(D_MODEL), _const_spec((1, D_MODEL)), row(D_MODEL), row(D_MODEL),
                  row(3 * D_MODEL), wspec, wspec, wspec, wspec],
        out_specs=row(D_MODEL),
        out_shape=jax.ShapeDtypeStruct((n, D_MODEL), F32),
        compiler_params=_cparams(("parallel",)),
        name="merge",
    )(x2d, hh, osig, ng, ug, att, gates, wa, wb, wc, wo)


MOE_T = 256
MOE_STEP_SUBS = 2
MOE_CAP = 32
MOE_S = N_EXPERTS * MOE_CAP
MOE_OVF = 512
SEG_ALIGN = 16
EXPERT_BLOCK = 128
MAIN_RING = 3


def _moe_route_kernel(hp_ref, hs_ref, g_ref, wrg_ref, brg_ref, wre_ref, bre_ref,
                      main_ref, ovf_ref, info_ref, meta_ref, *, nsp):
    n = MOE_STEP_SUBS * MOE_T
    h = jnp.where(pl.program_id(0) < nsp // MOE_STEP_SUBS, hp_ref[...], hs_ref[...])
    xm = _rmsnorm(h, g_ref[...]).astype(BF16)
    lane = lax.broadcasted_iota(jnp.int32, (n, LANES), 1).astype(F32)
    lg = jnp.where(lane < N_GROUPS, _dot(xm, wrg_ref[...]) + brg_ref[...], NEG_INF)
    gmax = jnp.max(lg, axis=-1, keepdims=True)
    p_top = 1.0 / jnp.sum(jnp.exp(lg - gmax), axis=-1, keepdims=True)
    grp = jnp.min(jnp.where(lg == gmax, lane, float(LANES)), axis=-1, keepdims=True)
    el = _dot(xm, wre_ref[...]) + bre_ref[...]
    in_grp = (lane >= grp * EXPERTS_PER_GROUP) & (lane < (grp + 1.0) * EXPERTS_PER_GROUP)
    vals = jnp.where(in_grp, el, NEG_INF)
    v1 = jnp.max(vals, axis=-1, keepdims=True)
    i1 = jnp.min(jnp.where(vals == v1, lane, float(LANES)), axis=-1, keepdims=True)
    vals2 = jnp.where(lane == i1, NEG_INF, vals)
    v2 = jnp.max(vals2, axis=-1, keepdims=True)
    i2 = jnp.min(jnp.where(vals2 == v2, lane, float(LANES)), axis=-1, keepdims=True)
    r = jnp.exp(v2 - v1)
    p1 = p_top / (1.0 + r)
    p2 = p_top * r / (1.0 + r)
    sel1 = lane == i1
    sel2 = lane == i2
    onehot = jnp.where(sel1 | sel2, 1.0, 0.0)

    deferred = []
    for s in range(MOE_STEP_SUBS):
        rows = slice(s * MOE_T, (s + 1) * MOE_T)
        deferred.append(_route_sub_tile(
            xm[rows, :], onehot[rows, :], i1[rows, :], i2[rows, :],
            p1[rows, :], p2[rows, :], main_ref.at[:, s], ovf_ref.at[s], info_ref.at[rows, :], meta_ref.at[s]))
    for write_overflow in deferred:
        write_overflow()


def _route_sub_tile(xm, onehot, i1, i2, p1, p2, main_ref, ovf_ref, info_ref, meta_ref):
    t = MOE_T
    lane = lax.broadcasted_iota(jnp.int32, (t, LANES), 1)
    sel1 = lane.astype(F32) == i1
    sel2 = lane.astype(F32) == i2
    cnt = jnp.sum(onehot, axis=0, keepdims=True).astype(jnp.int32)
    pn = jnp.bitwise_and(jnp.maximum(cnt - MOE_CAP, 0) + (SEG_ALIGN - 1), -SEG_ALIGN)
    pn8 = jnp.broadcast_to(pn, (8, LANES))
    earlier = jnp.where(lax.broadcasted_iota(jnp.int32, (LANES, LANES), 0)
                        < lax.broadcasted_iota(jnp.int32, (LANES, LANES), 1), 1.0, 0.0).astype(BF16)
    off_f8 = _dot(pn8.astype(F32).astype(BF16), earlier)
    row8 = lax.broadcasted_iota(jnp.int32, (8, LANES), 0)
    meta_ref[...] = jnp.where(row8 == 0, off_f8.astype(jnp.int32), jnp.where(row8 == 1, pn8, 0))

    ti = lax.broadcasted_iota(jnp.int32, (t, t), 0)
    si = lax.broadcasted_iota(jnp.int32, (t, t), 1)
    before = jnp.where(ti > si, 1.0, 0.0).astype(BF16)
    rank = _dot(before, onehot.astype(BF16))
    off_f = off_f8[0:1, :]

    def slot_row(sel, idx):
        rk = jnp.sum(jnp.where(sel, rank, 0.0), axis=-1, keepdims=True)
        of = jnp.sum(jnp.where(sel, off_f, 0.0), axis=-1, keepdims=True)
        return jnp.where(rk < MOE_CAP, idx * MOE_CAP + rk, MOE_S - MOE_CAP + of + rk)

    pos1 = slot_row(sel1, i1)
    pos2 = slot_row(sel2, i2)
    info_ref[...] = (jnp.where(lane == 0, pos1, 0.0) + jnp.where(lane == 1, pos2, 0.0)
                     + jnp.where(lane == 2, p1, 0.0) + jnp.where(lane == 3, p2, 0.0))

    eye = ti == si
    pos1_r = jnp.sum(jnp.where(eye, pos1, 0.0), axis=0, keepdims=True)
    pos2_r = jnp.sum(jnp.where(eye, pos2, 0.0), axis=0, keepdims=True)

    def gather(rows, base):
        srow = (lax.broadcasted_iota(jnp.int32, (rows, t), 0) + base).astype(F32)
        pick = jnp.where((srow == pos1_r) | (srow == pos2_r), 1.0, 0.0).astype(BF16)
        return _dot(pick, xm).astype(BF16)

    main_ref[...] = gather(MOE_S, 0).reshape(N_EXPERTS, MOE_CAP, D_MODEL)
    has_ovf = jnp.sum(pn) > 0

    def write_overflow():
        @pl.when(has_ovf)
        def _():
            ovf_ref[...] = gather(MOE_OVF, MOE_S)

        @pl.when(jnp.logical_not(has_ovf))
        def _():
            ovf_ref[...] = jnp.zeros(ovf_ref.shape, ovf_ref.dtype)
    return write_overflow


def _two_group_specs(nsp, cols):
    rows = MOE_STEP_SUBS * MOE_T
    npb = nsp // MOE_STEP_SUBS
    return (pl.BlockSpec((rows, cols), lambda j, *_: (jnp.minimum(j, npb - 1), 0)),
            pl.BlockSpec((rows, cols), lambda j, *_: (jnp.maximum(j - npb, 0), 0)))


def _moe_route(hp, hs, g, wrg, brg, wre, bre):
    nsp = hp.shape[0] // MOE_T
    ns = nsp + hs.shape[0] // MOE_T
    sub = MOE_STEP_SUBS
    return pl.pallas_call(
        functools.partial(_moe_route_kernel, nsp=nsp),
        grid=(ns // sub,),
        in_specs=[*_two_group_specs(nsp, D_MODEL), _const_spec((1, D_MODEL)),
                  _const_spec((D_MODEL, LANES)), _const_spec((1, LANES)),
                  _const_spec((D_MODEL, LANES)), _const_spec((1, LANES))],
        out_specs=[pl.BlockSpec((N_EXPERTS, sub, MOE_CAP, D_MODEL), lambda j: (0, j, 0, 0)),
                   pl.BlockSpec((sub, MOE_OVF, D_MODEL), lambda j: (j, 0, 0)),
                   pl.BlockSpec((sub * MOE_T, LANES), lambda j: (j, 0)),
                   pl.BlockSpec((sub, 8, LANES), lambda j: (j, 0, 0))],
        out_shape=[jax.ShapeDtypeStruct((N_EXPERTS, ns, MOE_CAP, D_MODEL), BF16),
                   jax.ShapeDtypeStruct((ns, MOE_OVF, D_MODEL), BF16),
                   jax.ShapeDtypeStruct((ns * MOE_T, LANES), F32),
                   jax.ShapeDtypeStruct((ns, 8, LANES), jnp.int32)],
        compiler_params=_cparams(("arbitrary",)),
        name="moe_route",
    )(hp, hs, g, wrg, brg, wre, bre)


def _swiglu(x, wgu, wdb):
    gu = _dot(x, wgu[...])
    gate = gu[:, :D_EXPERT]
    hid = (gate * _sigmoid(gate)) * gu[:, D_EXPERT:]
    return _dot(hid.astype(BF16), wdb[...]).astype(BF16)


def _moe_expert_kernel(offs_ref, pns_ref, eflag_ref, main_hbm, slots_hbm, wg_ref, wu_ref, wd_ref,
                       mout_ref, out_hbm, ring, xbuf, obuf, wgu, wdb, sem_ring, sem_in, sem_out, *, n_sub, jb, n_rb):
    e = pl.program_id(0)
    rb = pl.program_id(1)

    def fetch(ex, slot):
        return pltpu.make_async_copy(main_hbm.at[ex], ring.at[slot], sem_ring.at[slot])

    @pl.when(e == 0)
    def _():
        for ex in range(MAIN_RING - 1):
            fetch(ex, ex).start()

    @pl.when(e + MAIN_RING - 1 < N_EXPERTS)
    def _():
        fetch(e + MAIN_RING - 1, (e + MAIN_RING - 1) % MAIN_RING).start()

    slot = e % MAIN_RING
    fetch(e, slot).wait()
    main_ref = ring.at[slot]

    @pl.when((e == 0) & (rb == 0))
    def _():
        xbuf[...] = jnp.zeros_like(xbuf)

    @pl.when(rb == 0)
    def _():
        wgu[:, :D_EXPERT] = wg_ref[...].astype(BF16)
        wgu[:, D_EXPERT:] = wu_ref[...].astype(BF16)
        wdb[...] = wd_ref[...].astype(BF16)

    sb = main_ref.shape[0]
    part = sb // 2 if sb % 2 == 0 else sb
    for s0 in range(0, sb, part):
        y = _swiglu(main_ref[s0:s0 + part].reshape(part * MOE_CAP, D_MODEL), wgu, wdb)
        mout_ref[s0:s0 + part] = y.reshape(part, MOE_CAP, D_MODEL)

    def copy_in(j, src, dst):
        return pltpu.make_async_copy(slots_hbm.at[j, pl.ds(src, SEG_ALIGN), :],
                                     xbuf.at[pl.ds(dst, SEG_ALIGN), :], sem_in)

    def copy_out(j, src, dst):
        return pltpu.make_async_copy(obuf.at[pl.ds(src, SEG_ALIGN), :],
                                     out_hbm.at[j, pl.ds(dst, SEG_ALIGN), :], sem_out)

    def for_each_chunk(g, fn):
        def seg(jj, cur):
            j = g * jb + jj
            n = pns_ref[j * N_EXPERTS + e]
            off = offs_ref[j * N_EXPERTS + e]

            def chunk(k, c):
                fn(j, pl.multiple_of(off + k * SEG_ALIGN, SEG_ALIGN), pl.multiple_of(cur + k * SEG_ALIGN, SEG_ALIGN))
                return c
            lax.fori_loop(0, n // SEG_ALIGN, chunk, 0)
            return cur + n
        return lax.fori_loop(0, jb, seg, 0)

    def group(g, carry):
        total = for_each_chunk(g, lambda j, r, b: copy_in(j, r, b).start())
        nchunk = total // SEG_ALIGN

        def wait_in(k, c):
            copy_in(0, 0, 0).wait()
            return c
        lax.fori_loop(0, nchunk, wait_in, 0)

        def block(bi, c):
            r0 = pl.multiple_of(bi * EXPERT_BLOCK, EXPERT_BLOCK)
            obuf[pl.ds(r0, EXPERT_BLOCK), :] = _swiglu(xbuf[pl.ds(r0, EXPERT_BLOCK), :], wgu, wdb)
            return c
        lax.fori_loop(0, (total + EXPERT_BLOCK - 1) // EXPERT_BLOCK, block, 0)

        for_each_chunk(g, lambda j, r, b: copy_out(j, b, r).start())

        def wait_out(k, c):
            copy_out(0, 0, 0).wait()
            return c
        lax.fori_loop(0, nchunk, wait_out, 0)
        return carry

    @pl.when((rb == n_rb - 1) & (eflag_ref[e] > 0))
    def _():
        lax.fori_loop(0, n_sub // jb, group, 0)


def _moe_expert(main, ovf, offs, pns, eflag, wg, wu, wd):
    ns = main.shape[1]
    n_rb = 1
    sb = ns // n_rb
    jb = max(d for d in range(1, 12) if ns % d == 0)
    rows = jb * MOE_T + EXPERT_BLOCK
    mspec = pl.BlockSpec((None, sb, MOE_CAP, D_MODEL), lambda e, rb, *_: (e, rb, 0, 0))
    wspec = lambda a, b: pl.BlockSpec((None, a, b), lambda e, rb, *_: (e, 0, 0))
    grid_spec = pltpu.PrefetchScalarGridSpec(
        num_scalar_prefetch=3,
        grid=(N_EXPERTS, n_rb),
        in_specs=[pl.BlockSpec(memory_space=pl.ANY), pl.BlockSpec(memory_space=pl.ANY),
                  wspec(D_MODEL, D_EXPERT), wspec(D_MODEL, D_EXPERT), wspec(D_EXPERT, D_MODEL)],
        out_specs=[mspec, pl.BlockSpec(memory_space=pl.ANY)],
        scratch_shapes=[pltpu.VMEM((MAIN_RING, sb, MOE_CAP, D_MODEL), BF16),
                        pltpu.VMEM((rows, D_MODEL), BF16), pltpu.VMEM((rows, D_MODEL), BF16),
                        pltpu.VMEM((D_MODEL, 2 * D_EXPERT), BF16), pltpu.VMEM((D_EXPERT, D_MODEL), BF16),
                        pltpu.SemaphoreType.DMA((MAIN_RING,)),
                        pltpu.SemaphoreType.DMA(()), pltpu.SemaphoreType.DMA(())],
    )
    mout, oout = pl.pallas_call(
        functools.partial(_moe_expert_kernel, n_sub=ns, jb=jb, n_rb=n_rb),
        grid_spec=grid_spec,
        out_shape=[jax.ShapeDtypeStruct(main.shape, main.dtype), jax.ShapeDtypeStruct(ovf.shape, ovf.dtype)],
        input_output_aliases={4: 1},
        compiler_params=_cparams(("arbitrary", "arbitrary")),
        name="moe_expert",
    )(offs, pns, eflag, main, ovf, wg, wu, wd)
    return mout, oout


def _moe_combine_kernel(jflag_ref, sflag_ref, hp_hbm, hs_hbm, main_hbm, ovf_ref, info_ref, gf_ref,
                        yp_ref, ys_ref, acc_ref, ring, ring_h, sem_ring, sem_h, *, nsp, nstep):
    j = pl.program_id(0)
    npb = nsp // MOE_STEP_SUBS
    tok = MOE_STEP_SUBS * MOE_T

    def fetch(step, slot):
        sub0 = pl.multiple_of(step * MOE_STEP_SUBS, MOE_STEP_SUBS)
        return pltpu.make_async_copy(main_hbm.at[:, pl.ds(sub0, MOE_STEP_SUBS)], ring.at[slot], sem_ring.at[slot])

    def fetch_h(src, block, slot):
        row0 = pl.multiple_of(block * tok, tok)
        return pltpu.make_async_copy(src.at[pl.ds(row0, tok), :], ring_h.at[slot], sem_h.at[slot])

    def start(step, slot):
        fetch(step, slot).start()

        @pl.when(step < npb)
        def _():
            fetch_h(hp_hbm, step, slot).start()

        @pl.when(step >= npb)
        def _():
            fetch_h(hs_hbm, step - npb, slot).start()

    @pl.when(j == 0)
    def _():
        for step in range(min(MAIN_RING - 1, nstep)):
            start(jnp.int32(step), step)

    @pl.when(j + MAIN_RING - 1 < nstep)
    def _():
        start(j + MAIN_RING - 1, (j + MAIN_RING - 1) % MAIN_RING)

    slot = j % MAIN_RING
    fetch(j, slot).wait()
    fetch_h(hp_hbm, 0, slot).wait()
    main_ref = ring.at[slot]

    def weights(s, cols, base):
        info = info_ref[s * MOE_T:(s + 1) * MOE_T, :]
        scol = (lax.broadcasted_iota(jnp.int32, (MOE_T, cols), 1) + base).astype(F32)
        return (jnp.where(scol == info[:, 0:1], info[:, 2:3], 0.0)
                + jnp.where(scol == info[:, 1:2], info[:, 3:4], 0.0)).astype(BF16)

    for s in range(MOE_STEP_SUBS):
        rows = slice(s * MOE_T, (s + 1) * MOE_T)
        h = ring_h[slot, rows, :]
        acc_ref[rows, :] = h + _dot(weights(s, MOE_S, 0), main_ref[:, s].reshape(MOE_S, D_MODEL))

    for s in range(MOE_STEP_SUBS):
        @pl.when(jflag_ref[j * MOE_STEP_SUBS + s] > 0)
        def _(s=s):
            rows = slice(s * MOE_T, (s + 1) * MOE_T)
            acc_ref[rows, :] += _dot(weights(s, MOE_OVF, MOE_S), ovf_ref[s])

    y = _rmsnorm(acc_ref[...], gf_ref[...])

    @pl.when(j < npb)
    def _():
        yp_ref[...] = y

    @pl.when(j >= npb)
    def _():
        ys_ref[...] = y


def _moe_combine(hp, hs, main, ovf, info, jflag, gf):
    nsp = hp.shape[0] // MOE_T
    ns = main.shape[1]
    sub = MOE_STEP_SUBS
    sflag = jnp.max(jflag.reshape(ns // sub, sub), axis=1)
    grid_spec = pltpu.PrefetchScalarGridSpec(
        num_scalar_prefetch=2,
        grid=(ns // sub,),
        in_specs=[pl.BlockSpec(memory_space=pl.ANY), pl.BlockSpec(memory_space=pl.ANY),
                  pl.BlockSpec(memory_space=pl.ANY),
                  pl.BlockSpec((sub, MOE_OVF, D_MODEL), lambda j, jf, sf: (jnp.where(sf[j] > 0, j, 0), 0, 0)),
                  pl.BlockSpec((sub * MOE_T, LANES), lambda j, jf, sf: (j, 0)),
                  pl.BlockSpec((1, D_MODEL), lambda j, jf, sf: (0, 0))],
        out_specs=list(_two_group_specs(nsp, D_MODEL)),
        scratch_shapes=[pltpu.VMEM((sub * MOE_T, D_MODEL), F32),
                        pltpu.VMEM((MAIN_RING, N_EXPERTS, sub, MOE_CAP, D_MODEL), BF16),
                        pltpu.VMEM((MAIN_RING, sub * MOE_T, D_MODEL), F32),
                        pltpu.SemaphoreType.DMA((MAIN_RING,)), pltpu.SemaphoreType.DMA((MAIN_RING,))],
    )
    return pl.pallas_call(
        functools.partial(_moe_combine_kernel, nsp=nsp, nstep=ns // sub),
        grid_spec=grid_spec,
        out_shape=[jax.ShapeDtypeStruct(hp.shape, F32), jax.ShapeDtypeStruct(hs.shape, F32)],
        compiler_params=_cparams(("arbitrary",)),
        name="moe_combine",
    )(jflag, sflag, hp, hs, main, ovf, info, gf)


def _moe(hp, hs, p):
    main, ovf, info, meta = _moe_route(hp, hs, p["norm_moe_g"], p["w_rg"], p["b_rg"], p["w_re"], p["b_re"])
    pn = meta[:, 1, :N_EXPERTS]
    offs = meta[:, 0, :N_EXPERTS].reshape(-1)
    eflag = (jnp.sum(pn, axis=0) > 0).astype(jnp.int32)
    jflag = (jnp.sum(pn, axis=1) > 0).astype(jnp.int32)
    main, ovf = _moe_expert(main, ovf, offs, pn.reshape(-1), eflag, p["w_eg"], p["w_eu"], p["w_ed"])
    return _moe_combine(hp, hs, main, ovf, info, jflag, p["norm_final_g"])


def _trunk(x, mem_k, mem_v, c0, n0, m0, conv0, p, *, emit_gv):
    b, l, _ = x.shape
    n = b * l
    cl = min(l, GMLP_CHUNK)
    x2d = x.reshape(n, D_MODEL)
    xn, q, kt, v, osig, gt, conv_new = _inproj_a(
        x2d, b, l, p["norm_mix_g"], p["w_a"], p["b_if"], p["conv_w"], p["conv_b"], conv0)
    outs = _inproj_b(xn, b, l, p["w_b"], p["gmlp_norm_g"], p["gmlp_norm_b"], p["w_s"][:, :cl, :cl], p["b_st"][:cl],
                     mem_k, mem_v, cl, emit_gv)
    ug, att, gates = outs[:3]
    on_lanes = lambda a: jnp.broadcast_to(a[..., None], a.shape + (LANES,))
    state = None if c0 is None else (c0, on_lanes(n0), on_lanes(m0))
    hh, c1, n1, m1 = _mlstm(q, kt, v, gt, state, b, l)
    n1 = n1[..., 0]
    h = _merge(x2d, hh, osig, p["mlstm_norm_g"], ug, att, gates,
               p["w_br_mlstm"], p["w_br_gmlp"], p["w_br_mem"], p["w_out"], TOKEN_TILE)
    return h, c1, n1, m1[..., 0], conv_new, (outs[3].reshape(b, l, D_MODEL) if emit_gv else None)


def kernel(x_prompt, x_sample, mem_prompt, cache_mem_k, cache_mem_v, state_mlstm_C, state_mlstm_n, state_mlstm_m, state_mlstm_conv, norm_mix_g, w_in, mlstm_i_b, mlstm_f_b, mlstm_conv_w, mlstm_conv_b, mlstm_norm_g, gmlp_norm_g, gmlp_norm_b, gmlp_w_s, gmlp_b_s, mem_norm_g, w_mem_k, w_mem_v, w_br_mlstm, w_br_gmlp, w_br_mem, w_out, norm_moe_g, w_router_group, b_router_group, w_router_expert, b_router_expert, w_exp_gate, w_exp_up, w_exp_down, norm_final_g):
    bp = x_prompt.shape[0]
    bs = x_sample.shape[0]
    W = D_MODEL
    wt = w_in[0].T
    o_u = 4 * W + 2 * HEADS
    w_a = _w_cols(wt, 4 * W + LANES)
    w_b = _w_rows(wt, o_u, (wt.shape[0] - o_u) // W)
    row = lambda a: a.reshape(1, -1)
    pad_l = lambda a: jnp.pad(a, ((0, 0), (0, LANES - a.shape[1])))
    p = {
        "norm_mix_g": row(norm_mix_g[0]),
        "w_a": w_a, "w_b": w_b,
        "b_if": jnp.concatenate([mlstm_i_b[0], mlstm_f_b[0]]).reshape(2 * HEADS, 1),
        "conv_w": mlstm_conv_w[0], "conv_b": row(mlstm_conv_b[0]),
        "gmlp_norm_g": row(gmlp_norm_g[0]), "gmlp_norm_b": row(gmlp_norm_b[0]),
        "w_s": gmlp_w_s[0], "b_st": gmlp_b_s[0].T,
        "mlstm_norm_g": row(mlstm_norm_g[0]),
        "w_br_mlstm": w_br_mlstm[0].astype(BF16), "w_br_gmlp": w_br_gmlp[0].astype(BF16),
        "w_br_mem": w_br_mem[0].astype(BF16), "w_out": w_out[0].astype(BF16),
        "norm_moe_g": row(norm_moe_g[0]),
        "w_rg": pad_l(w_router_group[0]).astype(BF16), "b_rg": pad_l(row(b_router_group[0])),
        "w_re": pad_l(w_router_expert[0]).astype(BF16), "b_re": pad_l(row(b_router_expert[0])),
        "w_eg": w_exp_gate[0], "w_eu": w_exp_up[0], "w_ed": w_exp_down[0],
        "norm_final_g": row(norm_final_g),
    }

    mk_p, mv_p, mk_pb, mv_pb = _memory_kv(mem_prompt.reshape(bp * N_MEM, W), row(mem_norm_g[0]),
                                          w_mem_k[0], w_mem_v[0])

    zeros = lambda *s: jnp.zeros(s, F32)
    hp, cp, np_, mp, cvp, _ = _trunk(
        x_prompt, mk_pb.reshape(bp, N_MEM, W), mv_pb.reshape(bp, N_MEM, W),
        None, None, None, zeros(bp, CONV_W - 1, 2 * W), p, emit_gv=False)
    hs, cs, ns, ms, cvs, gvs = _trunk(
        x_sample, cache_mem_k[0], cache_mem_v[0],
        state_mlstm_C[0], state_mlstm_n[0], state_mlstm_m[0], state_mlstm_conv[0], p,
        emit_gv=True)

    yp, ys = _moe(hp, hs, p)
    return (yp.reshape(x_prompt.shape), ys.reshape(x_sample.shape), mk_p[None], mv_p[None],
            cp[None], np_[None], mp[None], cvp[None],
            cs[None], ns[None], ms[None], cvs[None], gvs[None])
```

```python
import functools

import jax
import jax.numpy as jnp
from jax import lax
from jax.experimental import pallas as pl
from jax.experimental.pallas import tpu as pltpu

D_MODEL = 1024
MLSTM_BLOCK = 512
TOKEN_TILE = 512
INPROJ_B_TILE = 1024
MLSTM_TILE = 2048
EPS = 1e-6
HEADS = 4
HEAD_DIM = 256
CONV_W = 4
GMLP_GROUPS = 4
GMLP_GROUP_DIM = 256
GMLP_CHUNK = 128
N_MEM = 256
N_GROUPS = 4
EXPERTS_PER_GROUP = 8
N_EXPERTS = 32
D_EXPERT = 256
LANES = 128
CONV_PAD = 8
NORM_ROWS = 128

F32 = jnp.float32
BF16 = jnp.bfloat16
NEG_INF = float("-inf")

VMEM_LIMIT = 56 * 1024 * 1024


def _cparams(sem):
    return pltpu.CompilerParams(dimension_semantics=sem, vmem_limit_bytes=VMEM_LIMIT)


def _const_spec(shape):
    nd = len(shape)
    return pl.BlockSpec(shape, lambda *_: (0,) * nd, pipeline_mode=pl.Buffered(1))


def _const_block(shape, idx):
    return pl.BlockSpec(shape, lambda *_: idx, pipeline_mode=pl.Buffered(1))


def _sigmoid(x):
    return 0.5 * (jnp.tanh(0.5 * x) + 1.0)


def _log_sigmoid(x):
    return jnp.minimum(x, 0.0) - jnp.log(1.0 + jnp.exp(-jnp.abs(x)))


def _rmsnorm(x, g):
    r = lax.rsqrt(jnp.mean(x * x, axis=-1, keepdims=True) + EPS)
    return (x * r) * g


def _dot(a, b):
    return jnp.dot(a, b, preferred_element_type=F32)


def _dot_nt(a, b):
    return lax.dot_general(a, b, (((1,), (1,)), ((), ())), preferred_element_type=F32)


def _memkv_kernel(mem_ref, g_ref, wk_ref, wv_ref, k_ref, v_ref, kb_ref, vb_ref):
    mn = _rmsnorm(mem_ref[...], g_ref[...]).astype(BF16)
    k = _dot(mn, wk_ref[...].astype(BF16))
    v = _dot(mn, wv_ref[...].astype(BF16))
    for s in range(k_ref.shape[0]):
        for h in range(HEADS):
            k_ref[s, :, h, :] = k[s * N_MEM:(s + 1) * N_MEM, h * HEAD_DIM:(h + 1) * HEAD_DIM]
            v_ref[s, :, h, :] = v[s * N_MEM:(s + 1) * N_MEM, h * HEAD_DIM:(h + 1) * HEAD_DIM]
    kb_ref[...] = k.astype(BF16)
    vb_ref[...] = v.astype(BF16)


def _memory_kv(mem2d, g, wk, wv):
    n = mem2d.shape[0]
    tm = TOKEN_TILE
    row = pl.BlockSpec((tm, D_MODEL), lambda i: (i, 0))
    cache = pl.BlockSpec((tm // N_MEM, N_MEM, HEADS, HEAD_DIM), lambda i: (i, 0, 0, 0))
    return pl.pallas_call(
        _memkv_kernel,
        grid=(n // tm,),
        in_specs=[row, _const_spec((1, D_MODEL)), _const_spec((D_MODEL, D_MODEL)),
                  _const_spec((D_MODEL, D_MODEL))],
        out_specs=[cache, cache, row, row],
        out_shape=[jax.ShapeDtypeStruct((n // N_MEM, N_MEM, HEADS, HEAD_DIM), F32)] * 2
        + [jax.ShapeDtypeStruct((n, D_MODEL), BF16)] * 2,
        compiler_params=_cparams(("parallel",)),
        name="memory_kv",
    )(mem2d, g, wk, wv)


W_COLS_BLOCK = 1408


def _w_cols_kernel(w_ref, o_ref):
    o_ref[...] = w_ref[...].T.astype(BF16)


def _w_cols(wt, ncols):
    rb = W_COLS_BLOCK
    return pl.pallas_call(
        _w_cols_kernel,
        grid=(ncols // rb,),
        in_specs=[pl.BlockSpec((rb, D_MODEL), lambda j: (j, 0))],
        out_specs=pl.BlockSpec((D_MODEL, rb), lambda j: (0, j)),
        out_shape=jax.ShapeDtypeStruct((D_MODEL, ncols), BF16),
        compiler_params=_cparams(("parallel",)),
        name="w_cols",
    )(wt)


def _w_rows_kernel(w_hbm, o_ref, buf, sem, *, row0, nblk):
    i = pl.program_id(0)

    def fetch(blk, slot):
        start = pl.multiple_of(row0 + blk * D_MODEL, 8)
        return pltpu.make_async_copy(w_hbm.at[pl.ds(start, D_MODEL), :], buf.at[slot], sem.at[slot])

    @pl.when(i == 0)
    def _():
        fetch(0, 0).start()

    @pl.when(i + 1 < nblk)
    def _():
        fetch(i + 1, (i + 1) % 2).start()

    fetch(i, i % 2).wait()
    o_ref[...] = buf[i % 2].astype(BF16)


def _w_rows(wt, row0, nblk):
    return pl.pallas_call(
        functools.partial(_w_rows_kernel, row0=row0, nblk=nblk),
        grid=(nblk,),
        in_specs=[pl.BlockSpec(memory_space=pl.ANY)],
        out_specs=pl.BlockSpec((D_MODEL, D_MODEL), lambda i: (i, 0)),
        out_shape=jax.ShapeDtypeStruct((nblk * D_MODEL, D_MODEL), BF16),
        scratch_shapes=[pltpu.VMEM((2, D_MODEL, D_MODEL), F32), pltpu.SemaphoreType.DMA((2,))],
        compiler_params=_cparams(("arbitrary",)),
        name="w_rows",
    )(wt)


def _inproj_a_kernel(x_ref, g_ref, wqk_ref, wv_ref, wo_ref, wif_ref, bif_ref, cw_ref, cb_ref, cs_ref,
                     xn_ref, q_ref, kt_ref, v_ref, o_ref, gt_ref, cn_ref, ext_ref, *, nseg, sl, lc):
    i = pl.program_id(1)
    tm = nseg * sl
    tail = CONV_PAD - (CONV_W - 1)

    @pl.when(i == 0)
    def _():
        for s in range(nseg):
            ext_ref[s, 0:tail, :] = jnp.zeros((tail, 2 * D_MODEL), F32)
            ext_ref[s, tail:CONV_PAD, :] = cs_ref[s]

    parts = [_rmsnorm(x_ref[r:r + NORM_ROWS, :], g_ref[...]).astype(BF16) for r in range(0, tm, NORM_ROWS)]
    xn = jnp.concatenate(parts, axis=0)
    xn_ref[...] = xn

    zqk = jnp.concatenate([_dot(part, wqk_ref[...]) for part in parts], axis=0)
    ks = []
    row8 = lax.broadcasted_iota(jnp.int32, (CONV_PAD, 2 * D_MODEL), 0)
    for s in range(nseg):
        cur = zqk[s * sl:(s + 1) * sl, :]
        prev = ext_ref[s]
        acc = cb_ref[...] + cur * cw_ref[CONV_W - 1:CONV_W, :]
        for d in range(1, CONV_W):
            back = pltpu.roll(cur, d, axis=0)
            head = jnp.where(row8 < d, pltpu.roll(prev, d, axis=0), back[0:CONV_PAD, :])
            back = jnp.concatenate([head, back[CONV_PAD:, :]], axis=0)
            acc = acc + back * cw_ref[CONV_W - 1 - d:CONV_W - d, :]
        qk = acc * _sigmoid(acc)
        q_ref[s * sl:(s + 1) * sl, :] = (qk[:, :D_MODEL] * (HEAD_DIM ** -0.5)).astype(BF16)
        ks.append(qk[:, D_MODEL:])
        ext_ref[s] = cur[sl - CONV_PAD:sl, :]
        cn_ref[s] = ext_ref[s, tail:CONV_PAD, :]
    if tm % LANES:
        ks.append(jnp.zeros((LANES - tm % LANES, D_MODEL), F32))
    k = jnp.concatenate(ks, axis=0) if len(ks) > 1 else ks[0]
    kt = k.T.astype(BF16)
    nch = tm // lc
    per_seq = sl // lc
    for c in range(nch):
        kt_ref[c // per_seq, c % per_seq] = kt[:, c * lc:(c + 1) * lc]

    v_ref[...] = _dot(xn, wv_ref[...]).astype(BF16)
    o_ref[...] = _sigmoid(_dot(xn, wo_ref[...])).astype(BF16)

    zg = _dot(xn, wif_ref[...])
    if tm % LANES:
        zg = jnp.concatenate([zg, jnp.zeros((LANES - tm % LANES, LANES), F32)], axis=0)
    zt = zg.T[0:2 * HEADS, :]
    z = jnp.concatenate([zt[:, c * lc:(c + 1) * lc] + bif_ref[...] for c in range(nch)], axis=0)
    is_ig = (lax.broadcasted_iota(jnp.int32, z.shape, 0) % (2 * HEADS)) < HEADS
    g = jnp.where(is_ig, z, _log_sigmoid(z))
    upper = jnp.where(lax.broadcasted_iota(jnp.int32, (lc, lc), 0)
                      <= lax.broadcasted_iota(jnp.int32, (lc, lc), 1), 1.0, 0.0)
    bc = jnp.dot(g, upper, preferred_element_type=F32, precision=lax.Precision.HIGHEST)
    a = g - pltpu.roll(bc, nch * 2 * HEADS - HEADS, axis=0)
    amax = jnp.broadcast_to(jnp.max(a, axis=-1, keepdims=True), z.shape)
    gb = jnp.where(is_ig, g, bc)
    for c in range(nch):
        gt_ref[c // per_seq, c % per_seq, 0:2 * HEADS, :] = gb[c * 8:(c + 1) * 8, :]
        gt_ref[c // per_seq, c % per_seq, 2 * HEADS:4 * HEADS, :] = amax[c * 8:(c + 1) * 8, :]


def _tile_geometry(b, l, tile=TOKEN_TILE):
    sl = min(l, tile)
    nseg = max(1, min(b, TOKEN_TILE // sl))
    return nseg, sl


def _tok_spec(nseg, sl, nt, w):
    return pl.BlockSpec((nseg * sl, w), lambda bi, i: (bi * nt + i, 0))


def _inproj_a(x2d, b, l, g, wa, bif, cw, cb, cs):
    nseg, sl = _tile_geometry(b, l)
    nt = l // sl
    n = b * l
    CHUNK = min(MLSTM_BLOCK, l)
    per_seq = sl // CHUNK
    tok = functools.partial(_tok_spec, nseg, sl, nt)
    state = pl.BlockSpec((nseg, CONV_W - 1, 2 * D_MODEL), lambda bi, i: (bi, 0, 0))
    return pl.pallas_call(
        functools.partial(_inproj_a_kernel, nseg=nseg, sl=sl, lc=CHUNK),
        grid=(b // nseg, nt),
        in_specs=[tok(D_MODEL), _const_spec((1, D_MODEL)), _const_block((D_MODEL, 2 * D_MODEL), (0, 0)),
                  _const_block((D_MODEL, D_MODEL), (0, 2)), _const_block((D_MODEL, D_MODEL), (0, 3)),
                  _const_block((D_MODEL, LANES), (0, 4 * D_MODEL // LANES)), _const_spec((2 * HEADS, 1)),
                  _const_spec((CONV_W, 2 * D_MODEL)), _const_spec((1, 2 * D_MODEL)), state],
        out_specs=[tok(D_MODEL), tok(D_MODEL),
                   pl.BlockSpec((nseg, per_seq, D_MODEL, CHUNK), lambda bi, i: (bi, i, 0, 0)),
                   tok(D_MODEL), tok(D_MODEL),
                   pl.BlockSpec((nseg, per_seq, 4 * HEADS, CHUNK), lambda bi, i: (bi, i, 0, 0)),
                   state],
        out_shape=[jax.ShapeDtypeStruct((n, D_MODEL), BF16), jax.ShapeDtypeStruct((n, D_MODEL), BF16),
                   jax.ShapeDtypeStruct((b, l // CHUNK, D_MODEL, CHUNK), BF16),
                   jax.ShapeDtypeStruct((n, D_MODEL), BF16), jax.ShapeDtypeStruct((n, D_MODEL), BF16),
                   jax.ShapeDtypeStruct((b, l // CHUNK, 4 * HEADS, CHUNK), F32),
                   jax.ShapeDtypeStruct((b, CONV_W - 1, 2 * D_MODEL), F32)],
        scratch_shapes=[pltpu.VMEM((nseg, CONV_PAD, 2 * D_MODEL), F32)],
        compiler_params=_cparams(("parallel", "arbitrary")),
        name="inproj_a",
    )(x2d, g, wa, wa, wa, wa, bif, cw, cb, cs)


def _inproj_b_kernel(x_ref, wu_ref, wgv_ref, wmq_ref, wgate_ref, lng_ref, lnb_ref, ws_ref, bst_ref,
                     mk_ref, mv_ref, ug_ref, att_ref, gates_ref, *rest, nseg, sl, cl, emit_gv):
    tm = nseg * sl
    xn = x_ref[...]

    gates_ref[...] = _sigmoid(_dot_nt(xn, wgate_ref[...])).astype(BF16)

    gvr = jax.nn.gelu(_dot_nt(xn, wgv_ref[...]))
    mu = jnp.mean(gvr, axis=-1, keepdims=True)
    xc = gvr - mu
    r = lax.rsqrt(jnp.mean(xc * xc, axis=-1, keepdims=True) + EPS)
    gv = (xc * r) * lng_ref[...] + lnb_ref[...]
    if emit_gv:
        rest[0][...] = gv
    gvb = gv.astype(BF16)
    u = jax.nn.gelu(_dot_nt(xn, wu_ref[...]))
    tri = (lax.broadcasted_iota(jnp.int32, (cl, cl), 0) >= lax.broadcasted_iota(jnp.int32, (cl, cl), 1))
    for gi in range(GMLP_GROUPS):
        wsg = jnp.where(tri, ws_ref[gi], 0.0).astype(BF16)
        lo, hi = gi * GMLP_GROUP_DIM, (gi + 1) * GMLP_GROUP_DIM
        for c in range(tm // cl):
            sp = _dot(wsg, gvb[c * cl:(c + 1) * cl, lo:hi]) + bst_ref[:, gi:gi + 1]
            ug_ref[c * cl:(c + 1) * cl, lo:hi] = (u[c * cl:(c + 1) * cl, lo:hi] * sp).astype(BF16)

    mq = _dot_nt(xn, wmq_ref[...]).astype(BF16)

    def mem_head(ref, s, h):
        if len(ref.shape) == 4:
            return ref[s, :, h, :].astype(BF16)
        return ref[s, :, h * HEAD_DIM:(h + 1) * HEAD_DIM]

    for s in range(nseg):
        r0, r1 = s * sl, (s + 1) * sl
        for h in range(HEADS):
            lo, hi = h * HEAD_DIM, (h + 1) * HEAD_DIM
            sc = _dot_nt(mq[r0:r1, lo:hi], mem_head(mk_ref, s, h)) * (HEAD_DIM ** -0.5)
            e = jnp.exp(sc - jnp.max(sc, axis=-1, keepdims=True))
            a = (e / jnp.sum(e, axis=-1, keepdims=True)).astype(BF16)
            att_ref[r0:r1, lo:hi] = _dot(a, mem_head(mv_ref, s, h)).astype(BF16)


def _inproj_b(xn2d, b, l, wb, lng, lnb, ws, bst, mk, mv, cl, emit_gv):
    nseg, sl = _tile_geometry(b, l, INPROJ_B_TILE)
    nt = l // sl
    n = b * l
    tok = functools.partial(_tok_spec, nseg, sl, nt)
    if mk.ndim == 4:
        mem = pl.BlockSpec((nseg, N_MEM, HEADS, HEAD_DIM), lambda bi, i: (bi, 0, 0, 0), pipeline_mode=pl.Buffered(1))
    else:
        mem = pl.BlockSpec((nseg, N_MEM, D_MODEL), lambda bi, i: (bi, 0, 0))
    out_specs = [tok(D_MODEL), tok(D_MODEL), tok(3 * D_MODEL)]
    out_shape = [jax.ShapeDtypeStruct((n, D_MODEL), BF16), jax.ShapeDtypeStruct((n, D_MODEL), BF16),
                 jax.ShapeDtypeStruct((n, 3 * D_MODEL), BF16)]
    if emit_gv:
        out_specs.append(tok(D_MODEL))
        out_shape.append(jax.ShapeDtypeStruct((n, D_MODEL), F32))
    return pl.pallas_call(
        functools.partial(_inproj_b_kernel, nseg=nseg, sl=sl, cl=cl, emit_gv=emit_gv),
        grid=(b // nseg, nt),
        in_specs=[tok(D_MODEL), _const_block((D_MODEL, D_MODEL), (0, 0)),
                  _const_block((D_MODEL, D_MODEL), (1, 0)), _const_block((D_MODEL, D_MODEL), (2, 0)),
                  _const_block((3 * D_MODEL, D_MODEL), (1, 0)), _const_spec((1, D_MODEL)), _const_spec((1, D_MODEL)),
                  _const_spec((GMLP_GROUPS, cl, cl)), _const_spec((cl, GMLP_GROUPS)), mem, mem],
        out_specs=out_specs,
        out_shape=out_shape,
        compiler_params=_cparams(("parallel", "parallel")),
        name="inproj_b",
    )(xn2d, wb, wb, wb, wb, lng, lnb, ws, bst, mk, mv)


def _mlstm_kernel(q_ref, kt_ref, v_ref, gt_ref, *rest, nseg, cb, zero_state):
    i = pl.program_id(1)

    if zero_state:
        hm_ref, c_ref, n_ref, m_ref, st_ref = rest
    else:
        c0_ref, n0_ref, m0_ref, hm_ref, c_ref, n_ref, m_ref, st_ref = rest

    @pl.when(i == 0)
    def _():
        if zero_state:
            st_ref[...] = jnp.zeros(st_ref.shape, F32)
            m_ref[...] = jnp.zeros(m_ref.shape, F32)
        else:
            st_ref[:, :, :, :HEAD_DIM] = c0_ref[...]
            st_ref[:, :, :, HEAD_DIM:] = n0_ref[...]
            m_ref[...] = m0_ref[...]

    for s in range(nseg):
        _mlstm_sequence(q_ref, kt_ref.at[s], v_ref, gt_ref.at[s], st_ref.at[s], m_ref.at[s], hm_ref, s * cb, cb)

    @pl.when(i == pl.num_programs(1) - 1)
    def _():
        c_ref[...] = st_ref[:, :, :, :HEAD_DIM]
        n_ref[...] = st_ref[:, :, :, HEAD_DIM:]


def _mlstm_sequence(q_ref, kt_ref, v_ref, gt_ref, c_ref, m_ref, hm_ref, row0, cb):
    L = kt_ref.shape[-1]
    nch = cb // L
    ti = lax.broadcasted_iota(jnp.int32, (L, L), 0)
    si = lax.broadcasted_iota(jnp.int32, (L, L), 1)
    tri = ti >= si
    eye = ti == si

    rows = 4 * HEADS
    g_all = gt_ref[...].reshape(nch * rows, L)

    m_in = [m_ref[:, 0:1]]
    for c in range(nch):
        b_last4 = g_all[c * rows + HEADS:c * rows + 2 * HEADS, L - 1:L]
        amax4 = g_all[c * rows + 2 * HEADS:c * rows + 3 * HEADS, 0:1]
        m_in.append(jnp.maximum(b_last4 + m_in[-1], b_last4 + amax4))

    ones = jnp.ones((L, LANES), BF16)
    st = [c_ref[h] for h in range(HEADS)]
    for c in range(nch):
        r0, r1 = row0 + c * L, row0 + (c + 1) * L
        for h in range(HEADS):
            lo, hi = h * HEAD_DIM, (h + 1) * HEAD_DIM
            ig_r = g_all[c * rows + h:c * rows + h + 1, :]
            bc_r = g_all[c * rows + HEADS + h:c * rows + HEADS + h + 1, :]
            a_r = ig_r - bc_r
            bc_c = jnp.sum(jnp.where(eye, bc_r, 0.0), axis=-1, keepdims=True)
            m0 = m_in[c][h:h + 1, :]
            m_last = m_in[c + 1][h:h + 1, :]
            dmat = jnp.where(tri, bc_c + a_r, NEG_INF)
            inter = bc_c + m0
            m = jnp.maximum(inter, jnp.max(dmat, axis=-1, keepdims=True))
            w_intra = jnp.exp(dmat - m)
            w_inter = jnp.exp(inter - m)
            q = q_ref[r0:r1, lo:hi]
            kt = kt_ref[c, lo:hi, :]
            v = v_ref[r0:r1, lo:hi]
            s = _dot(q, kt) * w_intra
            qs = _dot(q, st[h].astype(BF16))
            num = w_inter * qs[:, :HEAD_DIM] + _dot(s.astype(BF16), v)
            den = w_inter * qs[:, HEAD_DIM:HEAD_DIM + 1] + jnp.sum(s, axis=-1, keepdims=True)
            hh = num / jnp.maximum(jnp.abs(den), jnp.exp(-m))
            bc_last = bc_r[:, L - 1:L]
            w_last = jnp.exp(bc_last + a_r - m_last)
            decay = jnp.exp(bc_last + m0 - m_last)
            ktw = (kt.astype(F32) * w_last).astype(BF16)
            st[h] = decay * st[h] + _dot(ktw, jnp.concatenate([v, ones], axis=1))
            hm_ref[r0:r1, lo:hi] = hh.astype(BF16)

    for h in range(HEADS):
        c_ref[h] = st[h]
    m_ref[...] = jnp.broadcast_to(m_in[nch], (HEADS, LANES))


def _mlstm(q, kt, v, gt, state, b, l):
    nseg, cb = _tile_geometry(b, l, MLSTM_TILE)
    nt = l // cb
    CHUNK = kt.shape[-1]
    tok = _tok_spec(nseg, cb, nt, D_MODEL)
    cs = pl.BlockSpec((nseg, HEADS, HEAD_DIM, HEAD_DIM), lambda bi, i: (bi, 0, 0, 0))
    ns = pl.BlockSpec((nseg, HEADS, HEAD_DIM, LANES), lambda bi, i: (bi, 0, 0, 0))
    ms = pl.BlockSpec((nseg, HEADS, LANES), lambda bi, i: (bi, 0, 0))
    state_specs = [] if state is None else [cs, ns, ms]
    return pl.pallas_call(
        functools.partial(_mlstm_kernel, nseg=nseg, cb=cb, zero_state=state is None),
        grid=(b // nseg, nt),
        in_specs=[tok, pl.BlockSpec((nseg, cb // CHUNK, D_MODEL, CHUNK), lambda bi, i: (bi, i, 0, 0)), tok,
                  pl.BlockSpec((nseg, cb // CHUNK, 4 * HEADS, CHUNK), lambda bi, i: (bi, i, 0, 0))] + state_specs,
        out_specs=[tok, cs, ns, ms],
        out_shape=[jax.ShapeDtypeStruct((b * l, D_MODEL), BF16),
                   jax.ShapeDtypeStruct((b, HEADS, HEAD_DIM, HEAD_DIM), F32),
                   jax.ShapeDtypeStruct((b, HEADS, HEAD_DIM, LANES), F32),
                   jax.ShapeDtypeStruct((b, HEADS, LANES), F32)],
        scratch_shapes=[pltpu.VMEM((nseg, HEADS, HEAD_DIM, HEAD_DIM + LANES), F32)],
        compiler_params=_cparams(("parallel", "arbitrary")),
        name="mlstm",
    )(q, kt, v, gt, *(state or ()))


def _merge_kernel(x_ref, hh_ref, o_ref, ng_ref, ug_ref, att_ref, gates_ref, wa_ref, wb_ref, wc_ref, wo_ref, h_ref):
    parts = []
    for h in range(HEADS):
        lo, hi = h * HEAD_DIM, (h + 1) * HEAD_DIM
        hh = hh_ref[:, lo:hi].astype(F32)
        hn = hh * lax.rsqrt(jnp.mean(hh * hh, axis=-1, keepdims=True) + EPS)
        parts.append(((hn * ng_ref[:, lo:hi]) * o_ref[:, lo:hi].astype(F32)).astype(BF16))
    br = _dot(parts[0], wa_ref[0:HEAD_DIM, :])
    for h in range(1, HEADS):
        br = br + _dot(parts[h], wa_ref[h * HEAD_DIM:(h + 1) * HEAD_DIM, :])
    g = gates_ref[...].astype(F32)
    mixed = g[:, :D_MODEL] * br
    mixed = mixed + g[:, D_MODEL:2 * D_MODEL] * _dot(ug_ref[...], wb_ref[...])
    mixed = mixed + g[:, 2 * D_MODEL:] * _dot(att_ref[...], wc_ref[...])
    h_ref[...] = x_ref[...] + _dot(mixed.astype(BF16), wo_ref[...])


def _merge(x2d, hh, osig, ng, ug, att, gates, wa, wb, wc, wo, tm):
    n = x2d.shape[0]
    row = lambda w: pl.BlockSpec((tm, w), lambda i: (i, 0))
    wspec = _const_spec((D_MODEL, D_MODEL))
    return pl.pallas_call(
        _merge_kernel,
        grid=(n // tm,),
        in_specs=[row(D_MODEL), row(D_MODEL), row(D_MODEL), _const_spec((1, D_MODEL)), row(D_MODEL), row(D_MODEL),
                  row(3 * D_MODEL), wspec, wspec, wspec, wspec],
        out_specs=row(D_MODEL),
        out_shape=jax.ShapeDtypeStruct((n, D_MODEL), F32),
        compiler_params=_cparams(("parallel",)),
        name="merge",
    )(x2d, hh, osig, ng, ug, att, gates, wa, wb, wc, wo)


MOE_T = 256
MOE_STEP_SUBS = 2
MOE_CAP = 32
MOE_S = N_EXPERTS * MOE_CAP
MOE_OVF = 512
SEG_ALIGN = 16
EXPERT_BLOCK = 128
MAIN_RING = 3


def _moe_route_kernel(hp_ref, hs_ref, g_ref, wrg_ref, brg_ref, wre_ref, bre_ref,
                      main_ref, ovf_ref, info_ref, meta_ref, *, nsp):
    n = MOE_STEP_SUBS * MOE_T
    h = jnp.where(pl.program_id(0) < nsp // MOE_STEP_SUBS, hp_ref[...], hs_ref[...])
    xm = _rmsnorm(h, g_ref[...]).astype(BF16)
    lane = lax.broadcasted_iota(jnp.int32, (n, LANES), 1).astype(F32)
    lg = jnp.where(lane < N_GROUPS, _dot(xm, wrg_ref[...]) + brg_ref[...], NEG_INF)
    gmax = jnp.max(lg, axis=-1, keepdims=True)
    p_top = 1.0 / jnp.sum(jnp.exp(lg - gmax), axis=-1, keepdims=True)
    grp = jnp.min(jnp.where(lg == gmax, lane, float(LANES)), axis=-1, keepdims=True)
    el = _dot(xm, wre_ref[...]) + bre_ref[...]
    in_grp = (lane >= grp * EXPERTS_PER_GROUP) & (lane < (grp + 1.0) * EXPERTS_PER_GROUP)
    vals = jnp.where(in_grp, el, NEG_INF)
    v1 = jnp.max(vals, axis=-1, keepdims=True)
    i1 = jnp.min(jnp.where(vals == v1, lane, float(LANES)), axis=-1, keepdims=True)
    vals2 = jnp.where(lane == i1, NEG_INF, vals)
    v2 = jnp.max(vals2, axis=-1, keepdims=True)
    i2 = jnp.min(jnp.where(vals2 == v2, lane, float(LANES)), axis=-1, keepdims=True)
    r = jnp.exp(v2 - v1)
    p1 = p_top / (1.0 + r)
    p2 = p_top * r / (1.0 + r)
    sel1 = lane == i1
    sel2 = lane == i2
    onehot = jnp.where(sel1 | sel2, 1.0, 0.0)

    deferred = []
    for s in range(MOE_STEP_SUBS):
        rows = slice(s * MOE_T, (s + 1) * MOE_T)
        deferred.append(_route_sub_tile(
            xm[rows, :], onehot[rows, :], i1[rows, :], i2[rows, :],
            p1[rows, :], p2[rows, :], main_ref.at[:, s], ovf_ref.at[s], info_ref.at[rows, :], meta_ref.at[s]))
    for write_overflow in deferred:
        write_overflow()


def _route_sub_tile(xm, onehot, i1, i2, p1, p2, main_ref, ovf_ref, info_ref, meta_ref):
    t = MOE_T
    lane = lax.broadcasted_iota(jnp.int32, (t, LANES), 1)
    sel1 = lane.astype(F32) == i1
    sel2 = lane.astype(F32) == i2
    cnt = jnp.sum(onehot, axis=0, keepdims=True).astype(jnp.int32)
    pn = jnp.bitwise_and(jnp.maximum(cnt - MOE_CAP, 0) + (SEG_ALIGN - 1), -SEG_ALIGN)
    pn8 = jnp.broadcast_to(pn, (8, LANES))
    earlier = jnp.where(lax.broadcasted_iota(jnp.int32, (LANES, LANES), 0)
                        < lax.broadcasted_iota(jnp.int32, (LANES, LANES), 1), 1.0, 0.0).astype(BF16)
    off_f8 = _dot(pn8.astype(F32).astype(BF16), earlier)
    row8 = lax.broadcasted_iota(jnp.int32, (8, LANES), 0)
    meta_ref[...] = jnp.where(row8 == 0, off_f8.astype(jnp.int32), jnp.where(row8 == 1, pn8, 0))

    ti = lax.broadcasted_iota(jnp.int32, (t, t), 0)
    si = lax.broadcasted_iota(jnp.int32, (t, t), 1)
    before = jnp.where(ti > si, 1.0, 0.0).astype(BF16)
    rank = _dot(before, onehot.astype(BF16))
    off_f = off_f8[0:1, :]

    def slot_row(sel, idx):
        rk = jnp.sum(jnp.where(sel, rank, 0.0), axis=-1, keepdims=True)
        of = jnp.sum(jnp.where(sel, off_f, 0.0), axis=-1, keepdims=True)
        return jnp.where(rk < MOE_CAP, idx * MOE_CAP + rk, MOE_S - MOE_CAP + of + rk)

    pos1 = slot_row(sel1, i1)
    pos2 = slot_row(sel2, i2)
    info_ref[...] = (jnp.where(lane == 0, pos1, 0.0) + jnp.where(lane == 1, pos2, 0.0)
                     + jnp.where(lane == 2, p1, 0.0) + jnp.where(lane == 3, p2, 0.0))

    eye = ti == si
    pos1_r = jnp.sum(jnp.where(eye, pos1, 0.0), axis=0, keepdims=True)
    pos2_r = jnp.sum(jnp.where(eye, pos2, 0.0), axis=0, keepdims=True)

    def gather(rows, base):
        srow = (lax.broadcasted_iota(jnp.int32, (rows, t), 0) + base).astype(F32)
        pick = jnp.where((srow == pos1_r) | (srow == pos2_r), 1.0, 0.0).astype(BF16)
        return _dot(pick, xm).astype(BF16)

    main_ref[...] = gather(MOE_S, 0).reshape(N_EXPERTS, MOE_CAP, D_MODEL)
    has_ovf = jnp.sum(pn) > 0

    def write_overflow():
        @pl.when(has_ovf)
        def _():
            ovf_ref[...] = gather(MOE_OVF, MOE_S)

        @pl.when(jnp.logical_not(has_ovf))
        def _():
            ovf_ref[...] = jnp.zeros(ovf_ref.shape, ovf_ref.dtype)
    return write_overflow


def _two_group_specs(nsp, cols):
    rows = MOE_STEP_SUBS * MOE_T
    npb = nsp // MOE_STEP_SUBS
    return (pl.BlockSpec((rows, cols), lambda j, *_: (jnp.minimum(j, npb - 1), 0)),
            pl.BlockSpec((rows, cols), lambda j, *_: (jnp.maximum(j - npb, 0), 0)))


def _moe_route(hp, hs, g, wrg, brg, wre, bre):
    nsp = hp.shape[0] // MOE_T
    ns = nsp + hs.shape[0] // MOE_T
    sub = MOE_STEP_SUBS
    return pl.pallas_call(
        functools.partial(_moe_route_kernel, nsp=nsp),
        grid=(ns // sub,),
        in_specs=[*_two_group_specs(nsp, D_MODEL), _const_spec((1, D_MODEL)),
                  _const_spec((D_MODEL, LANES)), _const_spec((1, LANES)),
                  _const_spec((D_MODEL, LANES)), _const_spec((1, LANES))],
        out_specs=[pl.BlockSpec((N_EXPERTS, sub, MOE_CAP, D_MODEL), lambda j: (0, j, 0, 0)),
                   pl.BlockSpec((sub, MOE_OVF, D_MODEL), lambda j: (j, 0, 0)),
                   pl.BlockSpec((sub * MOE_T, LANES), lambda j: (j, 0)),
                   pl.BlockSpec((sub, 8, LANES), lambda j: (j, 0, 0))],
        out_shape=[jax.ShapeDtypeStruct((N_EXPERTS, ns, MOE_CAP, D_MODEL), BF16),
                   jax.ShapeDtypeStruct((ns, MOE_OVF, D_MODEL), BF16),
                   jax.ShapeDtypeStruct((ns * MOE_T, LANES), F32),
                   jax.ShapeDtypeStruct((ns, 8, LANES), jnp.int32)],
        compiler_params=_cparams(("arbitrary",)),
        name="moe_route",
    )(hp, hs, g, wrg, brg, wre, bre)


def _swiglu(x, wgu, wdb):
    gu = _dot(x, wgu[...])
    gate = gu[:, :D_EXPERT]
    hid = (gate * _sigmoid(gate)) * gu[:, D_EXPERT:]
    return _dot(hid.astype(BF16), wdb[...]).astype(BF16)


def _moe_expert_kernel(offs_ref, pns_ref, eflag_ref, main_hbm, slots_hbm, wg_ref, wu_ref, wd_ref,
                       mout_ref, out_hbm, ring, xbuf, obuf, wgu, wdb, sem_ring, sem_in, sem_out, *, n_sub, jb, n_rb):
    e = pl.program_id(0)
    rb = pl.program_id(1)

    def fetch(ex, slot):
        return pltpu.make_async_copy(main_hbm.at[ex], ring.at[slot], sem_ring.at[slot])

    @pl.when(e == 0)
    def _():
        for ex in range(MAIN_RING - 1):
            fetch(ex, ex).start()

    @pl.when(e + MAIN_RING - 1 < N_EXPERTS)
    def _():
        fetch(e + MAIN_RING - 1, (e + MAIN_RING - 1) % MAIN_RING).start()

    slot = e % MAIN_RING
    fetch(e, slot).wait()
    main_ref = ring.at[slot]

    @pl.when((e == 0) & (rb == 0))
    def _():
        xbuf[...] = jnp.zeros_like(xbuf)

    @pl.when(rb == 0)
    def _():
        wgu[:, :D_EXPERT] = wg_ref[...].astype(BF16)
        wgu[:, D_EXPERT:] = wu_ref[...].astype(BF16)
        wdb[...] = wd_ref[...].astype(BF16)

    sb = main_ref.shape[0]
    part = sb // 2 if sb % 2 == 0 else sb
    for s0 in range(0, sb, part):
        y = _swiglu(main_ref[s0:s0 + part].reshape(part * MOE_CAP, D_MODEL), wgu, wdb)
        mout_ref[s0:s0 + part] = y.reshape(part, MOE_CAP, D_MODEL)

    def copy_in(j, src, dst):
        return pltpu.make_async_copy(slots_hbm.at[j, pl.ds(src, SEG_ALIGN), :],
                                     xbuf.at[pl.ds(dst, SEG_ALIGN), :], sem_in)

    def copy_out(j, src, dst):
        return pltpu.make_async_copy(obuf.at[pl.ds(src, SEG_ALIGN), :],
                                     out_hbm.at[j, pl.ds(dst, SEG_ALIGN), :], sem_out)

    def for_each_chunk(g, fn):
        def seg(jj, cur):
            j = g * jb + jj
            n = pns_ref[j * N_EXPERTS + e]
            off = offs_ref[j * N_EXPERTS + e]

            def chunk(k, c):
                fn(j, pl.multiple_of(off + k * SEG_ALIGN, SEG_ALIGN), pl.multiple_of(cur + k * SEG_ALIGN, SEG_ALIGN))
                return c
            lax.fori_loop(0, n // SEG_ALIGN, chunk, 0)
            return cur + n
        return lax.fori_loop(0, jb, seg, 0)

    def group(g, carry):
        total = for_each_chunk(g, lambda j, r, b: copy_in(j, r, b).start())
        nchunk = total // SEG_ALIGN

        def wait_in(k, c):
            copy_in(0, 0, 0).wait()
            return c
        lax.fori_loop(0, nchunk, wait_in, 0)

        def block(bi, c):
            r0 = pl.multiple_of(bi * EXPERT_BLOCK, EXPERT_BLOCK)
            obuf[pl.ds(r0, EXPERT_BLOCK), :] = _swiglu(xbuf[pl.ds(r0, EXPERT_BLOCK), :], wgu, wdb)
            return c
        lax.fori_loop(0, (total + EXPERT_BLOCK - 1) // EXPERT_BLOCK, block, 0)

        for_each_chunk(g, lambda j, r, b: copy_out(j, b, r).start())

        def wait_out(k, c):
            copy_out(0, 0, 0).wait()
            return c
        lax.fori_loop(0, nchunk, wait_out, 0)
        return carry

    @pl.when((rb == n_rb - 1) & (eflag_ref[e] > 0))
    def _():
        lax.fori_loop(0, n_sub // jb, group, 0)


def _moe_expert(main, ovf, offs, pns, eflag, wg, wu, wd):
    ns = main.shape[1]
    n_rb = 1
    sb = ns // n_rb
    jb = max(d for d in range(1, 12) if ns % d == 0)
    rows = jb * MOE_T + EXPERT_BLOCK
    mspec = pl.BlockSpec((None, sb, MOE_CAP, D_MODEL), lambda e, rb, *_: (e, rb, 0, 0))
    wspec = lambda a, b: pl.BlockSpec((None, a, b), lambda e, rb, *_: (e, 0, 0))
    grid_spec = pltpu.PrefetchScalarGridSpec(
        num_scalar_prefetch=3,
        grid=(N_EXPERTS, n_rb),
        in_specs=[pl.BlockSpec(memory_space=pl.ANY), pl.BlockSpec(memory_space=pl.ANY),
                  wspec(D_MODEL, D_EXPERT), wspec(D_MODEL, D_EXPERT), wspec(D_EXPERT, D_MODEL)],
        out_specs=[mspec, pl.BlockSpec(memory_space=pl.ANY)],
        scratch_shapes=[pltpu.VMEM((MAIN_RING, sb, MOE_CAP, D_MODEL), BF16),
                        pltpu.VMEM((rows, D_MODEL), BF16), pltpu.VMEM((rows, D_MODEL), BF16),
                        pltpu.VMEM((D_MODEL, 2 * D_EXPERT), BF16), pltpu.VMEM((D_EXPERT, D_MODEL), BF16),
                        pltpu.SemaphoreType.DMA((MAIN_RING,)),
                        pltpu.SemaphoreType.DMA(()), pltpu.SemaphoreType.DMA(())],
    )
    mout, oout = pl.pallas_call(
        functools.partial(_moe_expert_kernel, n_sub=ns, jb=jb, n_rb=n_rb),
        grid_spec=grid_spec,
        out_shape=[jax.ShapeDtypeStruct(main.shape, main.dtype), jax.ShapeDtypeStruct(ovf.shape, ovf.dtype)],
        input_output_aliases={4: 1},
        compiler_params=_cparams(("arbitrary", "arbitrary")),
        name="moe_expert",
    )(offs, pns, eflag, main, ovf, wg, wu, wd)
    return mout, oout


def _moe_combine_kernel(jflag_ref, sflag_ref, hp_ref, hs_ref, main_hbm, ovf_ref, info_ref, gf_ref,
                        yp_ref, ys_ref, acc_ref, ring, sem_ring, *, nsp, nstep):
    j = pl.program_id(0)
    npb = nsp // MOE_STEP_SUBS

    def fetch(step, slot):
        sub0 = pl.multiple_of(step * MOE_STEP_SUBS, MOE_STEP_SUBS)
        half = N_EXPERTS // 2
        return [pltpu.make_async_copy(main_hbm.at[pl.ds(k * half, half), pl.ds(sub0, MOE_STEP_SUBS)],
                                      ring.at[slot, pl.ds(k * half, half)], sem_ring.at[slot, k]) for k in range(2)]

    def start(step, slot):
        for k, copy in enumerate(fetch(step, slot)):
            copy.start(priority=k)

    @pl.when(j == 0)
    def _():
        for step in range(min(MAIN_RING - 1, nstep)):
            start(step, step)

    @pl.when(j + MAIN_RING - 1 < nstep)
    def _():
        start(j + MAIN_RING - 1, (j + MAIN_RING - 1) % MAIN_RING)

    slot = j % MAIN_RING
    for copy in fetch(j, slot):
        copy.wait()
    main_ref = ring.at[slot]

    def weights(s, cols, base):
        info = info_ref[s * MOE_T:(s + 1) * MOE_T, :]
        scol = (lax.broadcasted_iota(jnp.int32, (MOE_T, cols), 1) + base).astype(F32)
        return (jnp.where(scol == info[:, 0:1], info[:, 2:3], 0.0)
                + jnp.where(scol == info[:, 1:2], info[:, 3:4], 0.0)).astype(BF16)

    for s in range(MOE_STEP_SUBS):
        rows = slice(s * MOE_T, (s + 1) * MOE_T)
        h = jnp.where(j < npb, hp_ref[rows, :], hs_ref[rows, :])
        acc_ref[rows, :] = h + _dot(weights(s, MOE_S, 0), main_ref[:, s].reshape(MOE_S, D_MODEL))

    for s in range(MOE_STEP_SUBS):
        @pl.when(jflag_ref[j * MOE_STEP_SUBS + s] > 0)
        def _(s=s):
            rows = slice(s * MOE_T, (s + 1) * MOE_T)
            acc_ref[rows, :] += _dot(weights(s, MOE_OVF, MOE_S), ovf_ref[s])

    y = _rmsnorm(acc_ref[...], gf_ref[...])

    @pl.when(j < npb)
    def _():
        yp_ref[...] = y

    @pl.when(j >= npb)
    def _():
        ys_ref[...] = y


def _moe_combine(hp, hs, main, ovf, info, jflag, gf):
    nsp = hp.shape[0] // MOE_T
    ns = main.shape[1]
    sub = MOE_STEP_SUBS
    sflag = jnp.max(jflag.reshape(ns // sub, sub), axis=1)
    grid_spec = pltpu.PrefetchScalarGridSpec(
        num_scalar_prefetch=2,
        grid=(ns // sub,),
        in_specs=[*_two_group_specs(nsp, D_MODEL),
                  pl.BlockSpec(memory_space=pl.ANY),
                  pl.BlockSpec((sub, MOE_OVF, D_MODEL), lambda j, jf, sf: (jnp.where(sf[j] > 0, j, 0), 0, 0)),
                  pl.BlockSpec((sub * MOE_T, LANES), lambda j, jf, sf: (j, 0)),
                  pl.BlockSpec((1, D_MODEL), lambda j, jf, sf: (0, 0))],
        out_specs=list(_two_group_specs(nsp, D_MODEL)),
        scratch_shapes=[pltpu.VMEM((sub * MOE_T, D_MODEL), F32),
                        pltpu.VMEM((MAIN_RING, N_EXPERTS, sub, MOE_CAP, D_MODEL), BF16),
                        pltpu.SemaphoreType.DMA((MAIN_RING, 2))],
    )
    return pl.pallas_call(
        functools.partial(_moe_combine_kernel, nsp=nsp, nstep=ns // sub),
        grid_spec=grid_spec,
        out_shape=[jax.ShapeDtypeStruct(hp.shape, F32), jax.ShapeDtypeStruct(hs.shape, F32)],
        compiler_params=_cparams(("arbitrary",)),
        name="moe_combine",
    )(jflag, sflag, hp, hs, main, ovf, info, gf)


def _moe(hp, hs, p):
    main, ovf, info, meta = _moe_route(hp, hs, p["norm_moe_g"], p["w_rg"], p["b_rg"], p["w_re"], p["b_re"])
    pn = meta[:, 1, :N_EXPERTS]
    offs = meta[:, 0, :N_EXPERTS].reshape(-1)
    eflag = (jnp.sum(pn, axis=0) > 0).astype(jnp.int32)
    jflag = (jnp.sum(pn, axis=1) > 0).astype(jnp.int32)
    main, ovf = _moe_expert(main, ovf, offs, pn.reshape(-1), eflag, p["w_eg"], p["w_eu"], p["w_ed"])
    return _moe_combine(hp, hs, main, ovf, info, jflag, p["norm_final_g"])


def _trunk(x, mem_k, mem_v, c0, n0, m0, conv0, p, *, emit_gv):
    b, l, _ = x.shape
    n = b * l
    cl = min(l, GMLP_CHUNK)
    x2d = x.reshape(n, D_MODEL)
    xn, q, kt, v, osig, gt, conv_new = _inproj_a(
        x2d, b, l, p["norm_mix_g"], p["w_a"], p["b_if"], p["conv_w"], p["conv_b"], conv0)
    outs = _inproj_b(xn, b, l, p["w_b"], p["gmlp_norm_g"], p["gmlp_norm_b"], p["w_s"][:, :cl, :cl], p["b_st"][:cl],
                     mem_k, mem_v, cl, emit_gv)
    ug, att, gates = outs[:3]
    on_lanes = lambda a: jnp.broadcast_to(a[..., None], a.shape + (LANES,))
    state = None if c0 is None else (c0, on_lanes(n0), on_lanes(m0))
    hh, c1, n1, m1 = _mlstm(q, kt, v, gt, state, b, l)
    n1 = n1[..., 0]
    h = _merge(x2d, hh, osig, p["mlstm_norm_g"], ug, att, gates,
               p["w_br_mlstm"], p["w_br_gmlp"], p["w_br_mem"], p["w_out"], TOKEN_TILE)
    return h, c1, n1, m1[..., 0], conv_new, (outs[3].reshape(b, l, D_MODEL) if emit_gv else None)


def kernel(x_prompt, x_sample, mem_prompt, cache_mem_k, cache_mem_v, state_mlstm_C, state_mlstm_n, state_mlstm_m, state_mlstm_conv, norm_mix_g, w_in, mlstm_i_b, mlstm_f_b, mlstm_conv_w, mlstm_conv_b, mlstm_norm_g, gmlp_norm_g, gmlp_norm_b, gmlp_w_s, gmlp_b_s, mem_norm_g, w_mem_k, w_mem_v, w_br_mlstm, w_br_gmlp, w_br_mem, w_out, norm_moe_g, w_router_group, b_router_group, w_router_expert, b_router_expert, w_exp_gate, w_exp_up, w_exp_down, norm_final_g):
    bp = x_prompt.shape[0]
    bs = x_sample.shape[0]
    W = D_MODEL
    wt = w_in[0].T
    o_u = 4 * W + 2 * HEADS
    w_a = _w_cols(wt, 4 * W + LANES)
    w_b = _w_rows(wt, o_u, (wt.shape[0] - o_u) // W)
    row = lambda a: a.reshape(1, -1)
    pad_l = lambda a: jnp.pad(a, ((0, 0), (0, LANES - a.shape[1])))
    p = {
        "norm_mix_g": row(norm_mix_g[0]),
        "w_a": w_a, "w_b": w_b,
        "b_if": jnp.concatenate([mlstm_i_b[0], mlstm_f_b[0]]).reshape(2 * HEADS, 1),
        "conv_w": mlstm_conv_w[0], "conv_b": row(mlstm_conv_b[0]),
        "gmlp_norm_g": row(gmlp_norm_g[0]), "gmlp_norm_b": row(gmlp_norm_b[0]),
        "w_s": gmlp_w_s[0], "b_st": gmlp_b_s[0].T,
        "mlstm_norm_g": row(mlstm_norm_g[0]),
        "w_br_mlstm": w_br_mlstm[0].astype(BF16), "w_br_gmlp": w_br_gmlp[0].astype(BF16),
        "w_br_mem": w_br_mem[0].astype(BF16), "w_out": w_out[0].astype(BF16),
        "norm_moe_g": row(norm_moe_g[0]),
        "w_rg": pad_l(w_router_group[0]).astype(BF16), "b_rg": pad_l(row(b_router_group[0])),
        "w_re": pad_l(w_router_expert[0]).astype(BF16), "b_re": pad_l(row(b_router_expert[0])),
        "w_eg": w_exp_gate[0], "w_eu": w_exp_up[0], "w_ed": w_exp_down[0],
        "norm_final_g": row(norm_final_g),
    }

    mk_p, mv_p, mk_pb, mv_pb = _memory_kv(mem_prompt.reshape(bp * N_MEM, W), row(mem_norm_g[0]),
                                          w_mem_k[0], w_mem_v[0])

    zeros = lambda *s: jnp.zeros(s, F32)
    hp, cp, np_, mp, cvp, _ = _trunk(
        x_prompt, mk_pb.reshape(bp, N_MEM, W), mv_pb.reshape(bp, N_MEM, W),
        None, None, None, zeros(bp, CONV_W - 1, 2 * W), p, emit_gv=False)
    hs, cs, ns, ms, cvs, gvs = _trunk(
        x_sample, cache_mem_k[0], cache_mem_v[0],
        state_mlstm_C[0], state_mlstm_n[0], state_mlstm_m[0], state_mlstm_conv[0], p,
        emit_gv=True)

    yp, ys = _moe(hp, hs, p)
    return (yp.reshape(x_prompt.shape), ys.reshape(x_sample.shape), mk_p[None], mv_p[None],
            cp[None], np_[None], mp[None], cvp[None],
            cs[None], ns[None], ms[None], cvs[None], gvs[None])
```
